```python
import functools
import jax, jax.numpy as jnp
from jax import lax
import numpy as np

D_MODEL = 1024
BATCH = 16
SEQ = 256
DEPTH = 2
DEC_BATCH = 4
DEC_SEQ = 2048
PAST_LEN = 512

GRID_W = 64
N_EVEN = (DEPTH + 1) // 2
N_ODD = DEPTH // 2
MOD_CHUNKS = 6
EPS = 1e-6
CONV_DIM = D_MODEL // 2
CONV_W = 3
N_Q_HEADS = 8
N_KV_HEADS = 2
GQA_GROUP = N_Q_HEADS // N_KV_HEADS
HEAD_DIM = (D_MODEL // 2) // N_Q_HEADS
WINDOW = 128
ATTN_BLOCK = 128
ROPE_BASE = 10000.0
AB_IN = 3 * CONV_DIM + (N_Q_HEADS + 2 * N_KV_HEADS) * HEAD_DIM
AB_OUT = CONV_DIM + N_Q_HEADS * HEAD_DIM
GLA_HEADS = 4
GLA_DK = (D_MODEL // 2) // GLA_HEADS
GLA_DV = D_MODEL // GLA_HEADS
GLA_RANK = 16
GLA_GATE_NORM = 16.0
GLA_CHUNK = 64
GLA_IN = 2 * GLA_HEADS * GLA_DK + 2 * GLA_HEADS * GLA_DV + 2 * GLA_RANK
D_FF = 4 * D_MODEL
NEG_INF = -1e30

kernel_name = 'hybrid_diffusion_conv_swa_gla_step'


def rms_norm(x, g):
    xf = x.astype(jnp.float32)
    y = xf * lax.rsqrt(jnp.mean(xf * xf, axis=-1, keepdims=True) + EPS)
    return (y * g.astype(jnp.float32)).astype(x.dtype)


def modulation(cond, w, b):
    m = jax.nn.silu(cond) @ w + b
    return [t[:, None, :] for t in jnp.split(m, MOD_CHUNKS, axis=-1)]


def sq_relu_mlp(h, w1, w2):
    return jnp.square(jax.nn.relu(h @ w1)) @ w2


def adaln_layer(x, cond, mod_w, mod_b, norm_g, w1, w2, mixer):
    sh1, sc1, gt1, sh2, sc2, gt2 = modulation(cond, mod_w, mod_b)
    h = rms_norm(x, norm_g[0]) * (1 + sc1) + sh1
    mix, aux = mixer(h)
    x = x + gt1 * rms_norm(mix, norm_g[1])
    h = rms_norm(x, norm_g[2]) * (1 + sc2) + sh2
    x = x + gt2 * rms_norm(sq_relu_mlp(h, w1, w2), norm_g[3])
    return x, aux


def short_conv(u, w):
    n = u.shape[1]
    pad = CONV_W // 2
    up = jnp.pad(u, ((0, 0), (pad, pad), (0, 0)))
    out = up[:, 0:n] * w[0]
    for i in range(1, CONV_W):
        out = out + up[:, i:i + n] * w[i]
    return out


def axial_rope(x):
    n = x.shape[1]
    rows = n // GRID_W
    pos_r = jnp.repeat(jnp.arange(rows), GRID_W)
    pos_c = jnp.tile(jnp.arange(GRID_W), rows)
    half = HEAD_DIM // 2
    quarter = half // 2
    inv = ROPE_BASE ** (-(jnp.arange(quarter, dtype=jnp.float32) * 2.0 / half))

    def rot(xa, pos):
        ang = pos.astype(jnp.float32)[:, None] * inv[None, :]
        cos = jnp.cos(ang)[None, :, None, :].astype(x.dtype)
        sin = jnp.sin(ang)[None, :, None, :].astype(x.dtype)
        x1, x2 = xa[..., :quarter], xa[..., quarter:]
        return jnp.concatenate([x1 * cos - x2 * sin, x2 * cos + x1 * sin], axis=-1)

    return jnp.concatenate([rot(x[..., :half], pos_r), rot(x[..., half:], pos_c)], axis=-1)


def sink_softmax(logits, sink):
    sb = jnp.broadcast_to(sink[:, :, None, None], logits.shape[:-1] + (1,))
    p = jax.nn.softmax(jnp.concatenate([logits, sb], axis=-1), axis=-1)
    return p[..., :-1]


def ctx_attention(q, k, v, sink):
    b, s = q.shape[:2]
    nb = s // ATTN_BLOCK
    scale = HEAD_DIM ** -0.5
    qb = q.reshape(b, nb, ATTN_BLOCK, N_KV_HEADS, GQA_GROUP, HEAD_DIM).swapaxes(0, 1)
    sink_f = sink.astype(jnp.float32).reshape(N_KV_HEADS, GQA_GROUP)

    def one_block(qblk):
        sc = jnp.einsum('bqhgd,bkhd->bhgqk', qblk, k, preferred_element_type=jnp.float32) * scale
        p = sink_softmax(sc, sink_f).astype(v.dtype)
        return jnp.einsum('bhgqk,bkhd->bqhgd', p, v)

    o = lax.map(one_block, qb)
    return o.swapaxes(0, 1).reshape(b, s, N_Q_HEADS * HEAD_DIM)


def latent_attention(q, k, v, k_ctx, v_ctx, sink):
    b, n = q.shape[:2]
    nb = n // ATTN_BLOCK
    scale = HEAD_DIM ** -0.5
    qb = q.reshape(b, nb, ATTN_BLOCK, N_KV_HEADS, GQA_GROUP, HEAD_DIM)

    def bands(t):
        tp = jnp.pad(t, ((0, 0), (ATTN_BLOCK, ATTN_BLOCK), (0, 0), (0, 0)))
        tp = tp.reshape(b, nb + 2, ATTN_BLOCK, N_KV_HEADS, HEAD_DIM)
        return jnp.concatenate([tp[:, :-2], tp[:, 1:-1], tp[:, 2:]], axis=2)

    kb, vb = bands(k), bands(v)
    blk = jnp.arange(nb)[:, None, None]
    qi = jnp.arange(ATTN_BLOCK)[None, :, None]
    kj = jnp.arange(3 * ATTN_BLOCK)[None, None, :]
    qpos = blk * ATTN_BLOCK + qi
    kpos = (blk - 1) * ATTN_BLOCK + kj
    valid = (jnp.abs(kpos - qpos) <= WINDOW) & (kpos >= 0) & (kpos < n)
    s_loc = jnp.einsum('bnqhgd,bnkhd->bnhgqk', qb, kb, preferred_element_type=jnp.float32) * scale
    s_loc = jnp.where(valid[None, :, None, None], s_loc, NEG_INF)
    s_ctx = jnp.einsum('bnqhgd,bphd->bnhgqp', qb, k_ctx, preferred_element_type=jnp.float32) * scale
    sink_f = sink.astype(jnp.float32).reshape(N_KV_HEADS, GQA_GROUP)
    p = sink_softmax(jnp.concatenate([s_loc, s_ctx], axis=-1), sink_f).astype(v.dtype)
    n_loc = 3 * ATTN_BLOCK
    o = (jnp.einsum('bnhgqk,bnkhd->bnqhgd', p[..., :n_loc], vb)
         + jnp.einsum('bnhgqp,bphd->bnqhgd', p[..., n_loc:], v_ctx))
    return o.reshape(b, n, N_Q_HEADS * HEAD_DIM)


def ab_split(h, w_in, conv_w):
    b, n, _ = h.shape
    z = h @ w_in
    c = CONV_DIM
    qd = N_Q_HEADS * HEAD_DIM
    kd = N_KV_HEADS * HEAD_DIM
    gate_b, gate_c, xin = z[..., :c], z[..., c:2 * c], z[..., 2 * c:3 * c]
    o = 3 * c
    q = z[..., o:o + qd].reshape(b, n, N_Q_HEADS, HEAD_DIM)
    k = z[..., o + qd:o + qd + kd].reshape(b, n, N_KV_HEADS, HEAD_DIM)
    v = z[..., o + qd + kd:].reshape(b, n, N_KV_HEADS, HEAD_DIM)
    conv_out = gate_b * short_conv(gate_c * xin, conv_w)
    return conv_out, q, k, v


def ab_context_mixer(h, w_in, conv_w, sink, w_out):
    conv_out, q, k, v = ab_split(h, w_in, conv_w)
    attn = ctx_attention(q, k, v, sink)
    return jnp.concatenate([conv_out, attn], axis=-1) @ w_out, (k, v)


def ab_latent_mixer(h, k_ctx, v_ctx, w_in, conv_w, sink, w_out):
    conv_out, q, k, v = ab_split(h, w_in, conv_w)
    attn = latent_attention(axial_rope(q), axial_rope(k), v, k_ctx.astype(v.dtype), v_ctx.astype(v.dtype), sink)
    return jnp.concatenate([conv_out, attn], axis=-1) @ w_out, None


def gla_split(h, w_in, gate_w, gate_b):
    b, n, _ = h.shape
    z = h @ w_in
    qk = GLA_HEADS * GLA_DK
    vd = GLA_HEADS * GLA_DV
    q = z[..., :qk]
    k = z[..., qk:2 * qk]
    v = z[..., 2 * qk:2 * qk + vd]
    og = z[..., 2 * qk + vd:2 * qk + 2 * vd]
    rf = z[..., 2 * qk + 2 * vd:2 * qk + 2 * vd + GLA_RANK]
    rb = z[..., 2 * qk + 2 * vd + GLA_RANK:]

    def heads(t, d):
        return t.reshape(b, n, GLA_HEADS, d).transpose(0, 2, 1, 3)

    def log_gate(r, i):
        return jax.nn.log_sigmoid((r @ gate_w[i] + gate_b[i]).astype(jnp.float32)) / GLA_GATE_NORM

    return (heads(q, GLA_DK) * GLA_DK ** -0.5, heads(k, GLA_DK), heads(v, GLA_DV), og,
            heads(log_gate(rf, 0), GLA_DK), heads(log_gate(rb, 1), GLA_DK))


def gla_chunk_scan(q, k, v, g, s0):
    b, hh, n, _ = q.shape
    nc = n // GLA_CHUNK

    def chunks(t):
        return t.astype(jnp.float32).reshape(b, hh, nc, GLA_CHUNK, t.shape[-1]).transpose(2, 0, 1, 3, 4)

    causal = jnp.tril(jnp.ones((GLA_CHUNK, GLA_CHUNK), dtype=bool))[:, :, None]

    def step(S, inp):
        qc, kc, vc, gc = inp
        bc = jnp.cumsum(gc, axis=2)
        o = jnp.einsum('bhcd,bhde->bhce', qc * jnp.exp(bc), S)
        diff = bc[:, :, :, None, :] - bc[:, :, None, :, :]
        decay = jnp.exp(jnp.where(causal, diff, -jnp.inf))
        att = jnp.einsum('bhid,bhijd->bhij', qc, decay * kc[:, :, None, :, :])
        o = o + jnp.einsum('bhij,bhje->bhie', att, vc)
        bl = bc[:, :, -1:, :]
        S = (jnp.exp(bl[:, :, 0, :])[..., None] * S
             + jnp.einsum('bhcd,bhce->bhde', kc * jnp.exp(bl - bc), vc))
        return S, o

    S, o = lax.scan(step, s0.astype(jnp.float32), (chunks(q), chunks(k), chunks(v), chunks(g)))
    return o.transpose(1, 2, 0, 3, 4).reshape(b, hh, n, GLA_DV), S


def gla_mixer(q, k, v, og, gf, gb, s0f, s0b, norm_g, w_out):
    b, _, n, _ = q.shape
    of, sf = gla_chunk_scan(q, k, v, gf, s0f)
    rev = lambda t: jnp.flip(t, axis=2)
    ob, sb = gla_chunk_scan(rev(q), rev(k), rev(v), rev(gb), s0b)
    o = (of + rev(ob)).transpose(0, 2, 1, 3)
    o = rms_norm(o, norm_g).reshape(b, n, GLA_HEADS * GLA_DV).astype(og.dtype) * jax.nn.silu(og)
    return o @ w_out, sf, sb


def gla_context_mixer(h, w_in, gate_w, gate_b, norm_g, w_out):
    q, k, v, og, gf, gb = gla_split(h, w_in, gate_w, gate_b)
    s0 = jnp.zeros((h.shape[0], GLA_HEADS, GLA_DK, GLA_DV), jnp.float32)
    out, sf, sb = gla_mixer(q, k, v, og, gf, gb, s0, s0, norm_g, w_out)
    return out, (sf, sb)


def gla_latent_mixer(h, s0f, s0b, w_in, gate_w, gate_b, norm_g, w_out):
    q, k, v, og, gf, gb = gla_split(h, w_in, gate_w, gate_b)
    out, _, _ = gla_mixer(q, k, v, og, gf, gb, s0f, s0b, norm_g, w_out)
    return out, None


def setup_inputs(seed: int = 0) -> dict:
    key = jax.random.key(seed)
    ks = jax.random.split(key, 24)
    f32 = jnp.float32

    def nrm(k, shape, s):
        return jax.random.normal(k, shape, f32) * s

    return {
        'x_prompt': nrm(ks[0], (BATCH, SEQ, D_MODEL), 1.0),
        'x_sample': nrm(ks[1], (DEC_BATCH, DEC_SEQ, D_MODEL), 1.0),
        'cache_k': nrm(ks[2], (DEC_BATCH, N_EVEN, PAST_LEN, N_KV_HEADS, HEAD_DIM), 1.0),
        'cache_v': nrm(ks[3], (DEC_BATCH, N_EVEN, PAST_LEN, N_KV_HEADS, HEAD_DIM), 1.0),
        'state_fwd': nrm(ks[4], (DEC_BATCH, N_ODD, GLA_HEADS, GLA_DK, GLA_DV), 0.5),
        'state_bwd': nrm(ks[5], (DEC_BATCH, N_ODD, GLA_HEADS, GLA_DK, GLA_DV), 0.5),
        'c': nrm(ks[6], (DEC_BATCH, D_MODEL), 1.0),
        'c_ctx': nrm(ks[7], (D_MODEL,), 1.0),
        'mod_w': nrm(ks[8], (DEPTH, D_MODEL, MOD_CHUNKS * D_MODEL), D_MODEL ** -0.5),
        'mod_b': nrm(ks[9], (DEPTH, MOD_CHUNKS * D_MODEL), 0.02),
        'norm_g': 1.0 + nrm(ks[10], (DEPTH, 4, D_MODEL), 0.05),
        'ab_w_in': nrm(ks[11], (N_EVEN, D_MODEL, AB_IN), D_MODEL ** -0.5),
        'conv_w': nrm(ks[12], (N_EVEN, CONV_W, CONV_DIM), CONV_W ** -0.5),
        'attn_sink': nrm(ks[13], (N_EVEN, N_Q_HEADS), 1.0),
        'ab_w_out': nrm(ks[14], (N_EVEN, AB_OUT, D_MODEL), AB_OUT ** -0.5),
        'gla_w_in': nrm(ks[15], (N_ODD, D_MODEL, GLA_IN), D_MODEL ** -0.5),
        'gla_gate_w': nrm(ks[16], (N_ODD, 2, GLA_RANK, GLA_HEADS * GLA_DK), GLA_RANK ** -0.5),
        'gla_gate_b': nrm(ks[17], (N_ODD, 2, GLA_HEADS * GLA_DK), 0.1),
        'gla_norm_g': 1.0 + nrm(ks[18], (N_ODD, GLA_DV), 0.05),
        'gla_w_out': nrm(ks[19], (N_ODD, GLA_HEADS * GLA_DV, D_MODEL), (GLA_HEADS * GLA_DV) ** -0.5),
        'mlp_w1': nrm(ks[20], (DEPTH, D_MODEL, D_FF), D_MODEL ** -0.5),
        'mlp_w2': nrm(ks[21], (DEPTH, D_FF, D_MODEL), D_FF ** -0.5),
    }


def reference(x_prompt, x_sample, cache_k, cache_v, state_fwd, state_bwd, c, c_ctx,
              mod_w, mod_b, norm_g, ab_w_in, conv_w, attn_sink, ab_w_out,
              gla_w_in, gla_gate_w, gla_gate_b, gla_norm_g, gla_w_out, mlp_w1, mlp_w2):
    x = x_prompt
    cond_ctx = c_ctx[None, :]
    ks, vs, sfs, sbs = [], [], [], []
    for l in range(DEPTH):
        j = l // 2
        if l % 2 == 0:
            mixer = functools.partial(ab_context_mixer, w_in=ab_w_in[j], conv_w=conv_w[j],
                                      sink=attn_sink[j], w_out=ab_w_out[j])
        else:
            mixer = functools.partial(gla_context_mixer, w_in=gla_w_in[j], gate_w=gla_gate_w[j],
                                      gate_b=gla_gate_b[j], norm_g=gla_norm_g[j], w_out=gla_w_out[j])
        x, aux = adaln_layer(x, cond_ctx, mod_w[l], mod_b[l], norm_g[l], mlp_w1[l], mlp_w2[l], mixer)
        if l % 2 == 0:
            ks.append(aux[0])
            vs.append(aux[1])
        else:
            sfs.append(aux[0])
            sbs.append(aux[1])
    y_prompt = x
    new_cache_k = jnp.stack(ks, axis=1)
    new_cache_v = jnp.stack(vs, axis=1)
    new_state_fwd = jnp.stack(sfs, axis=1)
    new_state_bwd = jnp.stack(sbs, axis=1)

    x = x_sample
    for l in range(DEPTH):
        j = l // 2
        if l % 2 == 0:
            mixer = functools.partial(ab_latent_mixer, k_ctx=cache_k[:, j], v_ctx=cache_v[:, j],
                                      w_in=ab_w_in[j], conv_w=conv_w[j], sink=attn_sink[j], w_out=ab_w_out[j])
        else:
            mixer = functools.partial(gla_latent_mixer, s0f=state_fwd[:, j], s0b=state_bwd[:, j],
                                      w_in=gla_w_in[j], gate_w=gla_gate_w[j], gate_b=gla_gate_b[j],
                                      norm_g=gla_norm_g[j], w_out=gla_w_out[j])
        x, _ = adaln_layer(x, c, mod_w[l], mod_b[l], norm_g[l], mlp_w1[l], mlp_w2[l], mixer)
    y_sample = x
    return (y_prompt, y_sample, new_cache_k, new_cache_v, new_state_fwd, new_state_bwd)
```

```python
import functools

import jax
import jax.numpy as jnp
from jax import lax
from jax.experimental import pallas as pl
from jax.experimental.pallas import tpu as pltpu

F32 = jnp.float32
BF16 = jnp.bfloat16

D_MODEL = 1024
MOD_CHUNKS = 6
EPS = 1e-6
CONV_DIM = 512
N_Q_HEADS = 8
N_KV_HEADS = 2
GQA_GROUP = 4
HEAD_DIM = 64
WINDOW = 128
ATTN_BLOCK = 128
GRID_W = 64
ROPE_BASE = 10000.0
QD = N_Q_HEADS * HEAD_DIM
KD = N_KV_HEADS * HEAD_DIM
AB_IN = 3 * CONV_DIM + QD + 2 * KD
GLA_HEADS = 4
GLA_DK = 128
GLA_DV = 256
GLA_RANK = 16
GLA_GATE_NORM = 16.0
GLA_CHUNK = 64
GLA_SUB = 16
GLA_QK = GLA_HEADS * GLA_DK
GLA_VD = GLA_HEADS * GLA_DV
GLA_MAIN = 2 * GLA_QK + 2 * GLA_VD
D_FF = 4 * D_MODEL
NEG_INF = -1e30
LANES = 128
VMEM_LIMIT = 56 * 1024 * 1024

NT_DIMS = (((1,), (1,)), ((), ()))


def _cparams(*sem):
    return pltpu.CompilerParams(dimension_semantics=sem, vmem_limit_bytes=VMEM_LIMIT)


def _bdot(a, b):
    return jnp.dot(a.astype(BF16), b.astype(BF16), preferred_element_type=F32)


def _bdot_nt(a, b):
    return lax.dot_general(a.astype(BF16), b.astype(BF16), NT_DIMS, preferred_element_type=F32)


def _rms(x, g):
    ms = jnp.mean(x * x, axis=-1, keepdims=True)
    return x * lax.rsqrt(ms + EPS) * g


def _mod_chunk(mod_ref, i):
    return mod_ref[:, i * D_MODEL:(i + 1) * D_MODEL]


def _const_spec(shape):
    return pl.BlockSpec(shape, lambda *_: (0,) * len(shape))


def _mod_kernel(cond_ref, w_ref, b_ref, o_ref):
    cnd = cond_ref[...]
    s = cnd / (1.0 + jnp.exp(-cnd))
    o_ref[...] = _bdot(s, w_ref[...]) + b_ref[...]


def _modulation(cond8, mod_w, mod_b):
    depth = mod_w.shape[0]
    n = mod_w.shape[2]
    tn = 1536
    return pl.pallas_call(
        _mod_kernel,
        grid=(depth, n // tn),
        in_specs=[
            pl.BlockSpec((8, D_MODEL), lambda l, j: (0, 0)),
            pl.BlockSpec((None, D_MODEL, tn), lambda l, j: (l, 0, j)),
            pl.BlockSpec((None, 1, tn), lambda l, j: (l, 0, j)),
        ],
        out_specs=pl.BlockSpec((None, 8, tn), lambda l, j: (l, 0, j)),
        out_shape=jax.ShapeDtypeStruct((depth, 8, n), F32),
        compiler_params=_cparams("arbitrary", "arbitrary"),
        name="modulation",
    )(cond8, mod_w, mod_b.reshape(depth, 1, n))


class _Group:
    def __init__(self, b, n, per_seq_mod):
        self.b, self.n, self.t = b, n, b * n
        self.per_seq_mod = per_seq_mod

    def mod_spec(self, tm):
        if self.per_seq_mod:
            assert self.n % tm == 0
            per = self.n // tm
            return pl.BlockSpec((None, 1, MOD_CHUNKS * D_MODEL), lambda t: (t // per, 0, 0))
        return pl.BlockSpec((None, 1, MOD_CHUNKS * D_MODEL), lambda t: (0, 0, 0))


def _row_spec(tm, width):
    return pl.BlockSpec((tm, width), lambda t: (t, 0))


def _rope(x, cos, sin_lo, sin_hi):
    return (x * cos + pltpu.roll(x, LANES - 16, axis=1) * sin_lo
            + pltpu.roll(x, 16, axis=1) * sin_hi)


def _premix0_kernel(*refs, rope):
    if rope:
        x_ref, mod_ref, g_ref, w_ref, cos_ref, slo_ref, shi_ref, zc_ref, q_ref, kv_ref = refs
    else:
        x_ref, mod_ref, g_ref, w_ref, zc_ref, q_ref, kv_ref = refs
    sh1, sc1 = _mod_chunk(mod_ref, 0), _mod_chunk(mod_ref, 1)
    h = _rms(x_ref[...], g_ref[0:1, :]) * (1.0 + sc1) + sh1
    z = _bdot(h, w_ref[...])
    c3 = 3 * CONV_DIM
    zc_ref[...] = z[:, :c3]
    scale = HEAD_DIM ** -0.5
    if rope:
        cos, slo, shi = cos_ref[...], slo_ref[...], shi_ref[...]
        for j in range(QD // LANES):
            qs = z[:, c3 + j * LANES:c3 + (j + 1) * LANES]
            q_ref[:, j * LANES:(j + 1) * LANES] = (_rope(qs, cos, slo, shi) * scale).astype(BF16)
        kv_ref[:, :KD] = _rope(z[:, c3 + QD:c3 + QD + KD], cos, slo, shi)
    else:
        q_ref[...] = (z[:, c3:c3 + QD] * scale).astype(BF16)
        kv_ref[:, :KD] = z[:, c3 + QD:c3 + QD + KD]
    kv_ref[:, KD:] = z[:, c3 + QD + KD:]


def _premix0(grp, x2d, mod, norm_g, w_in, rope_tabs, tm):
    rope = rope_tabs is not None
    in_specs = [_row_spec(tm, D_MODEL), grp.mod_spec(tm), _const_spec((4, D_MODEL)),
                _const_spec((D_MODEL, AB_IN))]
    args = [x2d, mod, norm_g, w_in]
    if rope:
        per = grp.n // tm
        in_specs += [pl.BlockSpec((tm, LANES), lambda t: (t % per, 0))] * 3
        args += list(rope_tabs)
    return pl.pallas_call(
        functools.partial(_premix0_kernel, rope=rope),
        grid=(grp.t // tm,),
        in_specs=in_specs,
        out_specs=[_row_spec(tm, 3 * CONV_DIM), _row_spec(tm, QD), _row_spec(tm, 2 * KD)],
        out_shape=[jax.ShapeDtypeStruct((grp.t, 3 * CONV_DIM), F32),
                   jax.ShapeDtypeStruct((grp.t, QD), BF16),
                   jax.ShapeDtypeStruct((grp.t, 2 * KD), F32)],
        compiler_params=_cparams("arbitrary"),
        name="premix0_rope" if rope else "premix0",
    )(*args)


def _rope_tables(n):
    rows = n // GRID_W
    pos_r = jnp.repeat(jnp.arange(rows), GRID_W)
    pos_c = jnp.tile(jnp.arange(GRID_W), rows)
    half = HEAD_DIM // 2
    quarter = half // 2
    inv = ROPE_BASE ** (-(jnp.arange(quarter, dtype=F32) * 2.0 / half))

    def cs(pos):
        ang = pos.astype(F32)[:, None] * inv[None, :]
        return jnp.cos(ang), jnp.sin(ang)

    cr, sr = cs(pos_r)
    cc, sc = cs(pos_c)
    zero = jnp.zeros_like(sr)
    cos = jnp.concatenate([cr, cr, cc, cc], axis=1)
    sin_lo = jnp.concatenate([-sr, zero, -sc, zero], axis=1)
    sin_hi = jnp.concatenate([zero, sr, zero, sc], axis=1)
    rep = LANES // HEAD_DIM
    return tuple(jnp.tile(t, (1, rep)) for t in (cos, sin_lo, sin_hi))


def _sink_attend(s, sink, vs):
    m = sink
    for si in s:
        m = jnp.maximum(m, jnp.max(si, axis=-1, keepdims=True))
    den = jnp.exp(sink - m)
    o = None
    for si, vi in zip(s, vs):
        p = jnp.exp(si - m)
        den = den + jnp.sum(p, axis=-1, keepdims=True)
        oi = _bdot(p, vi)
        o = oi if o is None else o + oi
    return o / den


def _ctx_attn_kernel(sink_ref, q_ref, kv_ref, o_ref):
    for g in range(N_KV_HEADS):
        k = kv_ref[:, g * HEAD_DIM:(g + 1) * HEAD_DIM].astype(BF16)
        v = kv_ref[:, KD + g * HEAD_DIM:KD + (g + 1) * HEAD_DIM].astype(BF16)
        for hh in range(GQA_GROUP):
            h = g * GQA_GROUP + hh
            q = q_ref[:, h * HEAD_DIM:(h + 1) * HEAD_DIM]
            s = _bdot_nt(q, k)
            o = _sink_attend([s], sink_ref[h], [v])
            o_ref[:, h * HEAD_DIM:(h + 1) * HEAD_DIM] = o.astype(BF16)


def _ctx_attention(grp, q, kv, sink):
    n = grp.n
    return pl.pallas_call(
        _ctx_attn_kernel,
        grid=(grp.b,),
        in_specs=[pl.BlockSpec(memory_space=pltpu.SMEM), _row_spec(n, QD), _row_spec(n, 2 * KD)],
        out_specs=_row_spec(n, QD),
        out_shape=jax.ShapeDtypeStruct((grp.t, QD), BF16),
        compiler_params=_cparams("arbitrary"),
        name="ctx_attention",
    )(sink, q, kv)


def _lat_attn_kernel(sink_ref, q_ref, kvp_ref, kvc_ref, kvn_ref, kc_ref, vc_ref, o_ref, *, n):
    blk = pl.program_id(1)
    kv = jnp.concatenate([kvp_ref[...], kvc_ref[...], kvn_ref[...]], axis=0)
    qi = lax.broadcasted_iota(jnp.int32, (ATTN_BLOCK, 3 * ATTN_BLOCK), 0)
    kj = lax.broadcasted_iota(jnp.int32, (ATTN_BLOCK, 3 * ATTN_BLOCK), 1)
    rel = kj - ATTN_BLOCK - qi
    kpos = (blk - 1) * ATTN_BLOCK + kj
    valid = (jnp.abs(rel) <= WINDOW) & (kpos >= 0) & (kpos < n)
    for g in range(N_KV_HEADS):
        k = kv[:, g * HEAD_DIM:(g + 1) * HEAD_DIM].astype(BF16)
        v = kv[:, KD + g * HEAD_DIM:KD + (g + 1) * HEAD_DIM].astype(BF16)
        kc = kc_ref[:, g * HEAD_DIM:(g + 1) * HEAD_DIM].astype(BF16)
        vc = vc_ref[:, g * HEAD_DIM:(g + 1) * HEAD_DIM].astype(BF16)
        for hh in range(GQA_GROUP):
            h = g * GQA_GROUP + hh
            q = q_ref[:, h * HEAD_DIM:(h + 1) * HEAD_DIM]
            s_loc = jnp.where(valid, _bdot_nt(q, k), NEG_INF)
            s_ctx = _bdot_nt(q, kc)
            o = _sink_attend([s_loc, s_ctx], sink_ref[h], [v, vc])
            o_ref[:, h * HEAD_DIM:(h + 1) * HEAD_DIM] = o.astype(BF16)


def _lat_attention(grp, q, kv, k_ctx, v_ctx, sink):
    nb = grp.n // ATTN_BLOCK
    past = k_ctx.shape[1]

    def band(off):
        return pl.BlockSpec((ATTN_BLOCK, 2 * KD),
                            lambda b, i: (b * nb + jnp.clip(i + off, 0, nb - 1), 0))

    ctx_spec = pl.BlockSpec((None, past, KD), lambda b, i: (b, 0, 0))
    return pl.pallas_call(
        functools.partial(_lat_attn_kernel, n=grp.n),
        grid=(grp.b, nb),
        in_specs=[pl.BlockSpec(memory_space=pltpu.SMEM),
                  pl.BlockSpec((ATTN_BLOCK, QD), lambda b, i: (b * nb + i, 0)),
                  band(-1), band(0), band(1), ctx_spec, ctx_spec],
        out_specs=pl.BlockSpec((ATTN_BLOCK, QD), lambda b, i: (b * nb + i, 0)),
        out_shape=jax.ShapeDtypeStruct((grp.t, QD), BF16),
        compiler_params=_cparams("arbitrary", "arbitrary"),
        name="lat_attention",
    )(sink, q, kv, kv, kv, k_ctx, v_ctx)


def _postmix0_kernel(zc_ref, zp_ref, zn_ref, at_ref, x_ref, mod_ref, g_ref, cw_ref, w_ref,
                     o_ref, *, n, tm):
    c = CONV_DIM
    zc = zc_ref[...]
    u = zc[:, c:2 * c] * zc[:, 2 * c:]
    u_before = zp_ref[7:8, c:2 * c] * zp_ref[7:8, 2 * c:]
    u_after = zn_ref[0:1, c:2 * c] * zn_ref[0:1, 2 * c:]
    row = lax.broadcasted_iota(jnp.int32, (tm, 1), 0)
    pos = (pl.program_id(0) * tm + row) % n
    u_prev = jnp.where(row == 0, u_before, pltpu.roll(u, 1, axis=0))
    u_prev = jnp.where(pos == 0, 0.0, u_prev)
    u_next = jnp.where(row == tm - 1, u_after, pltpu.roll(u, tm - 1, axis=0))
    u_next = jnp.where(pos == n - 1, 0.0, u_next)
    conv = u_prev * cw_ref[0:1, :] + u * cw_ref[1:2, :] + u_next * cw_ref[2:3, :]
    mix = _bdot(zc[:, :c] * conv, w_ref[:c, :]) + _bdot(at_ref[...], w_ref[c:, :])
    gt1 = _mod_chunk(mod_ref, 2)
    o_ref[...] = x_ref[...] + gt1 * _rms(mix, g_ref[1:2, :])


def _postmix0(grp, zc, attn, x2d, mod, norm_g, conv_w, w_out, tm):
    r8 = tm // 8
    last8 = grp.t // 8 - 1
    return pl.pallas_call(
        functools.partial(_postmix0_kernel, n=grp.n, tm=tm),
        grid=(grp.t // tm,),
        in_specs=[_row_spec(tm, 3 * CONV_DIM),
                  pl.BlockSpec((8, 3 * CONV_DIM), lambda t: (jnp.maximum(t * r8 - 1, 0), 0)),
                  pl.BlockSpec((8, 3 * CONV_DIM), lambda t: (jnp.minimum((t + 1) * r8, last8), 0)),
                  _row_spec(tm, QD), _row_spec(tm, D_MODEL), grp.mod_spec(tm),
                  _const_spec((4, D_MODEL)), _const_spec((3, CONV_DIM)),
                  _const_spec((CONV_DIM + QD, D_MODEL))],
        out_specs=_row_spec(tm, D_MODEL),
        out_shape=jax.ShapeDtypeStruct((grp.t, D_MODEL), F32),
        compiler_params=_cparams("arbitrary"),
        name="postmix0",
    )(zc, zc, zc, attn, x2d, mod, norm_g, conv_w, w_out)


def _mlp_kernel(x_ref, mod_ref, g_ref, w1_ref, w2_ref, o_ref, *, fchunk):
    x = x_ref[...]
    sh2, sc2, gt2 = _mod_chunk(mod_ref, 3), _mod_chunk(mod_ref, 4), _mod_chunk(mod_ref, 5)
    hb = (_rms(x, g_ref[2:3, :]) * (1.0 + sc2) + sh2).astype(BF16)
    acc = None
    for j in range(D_FF // fchunk):
        a = jnp.dot(hb, w1_ref[:, j * fchunk:(j + 1) * fchunk], preferred_element_type=F32)
        a = jnp.maximum(a, 0.0)
        part = _bdot(a * a, w2_ref[j * fchunk:(j + 1) * fchunk, :])
        acc = part if acc is None else acc + part
    o_ref[...] = x + gt2 * _rms(acc, g_ref[3:4, :])


def _mlp(grp, x2d, mod, norm_g, w1, w2, tm):
    single = pl.Buffered(1)
    return pl.pallas_call(
        functools.partial(_mlp_kernel, fchunk=1024),
        grid=(grp.t // tm,),
        in_specs=[_row_spec(tm, D_MODEL), grp.mod_spec(tm), _const_spec((4, D_MODEL)),
                  pl.BlockSpec((D_MODEL, D_FF), lambda t: (0, 0), pipeline_mode=single),
                  pl.BlockSpec((D_FF, D_MODEL), lambda t: (0, 0), pipeline_mode=single)],
        out_specs=_row_spec(tm, D_MODEL),
        out_shape=jax.ShapeDtypeStruct((grp.t, D_MODEL), F32),
        compiler_params=_cparams("arbitrary"),
        name="mlp",
    )(x2d, mod, norm_g, w1, w2)


def _premix1_kernel(x_ref, mod_ref, g_ref, w_ref, wvt_ref, wr_ref, wg_ref, gb_ref,
                    q_ref, k_ref, v_ref, vt_ref, og_ref, gf_ref, gbk_ref):
    sh1, sc1 = _mod_chunk(mod_ref, 0), _mod_chunk(mod_ref, 1)
    hb = (_rms(x_ref[...], g_ref[0:1, :]) * (1.0 + sc1) + sh1).astype(BF16)
    z = jnp.dot(hb, w_ref[...], preferred_element_type=F32)
    q_ref[...] = z[:, :GLA_QK] * (GLA_DK ** -0.5)
    k_ref[...] = z[:, GLA_QK:2 * GLA_QK]
    v_ref[...] = z[:, 2 * GLA_QK:2 * GLA_QK + GLA_VD].astype(BF16)
    og_ref[...] = z[:, 2 * GLA_QK + GLA_VD:]
    vt_ref[...] = lax.dot_general(wvt_ref[...], hb, NT_DIMS,
                                  preferred_element_type=F32).astype(BF16)
    r = jnp.dot(hb, wr_ref[...], preferred_element_type=F32)
    pre = _bdot(r, wg_ref[...]) + gb_ref[...]
    gate = (jnp.minimum(pre, 0.0) - jnp.log1p(jnp.exp(-jnp.abs(pre)))) / GLA_GATE_NORM
    gf_ref[...] = gate[:, :GLA_QK]
    gbk_ref[...] = gate[:, GLA_QK:]


def _premix1(grp, x2d, mod, norm_g, w_main, w_vt, w_r, w_gate, gate_bias, tm):
    t = grp.t
    return pl.pallas_call(
        _premix1_kernel,
        grid=(t // tm,),
        in_specs=[_row_spec(tm, D_MODEL), grp.mod_spec(tm), _const_spec((4, D_MODEL)),
                  _const_spec((D_MODEL, GLA_MAIN)), _const_spec((GLA_VD, D_MODEL)),
                  _const_spec((D_MODEL, LANES)), _const_spec((LANES, 2 * GLA_QK)),
                  _const_spec((1, 2 * GLA_QK))],
        out_specs=[_row_spec(tm, GLA_QK), _row_spec(tm, GLA_QK), _row_spec(tm, GLA_VD),
                   pl.BlockSpec((GLA_VD, tm), lambda i: (0, i)),
                   _row_spec(tm, GLA_VD), _row_spec(tm, GLA_QK), _row_spec(tm, GLA_QK)],
        out_shape=[jax.ShapeDtypeStruct((t, GLA_QK), F32), jax.ShapeDtypeStruct((t, GLA_QK), F32),
                   jax.ShapeDtypeStruct((t, GLA_VD), BF16), jax.ShapeDtypeStruct((GLA_VD, t), BF16),
                   jax.ShapeDtypeStruct((t, GLA_VD), F32), jax.ShapeDtypeStruct((t, GLA_QK), F32),
                   jax.ShapeDtypeStruct((t, GLA_QK), F32)],
        compiler_params=_cparams("arbitrary"),
        name="premix1",
    )(x2d, mod, norm_g, w_main, w_vt, w_r, w_gate, gate_bias)


def _gla_direction(q, k, g, v_ref, vt_ref, s_ref, sel_ref, o_ref, reverse):
    tg = q.shape[0]
    nc, ns = tg // GLA_CHUNK, tg // GLA_SUB
    per = GLA_CHUNK // GLA_SUB
    rc = lax.broadcasted_iota(jnp.int32, (tg, 1), 0) % GLA_CHUNK
    b = g
    s = 1
    while s < GLA_CHUNK:
        if reverse:
            b = b + jnp.where(rc < GLA_CHUNK - s, pltpu.roll(b, tg - s, axis=0), 0.0)
        else:
            b = b + jnp.where(rc >= s, pltpu.roll(b, s, axis=0), 0.0)
        s *= 2
    edge = 0 if reverse else GLA_CHUNK - 1
    b3 = b.reshape(nc, GLA_CHUNK, GLA_DK)
    b_last = b3[:, edge:edge + 1, :]
    qe = q * jnp.exp(b)
    ke = (k.reshape(nc, GLA_CHUNK, GLA_DK) * jnp.exp(b_last - b3)).reshape(tg, GLA_DK)

    first = GLA_SUB - 1 if reverse else 0
    b4 = b.reshape(ns, GLA_SUB, GLA_DK)
    q4 = q.reshape(ns, GLA_SUB, GLA_DK)
    k4 = k.reshape(ns, GLA_SUB, GLA_DK)
    bound = (b - g).reshape(ns, GLA_SUB, GLA_DK)[:, first:first + 1, :]
    qx = (q4 * jnp.exp(b4 - bound)).reshape(tg, GLA_DK)

    diag = jnp.zeros((tg, LANES), F32)
    for jl in range(GLA_SUB):
        pair = jnp.exp(jnp.minimum(b4 - b4[:, jl:jl + 1, :], 0.0))
        e = (q4 * k4[:, jl:jl + 1, :] * pair).reshape(tg, GLA_DK)
        diag = diag + _bdot(e, sel_ref[jl])

    ri = lax.broadcasted_iota(jnp.int32, (GLA_CHUNK, GLA_CHUNK), 0)
    cj = lax.broadcasted_iota(jnp.int32, (GLA_CHUNK, GLA_CHUNK), 1)
    si, sj = ri // GLA_SUB, cj // GLA_SUB
    if reverse:
        off_mask = sj > si
        diag_mask = (sj == si) & (cj >= ri)
    else:
        off_mask = sj < si
        diag_mask = (sj == si) & (cj <= ri)

    order = range(nc - 1, -1, -1) if reverse else range(nc)
    for ci in order:
        r0 = ci * GLA_CHUNK
        kc, bcc = k[r0:r0 + GLA_CHUNK], b[r0:r0 + GLA_CHUNK]
        offs = []
        for ii in range(per):
            if ii == (per - 1 if reverse else 0):
                offs.append(jnp.zeros((GLA_SUB, GLA_CHUNK), F32))
                continue
            bnd = bound[ci * per + ii]
            kx = kc * jnp.exp(jnp.minimum(bnd - bcc, 0.0))
            qs = qx[r0 + ii * GLA_SUB:r0 + (ii + 1) * GLA_SUB]
            offs.append(_bdot_nt(qs, kx))
        off = jnp.concatenate(offs, axis=0)
        att = jnp.where(off_mask, off,
                        jnp.where(diag_mask, diag[r0:r0 + GLA_CHUNK, :GLA_CHUNK], 0.0))
        st = s_ref[...]
        o = _bdot_nt(qe[r0:r0 + GLA_CHUNK], st) + _bdot(att, v_ref[r0:r0 + GLA_CHUNK, :])
        o_ref[r0:r0 + GLA_CHUNK, :] = o
        upd = jnp.dot(vt_ref[:, r0:r0 + GLA_CHUNK], ke[r0:r0 + GLA_CHUNK].astype(BF16),
                      preferred_element_type=F32)
        s_ref[...] = st * jnp.exp(b_last[ci]) + upd


def _gla_kernel(*refs, zero_init):
    if zero_init:
        (sel_ref, qf, kf, gf, vf, vtf, qb, kb, gb, vb, vtb,
         of_ref, ob_ref, sfo_ref, sbo_ref, sf_ref, sb_ref) = refs
    else:
        (sel_ref, qf, kf, gf, vf, vtf, qb, kb, gb, vb, vtb, s0f_ref, s0b_ref,
         of_ref, ob_ref, sfo_ref, sbo_ref, sf_ref, sb_ref) = refs
    t = pl.program_id(2)

    @pl.when(t == 0)
    def _():
        if zero_init:
            sf_ref[...] = jnp.zeros_like(sf_ref)
            sb_ref[...] = jnp.zeros_like(sb_ref)
        else:
            sf_ref[...] = s0f_ref[...]
            sb_ref[...] = s0b_ref[...]

    _gla_direction(qf[...], kf[...], gf[...], vf, vtf, sf_ref, sel_ref, of_ref, False)
    _gla_direction(qb[...], kb[...], gb[...], vb, vtb, sb_ref, sel_ref, ob_ref, True)

    @pl.when(t == pl.num_programs(2) - 1)
    def _():
        sfo_ref[...] = sf_ref[...]
        sbo_ref[...] = sb_ref[...]


def _gla_scan(grp, q, k, v, vt, gf, gb, s0f_t, s0b_t, sel, tg):
    nt = grp.n // tg
    zero_init = s0f_t is None

    def fwd(b, h, t):
        return b * nt + t

    def bwd(b, h, t):
        return b * nt + nt - 1 - t

    def tile_specs(tile):
        return [pl.BlockSpec((tg, GLA_DK), lambda b, h, t: (tile(b, h, t), h)),
                pl.BlockSpec((tg, GLA_DK), lambda b, h, t: (tile(b, h, t), h)),
                pl.BlockSpec((tg, GLA_DK), lambda b, h, t: (tile(b, h, t), h)),
                pl.BlockSpec((tg, GLA_DV), lambda b, h, t: (tile(b, h, t), h)),
                pl.BlockSpec((GLA_DV, tg), lambda b, h, t: (h, tile(b, h, t)))]

    state_spec = pl.BlockSpec((None, None, GLA_DV, GLA_DK), lambda b, h, t: (b, h, 0, 0))
    in_specs = [_const_spec((GLA_SUB, GLA_DK, LANES))] + tile_specs(fwd) + tile_specs(bwd)
    args = [sel, q, k, gf, v, vt, q, k, gb, v, vt]
    if not zero_init:
        in_specs += [state_spec, state_spec]
        args += [s0f_t, s0b_t]
    state_shape = jax.ShapeDtypeStruct((grp.b, GLA_HEADS, GLA_DV, GLA_DK), F32)
    return pl.pallas_call(
        functools.partial(_gla_kernel, zero_init=zero_init),
        grid=(grp.b, GLA_HEADS, nt),
        in_specs=in_specs,
        out_specs=[pl.BlockSpec((tg, GLA_DV), lambda b, h, t: (fwd(b, h, t), h)),
                   pl.BlockSpec((tg, GLA_DV), lambda b, h, t: (bwd(b, h, t), h)),
                   state_spec, state_spec],
        out_shape=[jax.ShapeDtypeStruct((grp.t, GLA_VD), F32),
                   jax.ShapeDtypeStruct((grp.t, GLA_VD), F32), state_shape, state_shape],
        scratch_shapes=[pltpu.VMEM((GLA_DV, GLA_DK), F32), pltpu.VMEM((GLA_DV, GLA_DK), F32)],
        compiler_params=_cparams("arbitrary", "arbitrary", "arbitrary"),
        name="gla_scan",
    )(*args)


def _sel_matrices():
    d = jnp.arange(LANES)[None, None, :]
    j = jnp.arange(GLA_SUB)[:, None, None]
    hit = jnp.logical_and(d % GLA_SUB == j, d < GLA_CHUNK)
    return jnp.broadcast_to(hit, (GLA_SUB, GLA_DK, LANES)).astype(BF16)


def _postmix1_kernel(of_ref, ob_ref, og_ref, x_ref, mod_ref, g_ref, gn_ref, w_ref, o_ref):
    gn = gn_ref[...]
    mix = None
    for h in range(GLA_HEADS):
        cols = slice(h * GLA_DV, (h + 1) * GLA_DV)
        o = _rms(of_ref[:, cols] + ob_ref[:, cols], gn)
        og = og_ref[:, cols]
        y = o * (og / (1.0 + jnp.exp(-og)))
        part = _bdot(y, w_ref[cols, :])
        mix = part if mix is None else mix + part
    gt1 = _mod_chunk(mod_ref, 2)
    o_ref[...] = x_ref[...] + gt1 * _rms(mix, g_ref[1:2, :])


def _postmix1(grp, o_f, o_b, og, x2d, mod, norm_g, gla_norm_g, w_out, tm):
    return pl.pallas_call(
        _postmix1_kernel,
        grid=(grp.t // tm,),
        in_specs=[_row_spec(tm, GLA_VD), _row_spec(tm, GLA_VD), _row_spec(tm, GLA_VD),
                  _row_spec(tm, D_MODEL), grp.mod_spec(tm), _const_spec((4, D_MODEL)),
                  _const_spec((1, GLA_DV)), _const_spec((GLA_VD, D_MODEL))],
        out_specs=_row_spec(tm, D_MODEL),
        out_shape=jax.ShapeDtypeStruct((grp.t, D_MODEL), F32),
        compiler_params=_cparams("arbitrary"),
        name="postmix1",
    )(o_f, o_b, og, x2d, mod, norm_g, gla_norm_g, w_out)


def _run_group(grp, x, mods, p, rope_tabs, k_ctx, v_ctx, s0f_t, s0b_t, tm, tg):
    x2d = x.reshape(grp.t, D_MODEL)
    zc, q, kv = _premix0(grp, x2d, mods[0], p["norm_g"][0], p["ab_w_in"], rope_tabs, tm)
    if k_ctx is None:
        attn = _ctx_attention(grp, q, kv, p["sink"])
    else:
        attn = _lat_attention(grp, q, kv, k_ctx, v_ctx, p["sink"])
    x2d = _postmix0(grp, zc, attn, x2d, mods[0], p["norm_g"][0], p["conv_w"], p["ab_w_out"], tm)
    x2d = _mlp(grp, x2d, mods[0], p["norm_g"][0], p["mlp_w1"][0], p["mlp_w2"][0], tm)
    gq, gk, gv, gvt, og, gf, gb = _premix1(grp, x2d, mods[1], p["norm_g"][1], p["gla_w_main"],
                                           p["gla_w_vt"], p["gla_w_r"], p["gla_w_gate"],
                                           p["gla_gate_bias"], tm)
    o_f, o_b, sf_t, sb_t = _gla_scan(grp, gq, gk, gv, gvt, gf, gb, s0f_t, s0b_t, p["sel"], tg)
    x2d = _postmix1(grp, o_f, o_b, og, x2d, mods[1], p["norm_g"][1], p["gla_norm_g"],
                    p["gla_w_out"], tm)
    x2d = _mlp(grp, x2d, mods[1], p["norm_g"][1], p["mlp_w1"][1], p["mlp_w2"][1], tm)
    return x2d.reshape(x.shape), kv, sf_t, sb_t


def kernel(x_prompt, x_sample, cache_k, cache_v, state_fwd, state_bwd, c, c_ctx, mod_w, mod_b,
           norm_g, ab_w_in, conv_w, attn_sink, ab_w_out, gla_w_in, gla_gate_w, gla_gate_b,
           gla_norm_g, gla_w_out, mlp_w1, mlp_w2):
    b_ctx, n_ctx, _ = x_prompt.shape
    b_lat, n_lat, _ = x_sample.shape
    assert mod_w.shape[0] == 2 and ab_w_in.shape[0] == 1 and gla_w_in.shape[0] == 1
    assert 1 + b_lat <= 8

    cond8 = jnp.zeros((8, D_MODEL), F32).at[0].set(c_ctx).at[1:1 + b_lat].set(c)
    mod = _modulation(cond8, mod_w, mod_b)
    mods_ctx = [mod[l, 0:1].reshape(1, 1, -1) for l in range(2)]
    mods_lat = [mod[l, 1:1 + b_lat].reshape(b_lat, 1, -1) for l in range(2)]

    gw = gla_w_in[0]
    v0 = 2 * GLA_QK
    w_gate = jnp.zeros((LANES, 2 * GLA_QK), F32)
    w_gate = w_gate.at[:GLA_RANK, :GLA_QK].set(gla_gate_w[0, 0])
    w_gate = w_gate.at[GLA_RANK:2 * GLA_RANK, GLA_QK:].set(gla_gate_w[0, 1])
    p = {
        "norm_g": norm_g,
        "ab_w_in": ab_w_in[0].astype(BF16),
        "conv_w": conv_w[0],
        "sink": attn_sink[0],
        "ab_w_out": ab_w_out[0].astype(BF16),
        "mlp_w1": mlp_w1.astype(BF16),
        "mlp_w2": mlp_w2.astype(BF16),
        "gla_w_main": gw[:, :GLA_MAIN].astype(BF16),
        "gla_w_vt": gw[:, v0:v0 + GLA_VD].T.astype(BF16),
        "gla_w_r": jnp.pad(gw[:, GLA_MAIN:], ((0, 0), (0, LANES - 2 * GLA_RANK))).astype(BF16),
        "gla_w_gate": w_gate.astype(BF16),
        "gla_gate_bias": gla_gate_b[0].reshape(1, 2 * GLA_QK),
        "gla_norm_g": gla_norm_g[0].reshape(1, GLA_DV),
        "gla_w_out": gla_w_out[0].astype(BF16),
        "sel": _sel_matrices(),
    }

    ctx = _Group(b_ctx, n_ctx, per_seq_mod=False)
    lat = _Group(b_lat, n_lat, per_seq_mod=True)

    y_prompt, kv_ctx, sf_t, sb_t = _run_group(ctx, x_prompt, mods_ctx, p, None, None, None,
                                              None, None, tm=512, tg=n_ctx)
    new_k = kv_ctx[:, :KD].reshape(b_ctx, 1, n_ctx, N_KV_HEADS, HEAD_DIM)
    new_v = kv_ctx[:, KD:].reshape(b_ctx, 1, n_ctx, N_KV_HEADS, HEAD_DIM)
    new_sf = jnp.swapaxes(sf_t, -1, -2)[:, None]
    new_sb = jnp.swapaxes(sb_t, -1, -2)[:, None]

    past = cache_k.shape[2]
    k_ctx = cache_k[:, 0].reshape(b_lat, past, KD)
    v_ctx = cache_v[:, 0].reshape(b_lat, past, KD)
    s0f_t = jnp.swapaxes(state_fwd[:, 0], -1, -2)
    s0b_t = jnp.swapaxes(state_bwd[:, 0], -1, -2)
    y_sample, _, _, _ = _run_group(lat, x_sample, mods_lat, p, _rope_tables(n_lat), k_ctx, v_ctx,
                                   s0f_t, s0b_t, tm=512, tg=256)
    return (y_prompt, y_sample, new_k, new_v, new_sf, new_sb)
```

```python
import functools

import jax
import jax.numpy as jnp
from jax import lax
from jax.experimental import pallas as pl
from jax.experimental.pallas import tpu as pltpu

F32 = jnp.float32
BF16 = jnp.bfloat16

D_MODEL = 1024
MOD_CHUNKS = 6
EPS = 1e-6
CONV_DIM = 512
N_Q_HEADS = 8
N_KV_HEADS = 2
GQA_GROUP = 4
HEAD_DIM = 64
WINDOW = 128
ATTN_BLOCK = 128
GRID_W = 64
ROPE_BASE = 10000.0
QD = N_Q_HEADS * HEAD_DIM
KD = N_KV_HEADS * HEAD_DIM
AB_IN = 3 * CONV_DIM + QD + 2 * KD
GLA_HEADS = 4
GLA_DK = 128
GLA_DV = 256
GLA_RANK = 16
GLA_GATE_NORM = 16.0
GLA_TILE = 256
LOG2E = 1.4426950408889634
GLA_QK = GLA_HEADS * GLA_DK
GLA_VD = GLA_HEADS * GLA_DV
GLA_MAIN = 2 * GLA_QK + 2 * GLA_VD
D_FF = 4 * D_MODEL
NEG_INF = -1e30
LANES = 128
VMEM_LIMIT = 56 * 1024 * 1024

NT_DIMS = (((1,), (1,)), ((), ()))


def _cparams(*sem):
    return pltpu.CompilerParams(dimension_semantics=sem, vmem_limit_bytes=VMEM_LIMIT)


def _bdot(a, b):
    return jnp.dot(a.astype(BF16), b.astype(BF16), preferred_element_type=F32)


def _bdot_nt(a, b):
    return lax.dot_general(a.astype(BF16), b.astype(BF16), NT_DIMS, preferred_element_type=F32)


def _rms(x, g):
    ms = jnp.mean(x * x, axis=-1, keepdims=True)
    return x * lax.rsqrt(ms + EPS) * g


def _mod_chunk(mod_ref, i):
    return mod_ref[:, i * D_MODEL:(i + 1) * D_MODEL]


def _const_spec(shape):
    return pl.BlockSpec(shape, lambda *_: (0,) * len(shape))


def _mod_kernel(cond_ref, w_ref, b_ref, o_ref):
    cnd = cond_ref[...]
    s = cnd / (1.0 + jnp.exp(-cnd))
    o_ref[...] = _bdot(s, w_ref[...]) + b_ref[...]


def _modulation(cond8, mod_w, mod_b):
    depth = mod_w.shape[0]
    n = mod_w.shape[2]
    tn = 1536
    return pl.pallas_call(
        _mod_kernel,
        grid=(depth, n // tn),
        in_specs=[
            pl.BlockSpec((8, D_MODEL), lambda l, j: (0, 0)),
            pl.BlockSpec((None, D_MODEL, tn), lambda l, j: (l, 0, j)),
            pl.BlockSpec((None, 1, tn), lambda l, j: (l, 0, j)),
        ],
        out_specs=pl.BlockSpec((None, 8, tn), lambda l, j: (l, 0, j)),
        out_shape=jax.ShapeDtypeStruct((depth, 8, n), F32),
        compiler_params=_cparams("arbitrary", "arbitrary"),
        name="modulation",
    )(cond8, mod_w, mod_b.reshape(depth, 1, n))


class _Group:
    def __init__(self, b, n, per_seq_mod):
        self.b, self.n, self.t = b, n, b * n
        self.per_seq_mod = per_seq_mod

    def mod_spec(self, tm):
        if self.per_seq_mod:
            assert self.n % tm == 0
            per = self.n // tm
            return pl.BlockSpec((None, 1, MOD_CHUNKS * D_MODEL), lambda t: (t // per, 0, 0))
        return pl.BlockSpec((None, 1, MOD_CHUNKS * D_MODEL), lambda t: (0, 0, 0))


def _row_spec(tm, width):
    return pl.BlockSpec((tm, width), lambda t: (t, 0))


def _rope(x, cos, sin_lo, sin_hi):
    return (x * cos + pltpu.roll(x, LANES - 16, axis=1) * sin_lo
            + pltpu.roll(x, 16, axis=1) * sin_hi)


def _premix0_kernel(*refs, rope):
    if rope:
        x_ref, mod_ref, g_ref, w_ref, cos_ref, slo_ref, shi_ref, zc_ref, q_ref, kv_ref = refs
    else:
        x_ref, mod_ref, g_ref, w_ref, zc_ref, q_ref, kv_ref = refs
    sh1, sc1 = _mod_chunk(mod_ref, 0), _mod_chunk(mod_ref, 1)
    h = _rms(x_ref[...], g_ref[0:1, :]) * (1.0 + sc1) + sh1
    z = _bdot(h, w_ref[...])
    c3 = 3 * CONV_DIM
    zc_ref[...] = z[:, :c3]
    scale = HEAD_DIM ** -0.5
    if rope:
        cos, slo, shi = cos_ref[...], slo_ref[...], shi_ref[...]
        for j in range(QD // LANES):
            qs = z[:, c3 + j * LANES:c3 + (j + 1) * LANES]
            q_ref[:, j * LANES:(j + 1) * LANES] = (_rope(qs, cos, slo, shi) * scale).astype(BF16)
        kv_ref[:, :KD] = _rope(z[:, c3 + QD:c3 + QD + KD], cos, slo, shi)
    else:
        q_ref[...] = (z[:, c3:c3 + QD] * scale).astype(BF16)
        kv_ref[:, :KD] = z[:, c3 + QD:c3 + QD + KD]
    kv_ref[:, KD:] = z[:, c3 + QD + KD:]


def _premix0(grp, x2d, mod, norm_g, w_in, rope_tabs, tm):
    rope = rope_tabs is not None
    in_specs = [_row_spec(tm, D_MODEL), grp.mod_spec(tm), _const_spec((4, D_MODEL)),
                _const_spec((D_MODEL, AB_IN))]
    args = [x2d, mod, norm_g, w_in]
    if rope:
        per = grp.n // tm
        in_specs += [pl.BlockSpec((tm, LANES), lambda t: (t % per, 0))] * 3
        args += list(rope_tabs)
    return pl.pallas_call(
        functools.partial(_premix0_kernel, rope=rope),
        grid=(grp.t // tm,),
        in_specs=in_specs,
        out_specs=[_row_spec(tm, 3 * CONV_DIM), _row_spec(tm, QD), _row_spec(tm, 2 * KD)],
        out_shape=[jax.ShapeDtypeStruct((grp.t, 3 * CONV_DIM), F32),
                   jax.ShapeDtypeStruct((grp.t, QD), BF16),
                   jax.ShapeDtypeStruct((grp.t, 2 * KD), F32)],
        compiler_params=_cparams("arbitrary"),
        name="premix0_rope" if rope else "premix0",
    )(*args)


def _rope_tables(n):
    rows = n // GRID_W
    pos_r = jnp.repeat(jnp.arange(rows), GRID_W)
    pos_c = jnp.tile(jnp.arange(GRID_W), rows)
    half = HEAD_DIM // 2
    quarter = half // 2
    inv = ROPE_BASE ** (-(jnp.arange(quarter, dtype=F32) * 2.0 / half))

    def cs(pos):
        ang = pos.astype(F32)[:, None] * inv[None, :]
        return jnp.cos(ang), jnp.sin(ang)

    cr, sr = cs(pos_r)
    cc, sc = cs(pos_c)
    zero = jnp.zeros_like(sr)
    cos = jnp.concatenate([cr, cr, cc, cc], axis=1)
    sin_lo = jnp.concatenate([-sr, zero, -sc, zero], axis=1)
    sin_hi = jnp.concatenate([zero, sr, zero, sc], axis=1)
    rep = LANES // HEAD_DIM
    return tuple(jnp.tile(t, (1, rep)) for t in (cos, sin_lo, sin_hi))


def _sink_attend(s, sink, vs):
    m = sink
    for si in s:
        m = jnp.maximum(m, jnp.max(si, axis=-1, keepdims=True))
    den = jnp.exp(sink - m)
    o = None
    for si, vi in zip(s, vs):
        p = jnp.exp(si - m)
        den = den + jnp.sum(p, axis=-1, keepdims=True)
        oi = _bdot(p, vi)
        o = oi if o is None else o + oi
    return o / den


def _ctx_attn_kernel(sink_ref, q_ref, kv_ref, o_ref):
    for g in range(N_KV_HEADS):
        k = kv_ref[:, g * HEAD_DIM:(g + 1) * HEAD_DIM].astype(BF16)
        v = kv_ref[:, KD + g * HEAD_DIM:KD + (g + 1) * HEAD_DIM].astype(BF16)
        for hh in range(GQA_GROUP):
            h = g * GQA_GROUP + hh
            q = q_ref[:, h * HEAD_DIM:(h + 1) * HEAD_DIM]
            s = _bdot_nt(q, k)
            o = _sink_attend([s], sink_ref[h], [v])
            o_ref[:, h * HEAD_DIM:(h + 1) * HEAD_DIM] = o.astype(BF16)


def _ctx_attention(grp, q, kv, sink):
    n = grp.n
    return pl.pallas_call(
        _ctx_attn_kernel,
        grid=(grp.b,),
        in_specs=[pl.BlockSpec(memory_space=pltpu.SMEM), _row_spec(n, QD), _row_spec(n, 2 * KD)],
        out_specs=_row_spec(n, QD),
        out_shape=jax.ShapeDtypeStruct((grp.t, QD), BF16),
        compiler_params=_cparams("arbitrary"),
        name="ctx_attention",
    )(sink, q, kv)


def _lat_attn_kernel(sink_ref, q_ref, kvp_ref, kvc_ref, kvn_ref, kc_ref, vc_ref, o_ref, *, n):
    blk = pl.program_id(1)
    kv = jnp.concatenate([kvp_ref[...], kvc_ref[...], kvn_ref[...]], axis=0)
    qi = lax.broadcasted_iota(jnp.int32, (ATTN_BLOCK, 3 * ATTN_BLOCK), 0)
    kj = lax.broadcasted_iota(jnp.int32, (ATTN_BLOCK, 3 * ATTN_BLOCK), 1)
    rel = kj - ATTN_BLOCK - qi
    kpos = (blk - 1) * ATTN_BLOCK + kj
    valid = (jnp.abs(rel) <= WINDOW) & (kpos >= 0) & (kpos < n)
    for g in range(N_KV_HEADS):
        k = kv[:, g * HEAD_DIM:(g + 1) * HEAD_DIM].astype(BF16)
        v = kv[:, KD + g * HEAD_DIM:KD + (g + 1) * HEAD_DIM].astype(BF16)
        kc = kc_ref[:, g * HEAD_DIM:(g + 1) * HEAD_DIM].astype(BF16)
        vc = vc_ref[:, g * HEAD_DIM:(g + 1) * HEAD_DIM].astype(BF16)
        for hh in range(GQA_GROUP):
            h = g * GQA_GROUP + hh
            q = q_ref[:, h * HEAD_DIM:(h + 1) * HEAD_DIM]
            s_loc = jnp.where(valid, _bdot_nt(q, k), NEG_INF)
            s_ctx = _bdot_nt(q, kc)
            o = _sink_attend([s_loc, s_ctx], sink_ref[h], [v, vc])
            o_ref[:, h * HEAD_DIM:(h + 1) * HEAD_DIM] = o.astype(BF16)


def _lat_attention(grp, q, kv, k_ctx, v_ctx, sink):
    nb = grp.n // ATTN_BLOCK
    past = k_ctx.shape[1]

    def band(off):
        return pl.BlockSpec((ATTN_BLOCK, 2 * KD),
                            lambda b, i: (b * nb + jnp.clip(i + off, 0, nb - 1), 0))

    ctx_spec = pl.BlockSpec((None, past, KD), lambda b, i: (b, 0, 0))
    return pl.pallas_call(
        functools.partial(_lat_attn_kernel, n=grp.n),
        grid=(grp.b, nb),
        in_specs=[pl.BlockSpec(memory_space=pltpu.SMEM),
                  pl.BlockSpec((ATTN_BLOCK, QD), lambda b, i: (b * nb + i, 0)),
                  band(-1), band(0), band(1), ctx_spec, ctx_spec],
        out_specs=pl.BlockSpec((ATTN_BLOCK, QD), lambda b, i: (b * nb + i, 0)),
        out_shape=jax.ShapeDtypeStruct((grp.t, QD), BF16),
        compiler_params=_cparams("arbitrary", "arbitrary"),
        name="lat_attention",
    )(sink, q, kv, kv, kv, k_ctx, v_ctx)


def _postmix0_kernel(zc_ref, zp_ref, zn_ref, at_ref, x_ref, mod_ref, g_ref, cw_ref, w_ref,
                     o_ref, *, n, tm):
    c = CONV_DIM
    zc = zc_ref[...]
    u = zc[:, c:2 * c] * zc[:, 2 * c:]
    u_before = zp_ref[7:8, c:2 * c] * zp_ref[7:8, 2 * c:]
    u_after = zn_ref[0:1, c:2 * c] * zn_ref[0:1, 2 * c:]
    row = lax.broadcasted_iota(jnp.int32, (tm, 1), 0)
    pos = (pl.program_id(0) * tm + row) % n
    u_prev = jnp.where(row == 0, u_before, pltpu.roll(u, 1, axis=0))
    u_prev = jnp.where(pos == 0, 0.0, u_prev)
    u_next = jnp.where(row == tm - 1, u_after, pltpu.roll(u, tm - 1, axis=0))
    u_next = jnp.where(pos == n - 1, 0.0, u_next)
    conv = u_prev * cw_ref[0:1, :] + u * cw_ref[1:2, :] + u_next * cw_ref[2:3, :]
    mix = _bdot(zc[:, :c] * conv, w_ref[:c, :]) + _bdot(at_ref[...], w_ref[c:, :])
    gt1 = _mod_chunk(mod_ref, 2)
    o_ref[...] = x_ref[...] + gt1 * _rms(mix, g_ref[1:2, :])


def _postmix0(grp, zc, attn, x2d, mod, norm_g, conv_w, w_out, tm):
    r8 = tm // 8
    last8 = grp.t // 8 - 1
    return pl.pallas_call(
        functools.partial(_postmix0_kernel, n=grp.n, tm=tm),
        grid=(grp.t // tm,),
        in_specs=[_row_spec(tm, 3 * CONV_DIM),
                  pl.BlockSpec((8, 3 * CONV_DIM), lambda t: (jnp.maximum(t * r8 - 1, 0), 0)),
                  pl.BlockSpec((8, 3 * CONV_DIM), lambda t: (jnp.minimum((t + 1) * r8, last8), 0)),
                  _row_spec(tm, QD), _row_spec(tm, D_MODEL), grp.mod_spec(tm),
                  _const_spec((4, D_MODEL)), _const_spec((3, CONV_DIM)),
                  _const_spec((CONV_DIM + QD, D_MODEL))],
        out_specs=_row_spec(tm, D_MODEL),
        out_shape=jax.ShapeDtypeStruct((grp.t, D_MODEL), F32),
        compiler_params=_cparams("arbitrary"),
        name="postmix0",
    )(zc, zc, zc, attn, x2d, mod, norm_g, conv_w, w_out)


def _mlp_kernel(x_ref, mod_ref, g_ref, w1_ref, w2_ref, o_ref, *, fchunk):
    x = x_ref[...]
    sh2, sc2, gt2 = _mod_chunk(mod_ref, 3), _mod_chunk(mod_ref, 4), _mod_chunk(mod_ref, 5)
    hb = (_rms(x, g_ref[2:3, :]) * (1.0 + sc2) + sh2).astype(BF16)
    acc = None
    for j in range(D_FF // fchunk):
        a = jnp.dot(hb, w1_ref[:, j * fchunk:(j + 1) * fchunk], preferred_element_type=F32)
        a = jnp.maximum(a, 0.0)
        part = _bdot(a * a, w2_ref[j * fchunk:(j + 1) * fchunk, :])
        acc = part if acc is None else acc + part
    o_ref[...] = x + gt2 * _rms(acc, g_ref[3:4, :])


def _mlp(grp, x2d, mod, norm_g, w1, w2, tm):
    single = pl.Buffered(1)
    return pl.pallas_call(
        functools.partial(_mlp_kernel, fchunk=1024),
        grid=(grp.t // tm,),
        in_specs=[_row_spec(tm, D_MODEL), grp.mod_spec(tm), _const_spec((4, D_MODEL)),
                  pl.BlockSpec((D_MODEL, D_FF), lambda t: (0, 0), pipeline_mode=single),
                  pl.BlockSpec((D_FF, D_MODEL), lambda t: (0, 0), pipeline_mode=single)],
        out_specs=_row_spec(tm, D_MODEL),
        out_shape=jax.ShapeDtypeStruct((grp.t, D_MODEL), F32),
        compiler_params=_cparams("arbitrary"),
        name="mlp",
    )(x2d, mod, norm_g, w1, w2)


def _premix1_kernel(x_ref, mod_ref, g_ref, w_ref, wvt_ref, wr_ref, wg_ref, gb_ref,
                    q_ref, k_ref, v_ref, vt_ref, og_ref, gf_ref, gbk_ref):
    sh1, sc1 = _mod_chunk(mod_ref, 0), _mod_chunk(mod_ref, 1)
    hb = (_rms(x_ref[...], g_ref[0:1, :]) * (1.0 + sc1) + sh1).astype(BF16)
    z = jnp.dot(hb, w_ref[...], preferred_element_type=F32)
    q_ref[...] = z[:, :GLA_QK] * (GLA_DK ** -0.5)
    k_ref[...] = z[:, GLA_QK:2 * GLA_QK]
    v_ref[...] = z[:, 2 * GLA_QK:2 * GLA_QK + GLA_VD].astype(BF16)
    og_ref[...] = z[:, 2 * GLA_QK + GLA_VD:]
    vt = lax.dot_general(wvt_ref[...], hb, NT_DIMS,
                         preferred_element_type=F32).astype(BF16)
    for j in range(vt_ref.shape[0]):
        vt_ref[j] = vt[:, j * GLA_TILE:(j + 1) * GLA_TILE]
    r = jnp.dot(hb, wr_ref[...], preferred_element_type=F32)
    pre = _bdot(r, wg_ref[...]) + gb_ref[...]
    gate = ((jnp.minimum(pre, 0.0) - jnp.log1p(jnp.exp(-jnp.abs(pre))))
            * (LOG2E / GLA_GATE_NORM))
    gf_ref[...] = gate[:, :GLA_QK]
    gbk_ref[...] = gate[:, GLA_QK:]


def _premix1(grp, x2d, mod, norm_g, w_main, w_vt, w_r, w_gate, gate_bias, tm):
    t = grp.t
    return pl.pallas_call(
        _premix1_kernel,
        grid=(t // tm,),
        in_specs=[_row_spec(tm, D_MODEL), grp.mod_spec(tm), _const_spec((4, D_MODEL)),
                  _const_spec((D_MODEL, GLA_MAIN)), _const_spec((GLA_VD, D_MODEL)),
                  _const_spec((D_MODEL, LANES)), _const_spec((LANES, 2 * GLA_QK)),
                  _const_spec((1, 2 * GLA_QK))],
        out_specs=[_row_spec(tm, GLA_QK), _row_spec(tm, GLA_QK), _row_spec(tm, GLA_VD),
                   pl.BlockSpec((tm // GLA_TILE, GLA_VD, GLA_TILE), lambda i: (i, 0, 0)),
                   _row_spec(tm, GLA_VD), _row_spec(tm, GLA_QK), _row_spec(tm, GLA_QK)],
        out_shape=[jax.ShapeDtypeStruct((t, GLA_QK), F32), jax.ShapeDtypeStruct((t, GLA_QK), F32),
                   jax.ShapeDtypeStruct((t, GLA_VD), BF16),
                   jax.ShapeDtypeStruct((t // GLA_TILE, GLA_VD, GLA_TILE), BF16),
                   jax.ShapeDtypeStruct((t, GLA_VD), F32), jax.ShapeDtypeStruct((t, GLA_QK), F32),
                   jax.ShapeDtypeStruct((t, GLA_QK), F32)],
        compiler_params=_cparams("arbitrary"),
        name="premix1",
    )(x2d, mod, norm_g, w_main, w_vt, w_r, w_gate, gate_bias)


def _split3(x):
    hi = x.astype(BF16)
    r1 = x - hi.astype(F32)
    mid = r1.astype(BF16)
    lo = (r1 - mid.astype(F32)).astype(BF16)
    return hi, mid, lo


def _block_boundary(b, s, reverse):
    idx = s if reverse else s - 1
    if 2 * s >= 8:
        n = GLA_TILE // (2 * s)
        r = b.reshape(n, 2 * s, GLA_DK)[:, idx:idx + 1, :]
        return jnp.broadcast_to(r, (n, 2 * s, GLA_DK)).reshape(GLA_TILE, GLA_DK)
    assert s == 2
    b8 = b.reshape(GLA_TILE // 8, 8, GLA_DK)
    sub = lax.broadcasted_iota(jnp.int32, (1, 8, 1), 1)
    r = jnp.where(sub < 4, b8[:, idx:idx + 1, :], b8[:, 4 + idx:5 + idx, :])
    return r.reshape(GLA_TILE, GLA_DK)


def _gla_tile(q, k, g, v, vt, tri, lvl, s_ref, reverse):
    half = GLA_TILE // 2
    b = None
    for part in _split3(g):
        term = jnp.dot(tri, part, preferred_element_type=F32)
        b = term if b is None else b + term
    edge = 0 if reverse else GLA_TILE - 1
    b_last = b[edge:edge + 1, :]
    qe = (q * jnp.exp2(b)).astype(BF16)
    ke = (k * jnp.exp2(b_last - b)).astype(BF16)

    lo_rows, hi_rows = slice(0, half), slice(half, GLA_TILE)
    qb, kb = q.astype(BF16), k.astype(BF16)
    blocks = [jnp.where(lvl == 0, _bdot_nt(qb[rows], kb[rows]), 0.0) for rows in (lo_rows, hi_rows)]
    cross = None
    row = lax.broadcasted_iota(jnp.int32, (GLA_TILE, 1), 0)
    s, level = 1, 1
    while s < GLA_TILE:
        if s == 1:
            arg = jnp.where(row % 2 == (0 if reverse else 1), g, 0.0)
        else:
            arg = -jnp.abs(b - _block_boundary(b, s, reverse))
        f = jnp.exp2(arg)
        qf, kf = (q * f).astype(BF16), (k * f).astype(BF16)
        if s == half:
            q_rows, k_rows = (lo_rows, hi_rows) if reverse else (hi_rows, lo_rows)
            cross = _bdot_nt(qf[q_rows], kf[k_rows])
        else:
            blocks = [jnp.where(lvl == level, _bdot_nt(qf[rows], kf[rows]), blk)
                      for rows, blk in zip((lo_rows, hi_rows), blocks)]
        s, level = 2 * s, level + 1

    o_lo = _bdot(blocks[0], v[lo_rows])
    o_hi = _bdot(blocks[1], v[hi_rows])
    if reverse:
        o_lo = o_lo + _bdot(cross, v[hi_rows])
    else:
        o_hi = o_hi + _bdot(cross, v[lo_rows])
    st = s_ref[...]
    o = jnp.concatenate([o_lo, o_hi], axis=0) + _bdot_nt(qe, st)
    s_ref[...] = st * jnp.exp2(b_last) + jnp.dot(vt, ke, preferred_element_type=F32)
    return o


def _gla_kernel(*refs, zero_init, nt):
    if zero_init:
        (tri_ref, lvl_ref, q_ref, k_ref, gf_ref, gb_ref, v_ref, vt_ref,
         o_ref, sf_ref, sb_ref) = refs
        sf_ref[...] = jnp.zeros_like(sf_ref)
        sb_ref[...] = jnp.zeros_like(sb_ref)
    else:
        (tri_ref, lvl_ref, q_ref, k_ref, gf_ref, gb_ref, v_ref, vt_ref, s0f_ref, s0b_ref,
         o_ref, sf_ref, sb_ref) = refs
        sf_ref[...] = s0f_ref[...]
        sb_ref[...] = s0b_ref[...]
    o_ref[...] = jnp.zeros_like(o_ref)

    def step(t, carry):
        for d, (g_ref, s_ref) in enumerate(((gf_ref, sf_ref), (gb_ref, sb_ref))):
            tile = nt - 1 - t if d else t
            r0 = tile * GLA_TILE
            if not isinstance(r0, int):
                r0 = pl.multiple_of(r0, GLA_TILE)
            rows = pl.ds(r0, GLA_TILE)
            o = _gla_tile(q_ref[rows, :], k_ref[rows, :], g_ref[rows, :], v_ref[rows, :],
                          vt_ref[tile], tri_ref[d], lvl_ref[d], s_ref, bool(d))
            o_ref[rows, :] += o
        return carry

    if nt == 1:
        step(0, 0)
    else:
        lax.fori_loop(0, nt, step, 0)


def _gla_scan(grp, q, k, v, vt, gf, gb, s0f_t, s0b_t, tri, lvl):
    n = grp.n
    nt = n // GLA_TILE
    zero_init = s0f_t is None
    half = GLA_TILE // 2
    seq_dk = pl.BlockSpec((n, GLA_DK), lambda b, h: (b, h))
    seq_dv = pl.BlockSpec((n, GLA_DV), lambda b, h: (b, h))
    state_spec = pl.BlockSpec((None, None, GLA_DV, GLA_DK), lambda b, h: (b, h, 0, 0))
    in_specs = [_const_spec((2, GLA_TILE, GLA_TILE)), _const_spec((2, half, half)),
                seq_dk, seq_dk, seq_dk, seq_dk, seq_dv,
                pl.BlockSpec((nt, GLA_DV, GLA_TILE), lambda b, h: (b, h, 0))]
    args = [tri, lvl, q, k, gf, gb, v, vt]
    if not zero_init:
        in_specs += [state_spec, state_spec]
        args += [s0f_t, s0b_t]
    state_shape = jax.ShapeDtypeStruct((grp.b, GLA_HEADS, GLA_DV, GLA_DK), F32)
    return pl.pallas_call(
        functools.partial(_gla_kernel, zero_init=zero_init, nt=nt),
        grid=(grp.b, GLA_HEADS),
        in_specs=in_specs,
        out_specs=[seq_dv, state_spec, state_spec],
        out_shape=[jax.ShapeDtypeStruct((grp.t, GLA_VD), F32), state_shape, state_shape],
        compiler_params=_cparams("arbitrary", "arbitrary"),
        name="gla_scan",
    )(*args)


def _gla_constants():
    half = GLA_TILE // 2
    i = jnp.arange(GLA_TILE)[:, None]
    j = jnp.arange(GLA_TILE)[None, :]
    tri = jnp.stack([j <= i, j >= i]).astype(BF16)
    ih, jh = i[:half], j[:, :half]
    x = jnp.bitwise_xor(ih, jh)
    level = sum((x >= (1 << p)).astype(jnp.int32) for p in range(half.bit_length() - 1))
    lvl = jnp.stack([jnp.where(jh <= ih, level, -1), jnp.where(jh >= ih, level, -1)])
    return tri, lvl


def _postmix1_kernel(o_ref, og_ref, x_ref, mod_ref, g_ref, gn_ref, w_ref, out_ref):
    gn = gn_ref[...]
    mix = None
    for h in range(GLA_HEADS):
        cols = slice(h * GLA_DV, (h + 1) * GLA_DV)
        o = _rms(o_ref[:, cols], gn)
        og = og_ref[:, cols]
        y = o * (og / (1.0 + jnp.exp(-og)))
        part = _bdot(y, w_ref[cols, :])
        mix = part if mix is None else mix + part
    gt1 = _mod_chunk(mod_ref, 2)
    out_ref[...] = x_ref[...] + gt1 * _rms(mix, g_ref[1:2, :])


def _postmix1(grp, o, og, x2d, mod, norm_g, gla_norm_g, w_out, tm):
    return pl.pallas_call(
        _postmix1_kernel,
        grid=(grp.t // tm,),
        in_specs=[_row_spec(tm, GLA_VD), _row_spec(tm, GLA_VD),
                  _row_spec(tm, D_MODEL), grp.mod_spec(tm), _const_spec((4, D_MODEL)),
                  _const_spec((1, GLA_DV)), _const_spec((GLA_VD, D_MODEL))],
        out_specs=_row_spec(tm, D_MODEL),
        out_shape=jax.ShapeDtypeStruct((grp.t, D_MODEL), F32),
        compiler_params=_cparams("arbitrary"),
        name="postmix1",
    )(o, og, x2d, mod, norm_g, gla_norm_g, w_out)


def _run_group(grp, x, mods, p, rope_tabs, k_ctx, v_ctx, s0f_t, s0b_t, tm):
    x2d = x.reshape(grp.t, D_MODEL)
    zc, q, kv = _premix0(grp, x2d, mods[0], p["norm_g"][0], p["ab_w_in"], rope_tabs, tm)
    if k_ctx is None:
        attn = _ctx_attention(grp, q, kv, p["sink"])
    else:
        attn = _lat_attention(grp, q, kv, k_ctx, v_ctx, p["sink"])
    x2d = _postmix0(grp, zc, attn, x2d, mods[0], p["norm_g"][0], p["conv_w"], p["ab_w_out"], tm)
    x2d = _mlp(grp, x2d, mods[0], p["norm_g"][0], p["mlp_w1"][0], p["mlp_w2"][0], tm)
    gq, gk, gv, gvt, og, gf, gb = _premix1(grp, x2d, mods[1], p["norm_g"][1], p["gla_w_main"],
                                           p["gla_w_vt"], p["gla_w_r"], p["gla_w_gate"],
                                           p["gla_gate_bias"], tm)
    o, sf_t, sb_t = _gla_scan(grp, gq, gk, gv, gvt, gf, gb, s0f_t, s0b_t, p["tri"], p["lvl"])
    x2d = _postmix1(grp, o, og, x2d, mods[1], p["norm_g"][1], p["gla_norm_g"], p["gla_w_out"], tm)
    x2d = _mlp(grp, x2d, mods[1], p["norm_g"][1], p["mlp_w1"][1], p["mlp_w2"][1], tm)
    return x2d.reshape(x.shape), kv, sf_t, sb_t


def kernel(x_prompt, x_sample, cache_k, cache_v, state_fwd, state_bwd, c, c_ctx, mod_w, mod_b,
           norm_g, ab_w_in, conv_w, attn_sink, ab_w_out, gla_w_in, gla_gate_w, gla_gate_b,
           gla_norm_g, gla_w_out, mlp_w1, mlp_w2):
    b_ctx, n_ctx, _ = x_prompt.shape
    b_lat, n_lat, _ = x_sample.shape
    assert mod_w.shape[0] == 2 and ab_w_in.shape[0] == 1 and gla_w_in.shape[0] == 1
    assert 1 + b_lat <= 8

    cond8 = jnp.zeros((8, D_MODEL), F32).at[0].set(c_ctx).at[1:1 + b_lat].set(c)
    mod = _modulation(cond8, mod_w, mod_b)
    mods_ctx = [mod[l, 0:1].reshape(1, 1, -1) for l in range(2)]
    mods_lat = [mod[l, 1:1 + b_lat].reshape(b_lat, 1, -1) for l in range(2)]

    gw = gla_w_in[0]
    v0 = 2 * GLA_QK
    w_gate = jnp.zeros((LANES, 2 * GLA_QK), F32)
    w_gate = w_gate.at[:GLA_RANK, :GLA_QK].set(gla_gate_w[0, 0])
    w_gate = w_gate.at[GLA_RANK:2 * GLA_RANK, GLA_QK:].set(gla_gate_w[0, 1])
    tri, lvl = _gla_constants()
    p = {
        "norm_g": norm_g,
        "ab_w_in": ab_w_in[0].astype(BF16),
        "conv_w": conv_w[0],
        "sink": attn_sink[0],
        "ab_w_out": ab_w_out[0].astype(BF16),
        "mlp_w1": mlp_w1.astype(BF16),
        "mlp_w2": mlp_w2.astype(BF16),
        "gla_w_main": gw[:, :GLA_MAIN].astype(BF16),
        "gla_w_vt": gw[:, v0:v0 + GLA_VD].T.astype(BF16),
        "gla_w_r": jnp.pad(gw[:, GLA_MAIN:], ((0, 0), (0, LANES - 2 * GLA_RANK))).astype(BF16),
        "gla_w_gate": w_gate.astype(BF16),
        "gla_gate_bias": gla_gate_b[0].reshape(1, 2 * GLA_QK),
        "gla_norm_g": gla_norm_g[0].reshape(1, GLA_DV),
        "gla_w_out": gla_w_out[0].astype(BF16),
        "tri": tri,
        "lvl": lvl,
    }

    ctx = _Group(b_ctx, n_ctx, per_seq_mod=False)
    lat = _Group(b_lat, n_lat, per_seq_mod=True)

    y_prompt, kv_ctx, sf_t, sb_t = _run_group(ctx, x_prompt, mods_ctx, p, None, None, None,
                                              None, None, tm=512)
    new_k = kv_ctx[:, :KD].reshape(b_ctx, 1, n_ctx, N_KV_HEADS, HEAD_DIM)
    new_v = kv_ctx[:, KD:].reshape(b_ctx, 1, n_ctx, N_KV_HEADS, HEAD_DIM)
    new_sf = jnp.swapaxes(sf_t, -1, -2)[:, None]
    new_sb = jnp.swapaxes(sb_t, -1, -2)[:, None]

    past = cache_k.shape[2]
    k_ctx = cache_k[:, 0].reshape(b_lat, past, KD)
    v_ctx = cache_v[:, 0].reshape(b_lat, past, KD)
    s0f_t = jnp.swapaxes(state_fwd[:, 0], -1, -2)
    s0b_t = jnp.swapaxes(state_bwd[:, 0], -1, -2)
    y_sample, _, _, _ = _run_group(lat, x_sample, mods_lat, p, _rope_tables(n_lat), k_ctx, v_ctx,
                                   s0f_t, s0b_t, tm=512)
    return (y_prompt, y_sample, new_k, new_v, new_sf, new_sb)
```

```python
import functools

import jax
import jax.numpy as jnp
from jax import lax
from jax.experimental import pallas as pl
from jax.experimental.pallas import tpu as pltpu

F32 = jnp.float32
BF16 = jnp.bfloat16

D_MODEL = 1024
MOD_CHUNKS = 6
EPS = 1e-6
CONV_DIM = 512
N_Q_HEADS = 8
N_KV_HEADS = 2
GQA_GROUP = 4
HEAD_DIM = 64
WINDOW = 128
ATTN_BLOCK = 128
GRID_W = 64
ROPE_BASE = 10000.0
QD = N_Q_HEADS * HEAD_DIM
KD = N_KV_HEADS * HEAD_DIM
AB_IN = 3 * CONV_DIM + QD + 2 * KD
GLA_HEADS = 4
GLA_DK = 128
GLA_DV = 256
GLA_RANK = 16
GLA_GATE_NORM = 16.0
GLA_TILE = 256
LOG2E = 1.4426950408889634
GLA_QK = GLA_HEADS * GLA_DK
GLA_VD = GLA_HEADS * GLA_DV
GLA_MAIN = 2 * GLA_QK + 2 * GLA_VD
D_FF = 4 * D_MODEL
NEG_INF = -1e30
LANES = 128
VMEM_LIMIT = 56 * 1024 * 1024

NT_DIMS = (((1,), (1,)), ((), ()))
TN_DIMS = (((0,), (0,)), ((), ()))


def _cparams(*sem):
    return pltpu.CompilerParams(dimension_semantics=sem, vmem_limit_bytes=VMEM_LIMIT)


def _bdot(a, b):
    return jnp.dot(a.astype(BF16), b.astype(BF16), preferred_element_type=F32)


def _bdot_nt(a, b):
    return lax.dot_general(a.astype(BF16), b.astype(BF16), NT_DIMS, preferred_element_type=F32)


def _rms(x, g):
    ms = jnp.mean(x * x, axis=-1, keepdims=True)
    return x * lax.rsqrt(ms + EPS) * g


def _mod_chunk(mod_ref, i):
    return mod_ref[:, i * D_MODEL:(i + 1) * D_MODEL]


def _const_spec(shape):
    return pl.BlockSpec(shape, lambda *_: (0,) * len(shape))


def _mod_kernel(cond_ref, w_ref, b_ref, o_ref):
    cnd = cond_ref[...]
    s = cnd / (1.0 + jnp.exp(-cnd))
    o_ref[...] = _bdot(s, w_ref[...]) + b_ref[...]


def _modulation(cond8, mod_w, mod_b):
    depth = mod_w.shape[0]
    n = mod_w.shape[2]
    tn = 1536
    return pl.pallas_call(
        _mod_kernel,
        grid=(depth, n // tn),
        in_specs=[
            pl.BlockSpec((8, D_MODEL), lambda l, j: (0, 0)),
            pl.BlockSpec((None, D_MODEL, tn), lambda l, j: (l, 0, j)),
            pl.BlockSpec((None, 1, tn), lambda l, j: (l, 0, j)),
        ],
        out_specs=pl.BlockSpec((None, 8, tn), lambda l, j: (l, 0, j)),
        out_shape=jax.ShapeDtypeStruct((depth, 8, n), F32),
        compiler_params=_cparams("arbitrary", "arbitrary"),
        name="modulation",
    )(cond8, mod_w, mod_b.reshape(depth, 1, n))


class _Group:
    def __init__(self, b, n, per_seq_mod):
        self.b, self.n, self.t = b, n, b * n
        self.per_seq_mod = per_seq_mod

    def mod_spec(self, tm):
        if self.per_seq_mod:
            assert self.n % tm == 0
            per = self.n // tm
            return pl.BlockSpec((None, 1, MOD_CHUNKS * D_MODEL), lambda t: (t // per, 0, 0))
        return pl.BlockSpec((None, 1, MOD_CHUNKS * D_MODEL), lambda t: (0, 0, 0))


def _row_spec(tm, width):
    return pl.BlockSpec((tm, width), lambda t: (t, 0))


def _rope(x, cos, sin_lo, sin_hi):
    return (x * cos + pltpu.roll(x, LANES - 16, axis=1) * sin_lo
            + pltpu.roll(x, 16, axis=1) * sin_hi)


def _premix0_kernel(*refs, rope):
    if rope:
        x_ref, mod_ref, g_ref, w_ref, cos_ref, slo_ref, shi_ref, zc_ref, q_ref, kv_ref = refs
    else:
        x_ref, mod_ref, g_ref, w_ref, zc_ref, q_ref, kv_ref = refs
    sh1, sc1 = _mod_chunk(mod_ref, 0), _mod_chunk(mod_ref, 1)
    h = _rms(x_ref[...], g_ref[0:1, :]) * (1.0 + sc1) + sh1
    z = _bdot(h, w_ref[...])
    c3 = 3 * CONV_DIM
    zc_ref[...] = z[:, :c3]
    scale = HEAD_DIM ** -0.5
    if rope:
        cos, slo, shi = cos_ref[...], slo_ref[...], shi_ref[...]
        for j in range(QD // LANES):
            qs = z[:, c3 + j * LANES:c3 + (j + 1) * LANES]
            q_ref[:, j * LANES:(j + 1) * LANES] = (_rope(qs, cos, slo, shi) * scale).astype(BF16)
        kv_ref[:, :KD] = _rope(z[:, c3 + QD:c3 + QD + KD], cos, slo, shi)
    else:
        q_ref[...] = (z[:, c3:c3 + QD] * scale).astype(BF16)
        kv_ref[:, :KD] = z[:, c3 + QD:c3 + QD + KD]
    kv_ref[:, KD:] = z[:, c3 + QD + KD:]


def _premix0(grp, x2d, mod, norm_g, w_in, rope_tabs, tm):
    rope = rope_tabs is not None
    in_specs = [_row_spec(tm, D_MODEL), grp.mod_spec(tm), _const_spec((4, D_MODEL)),
                _const_spec((D_MODEL, AB_IN))]
    args = [x2d, mod, norm_g, w_in]
    if rope:
        per = grp.n // tm
        in_specs += [pl.BlockSpec((tm, LANES), lambda t: (t % per, 0))] * 3
        args += list(rope_tabs)
    return pl.pallas_call(
        functools.partial(_premix0_kernel, rope=rope),
        grid=(grp.t // tm,),
        in_specs=in_specs,
        out_specs=[_row_spec(tm, 3 * CONV_DIM), _row_spec(tm, QD), _row_spec(tm, 2 * KD)],
        out_shape=[jax.ShapeDtypeStruct((grp.t, 3 * CONV_DIM), F32),
                   jax.ShapeDtypeStruct((grp.t, QD), BF16),
                   jax.ShapeDtypeStruct((grp.t, 2 * KD), F32)],
        compiler_params=_cparams("arbitrary"),
        name="premix0_rope" if rope else "premix0",
    )(*args)


def _rope_tables(n):
    rows = n // GRID_W
    pos_r = jnp.repeat(jnp.arange(rows), GRID_W)
    pos_c = jnp.tile(jnp.arange(GRID_W), rows)
    half = HEAD_DIM // 2
    quarter = half // 2
    inv = ROPE_BASE ** (-(jnp.arange(quarter, dtype=F32) * 2.0 / half))

    def cs(pos):
        ang = pos.astype(F32)[:, None] * inv[None, :]
        return jnp.cos(ang), jnp.sin(ang)

    cr, sr = cs(pos_r)
    cc, sc = cs(pos_c)
    zero = jnp.zeros_like(sr)
    cos = jnp.concatenate([cr, cr, cc, cc], axis=1)
    sin_lo = jnp.concatenate([-sr, zero, -sc, zero], axis=1)
    sin_hi = jnp.concatenate([zero, sr, zero, sc], axis=1)
    rep = LANES // HEAD_DIM
    return tuple(jnp.tile(t, (1, rep)) for t in (cos, sin_lo, sin_hi))


def _sink_attend(s, sink, vs):
    m = sink
    for si in s:
        m = jnp.maximum(m, jnp.max(si, axis=-1, keepdims=True))
    den = jnp.exp(sink - m)
    o = None
    for si, vi in zip(s, vs):
        p = jnp.exp(si - m)
        den = den + jnp.sum(p, axis=-1, keepdims=True)
        oi = _bdot(p, vi)
        o = oi if o is None else o + oi
    return o / den


def _ctx_attn_kernel(sink_ref, q_ref, kv_ref, o_ref):
    for g in range(N_KV_HEADS):
        k = kv_ref[:, g * HEAD_DIM:(g + 1) * HEAD_DIM].astype(BF16)
        v = kv_ref[:, KD + g * HEAD_DIM:KD + (g + 1) * HEAD_DIM].astype(BF16)
        for hh in range(GQA_GROUP):
            h = g * GQA_GROUP + hh
            q = q_ref[:, h * HEAD_DIM:(h + 1) * HEAD_DIM]
            s = _bdot_nt(q, k)
            o = _sink_attend([s], sink_ref[h], [v])
            o_ref[:, h * HEAD_DIM:(h + 1) * HEAD_DIM] = o.astype(BF16)


def _ctx_attention(grp, q, kv, sink):
    n = grp.n
    return pl.pallas_call(
        _ctx_attn_kernel,
        grid=(grp.b,),
        in_specs=[pl.BlockSpec(memory_space=pltpu.SMEM), _row_spec(n, QD), _row_spec(n, 2 * KD)],
        out_specs=_row_spec(n, QD),
        out_shape=jax.ShapeDtypeStruct((grp.t, QD), BF16),
        compiler_params=_cparams("arbitrary"),
        name="ctx_attention",
    )(sink, q, kv)


def _lat_attn_kernel(sink_ref, q_ref, kvp_ref, kvc_ref, kvn_ref, kc_ref, vc_ref, o_ref, *, n):
    blk = pl.program_id(1)
    kv = jnp.concatenate([kvp_ref[...], kvc_ref[...], kvn_ref[...]], axis=0)
    qi = lax.broadcasted_iota(jnp.int32, (ATTN_BLOCK, 3 * ATTN_BLOCK), 0)
    kj = lax.broadcasted_iota(jnp.int32, (ATTN_BLOCK, 3 * ATTN_BLOCK), 1)
    rel = kj - ATTN_BLOCK - qi
    kpos = (blk - 1) * ATTN_BLOCK + kj
    valid = (jnp.abs(rel) <= WINDOW) & (kpos >= 0) & (kpos < n)
    for g in range(N_KV_HEADS):
        k = kv[:, g * HEAD_DIM:(g + 1) * HEAD_DIM].astype(BF16)
        v = kv[:, KD + g * HEAD_DIM:KD + (g + 1) * HEAD_DIM].astype(BF16)
        kc = kc_ref[:, g * HEAD_DIM:(g + 1) * HEAD_DIM].astype(BF16)
        vc = vc_ref[:, g * HEAD_DIM:(g + 1) * HEAD_DIM].astype(BF16)
        for hh in range(GQA_GROUP):
            h = g * GQA_GROUP + hh
            q = q_ref[:, h * HEAD_DIM:(h + 1) * HEAD_DIM]
            s_loc = jnp.where(valid, _bdot_nt(q, k), NEG_INF)
            s_ctx = _bdot_nt(q, kc)
            o = _sink_attend([s_loc, s_ctx], sink_ref[h], [v, vc])
            o_ref[:, h * HEAD_DIM:(h + 1) * HEAD_DIM] = o.astype(BF16)


def _lat_attention(grp, q, kv, k_ctx, v_ctx, sink):
    nb = grp.n // ATTN_BLOCK
    past = k_ctx.shape[1]

    def band(off):
        return pl.BlockSpec((ATTN_BLOCK, 2 * KD),
                            lambda b, i: (b * nb + jnp.clip(i + off, 0, nb - 1), 0))

    ctx_spec = pl.BlockSpec((None, past, KD), lambda b, i: (b, 0, 0))
    return pl.pallas_call(
        functools.partial(_lat_attn_kernel, n=grp.n),
        grid=(grp.b, nb),
        in_specs=[pl.BlockSpec(memory_space=pltpu.SMEM),
                  pl.BlockSpec((ATTN_BLOCK, QD), lambda b, i: (b * nb + i, 0)),
                  band(-1), band(0), band(1), ctx_spec, ctx_spec],
        out_specs=pl.BlockSpec((ATTN_BLOCK, QD), lambda b, i: (b * nb + i, 0)),
        out_shape=jax.ShapeDtypeStruct((grp.t, QD), BF16),
        compiler_params=_cparams("arbitrary", "arbitrary"),
        name="lat_attention",
    )(sink, q, kv, kv, kv, k_ctx, v_ctx)


def _postmix0_kernel(zc_ref, zp_ref, zn_ref, at_ref, x_ref, mod_ref, g_ref, cw_ref, w_ref,
                     o_ref, *, n, tm):
    c = CONV_DIM
    zc = zc_ref[...]
    u = zc[:, c:2 * c] * zc[:, 2 * c:]
    u_before = zp_ref[7:8, c:2 * c] * zp_ref[7:8, 2 * c:]
    u_after = zn_ref[0:1, c:2 * c] * zn_ref[0:1, 2 * c:]
    row = lax.broadcasted_iota(jnp.int32, (tm, 1), 0)
    pos = (pl.program_id(0) * tm + row) % n
    u_prev = jnp.where(row == 0, u_before, pltpu.roll(u, 1, axis=0))
    u_prev = jnp.where(pos == 0, 0.0, u_prev)
    u_next = jnp.where(row == tm - 1, u_after, pltpu.roll(u, tm - 1, axis=0))
    u_next = jnp.where(pos == n - 1, 0.0, u_next)
    conv = u_prev * cw_ref[0:1, :] + u * cw_ref[1:2, :] + u_next * cw_ref[2:3, :]
    mix = _bdot(zc[:, :c] * conv, w_ref[:c, :]) + _bdot(at_ref[...], w_ref[c:, :])
    gt1 = _mod_chunk(mod_ref, 2)
    o_ref[...] = x_ref[...] + gt1 * _rms(mix, g_ref[1:2, :])


def _postmix0(grp, zc, attn, x2d, mod, norm_g, conv_w, w_out, tm):
    r8 = tm // 8
    last8 = grp.t // 8 - 1
    return pl.pallas_call(
        functools.partial(_postmix0_kernel, n=grp.n, tm=tm),
        grid=(grp.t // tm,),
        in_specs=[_row_spec(tm, 3 * CONV_DIM),
                  pl.BlockSpec((8, 3 * CONV_DIM), lambda t: (jnp.maximum(t * r8 - 1, 0), 0)),
                  pl.BlockSpec((8, 3 * CONV_DIM), lambda t: (jnp.minimum((t + 1) * r8, last8), 0)),
                  _row_spec(tm, QD), _row_spec(tm, D_MODEL), grp.mod_spec(tm),
                  _const_spec((4, D_MODEL)), _const_spec((3, CONV_DIM)),
                  _const_spec((CONV_DIM + QD, D_MODEL))],
        out_specs=_row_spec(tm, D_MODEL),
        out_shape=jax.ShapeDtypeStruct((grp.t, D_MODEL), F32),
        compiler_params=_cparams("arbitrary"),
        name="postmix0",
    )(zc, zc, zc, attn, x2d, mod, norm_g, conv_w, w_out)


def _mlp_kernel(x_ref, mod_ref, g_ref, w1_ref, w2_ref, o_ref, *, fchunk):
    x = x_ref[...]
    sh2, sc2, gt2 = _mod_chunk(mod_ref, 3), _mod_chunk(mod_ref, 4), _mod_chunk(mod_ref, 5)
    hb = (_rms(x, g_ref[2:3, :]) * (1.0 + sc2) + sh2).astype(BF16)
    acc = None
    for j in range(D_FF // fchunk):
        a = jnp.dot(hb, w1_ref[:, j * fchunk:(j + 1) * fchunk], preferred_element_type=F32)
        a = jnp.maximum(a, 0.0)
        part = _bdot(a * a, w2_ref[j * fchunk:(j + 1) * fchunk, :])
        acc = part if acc is None else acc + part
    o_ref[...] = x + gt2 * _rms(acc, g_ref[3:4, :])


def _mlp(grp, x2d, mod, norm_g, w1, w2, tm):
    single = pl.Buffered(1)
    return pl.pallas_call(
        functools.partial(_mlp_kernel, fchunk=1024),
        grid=(grp.t // tm,),
        in_specs=[_row_spec(tm, D_MODEL), grp.mod_spec(tm), _const_spec((4, D_MODEL)),
                  pl.BlockSpec((D_MODEL, D_FF), lambda t: (0, 0), pipeline_mode=single),
                  pl.BlockSpec((D_FF, D_MODEL), lambda t: (0, 0), pipeline_mode=single)],
        out_specs=_row_spec(tm, D_MODEL),
        out_shape=jax.ShapeDtypeStruct((grp.t, D_MODEL), F32),
        compiler_params=_cparams("arbitrary"),
        name="mlp",
    )(x2d, mod, norm_g, w1, w2)


def _premix1_kernel(x_ref, mod_ref, g_ref, w_ref, wg_ref, gb_ref,
                    q_ref, k_ref, v_ref, og_ref, gf_ref, gbk_ref):
    sh1, sc1 = _mod_chunk(mod_ref, 0), _mod_chunk(mod_ref, 1)
    hb = (_rms(x_ref[...], g_ref[0:1, :]) * (1.0 + sc1) + sh1).astype(BF16)
    z = jnp.dot(hb, w_ref[...], preferred_element_type=F32)
    q_ref[...] = z[:, :GLA_QK] * (GLA_DK ** -0.5)
    k_ref[...] = z[:, GLA_QK:2 * GLA_QK]
    v_ref[...] = z[:, 2 * GLA_QK:2 * GLA_QK + GLA_VD].astype(BF16)
    og_ref[...] = z[:, 2 * GLA_QK + GLA_VD:GLA_MAIN]
    r = z[:, GLA_MAIN:]
    pre = _bdot(r, wg_ref[...]) + gb_ref[...]
    gate = ((jnp.minimum(pre, 0.0) - jnp.log1p(jnp.exp(-jnp.abs(pre))))
            * (LOG2E / GLA_GATE_NORM))
    gf_ref[...] = gate[:, :GLA_QK]
    gbk_ref[...] = gate[:, GLA_QK:]


def _premix1(grp, x2d, mod, norm_g, w_in, w_gate, gate_bias, tm):
    t = grp.t
    return pl.pallas_call(
        _premix1_kernel,
        grid=(t // tm,),
        in_specs=[_row_spec(tm, D_MODEL), grp.mod_spec(tm), _const_spec((4, D_MODEL)),
                  _const_spec((D_MODEL, GLA_MAIN + LANES)), _const_spec((LANES, 2 * GLA_QK)),
                  _const_spec((1, 2 * GLA_QK))],
        out_specs=[_row_spec(tm, GLA_QK), _row_spec(tm, GLA_QK), _row_spec(tm, GLA_VD),
                   _row_spec(tm, GLA_VD), _row_spec(tm, GLA_QK), _row_spec(tm, GLA_QK)],
        out_shape=[jax.ShapeDtypeStruct((t, GLA_QK), F32), jax.ShapeDtypeStruct((t, GLA_QK), F32),
                   jax.ShapeDtypeStruct((t, GLA_VD), BF16),
                   jax.ShapeDtypeStruct((t, GLA_VD), F32), jax.ShapeDtypeStruct((t, GLA_QK), F32),
                   jax.ShapeDtypeStruct((t, GLA_QK), F32)],
        compiler_params=_cparams("arbitrary"),
        name="premix1",
    )(x2d, mod, norm_g, w_in, w_gate, gate_bias)


def _split3(x):
    hi = x.astype(BF16)
    r1 = x - hi.astype(F32)
    mid = r1.astype(BF16)
    lo = (r1 - mid.astype(F32)).astype(BF16)
    return hi, mid, lo


def _block_boundary(b, s, reverse):
    idx = s if reverse else s - 1
    if 2 * s >= 8:
        n = GLA_TILE // (2 * s)
        r = b.reshape(n, 2 * s, GLA_DK)[:, idx:idx + 1, :]
        return jnp.broadcast_to(r, (n, 2 * s, GLA_DK)).reshape(GLA_TILE, GLA_DK)
    assert s == 2
    b8 = b.reshape(GLA_TILE // 8, 8, GLA_DK)
    sub = lax.broadcasted_iota(jnp.int32, (1, 8, 1), 1)
    r = jnp.where(sub < 4, b8[:, idx:idx + 1, :], b8[:, 4 + idx:5 + idx, :])
    return r.reshape(GLA_TILE, GLA_DK)


def _gla_tile(q, k, g, v, tri, lvl, s_ref, reverse):
    half = GLA_TILE // 2
    b = None
    for part in _split3(g):
        term = jnp.dot(tri, part, preferred_element_type=F32)
        b = term if b is None else b + term
    edge = 0 if reverse else GLA_TILE - 1
    b_last = b[edge:edge + 1, :]
    qe = (q * jnp.exp2(b)).astype(BF16)
    ke = (k * jnp.exp2(b_last - b)).astype(BF16)

    lo_rows, hi_rows = slice(0, half), slice(half, GLA_TILE)
    qb, kb = q.astype(BF16), k.astype(BF16)
    blocks = [jnp.where(lvl == 0, _bdot_nt(qb[rows], kb[rows]), 0.0) for rows in (lo_rows, hi_rows)]
    cross = None
    row = lax.broadcasted_iota(jnp.int32, (GLA_TILE, 1), 0)
    s, level = 1, 1
    while s < GLA_TILE:
        if s == 1:
            arg = jnp.where(row % 2 == (0 if reverse else 1), g, 0.0)
        else:
            arg = -jnp.abs(b - _block_boundary(b, s, reverse))
        f = jnp.exp2(arg)
        qf, kf = (q * f).astype(BF16), (k * f).astype(BF16)
        if s == half:
            q_rows, k_rows = (lo_rows, hi_rows) if reverse else (hi_rows, lo_rows)
            cross = _bdot_nt(qf[q_rows], kf[k_rows])
        else:
            blocks = [jnp.where(lvl == level, _bdot_nt(qf[rows], kf[rows]), blk)
                      for rows, blk in zip((lo_rows, hi_rows), blocks)]
        s, level = 2 * s, level + 1

    o_lo = _bdot(blocks[0], v[lo_rows])
    o_hi = _bdot(blocks[1], v[hi_rows])
    if reverse:
        o_lo = o_lo + _bdot(cross, v[hi_rows])
    else:
        o_hi = o_hi + _bdot(cross, v[lo_rows])
    st = s_ref[...]
    o = jnp.concatenate([o_lo, o_hi], axis=0) + _bdot_nt(qe, st)
    s_ref[...] = st * jnp.exp2(b_last) + lax.dot_general(
        v, ke, TN_DIMS, preferred_element_type=F32)
    return o


def _gla_kernel(*refs, zero_init, nt):
    if zero_init:
        (tri_ref, lvl_ref, q_ref, k_ref, gf_ref, gb_ref, v_ref,
         o_ref, sf_ref, sb_ref) = refs
        sf_ref[...] = jnp.zeros_like(sf_ref)
        sb_ref[...] = jnp.zeros_like(sb_ref)
    else:
        (tri_ref, lvl_ref, q_ref, k_ref, gf_ref, gb_ref, v_ref, s0f_ref, s0b_ref,
         o_ref, sf_ref, sb_ref) = refs
        sf_ref[...] = s0f_ref[...]
        sb_ref[...] = s0b_ref[...]
    o_ref[...] = jnp.zeros_like(o_ref)

    def step(t, carry):
        for d, (g_ref, s_ref) in enumerate(((gf_ref, sf_ref), (gb_ref, sb_ref))):
            tile = nt - 1 - t if d else t
            r0 = tile * GLA_TILE
            if not isinstance(r0, int):
                r0 = pl.multiple_of(r0, GLA_TILE)
            rows = pl.ds(r0, GLA_TILE)
            o = _gla_tile(q_ref[rows, :], k_ref[rows, :], g_ref[rows, :], v_ref[rows, :],
                          tri_ref[d], lvl_ref[d], s_ref, bool(d))
            o_ref[rows, :] += o
        return carry

    if nt == 1:
        step(0, 0)
    else:
        lax.fori_loop(0, nt, step, 0)


def _gla_scan(grp, q, k, v, gf, gb, s0f_t, s0b_t, tri, lvl):
    n = grp.n
    nt = n // GLA_TILE
    zero_init = s0f_t is None
    half = GLA_TILE // 2
    seq_dk = pl.BlockSpec((n, GLA_DK), lambda b, h: (b, h))
    seq_dv = pl.BlockSpec((n, GLA_DV), lambda b, h: (b, h))
    state_spec = pl.BlockSpec((None, None, GLA_DV, GLA_DK), lambda b, h: (b, h, 0, 0))
    in_specs = [_const_spec((2, GLA_TILE, GLA_TILE)), _const_spec((2, half, half)),
                seq_dk, seq_dk, seq_dk, seq_dk, seq_dv]
    args = [tri, lvl, q, k, gf, gb, v]
    if not zero_init:
        in_specs += [state_spec, state_spec]
        args += [s0f_t, s0b_t]
    state_shape = jax.ShapeDtypeStruct((grp.b, GLA_HEADS, GLA_DV, GLA_DK), F32)
    return pl.pallas_call(
        functools.partial(_gla_kernel, zero_init=zero_init, nt=nt),
        grid=(grp.b, GLA_HEADS),
        in_specs=in_specs,
        out_specs=[seq_dv, state_spec, state_spec],
        out_shape=[jax.ShapeDtypeStruct((grp.t, GLA_VD), F32), state_shape, state_shape],
        compiler_params=_cparams("arbitrary", "arbitrary"),
        name="gla_scan",
    )(*args)


def _gla_constants():
    half = GLA_TILE // 2
    i = jnp.arange(GLA_TILE)[:, None]
    j = jnp.arange(GLA_TILE)[None, :]
    tri = jnp.stack([j <= i, j >= i]).astype(BF16)
    ih, jh = i[:half], j[:, :half]
    x = jnp.bitwise_xor(ih, jh)
    level = sum((x >= (1 << p)).astype(jnp.int32) for p in range(half.bit_length() - 1))
    lvl = jnp.stack([jnp.where(jh <= ih, level, -1), jnp.where(jh >= ih, level, -1)])
    return tri, lvl


def _postmix1_kernel(o_ref, og_ref, x_ref, mod_ref, g_ref, gn_ref, w_ref, out_ref):
    gn = gn_ref[...]
    mix = None
    for h in range(GLA_HEADS):
        cols = slice(h * GLA_DV, (h + 1) * GLA_DV)
        o = _rms(o_ref[:, cols], gn)
        og = og_ref[:, cols]
        y = o * (og / (1.0 + jnp.exp(-og)))
        part = _bdot(y, w_ref[cols, :])
        mix = part if mix is None else mix + part
    gt1 = _mod_chunk(mod_ref, 2)
    out_ref[...] = x_ref[...] + gt1 * _rms(mix, g_ref[1:2, :])


def _postmix1(grp, o, og, x2d, mod, norm_g, gla_norm_g, w_out, tm):
    return pl.pallas_call(
        _postmix1_kernel,
        grid=(grp.t // tm,),
        in_specs=[_row_spec(tm, GLA_VD), _row_spec(tm, GLA_VD),
                  _row_spec(tm, D_MODEL), grp.mod_spec(tm), _const_spec((4, D_MODEL)),
                  _const_spec((1, GLA_DV)), _const_spec((GLA_VD, D_MODEL))],
        out_specs=_row_spec(tm, D_MODEL),
        out_shape=jax.ShapeDtypeStruct((grp.t, D_MODEL), F32),
        compiler_params=_cparams("arbitrary"),
        name="postmix1",
    )(o, og, x2d, mod, norm_g, gla_norm_g, w_out)


def _run_group(grp, x, mods, p, rope_tabs, k_ctx, v_ctx, s0f_t, s0b_t, tm):
    x2d = x.reshape(grp.t, D_MODEL)
    zc, q, kv = _premix0(grp, x2d, mods[0], p["norm_g"][0], p["ab_w_in"], rope_tabs, tm)
    if k_ctx is None:
        attn = _ctx_attention(grp, q, kv, p["sink"])
    else:
        attn = _lat_attention(grp, q, kv, k_ctx, v_ctx, p["sink"])
    x2d = _postmix0(grp, zc, attn, x2d, mods[0], p["norm_g"][0], p["conv_w"], p["ab_w_out"], tm)
    x2d = _mlp(grp, x2d, mods[0], p["norm_g"][0], p["mlp_w1"][0], p["mlp_w2"][0], tm)
    gq, gk, gv, og, gf, gb = _premix1(grp, x2d, mods[1], p["norm_g"][1], p["gla_w_in"],
                                      p["gla_w_gate"], p["gla_gate_bias"], tm)
    o, sf_t, sb_t = _gla_scan(grp, gq, gk, gv, gf, gb, s0f_t, s0b_t, p["tri"], p["lvl"])
    x2d = _postmix1(grp, o, og, x2d, mods[1], p["norm_g"][1], p["gla_norm_g"], p["gla_w_out"], tm)
    x2d = _mlp(grp, x2d, mods[1], p["norm_g"][1], p["mlp_w1"][1], p["mlp_w2"][1], tm)
    return x2d.reshape(x.shape), kv, sf_t, sb_t


def kernel(x_prompt, x_sample, cache_k, cache_v, state_fwd, state_bwd, c, c_ctx, mod_w, mod_b,
           norm_g, ab_w_in, conv_w, attn_sink, ab_w_out, gla_w_in, gla_gate_w, gla_gate_b,
           gla_norm_g, gla_w_out, mlp_w1, mlp_w2):
    b_ctx, n_ctx, _ = x_prompt.shape
    b_lat, n_lat, _ = x_sample.shape
    assert mod_w.shape[0] == 2 and ab_w_in.shape[0] == 1 and gla_w_in.shape[0] == 1
    assert 1 + b_lat <= 8

    cond8 = jnp.zeros((8, D_MODEL), F32).at[0].set(c_ctx).at[1:1 + b_lat].set(c)
    mod = _modulation(cond8, mod_w, mod_b)
    mods_ctx = [mod[l, 0:1].reshape(1, 1, -1) for l in range(2)]
    mods_lat = [mod[l, 1:1 + b_lat].reshape(b_lat, 1, -1) for l in range(2)]

    gw = gla_w_in[0]
    w_gate = jnp.zeros((LANES, 2 * GLA_QK), F32)
    w_gate = w_gate.at[:GLA_RANK, :GLA_QK].set(gla_gate_w[0, 0])
    w_gate = w_gate.at[GLA_RANK:2 * GLA_RANK, GLA_QK:].set(gla_gate_w[0, 1])
    tri, lvl = _gla_constants()
    p = {
        "norm_g": norm_g,
        "ab_w_in": ab_w_in[0].astype(BF16),
        "conv_w": conv_w[0],
        "sink": attn_sink[0],
        "ab_w_out": ab_w_out[0].astype(BF16),
        "mlp_w1": mlp_w1.astype(BF16),
        "mlp_w2": mlp_w2.astype(BF16),
        "gla_w_in": jnp.pad(gw, ((0, 0), (0, LANES - 2 * GLA_RANK))).astype(BF16),
        "gla_w_gate": w_gate.astype(BF16),
        "gla_gate_bias": gla_gate_b[0].reshape(1, 2 * GLA_QK),
        "gla_norm_g": gla_norm_g[0].reshape(1, GLA_DV),
        "gla_w_out": gla_w_out[0].astype(BF16),
        "tri": tri,
        "lvl": lvl,
    }

    ctx = _Group(b_ctx, n_ctx, per_seq_mod=False)
    lat = _Group(b_lat, n_lat, per_seq_mod=True)

    y_prompt, kv_ctx, sf_t, sb_t = _run_group(ctx, x_prompt, mods_ctx, p, None, None, None,
                                              None, None, tm=512)
    new_k = kv_ctx[:, :KD].reshape(b_ctx, 1, n_ctx, N_KV_HEADS, HEAD_DIM)
    new_v = kv_ctx[:, KD:].reshape(b_ctx, 1, n_ctx, N_KV_HEADS, HEAD_DIM)
    new_sf = jnp.swapaxes(sf_t, -1, -2)[:, None]
    new_sb = jnp.swapaxes(sb_t, -1, -2)[:, None]

    past = cache_k.shape[2]
    k_ctx = cache_k[:, 0].reshape(b_lat, past, KD)
    v_ctx = cache_v[:, 0].reshape(b_lat, past, KD)
    s0f_t = jnp.swapaxes(state_fwd[:, 0], -1, -2)
    s0b_t = jnp.swapaxes(state_bwd[:, 0], -1, -2)
    y_sample, _, _, _ = _run_group(lat, x_sample, mods_lat, p, _rope_tables(n_lat), k_ctx, v_ctx,
                                   s0f_t, s0b_t, tm=512)
    return (y_prompt, y_sample, new_k, new_v, new_sf, new_sb)
```

```python
import functools

import jax
import jax.numpy as jnp
from jax import lax
from jax.experimental import pallas as pl
from jax.experimental.pallas import tpu as pltpu

F32 = jnp.float32
BF16 = jnp.bfloat16

D_MODEL = 1024
MOD_CHUNKS = 6
EPS = 1e-6
CONV_DIM = 512
N_Q_HEADS = 8
N_KV_HEADS = 2
GQA_GROUP = 4
HEAD_DIM = 64
WINDOW = 128
ATTN_BLOCK = 128
GRID_W = 64
ROPE_BASE = 10000.0
QD = N_Q_HEADS * HEAD_DIM
KD = N_KV_HEADS * HEAD_DIM
AB_IN = 3 * CONV_DIM + QD + 2 * KD
GLA_HEADS = 4
GLA_DK = 128
GLA_DV = 256
GLA_RANK = 16
GLA_GATE_NORM = 16.0
GLA_TILE = 256
LOG2E = 1.4426950408889634
GLA_QK = GLA_HEADS * GLA_DK
GLA_VD = GLA_HEADS * GLA_DV
GLA_MAIN = 2 * GLA_QK + 2 * GLA_VD
D_FF = 4 * D_MODEL
NEG_INF = -1e30
LANES = 128
VMEM_LIMIT = 56 * 1024 * 1024

NT_DIMS = (((1,), (1,)), ((), ()))
TN_DIMS = (((0,), (0,)), ((), ()))


def _cparams(*sem):
    return pltpu.CompilerParams(dimension_semantics=sem, vmem_limit_bytes=VMEM_LIMIT)


def _bdot(a, b):
    return jnp.dot(a.astype(BF16), b.astype(BF16), preferred_element_type=F32)


def _bdot_nt(a, b):
    return lax.dot_general(a.astype(BF16), b.astype(BF16), NT_DIMS, preferred_element_type=F32)


def _rms(x, g):
    ms = jnp.mean(x * x, axis=-1, keepdims=True)
    return x * lax.rsqrt(ms + EPS) * g


def _mod_chunk(mod_ref, i):
    return mod_ref[:, i * D_MODEL:(i + 1) * D_MODEL]


def _const_spec(shape):
    return pl.BlockSpec(shape, lambda *_: (0,) * len(shape))


def _mod_kernel(cond_ref, w_ref, b_ref, o_ref):
    cnd = cond_ref[...]
    s = cnd / (1.0 + jnp.exp(-cnd))
    o_ref[...] = _bdot(s, w_ref[...]) + b_ref[...]


def _modulation(cond8, mod_w, mod_b):
    depth = mod_w.shape[0]
    n = mod_w.shape[2]
    tn = 1536
    return pl.pallas_call(
        _mod_kernel,
        grid=(depth, n // tn),
        in_specs=[
            pl.BlockSpec((8, D_MODEL), lambda l, j: (0, 0)),
            pl.BlockSpec((None, D_MODEL, tn), lambda l, j: (l, 0, j)),
            pl.BlockSpec((None, 1, tn), lambda l, j: (l, 0, j)),
        ],
        out_specs=pl.BlockSpec((None, 8, tn), lambda l, j: (l, 0, j)),
        out_shape=jax.ShapeDtypeStruct((depth, 8, n), F32),
        compiler_params=_cparams("arbitrary", "arbitrary"),
        name="modulation",
    )(cond8, mod_w, mod_b.reshape(depth, 1, n))


class _Group:
    def __init__(self, b, n, per_seq_mod):
        self.b, self.n, self.t = b, n, b * n
        self.per_seq_mod = per_seq_mod

    def mod_spec(self, tm):
        if self.per_seq_mod:
            assert self.n % tm == 0
            per = self.n // tm
            return pl.BlockSpec((None, 1, MOD_CHUNKS * D_MODEL), lambda t: (t // per, 0, 0))
        return pl.BlockSpec((None, 1, MOD_CHUNKS * D_MODEL), lambda t: (0, 0, 0))


def _row_spec(tm, width):
    return pl.BlockSpec((tm, width), lambda t: (t, 0))


def _rope(x, cos, sin_lo, sin_hi):
    return (x * cos + pltpu.roll(x, LANES - 16, axis=1) * sin_lo
            + pltpu.roll(x, 16, axis=1) * sin_hi)


def _premix0_kernel(*refs, rope):
    if rope:
        x_ref, mod_ref, g_ref, w_ref, cos_ref, slo_ref, shi_ref, zc_ref, q_ref, kv_ref = refs
    else:
        x_ref, mod_ref, g_ref, w_ref, zc_ref, q_ref, kv_ref = refs
    sh1, sc1 = _mod_chunk(mod_ref, 0), _mod_chunk(mod_ref, 1)
    h = _rms(x_ref[...], g_ref[0:1, :]) * (1.0 + sc1) + sh1
    z = _bdot(h, w_ref[...])
    c3 = 3 * CONV_DIM
    zc_ref[...] = z[:, :c3]
    scale = HEAD_DIM ** -0.5 * LOG2E
    if rope:
        cos, slo, shi = cos_ref[...], slo_ref[...], shi_ref[...]
        for j in range(QD // LANES):
            qs = z[:, c3 + j * LANES:c3 + (j + 1) * LANES]
            q_ref[:, j * LANES:(j + 1) * LANES] = (_rope(qs, cos, slo, shi) * scale).astype(BF16)
        kv_ref[:, :KD] = _rope(z[:, c3 + QD:c3 + QD + KD], cos, slo, shi)
    else:
        q_ref[...] = (z[:, c3:c3 + QD] * scale).astype(BF16)
        kv_ref[:, :KD] = z[:, c3 + QD:c3 + QD + KD]
    kv_ref[:, KD:] = z[:, c3 + QD + KD:]


def _premix0(grp, x2d, mod, norm_g, w_in, rope_tabs, tm):
    rope = rope_tabs is not None
    in_specs = [_row_spec(tm, D_MODEL), grp.mod_spec(tm), _const_spec((4, D_MODEL)),
                _const_spec((D_MODEL, AB_IN))]
    args = [x2d, mod, norm_g, w_in]
    if rope:
        per = grp.n // tm
        in_specs += [pl.BlockSpec((tm, LANES), lambda t: (t % per, 0))] * 3
        args += list(rope_tabs)
    return pl.pallas_call(
        functools.partial(_premix0_kernel, rope=rope),
        grid=(grp.t // tm,),
        in_specs=in_specs,
        out_specs=[_row_spec(tm, 3 * CONV_DIM), _row_spec(tm, QD), _row_spec(tm, 2 * KD)],
        out_shape=[jax.ShapeDtypeStruct((grp.t, 3 * CONV_DIM), F32),
                   jax.ShapeDtypeStruct((grp.t, QD), BF16),
                   jax.ShapeDtypeStruct((grp.t, 2 * KD), F32)],
        compiler_params=_cparams("arbitrary"),
        name="premix0_rope" if rope else "premix0",
    )(*args)


def _rope_tables(n):
    rows = n // GRID_W
    pos_r = jnp.repeat(jnp.arange(rows), GRID_W)
    pos_c = jnp.tile(jnp.arange(GRID_W), rows)
    half = HEAD_DIM // 2
    quarter = half // 2
    inv = ROPE_BASE ** (-(jnp.arange(quarter, dtype=F32) * 2.0 / half))

    def cs(pos):
        ang = pos.astype(F32)[:, None] * inv[None, :]
        return jnp.cos(ang), jnp.sin(ang)

    cr, sr = cs(pos_r)
    cc, sc = cs(pos_c)
    zero = jnp.zeros_like(sr)
    cos = jnp.concatenate([cr, cr, cc, cc], axis=1)
    sin_lo = jnp.concatenate([-sr, zero, -sc, zero], axis=1)
    sin_hi = jnp.concatenate([zero, sr, zero, sc], axis=1)
    rep = LANES // HEAD_DIM
    return tuple(jnp.tile(t, (1, rep)) for t in (cos, sin_lo, sin_hi))


def _sink_attention(sink_ref, q_ref, k_all, bias_t, v_all, o_ref):
    assert KD == LANES == 2 * HEAD_DIM and GQA_GROUP == 4
    m = q_ref.shape[0]
    lane = lax.broadcasted_iota(jnp.int32, (1, LANES), 1)
    sub = lax.broadcasted_iota(jnp.int32, (LANES, 1), 0)
    v_t = v_all.T
    k_swapped = pltpu.roll(k_all, HEAD_DIM, axis=1)
    o_t = {}
    for g in range(N_KV_HEADS):
        k_low, k_high = (k_all, k_swapped) if g == 0 else (k_swapped, k_all)
        kz_even = jnp.where(lane < HEAD_DIM, k_low, 0.0).astype(BF16)
        kz_odd = jnp.where(lane >= HEAD_DIM, k_high, 0.0).astype(BF16)
        qq = jnp.concatenate([q_ref[:, (2 * g) * LANES:(2 * g + 1) * LANES],
                              q_ref[:, (2 * g + 1) * LANES:(2 * g + 2) * LANES]], axis=0)
        s = jnp.concatenate([_bdot_nt(kz_even, qq), _bdot_nt(kz_odd, qq)], axis=1)
        heads = [4 * g, 4 * g + 2, 4 * g + 1, 4 * g + 3]
        if bias_t is not None:
            nb = bias_t.shape[0]
            s = jnp.concatenate([s[:nb] + jnp.concatenate([bias_t] * GQA_GROUP, axis=1), s[nb:]],
                                axis=0)
        sink = jnp.concatenate([jnp.full((1, m), sink_ref[h] * LOG2E, F32) for h in heads], axis=1)
        mx = jnp.maximum(jnp.max(s, axis=0, keepdims=True), sink)
        p = jnp.exp2(s - mx).astype(BF16)
        own = (sub < HEAD_DIM) if g == 0 else (sub >= HEAD_DIM)
        v_ext_t = jnp.where(own, v_t, 1.0).astype(BF16)
        oe = jnp.dot(v_ext_t, p, preferred_element_type=F32)
        other = (1 - g) * HEAD_DIM
        den = oe[other:other + 1] + jnp.exp2(sink - mx)
        o_g = oe[g * HEAD_DIM:(g + 1) * HEAD_DIM] / den
        for i, h in enumerate(heads):
            o_t[h] = o_g[:, i * m:(i + 1) * m]
    for j in range(N_Q_HEADS // 2):
        pair_t = jnp.concatenate([o_t[2 * j], o_t[2 * j + 1]], axis=0)
        o_ref[:, j * LANES:(j + 1) * LANES] = pair_t.T.astype(BF16)


def _ctx_attn_kernel(sink_ref, q_ref, kv_ref, o_ref):
    _sink_attention(sink_ref, q_ref, kv_ref[:, :KD], None, kv_ref[:, KD:], o_ref)


def _ctx_attention(grp, q, kv, sink):
    n = grp.n
    return pl.pallas_call(
        _ctx_attn_kernel,
        grid=(grp.b,),
        in_specs=[pl.BlockSpec(memory_space=pltpu.SMEM), _row_spec(n, QD), _row_spec(n, 2 * KD)],
        out_specs=_row_spec(n, QD),
        out_shape=jax.ShapeDtypeStruct((grp.t, QD), BF16),
        compiler_params=_cparams("arbitrary"),
        name="ctx_attention",
    )(sink, q, kv)


def _lat_attn_kernel(sink_ref, q_ref, kvp_ref, kvc_ref, kvn_ref, kc_ref, vc_ref, o_ref, *, n):
    blk = pl.program_id(1)
    kv = jnp.concatenate([kvp_ref[...], kvc_ref[...], kvn_ref[...]], axis=0)
    kj = lax.broadcasted_iota(jnp.int32, (3 * ATTN_BLOCK, ATTN_BLOCK), 0)
    qi = lax.broadcasted_iota(jnp.int32, (3 * ATTN_BLOCK, ATTN_BLOCK), 1)
    rel = kj - ATTN_BLOCK - qi
    kpos = (blk - 1) * ATTN_BLOCK + kj
    valid = (jnp.abs(rel) <= WINDOW) & (kpos >= 0) & (kpos < n)
    bias_t = jnp.where(valid, 0.0, NEG_INF)
    k_all = jnp.concatenate([kv[:, :KD], kc_ref[...]], axis=0)
    v_all = jnp.concatenate([kv[:, KD:], vc_ref[...]], axis=0)
    _sink_attention(sink_ref, q_ref, k_all, bias_t, v_all, o_ref)


def _lat_attention(grp, q, kv, k_ctx, v_ctx, sink):
    nb = grp.n // ATTN_BLOCK
    past = k_ctx.shape[1]

    def band(off):
        return pl.BlockSpec((ATTN_BLOCK, 2 * KD),
                            lambda b, i: (b * nb + jnp.clip(i + off, 0, nb - 1), 0))

    ctx_spec = pl.BlockSpec((None, past, KD), lambda b, i: (b, 0, 0))
    return pl.pallas_call(
        functools.partial(_lat_attn_kernel, n=grp.n),
        grid=(grp.b, nb),
        in_specs=[pl.BlockSpec(memory_space=pltpu.SMEM),
                  pl.BlockSpec((ATTN_BLOCK, QD), lambda b, i: (b * nb + i, 0)),
                  band(-1), band(0), band(1), ctx_spec, ctx_spec],
        out_specs=pl.BlockSpec((ATTN_BLOCK, QD), lambda b, i: (b * nb + i, 0)),
        out_shape=jax.ShapeDtypeStruct((grp.t, QD), BF16),
        compiler_params=_cparams("arbitrary", "arbitrary"),
        name="lat_attention",
    )(sink, q, kv, kv, kv, k_ctx, v_ctx)


def _postmix0_kernel(zc_ref, zp_ref, zn_ref, at_ref, x_ref, mod_ref, g_ref, cw_ref, w_ref,
                     o_ref, *, n, tm):
    c = CONV_DIM
    zc = zc_ref[...]
    u = zc[:, c:2 * c] * zc[:, 2 * c:]
    u_before = zp_ref[7:8, c:2 * c] * zp_ref[7:8, 2 * c:]
    u_after = zn_ref[0:1, c:2 * c] * zn_ref[0:1, 2 * c:]
    row = lax.broadcasted_iota(jnp.int32, (tm, 1), 0)
    pos = (pl.program_id(0) * tm + row) % n
    u_prev = jnp.where(row == 0, u_before, pltpu.roll(u, 1, axis=0))
    u_prev = jnp.where(pos == 0, 0.0, u_prev)
    u_next = jnp.where(row == tm - 1, u_after, pltpu.roll(u, tm - 1, axis=0))
    u_next = jnp.where(pos == n - 1, 0.0, u_next)
    conv = u_prev * cw_ref[0:1, :] + u * cw_ref[1:2, :] + u_next * cw_ref[2:3, :]
    mix = _bdot(zc[:, :c] * conv, w_ref[:c, :]) + _bdot(at_ref[...], w_ref[c:, :])
    gt1 = _mod_chunk(mod_ref, 2)
    o_ref[...] = x_ref[...] + gt1 * _rms(mix, g_ref[1:2, :])


def _postmix0(grp, zc, attn, x2d, mod, norm_g, conv_w, w_out, tm):
    r8 = tm // 8
    last8 = grp.t // 8 - 1
    return pl.pallas_call(
        functools.partial(_postmix0_kernel, n=grp.n, tm=tm),
        grid=(grp.t // tm,),
        in_specs=[_row_spec(tm, 3 * CONV_DIM),
                  pl.BlockSpec((8, 3 * CONV_DIM), lambda t: (jnp.maximum(t * r8 - 1, 0), 0)),
                  pl.BlockSpec((8, 3 * CONV_DIM), lambda t: (jnp.minimum((t + 1) * r8, last8), 0)),
                  _row_spec(tm, QD), _row_spec(tm, D_MODEL), grp.mod_spec(tm),
                  _const_spec((4, D_MODEL)), _const_spec((3, CONV_DIM)),
                  _const_spec((CONV_DIM + QD, D_MODEL))],
        out_specs=_row_spec(tm, D_MODEL),
        out_shape=jax.ShapeDtypeStruct((grp.t, D_MODEL), F32),
        compiler_params=_cparams("arbitrary"),
        name="postmix0",
    )(zc, zc, zc, attn, x2d, mod, norm_g, conv_w, w_out)


def _mlp_kernel(x_ref, mod_ref, g_ref, w1_ref, w2_ref, o_ref, *, fchunk):
    x = x_ref[...]
    sh2, sc2, gt2 = _mod_chunk(mod_ref, 3), _mod_chunk(mod_ref, 4), _mod_chunk(mod_ref, 5)
    hb = (_rms(x, g_ref[2:3, :]) * (1.0 + sc2) + sh2).astype(BF16)
    acc = None
    for j in range(D_FF // fchunk):
        a = jnp.dot(hb, w1_ref[:, j * fchunk:(j + 1) * fchunk], preferred_element_type=F32)
        a = jnp.maximum(a, 0.0)
        part = _bdot(a * a, w2_ref[j * fchunk:(j + 1) * fchunk, :])
        acc = part if acc is None else acc + part
    o_ref[...] = x + gt2 * _rms(acc, g_ref[3:4, :])


def _mlp(grp, x2d, mod, norm_g, w1, w2, tm):
    single = pl.Buffered(1)
    return pl.pallas_call(
        functools.partial(_mlp_kernel, fchunk=1024),
        grid=(grp.t // tm,),
        in_specs=[_row_spec(tm, D_MODEL), grp.mod_spec(tm), _const_spec((4, D_MODEL)),
                  pl.BlockSpec((D_MODEL, D_FF), lambda t: (0, 0), pipeline_mode=single),
                  pl.BlockSpec((D_FF, D_MODEL), lambda t: (0, 0), pipeline_mode=single)],
        out_specs=_row_spec(tm, D_MODEL),
        out_shape=jax.ShapeDtypeStruct((grp.t, D_MODEL), F32),
        compiler_params=_cparams("arbitrary"),
        name="mlp",
    )(x2d, mod, norm_g, w1, w2)


def _premix1_kernel(x_ref, mod_ref, g_ref, w_ref, wg_ref, gb_ref,
                    q_ref, k_ref, v_ref, og_ref, gf_ref, gbk_ref):
    sh1, sc1 = _mod_chunk(mod_ref, 0), _mod_chunk(mod_ref, 1)
    hb = (_rms(x_ref[...], g_ref[0:1, :]) * (1.0 + sc1) + sh1).astype(BF16)
    z = jnp.dot(hb, w_ref[...], preferred_element_type=F32)
    q_ref[...] = z[:, :GLA_QK] * (GLA_DK ** -0.5)
    k_ref[...] = z[:, GLA_QK:2 * GLA_QK]
    v_ref[...] = z[:, 2 * GLA_QK:2 * GLA_QK + GLA_VD].astype(BF16)
    og_ref[...] = z[:, 2 * GLA_QK + GLA_VD:GLA_MAIN]
    r = z[:, GLA_MAIN:]
    pre = _bdot(r, wg_ref[...]) + gb_ref[...]
    gate = ((jnp.minimum(pre, 0.0) - jnp.log1p(jnp.exp(-jnp.abs(pre))))
            * (LOG2E / GLA_GATE_NORM))
    gf_ref[...] = gate[:, :GLA_QK]
    gbk_ref[...] = gate[:, GLA_QK:]


def _premix1(grp, x2d, mod, norm_g, w_in, w_gate, gate_bias, tm):
    t = grp.t
    return pl.pallas_call(
        _premix1_kernel,
        grid=(t // tm,),
        in_specs=[_row_spec(tm, D_MODEL), grp.mod_spec(tm), _const_spec((4, D_MODEL)),
                  _const_spec((D_MODEL, GLA_MAIN + LANES)), _const_spec((LANES, 2 * GLA_QK)),
                  _const_spec((1, 2 * GLA_QK))],
        out_specs=[_row_spec(tm, GLA_QK), _row_spec(tm, GLA_QK), _row_spec(tm, GLA_VD),
                   _row_spec(tm, GLA_VD), _row_spec(tm, GLA_QK), _row_spec(tm, GLA_QK)],
        out_shape=[jax.ShapeDtypeStruct((t, GLA_QK), F32), jax.ShapeDtypeStruct((t, GLA_QK), F32),
                   jax.ShapeDtypeStruct((t, GLA_VD), BF16),
                   jax.ShapeDtypeStruct((t, GLA_VD), F32), jax.ShapeDtypeStruct((t, GLA_QK), F32),
                   jax.ShapeDtypeStruct((t, GLA_QK), F32)],
        compiler_params=_cparams("arbitrary"),
        name="premix1",
    )(x2d, mod, norm_g, w_in, w_gate, gate_bias)


def _split3(x):
    hi = x.astype(BF16)
    r1 = x - hi.astype(F32)
    mid = r1.astype(BF16)
    lo = (r1 - mid.astype(F32)).astype(BF16)
    return hi, mid, lo


def _block_boundary(b, s, reverse):
    idx = s if reverse else s - 1
    if 2 * s >= 8:
        n = GLA_TILE // (2 * s)
        r = b.reshape(n, 2 * s, GLA_DK)[:, idx:idx + 1, :]
        return jnp.broadcast_to(r, (n, 2 * s, GLA_DK)).reshape(GLA_TILE, GLA_DK)
    assert s == 2
    b8 = b.reshape(GLA_TILE // 8, 8, GLA_DK)
    sub = lax.broadcasted_iota(jnp.int32, (1, 8, 1), 1)
    r = jnp.where(sub < 4, b8[:, idx:idx + 1, :], b8[:, 4 + idx:5 + idx, :])
    return r.reshape(GLA_TILE, GLA_DK)


def _gla_tile(q, k, g, v, tri, lvl, s_ref, reverse):
    half = GLA_TILE // 2
    b = None
    for part in _split3(g):
        term = jnp.dot(tri, part, preferred_element_type=F32)
        b = term if b is None else b + term
    edge = 0 if reverse else GLA_TILE - 1
    b_last = b[edge:edge + 1, :]
    qe = (q * jnp.exp2(b)).astype(BF16)
    ke = (k * jnp.exp2(b_last - b)).astype(BF16)

    lo_rows, hi_rows = slice(0, half), slice(half, GLA_TILE)
    qb, kb = q.astype(BF16), k.astype(BF16)
    blocks = [jnp.where(lvl == 0, _bdot_nt(qb[rows], kb[rows]), 0.0) for rows in (lo_rows, hi_rows)]
    cross = None
    row = lax.broadcasted_iota(jnp.int32, (GLA_TILE, 1), 0)
    s, level = 1, 1
    while s < GLA_TILE:
        if s == 1:
            arg = jnp.where(row % 2 == (0 if reverse else 1), g, 0.0)
        else:
            arg = -jnp.abs(b - _block_boundary(b, s, reverse))
        f = jnp.exp2(arg)
        qf, kf = (q * f).astype(BF16), (k * f).astype(BF16)
        if s == half:
            q_rows, k_rows = (lo_rows, hi_rows) if reverse else (hi_rows, lo_rows)
            cross = _bdot_nt(qf[q_rows], kf[k_rows])
        else:
            blocks = [jnp.where(lvl == level, _bdot_nt(qf[rows], kf[rows]), blk)
                      for rows, blk in zip((lo_rows, hi_rows), blocks)]
        s, level = 2 * s, level + 1

    o_lo = _bdot(blocks[0], v[lo_rows])
    o_hi = _bdot(blocks[1], v[hi_rows])
    if reverse:
        o_lo = o_lo + _bdot(cross, v[hi_rows])
    else:
        o_hi = o_hi + _bdot(cross, v[lo_rows])
    st = s_ref[...]
    o = jnp.concatenate([o_lo, o_hi], axis=0) + _bdot_nt(qe, st)
    s_ref[...] = st * jnp.exp2(b_last) + lax.dot_general(
        v, ke, TN_DIMS, preferred_element_type=F32)
    return o


def _gla_kernel(*refs, zero_init, nt):
    if zero_init:
        (tri_ref, lvl_ref, q_ref, k_ref, gf_ref, gb_ref, v_ref,
         o_ref, sf_ref, sb_ref) = refs
        sf_ref[...] = jnp.zeros_like(sf_ref)
        sb_ref[...] = jnp.zeros_like(sb_ref)
    else:
        (tri_ref, lvl_ref, q_ref, k_ref, gf_ref, gb_ref, v_ref, s0f_ref, s0b_ref,
         o_ref, sf_ref, sb_ref) = refs
        sf_ref[...] = s0f_ref[...]
        sb_ref[...] = s0b_ref[...]
    o_ref[...] = jnp.zeros_like(o_ref)

    def step(t, carry):
        for d, (g_ref, s_ref) in enumerate(((gf_ref, sf_ref), (gb_ref, sb_ref))):
            tile = nt - 1 - t if d else t
            r0 = tile * GLA_TILE
            if not isinstance(r0, int):
                r0 = pl.multiple_of(r0, GLA_TILE)
            rows = pl.ds(r0, GLA_TILE)
            o = _gla_tile(q_ref[rows, :], k_ref[rows, :], g_ref[rows, :], v_ref[rows, :],
                          tri_ref[d], lvl_ref[d], s_ref, bool(d))
            o_ref[rows, :] += o
        return carry

    if nt == 1:
        step(0, 0)
    else:
        lax.fori_loop(0, nt, step, 0)


def _gla_scan(grp, q, k, v, gf, gb, s0f_t, s0b_t, tri, lvl):
    n = grp.n
    nt = n // GLA_TILE
    zero_init = s0f_t is None
    half = GLA_TILE // 2
    seq_dk = pl.BlockSpec((n, GLA_DK), lambda b, h: (b, h))
    seq_dv = pl.BlockSpec((n, GLA_DV), lambda b, h: (b, h))
    state_spec = pl.BlockSpec((None, None, GLA_DV, GLA_DK), lambda b, h: (b, h, 0, 0))
    in_specs = [_const_spec((2, GLA_TILE, GLA_TILE)), _const_spec((2, half, half)),
                seq_dk, seq_dk, seq_dk, seq_dk, seq_dv]
    args = [tri, lvl, q, k, gf, gb, v]
    if not zero_init:
        in_specs += [state_spec, state_spec]
        args += [s0f_t, s0b_t]
    state_shape = jax.ShapeDtypeStruct((grp.b, GLA_HEADS, GLA_DV, GLA_DK), F32)
    return pl.pallas_call(
        functools.partial(_gla_kernel, zero_init=zero_init, nt=nt),
        grid=(grp.b, GLA_HEADS),
        in_specs=in_specs,
        out_specs=[seq_dv, state_spec, state_spec],
        out_shape=[jax.ShapeDtypeStruct((grp.t, GLA_VD), F32), state_shape, state_shape],
        compiler_params=_cparams("arbitrary", "arbitrary"),
        name="gla_scan",
    )(*args)


def _gla_constants():
    half = GLA_TILE // 2
    i = jnp.arange(GLA_TILE)[:, None]
    j = jnp.arange(GLA_TILE)[None, :]
    tri = jnp.stack([j <= i, j >= i]).astype(BF16)
    ih, jh = i[:half], j[:, :half]
    x = jnp.bitwise_xor(ih, jh)
    level = sum((x >= (1 << p)).astype(jnp.int32) for p in range(half.bit_length() - 1))
    lvl = jnp.stack([jnp.where(jh <= ih, level, -1), jnp.where(jh >= ih, level, -1)])
    return tri, lvl


def _postmix1_kernel(o_ref, og_ref, x_ref, mod_ref, g_ref, gn_ref, w_ref, out_ref):
    gn = gn_ref[...]
    mix = None
    for h in range(GLA_HEADS):
        cols = slice(h * GLA_DV, (h + 1) * GLA_DV)
        o = _rms(o_ref[:, cols], gn)
        og = og_ref[:, cols]
        y = o * (og / (1.0 + jnp.exp(-og)))
        part = _bdot(y, w_ref[cols, :])
        mix = part if mix is None else mix + part
    gt1 = _mod_chunk(mod_ref, 2)
    out_ref[...] = x_ref[...] + gt1 * _rms(mix, g_ref[1:2, :])


def _postmix1(grp, o, og, x2d, mod, norm_g, gla_norm_g, w_out, tm):
    return pl.pallas_call(
        _postmix1_kernel,
        grid=(grp.t // tm,),
        in_specs=[_row_spec(tm, GLA_VD), _row_spec(tm, GLA_VD),
                  _row_spec(tm, D_MODEL), grp.mod_spec(tm), _const_spec((4, D_MODEL)),
                  _const_spec((1, GLA_DV)), _const_spec((GLA_VD, D_MODEL))],
        out_specs=_row_spec(tm, D_MODEL),
        out_shape=jax.ShapeDtypeStruct((grp.t, D_MODEL), F32),
        compiler_params=_cparams("arbitrary"),
        name="postmix1",
    )(o, og, x2d, mod, norm_g, gla_norm_g, w_out)


def _run_group(grp, x, mods, p, rope_tabs, k_ctx, v_ctx, s0f_t, s0b_t, tm):
    x2d = x.reshape(grp.t, D_MODEL)
    zc, q, kv = _premix0(grp, x2d, mods[0], p["norm_g"][0], p["ab_w_in"], rope_tabs, tm)
    if k_ctx is None:
        attn = _ctx_attention(grp, q, kv, p["sink"])
    else:
        attn = _lat_attention(grp, q, kv, k_ctx, v_ctx, p["sink"])
    x2d = _postmix0(grp, zc, attn, x2d, mods[0], p["norm_g"][0], p["conv_w"], p["ab_w_out"], tm)
    x2d = _mlp(grp, x2d, mods[0], p["norm_g"][0], p["mlp_w1"][0], p["mlp_w2"][0], tm)
    gq, gk, gv, og, gf, gb = _premix1(grp, x2d, mods[1], p["norm_g"][1], p["gla_w_in"],
                                      p["gla_w_gate"], p["gla_gate_bias"], tm)
    o, sf_t, sb_t = _gla_scan(grp, gq, gk, gv, gf, gb, s0f_t, s0b_t, p["tri"], p["lvl"])
    x2d = _postmix1(grp, o, og, x2d, mods[1], p["norm_g"][1], p["gla_norm_g"], p["gla_w_out"], tm)
    x2d = _mlp(grp, x2d, mods[1], p["norm_g"][1], p["mlp_w1"][1], p["mlp_w2"][1], tm)
    return x2d.reshape(x.shape), kv, sf_t, sb_t


def kernel(x_prompt, x_sample, cache_k, cache_v, state_fwd, state_bwd, c, c_ctx, mod_w, mod_b,
           norm_g, ab_w_in, conv_w, attn_sink, ab_w_out, gla_w_in, gla_gate_w, gla_gate_b,
           gla_norm_g, gla_w_out, mlp_w1, mlp_w2):
    b_ctx, n_ctx, _ = x_prompt.shape
    b_lat, n_lat, _ = x_sample.shape
    assert mod_w.shape[0] == 2 and ab_w_in.shape[0] == 1 and gla_w_in.shape[0] == 1
    assert 1 + b_lat <= 8

    cond8 = jnp.zeros((8, D_MODEL), F32).at[0].set(c_ctx).at[1:1 + b_lat].set(c)
    mod = _modulation(cond8, mod_w, mod_b)
    mods_ctx = [mod[l, 0:1].reshape(1, 1, -1) for l in range(2)]
    mods_lat = [mod[l, 1:1 + b_lat].reshape(b_lat, 1, -1) for l in range(2)]

    gw = gla_w_in[0]
    w_gate = jnp.zeros((LANES, 2 * GLA_QK), F32)
    w_gate = w_gate.at[:GLA_RANK, :GLA_QK].set(gla_gate_w[0, 0])
    w_gate = w_gate.at[GLA_RANK:2 * GLA_RANK, GLA_QK:].set(gla_gate_w[0, 1])
    tri, lvl = _gla_constants()
    p = {
        "norm_g": norm_g,
        "ab_w_in": ab_w_in[0].astype(BF16),
        "conv_w": conv_w[0],
        "sink": attn_sink[0],
        "ab_w_out": ab_w_out[0].astype(BF16),
        "mlp_w1": mlp_w1.astype(BF16),
        "mlp_w2": mlp_w2.astype(BF16),
        "gla_w_in": jnp.pad(gw, ((0, 0), (0, LANES - 2 * GLA_RANK))).astype(BF16),
        "gla_w_gate": w_gate.astype(BF16),
        "gla_gate_bias": gla_gate_b[0].reshape(1, 2 * GLA_QK),
        "gla_norm_g": gla_norm_g[0].reshape(1, GLA_DV),
        "gla_w_out": gla_w_out[0].astype(BF16),
        "tri": tri,
        "lvl": lvl,
    }

    ctx = _Group(b_ctx, n_ctx, per_seq_mod=False)
    lat = _Group(b_lat, n_lat, per_seq_mod=True)

    y_prompt, kv_ctx, sf_t, sb_t = _run_group(ctx, x_prompt, mods_ctx, p, None, None, None,
                                              None, None, tm=512)
    new_k = kv_ctx[:, :KD].reshape(b_ctx, 1, n_ctx, N_KV_HEADS, HEAD_DIM)
    new_v = kv_ctx[:, KD:].reshape(b_ctx, 1, n_ctx, N_KV_HEADS, HEAD_DIM)
    new_sf = jnp.swapaxes(sf_t, -1, -2)[:, None]
    new_sb = jnp.swapaxes(sb_t, -1, -2)[:, None]

    past = cache_k.shape[2]
    k_ctx = cache_k[:, 0].reshape(b_lat, past, KD)
    v_ctx = cache_v[:, 0].reshape(b_lat, past, KD)
    s0f_t = jnp.swapaxes(state_fwd[:, 0], -1, -2)
    s0b_t = jnp.swapaxes(state_bwd[:, 0], -1, -2)
    y_sample, _, _, _ = _run_group(lat, x_sample, mods_lat, p, _rope_tables(n_lat), k_ctx, v_ctx,
                                   s0f_t, s0b_t, tm=512)
    return (y_prompt, y_sample, new_k, new_v, new_sf, new_sb)
```

```python
import functools

import jax
import jax.numpy as jnp
from jax import lax
from jax.experimental import pallas as pl
from jax.experimental.pallas import tpu as pltpu

F32 = jnp.float32
BF16 = jnp.bfloat16

D_MODEL = 1024
MOD_CHUNKS = 6
EPS = 1e-6
CONV_DIM = 512
N_Q_HEADS = 8
N_KV_HEADS = 2
GQA_GROUP = 4
HEAD_DIM = 64
WINDOW = 128
ATTN_BLOCK = 128
GRID_W = 64
ROPE_BASE = 10000.0
QD = N_Q_HEADS * HEAD_DIM
KD = N_KV_HEADS * HEAD_DIM
AB_IN = 3 * CONV_DIM + QD + 2 * KD
GLA_HEADS = 4
GLA_DK = 128
GLA_DV = 256
GLA_RANK = 16
GLA_GATE_NORM = 16.0
GLA_TILE = 256
LOG2E = 1.4426950408889634
GLA_QK = GLA_HEADS * GLA_DK
GLA_VD = GLA_HEADS * GLA_DV
GLA_MAIN = 2 * GLA_QK + 2 * GLA_VD
D_FF = 4 * D_MODEL
NEG_INF = -1e30
LANES = 128
VMEM_LIMIT = 56 * 1024 * 1024

NT_DIMS = (((1,), (1,)), ((), ()))
TN_DIMS = (((0,), (0,)), ((), ()))


def _cparams(*sem):
    return pltpu.CompilerParams(dimension_semantics=sem, vmem_limit_bytes=VMEM_LIMIT)


def _bdot(a, b):
    return jnp.dot(a.astype(BF16), b.astype(BF16), preferred_element_type=F32)


def _bdot_nt(a, b):
    return lax.dot_general(a.astype(BF16), b.astype(BF16), NT_DIMS, preferred_element_type=F32)


def _rms(x, g):
    ms = jnp.mean(x * x, axis=-1, keepdims=True)
    return x * lax.rsqrt(ms + EPS) * g


def _mod_chunk(mod_ref, i):
    return mod_ref[:, i * D_MODEL:(i + 1) * D_MODEL]


def _const_spec(shape):
    return pl.BlockSpec(shape, lambda *_: (0,) * len(shape))


def _mod_kernel(cond_ref, w_ref, b_ref, o_ref):
    cnd = cond_ref[...]
    s = cnd / (1.0 + jnp.exp(-cnd))
    o_ref[...] = _bdot(s, w_ref[...]) + b_ref[...]


def _modulation(cond8, mod_w, mod_b):
    depth = mod_w.shape[0]
    n = mod_w.shape[2]
    tn = 1536
    return pl.pallas_call(
        _mod_kernel,
        grid=(depth, n // tn),
        in_specs=[
            pl.BlockSpec((8, D_MODEL), lambda l, j: (0, 0)),
            pl.BlockSpec((None, D_MODEL, tn), lambda l, j: (l, 0, j)),
            pl.BlockSpec((None, 1, tn), lambda l, j: (l, 0, j)),
        ],
        out_specs=pl.BlockSpec((None, 8, tn), lambda l, j: (l, 0, j)),
        out_shape=jax.ShapeDtypeStruct((depth, 8, n), F32),
        compiler_params=_cparams("arbitrary", "arbitrary"),
        name="modulation",
    )(cond8, mod_w, mod_b.reshape(depth, 1, n))


class _Group:
    def __init__(self, b, n, per_seq_mod):
        self.b, self.n, self.t = b, n, b * n
        self.per_seq_mod = per_seq_mod

    def mod_spec(self, tm):
        if self.per_seq_mod:
            assert self.n % tm == 0
            per = self.n // tm
            return pl.BlockSpec((None, 1, MOD_CHUNKS * D_MODEL), lambda t: (t // per, 0, 0))
        return pl.BlockSpec((None, 1, MOD_CHUNKS * D_MODEL), lambda t: (0, 0, 0))


def _row_spec(tm, width):
    return pl.BlockSpec((tm, width), lambda t: (t, 0))


def _rope(x, cos, sin_lo, sin_hi):
    return (x * cos + pltpu.roll(x, LANES - 16, axis=1) * sin_lo
            + pltpu.roll(x, 16, axis=1) * sin_hi)


def _premix0_kernel(*refs, rope):
    if rope:
        x_ref, mod_ref, g_ref, w_ref, cos_ref, slo_ref, shi_ref, zc_ref, q_ref, kv_ref = refs
    else:
        x_ref, mod_ref, g_ref, w_ref, zc_ref, q_ref, kv_ref = refs
    sh1, sc1 = _mod_chunk(mod_ref, 0), _mod_chunk(mod_ref, 1)
    h = _rms(x_ref[...], g_ref[0:1, :]) * (1.0 + sc1) + sh1
    z = _bdot(h, w_ref[...])
    c3 = 3 * CONV_DIM
    zc_ref[...] = z[:, :c3]
    scale = HEAD_DIM ** -0.5 * LOG2E
    if rope:
        cos, slo, shi = cos_ref[...], slo_ref[...], shi_ref[...]
        for j in range(QD // LANES):
            qs = z[:, c3 + j * LANES:c3 + (j + 1) * LANES]
            q_ref[:, j * LANES:(j + 1) * LANES] = (_rope(qs, cos, slo, shi) * scale).astype(BF16)
        kv_ref[:, :KD] = _rope(z[:, c3 + QD:c3 + QD + KD], cos, slo, shi)
    else:
        q_ref[...] = (z[:, c3:c3 + QD] * scale).astype(BF16)
        kv_ref[:, :KD] = z[:, c3 + QD:c3 + QD + KD]
    kv_ref[:, KD:] = z[:, c3 + QD + KD:]


def _premix0(grp, x2d, mod, norm_g, w_in, rope_tabs, tm):
    rope = rope_tabs is not None
    in_specs = [_row_spec(tm, D_MODEL), grp.mod_spec(tm), _const_spec((4, D_MODEL)),
                _const_spec((D_MODEL, AB_IN))]
    args = [x2d, mod, norm_g, w_in]
    if rope:
        per = grp.n // tm
        in_specs += [pl.BlockSpec((tm, LANES), lambda t: (t % per, 0))] * 3
        args += list(rope_tabs)
    return pl.pallas_call(
        functools.partial(_premix0_kernel, rope=rope),
        grid=(grp.t // tm,),
        in_specs=in_specs,
        out_specs=[_row_spec(tm, 3 * CONV_DIM), _row_spec(tm, QD), _row_spec(tm, 2 * KD)],
        out_shape=[jax.ShapeDtypeStruct((grp.t, 3 * CONV_DIM), F32),
                   jax.ShapeDtypeStruct((grp.t, QD), BF16),
                   jax.ShapeDtypeStruct((grp.t, 2 * KD), F32)],
        compiler_params=_cparams("arbitrary"),
        name="premix0_rope" if rope else "premix0",
    )(*args)


def _rope_tables(n):
    rows = n // GRID_W
    pos_r = jnp.repeat(jnp.arange(rows), GRID_W)
    pos_c = jnp.tile(jnp.arange(GRID_W), rows)
    half = HEAD_DIM // 2
    quarter = half // 2
    inv = ROPE_BASE ** (-(jnp.arange(quarter, dtype=F32) * 2.0 / half))

    def cs(pos):
        ang = pos.astype(F32)[:, None] * inv[None, :]
        return jnp.cos(ang), jnp.sin(ang)

    cr, sr = cs(pos_r)
    cc, sc = cs(pos_c)
    zero = jnp.zeros_like(sr)
    cos = jnp.concatenate([cr, cr, cc, cc], axis=1)
    sin_lo = jnp.concatenate([-sr, zero, -sc, zero], axis=1)
    sin_hi = jnp.concatenate([zero, sr, zero, sc], axis=1)
    rep = LANES // HEAD_DIM
    return tuple(jnp.tile(t, (1, rep)) for t in (cos, sin_lo, sin_hi))


def _sink_attention(sink_ref, q_ref, k_all, bias_t, v_all, o_ref):
    assert KD == LANES == 2 * HEAD_DIM and GQA_GROUP == 4
    m = q_ref.shape[0]
    lane = lax.broadcasted_iota(jnp.int32, (1, LANES), 1)
    sub = lax.broadcasted_iota(jnp.int32, (LANES, 1), 0)
    v_t = v_all.T
    k_swapped = pltpu.roll(k_all, HEAD_DIM, axis=1)
    o_t = {}
    for g in range(N_KV_HEADS):
        k_low, k_high = (k_all, k_swapped) if g == 0 else (k_swapped, k_all)
        kz_even = jnp.where(lane < HEAD_DIM, k_low, 0.0).astype(BF16)
        kz_odd = jnp.where(lane >= HEAD_DIM, k_high, 0.0).astype(BF16)
        qq = jnp.concatenate([q_ref[:, (2 * g) * LANES:(2 * g + 1) * LANES],
                              q_ref[:, (2 * g + 1) * LANES:(2 * g + 2) * LANES]], axis=0)
        s = jnp.concatenate([_bdot_nt(kz_even, qq), _bdot_nt(kz_odd, qq)], axis=1)
        heads = [4 * g, 4 * g + 2, 4 * g + 1, 4 * g + 3]
        if bias_t is not None:
            nb = bias_t.shape[0]
            s = jnp.concatenate([s[:nb] + jnp.concatenate([bias_t] * GQA_GROUP, axis=1), s[nb:]],
                                axis=0)
        sink = jnp.concatenate([jnp.full((1, m), sink_ref[h] * LOG2E, F32) for h in heads], axis=1)
        mx = jnp.maximum(jnp.max(s, axis=0, keepdims=True), sink)
        p = jnp.exp2(s - mx).astype(BF16)
        own = (sub < HEAD_DIM) if g == 0 else (sub >= HEAD_DIM)
        v_ext_t = jnp.where(own, v_t, 1.0).astype(BF16)
        oe = jnp.dot(v_ext_t, p, preferred_element_type=F32)
        other = (1 - g) * HEAD_DIM
        den = oe[other:other + 1] + jnp.exp2(sink - mx)
        o_g = oe[g * HEAD_DIM:(g + 1) * HEAD_DIM] / den
        for i, h in enumerate(heads):
            o_t[h] = o_g[:, i * m:(i + 1) * m]
    for j in range(N_Q_HEADS // 2):
        pair_t = jnp.concatenate([o_t[2 * j], o_t[2 * j + 1]], axis=0)
        o_ref[:, j * LANES:(j + 1) * LANES] = pair_t.T.astype(BF16)


def _ctx_attn_kernel(sink_ref, q_ref, kv_ref, o_ref):
    _sink_attention(sink_ref, q_ref, kv_ref[:, :KD], None, kv_ref[:, KD:], o_ref)


def _ctx_attention(grp, q, kv, sink):
    n = grp.n
    return pl.pallas_call(
        _ctx_attn_kernel,
        grid=(grp.b,),
        in_specs=[pl.BlockSpec(memory_space=pltpu.SMEM), _row_spec(n, QD), _row_spec(n, 2 * KD)],
        out_specs=_row_spec(n, QD),
        out_shape=jax.ShapeDtypeStruct((grp.t, QD), BF16),
        compiler_params=_cparams("arbitrary"),
        name="ctx_attention",
    )(sink, q, kv)


def _lat_attn_kernel(sink_ref, q_ref, kvp_ref, kvc_ref, kvn_ref, kc_ref, vc_ref, o_ref, *, n):
    blk = pl.program_id(1)
    kv = jnp.concatenate([kvp_ref[...], kvc_ref[...], kvn_ref[...]], axis=0)
    kj = lax.broadcasted_iota(jnp.int32, (3 * ATTN_BLOCK, ATTN_BLOCK), 0)
    qi = lax.broadcasted_iota(jnp.int32, (3 * ATTN_BLOCK, ATTN_BLOCK), 1)
    rel = kj - ATTN_BLOCK - qi
    kpos = (blk - 1) * ATTN_BLOCK + kj
    valid = (jnp.abs(rel) <= WINDOW) & (kpos >= 0) & (kpos < n)
    bias_t = jnp.where(valid, 0.0, NEG_INF)
    k_all = jnp.concatenate([kv[:, :KD], kc_ref[...]], axis=0)
    v_all = jnp.concatenate([kv[:, KD:], vc_ref[...]], axis=0)
    _sink_attention(sink_ref, q_ref, k_all, bias_t, v_all, o_ref)


def _lat_attention(grp, q, kv, k_ctx, v_ctx, sink):
    nb = grp.n // ATTN_BLOCK
    past = k_ctx.shape[1]

    def band(off):
        return pl.BlockSpec((ATTN_BLOCK, 2 * KD),
                            lambda b, i: (b * nb + jnp.clip(i + off, 0, nb - 1), 0))

    ctx_spec = pl.BlockSpec((None, past, KD), lambda b, i: (b, 0, 0))
    return pl.pallas_call(
        functools.partial(_lat_attn_kernel, n=grp.n),
        grid=(grp.b, nb),
        in_specs=[pl.BlockSpec(memory_space=pltpu.SMEM),
                  pl.BlockSpec((ATTN_BLOCK, QD), lambda b, i: (b * nb + i, 0)),
                  band(-1), band(0), band(1), ctx_spec, ctx_spec],
        out_specs=pl.BlockSpec((ATTN_BLOCK, QD), lambda b, i: (b * nb + i, 0)),
        out_shape=jax.ShapeDtypeStruct((grp.t, QD), BF16),
        compiler_params=_cparams("arbitrary", "arbitrary"),
        name="lat_attention",
    )(sink, q, kv, kv, kv, k_ctx, v_ctx)


def _postmix0_kernel(zc_ref, zp_ref, zn_ref, at_ref, x_ref, mod_ref, g_ref, cw_ref, w_ref,
                     o_ref, *, n, tm):
    c = CONV_DIM
    zc = zc_ref[...]
    u = zc[:, c:2 * c] * zc[:, 2 * c:]
    u_before = zp_ref[7:8, c:2 * c] * zp_ref[7:8, 2 * c:]
    u_after = zn_ref[0:1, c:2 * c] * zn_ref[0:1, 2 * c:]
    row = lax.broadcasted_iota(jnp.int32, (tm, 1), 0)
    pos = (pl.program_id(0) * tm + row) % n
    u_prev = jnp.where(row == 0, u_before, pltpu.roll(u, 1, axis=0))
    u_prev = jnp.where(pos == 0, 0.0, u_prev)
    u_next = jnp.where(row == tm - 1, u_after, pltpu.roll(u, tm - 1, axis=0))
    u_next = jnp.where(pos == n - 1, 0.0, u_next)
    conv = u_prev * cw_ref[0:1, :] + u * cw_ref[1:2, :] + u_next * cw_ref[2:3, :]
    mix = _bdot(zc[:, :c] * conv, w_ref[:c, :]) + _bdot(at_ref[...], w_ref[c:, :])
    gt1 = _mod_chunk(mod_ref, 2)
    o_ref[...] = x_ref[...] + gt1 * _rms(mix, g_ref[1:2, :])


def _postmix0(grp, zc, attn, x2d, mod, norm_g, conv_w, w_out, tm):
    r8 = tm // 8
    last8 = grp.t // 8 - 1
    return pl.pallas_call(
        functools.partial(_postmix0_kernel, n=grp.n, tm=tm),
        grid=(grp.t // tm,),
        in_specs=[_row_spec(tm, 3 * CONV_DIM),
                  pl.BlockSpec((8, 3 * CONV_DIM), lambda t: (jnp.maximum(t * r8 - 1, 0), 0)),
                  pl.BlockSpec((8, 3 * CONV_DIM), lambda t: (jnp.minimum((t + 1) * r8, last8), 0)),
                  _row_spec(tm, QD), _row_spec(tm, D_MODEL), grp.mod_spec(tm),
                  _const_spec((4, D_MODEL)), _const_spec((3, CONV_DIM)),
                  _const_spec((CONV_DIM + QD, D_MODEL))],
        out_specs=_row_spec(tm, D_MODEL),
        out_shape=jax.ShapeDtypeStruct((grp.t, D_MODEL), F32),
        compiler_params=_cparams("arbitrary"),
        name="postmix0",
    )(zc, zc, zc, attn, x2d, mod, norm_g, conv_w, w_out)


def _mlp_kernel(x_ref, mod_ref, g_ref, w1_ref, w2_ref, o_ref, *, fchunk):
    x = x_ref[...]
    sh2, sc2, gt2 = _mod_chunk(mod_ref, 3), _mod_chunk(mod_ref, 4), _mod_chunk(mod_ref, 5)
    hb = (_rms(x, g_ref[2:3, :]) * (1.0 + sc2) + sh2).astype(BF16)
    acc = None
    for j in range(D_FF // fchunk):
        a = jnp.dot(hb, w1_ref[:, j * fchunk:(j + 1) * fchunk], preferred_element_type=F32)
        a = jnp.maximum(a, 0.0)
        part = _bdot(a * a, w2_ref[j * fchunk:(j + 1) * fchunk, :])
        acc = part if acc is None else acc + part
    o_ref[...] = x + gt2 * _rms(acc, g_ref[3:4, :])


def _mlp(grp, x2d, mod, norm_g, w1, w2, tm):
    single = pl.Buffered(1)
    return pl.pallas_call(
        functools.partial(_mlp_kernel, fchunk=1024),
        grid=(grp.t // tm,),
        in_specs=[_row_spec(tm, D_MODEL), grp.mod_spec(tm), _const_spec((4, D_MODEL)),
                  pl.BlockSpec((D_MODEL, D_FF), lambda t: (0, 0), pipeline_mode=single),
                  pl.BlockSpec((D_FF, D_MODEL), lambda t: (0, 0), pipeline_mode=single)],
        out_specs=_row_spec(tm, D_MODEL),
        out_shape=jax.ShapeDtypeStruct((grp.t, D_MODEL), F32),
        compiler_params=_cparams("arbitrary"),
        name="mlp",
    )(x2d, mod, norm_g, w1, w2)


def _premix1_kernel(x_ref, mod_ref, g_ref, w_ref, wg_ref, gb_ref,
                    q_ref, k_ref, v_ref, og_ref, gf_ref, gbk_ref):
    sh1, sc1 = _mod_chunk(mod_ref, 0), _mod_chunk(mod_ref, 1)
    hb = (_rms(x_ref[...], g_ref[0:1, :]) * (1.0 + sc1) + sh1).astype(BF16)
    z = jnp.dot(hb, w_ref[...], preferred_element_type=F32)
    q_ref[...] = z[:, :GLA_QK] * (GLA_DK ** -0.5)
    k_ref[...] = z[:, GLA_QK:2 * GLA_QK]
    v_ref[...] = z[:, 2 * GLA_QK:2 * GLA_QK + GLA_VD].astype(BF16)
    og_ref[...] = z[:, 2 * GLA_QK + GLA_VD:GLA_MAIN]
    r = z[:, GLA_MAIN:]
    pre = _bdot(r, wg_ref[...]) + gb_ref[...]
    gate = ((jnp.minimum(pre, 0.0) - jnp.log1p(jnp.exp(-jnp.abs(pre))))
            * (LOG2E / GLA_GATE_NORM))
    gf_ref[...] = gate[:, :GLA_QK]
    gbk_ref[...] = gate[:, GLA_QK:]


def _premix1(grp, x2d, mod, norm_g, w_in, w_gate, gate_bias, tm):
    t = grp.t
    return pl.pallas_call(
        _premix1_kernel,
        grid=(t // tm,),
        in_specs=[_row_spec(tm, D_MODEL), grp.mod_spec(tm), _const_spec((4, D_MODEL)),
                  _const_spec((D_MODEL, GLA_MAIN + LANES)), _const_spec((LANES, 2 * GLA_QK)),
                  _const_spec((1, 2 * GLA_QK))],
        out_specs=[_row_spec(tm, GLA_QK), _row_spec(tm, GLA_QK), _row_spec(tm, GLA_VD),
                   _row_spec(tm, GLA_VD), _row_spec(tm, GLA_QK), _row_spec(tm, GLA_QK)],
        out_shape=[jax.ShapeDtypeStruct((t, GLA_QK), F32), jax.ShapeDtypeStruct((t, GLA_QK), F32),
                   jax.ShapeDtypeStruct((t, GLA_VD), BF16),
                   jax.ShapeDtypeStruct((t, GLA_VD), F32), jax.ShapeDtypeStruct((t, GLA_QK), F32),
                   jax.ShapeDtypeStruct((t, GLA_QK), F32)],
        compiler_params=_cparams("arbitrary"),
        name="premix1",
    )(x2d, mod, norm_g, w_in, w_gate, gate_bias)


def _split3(x):
    hi = x.astype(BF16)
    r1 = x - hi.astype(F32)
    mid = r1.astype(BF16)
    lo = (r1 - mid.astype(F32)).astype(BF16)
    return hi, mid, lo


def _level_exponent(b, g, s, reverse):
    if s == 1:
        row = lax.broadcasted_iota(jnp.int32, (GLA_TILE, 1), 0)
        return jnp.where(row % 2 == (0 if reverse else 1), g, 0.0)
    idx = s if reverse else s - 1
    if s >= 8:
        n = GLA_TILE // (2 * s)
        b4 = b.reshape(n, 2, s, GLA_DK)
        first, second = b4[:, 0:1], b4[:, 1:2]
        r = (second[:, :, 0:1] if reverse else first[:, :, s - 1:s])
        parts = [first - r, r - second] if reverse else [r - first, second - r]
        return jnp.concatenate(parts, axis=1).reshape(GLA_TILE, GLA_DK)
    b8 = b.reshape(GLA_TILE // 8, 8, GLA_DK)
    sub = lax.broadcasted_iota(jnp.int32, (1, 8, 1), 1)
    if s == 4:
        r = b8[:, idx:idx + 1, :]
    else:
        assert s == 2
        r = jnp.where(sub < 4, b8[:, idx:idx + 1, :], b8[:, 4 + idx:5 + idx, :])
    in_second = (sub // s) % 2 == 1
    sign = jnp.where(in_second != reverse, 1.0, -1.0)
    return ((b8 - r) * sign).reshape(GLA_TILE, GLA_DK)


def _gla_tile(q, k, g, b, v, lvl, s_ref, reverse):
    half = GLA_TILE // 2
    edge = 0 if reverse else GLA_TILE - 1
    b_last = b[edge:edge + 1, :]
    qe = (q * jnp.exp2(b)).astype(BF16)
    ke = (k * jnp.exp2(b_last - b)).astype(BF16)

    lo, hi = slice(0, half), slice(half, GLA_TILE)
    qb, kb = q.astype(BF16), k.astype(BF16)
    pairs = _bdot_nt(qb, kb)
    blocks = [jnp.where(lvl == 0, pairs[lo, lo], 0.0), jnp.where(lvl == 0, pairs[hi, hi], 0.0)]
    cross = None
    s, level = 1, 1
    while s < GLA_TILE:
        f = jnp.exp2(_level_exponent(b, g, s, reverse)).astype(BF16)
        pairs = _bdot_nt(qb * f, kb * f)
        if s == half:
            cross = pairs[lo, hi] if reverse else pairs[hi, lo]
        else:
            blocks = [jnp.where(lvl == level, pairs[lo, lo], blocks[0]),
                      jnp.where(lvl == level, pairs[hi, hi], blocks[1])]
        s, level = 2 * s, level + 1

    zero = jnp.zeros((half, half), BF16)
    a_lo, a_hi, a_x = blocks[0].astype(BF16), blocks[1].astype(BF16), cross.astype(BF16)
    if reverse:
        att = jnp.concatenate([jnp.concatenate([a_lo, a_x], axis=1),
                               jnp.concatenate([zero, a_hi], axis=1)], axis=0)
    else:
        att = jnp.concatenate([jnp.concatenate([a_lo, zero], axis=1),
                               jnp.concatenate([a_x, a_hi], axis=1)], axis=0)
    st = s_ref[...]
    o = jnp.dot(att, v, preferred_element_type=F32) + _bdot_nt(qe, st)
    s_ref[...] = st * jnp.exp2(b_last) + lax.dot_general(
        v, ke, TN_DIMS, preferred_element_type=F32)
    return o


def _gla_kernel(*refs, zero_init, nt):
    if zero_init:
        (tri_ref, lvl_ref, q_ref, k_ref, gf_ref, gb_ref, v_ref,
         o_ref, sf_ref, sb_ref) = refs
        sf_ref[...] = jnp.zeros_like(sf_ref)
        sb_ref[...] = jnp.zeros_like(sb_ref)
    else:
        (tri_ref, lvl_ref, q_ref, k_ref, gf_ref, gb_ref, v_ref, s0f_ref, s0b_ref,
         o_ref, sf_ref, sb_ref) = refs
        sf_ref[...] = s0f_ref[...]
        sb_ref[...] = s0b_ref[...]
    o_ref[...] = jnp.zeros_like(o_ref)

    def tile_rows(tile):
        r0 = tile * GLA_TILE
        return pl.ds(r0 if isinstance(r0, int) else pl.multiple_of(r0, GLA_TILE), GLA_TILE)

    def step(t, carry):
        rows_f, rows_b = tile_rows(t), tile_rows(nt - 1 - t)
        g_f, g_b = gf_ref[rows_f, :], gb_ref[rows_b, :]
        c = None
        for part in _split3(jnp.concatenate([g_f, g_b], axis=1)):
            term = jnp.dot(tri_ref[...], part, preferred_element_type=F32)
            c = term if c is None else c + term
        c_b = c[:, GLA_DK:]
        b_f = c[:, :GLA_DK]
        b_b = (c_b[GLA_TILE - 1:GLA_TILE, :] - c_b) + g_b
        for d, (rows, g, b, s_ref) in enumerate(((rows_f, g_f, b_f, sf_ref),
                                                 (rows_b, g_b, b_b, sb_ref))):
            o = _gla_tile(q_ref[rows, :], k_ref[rows, :], g, b, v_ref[rows, :],
                          lvl_ref[d], s_ref, bool(d))
            o_ref[rows, :] += o
        return carry

    if nt == 1:
        step(0, 0)
    else:
        lax.fori_loop(0, nt, step, 0)


def _gla_scan(grp, q, k, v, gf, gb, s0f_t, s0b_t, tri, lvl):
    n = grp.n
    nt = n // GLA_TILE
    zero_init = s0f_t is None
    half = GLA_TILE // 2
    seq_dk = pl.BlockSpec((n, GLA_DK), lambda b, h: (b, h))
    seq_dv = pl.BlockSpec((n, GLA_DV), lambda b, h: (b, h))
    state_spec = pl.BlockSpec((None, None, GLA_DV, GLA_DK), lambda b, h: (b, h, 0, 0))
    in_specs = [_const_spec((GLA_TILE, GLA_TILE)), _const_spec((2, half, half)),
                seq_dk, seq_dk, seq_dk, seq_dk, seq_dv]
    args = [tri, lvl, q, k, gf, gb, v]
    if not zero_init:
        in_specs += [state_spec, state_spec]
        args += [s0f_t, s0b_t]
    state_shape = jax.ShapeDtypeStruct((grp.b, GLA_HEADS, GLA_DV, GLA_DK), F32)
    return pl.pallas_call(
        functools.partial(_gla_kernel, zero_init=zero_init, nt=nt),
        grid=(grp.b, GLA_HEADS),
        in_specs=in_specs,
        out_specs=[seq_dv, state_spec, state_spec],
        out_shape=[jax.ShapeDtypeStruct((grp.t, GLA_VD), F32), state_shape, state_shape],
        compiler_params=_cparams("arbitrary", "arbitrary"),
        name="gla_scan",
    )(*args)


def _gla_constants():
    half = GLA_TILE // 2
    i = jnp.arange(GLA_TILE)[:, None]
    j = jnp.arange(GLA_TILE)[None, :]
    tri = (j <= i).astype(BF16)
    ih, jh = i[:half], j[:, :half]
    x = jnp.bitwise_xor(ih, jh)
    level = sum((x >= (1 << p)).astype(jnp.int32) for p in range(half.bit_length() - 1))
    lvl = jnp.stack([jnp.where(jh <= ih, level, -1), jnp.where(jh >= ih, level, -1)])
    return tri, lvl


def _postmix1_kernel(o_ref, og_ref, x_ref, mod_ref, g_ref, gn_ref, w_ref, out_ref):
    gn = gn_ref[...]
    mix = None
    for h in range(GLA_HEADS):
        cols = slice(h * GLA_DV, (h + 1) * GLA_DV)
        o = _rms(o_ref[:, cols], gn)
        og = og_ref[:, cols]
        y = o * (og / (1.0 + jnp.exp(-og)))
        part = _bdot(y, w_ref[cols, :])
        mix = part if mix is None else mix + part
    gt1 = _mod_chunk(mod_ref, 2)
    out_ref[...] = x_ref[...] + gt1 * _rms(mix, g_ref[1:2, :])


def _postmix1(grp, o, og, x2d, mod, norm_g, gla_norm_g, w_out, tm):
    return pl.pallas_call(
        _postmix1_kernel,
        grid=(grp.t // tm,),
        in_specs=[_row_spec(tm, GLA_VD), _row_spec(tm, GLA_VD),
                  _row_spec(tm, D_MODEL), grp.mod_spec(tm), _const_spec((4, D_MODEL)),
                  _const_spec((1, GLA_DV)), _const_spec((GLA_VD, D_MODEL))],
        out_specs=_row_spec(tm, D_MODEL),
        out_shape=jax.ShapeDtypeStruct((grp.t, D_MODEL), F32),
        compiler_params=_cparams("arbitrary"),
        name="postmix1",
    )(o, og, x2d, mod, norm_g, gla_norm_g, w_out)


def _run_group(grp, x, mods, p, rope_tabs, k_ctx, v_ctx, s0f_t, s0b_t, tm):
    x2d = x.reshape(grp.t, D_MODEL)
    zc, q, kv = _premix0(grp, x2d, mods[0], p["norm_g"][0], p["ab_w_in"], rope_tabs, tm)
    if k_ctx is None:
        attn = _ctx_attention(grp, q, kv, p["sink"])
    else:
        attn = _lat_attention(grp, q, kv, k_ctx, v_ctx, p["sink"])
    x2d = _postmix0(grp, zc, attn, x2d, mods[0], p["norm_g"][0], p["conv_w"], p["ab_w_out"], tm)
    x2d = _mlp(grp, x2d, mods[0], p["norm_g"][0], p["mlp_w1"][0], p["mlp_w2"][0], tm)
    gq, gk, gv, og, gf, gb = _premix1(grp, x2d, mods[1], p["norm_g"][1], p["gla_w_in"],
                                      p["gla_w_gate"], p["gla_gate_bias"], tm)
    o, sf_t, sb_t = _gla_scan(grp, gq, gk, gv, gf, gb, s0f_t, s0b_t, p["tri"], p["lvl"])
    x2d = _postmix1(grp, o, og, x2d, mods[1], p["norm_g"][1], p["gla_norm_g"], p["gla_w_out"], tm)
    x2d = _mlp(grp, x2d, mods[1], p["norm_g"][1], p["mlp_w1"][1], p["mlp_w2"][1], tm)
    return x2d.reshape(x.shape), kv, sf_t, sb_t


def kernel(x_prompt, x_sample, cache_k, cache_v, state_fwd, state_bwd, c, c_ctx, mod_w, mod_b,
           norm_g, ab_w_in, conv_w, attn_sink, ab_w_out, gla_w_in, gla_gate_w, gla_gate_b,
           gla_norm_g, gla_w_out, mlp_w1, mlp_w2):
    b_ctx, n_ctx, _ = x_prompt.shape
    b_lat, n_lat, _ = x_sample.shape
    assert mod_w.shape[0] == 2 and ab_w_in.shape[0] == 1 and gla_w_in.shape[0] == 1
    assert 1 + b_lat <= 8

    cond8 = jnp.zeros((8, D_MODEL), F32).at[0].set(c_ctx).at[1:1 + b_lat].set(c)
    mod = _modulation(cond8, mod_w, mod_b)
    mods_ctx = [mod[l, 0:1].reshape(1, 1, -1) for l in range(2)]
    mods_lat = [mod[l, 1:1 + b_lat].reshape(b_lat, 1, -1) for l in range(2)]

    gw = gla_w_in[0]
    w_gate = jnp.zeros((LANES, 2 * GLA_QK), F32)
    w_gate = w_gate.at[:GLA_RANK, :GLA_QK].set(gla_gate_w[0, 0])
    w_gate = w_gate.at[GLA_RANK:2 * GLA_RANK, GLA_QK:].set(gla_gate_w[0, 1])
    tri, lvl = _gla_constants()
    p = {
        "norm_g": norm_g,
        "ab_w_in": ab_w_in[0].astype(BF16),
        "conv_w": conv_w[0],
        "sink": attn_sink[0],
        "ab_w_out": ab_w_out[0].astype(BF16),
        "mlp_w1": mlp_w1.astype(BF16),
        "mlp_w2": mlp_w2.astype(BF16),
        "gla_w_in": jnp.pad(gw, ((0, 0), (0, LANES - 2 * GLA_RANK))).astype(BF16),
        "gla_w_gate": w_gate.astype(BF16),
        "gla_gate_bias": gla_gate_b[0].reshape(1, 2 * GLA_QK),
        "gla_norm_g": gla_norm_g[0].reshape(1, GLA_DV),
        "gla_w_out": gla_w_out[0].astype(BF16),
        "tri": tri,
        "lvl": lvl,
    }

    ctx = _Group(b_ctx, n_ctx, per_seq_mod=False)
    lat = _Group(b_lat, n_lat, per_seq_mod=True)

    y_prompt, kv_ctx, sf_t, sb_t = _run_group(ctx, x_prompt, mods_ctx, p, None, None, None,
                                              None, None, tm=512)
    new_k = kv_ctx[:, :KD].reshape(b_ctx, 1, n_ctx, N_KV_HEADS, HEAD_DIM)
    new_v = kv_ctx[:, KD:].reshape(b_ctx, 1, n_ctx, N_KV_HEADS, HEAD_DIM)
    new_sf = jnp.swapaxes(sf_t, -1, -2)[:, None]
    new_sb = jnp.swapaxes(sb_t, -1, -2)[:, None]

    past = cache_k.shape[2]
    k_ctx = cache_k[:, 0].reshape(b_lat, past, KD)
    v_ctx = cache_v[:, 0].reshape(b_lat, past, KD)
    s0f_t = jnp.swapaxes(state_fwd[:, 0], -1, -2)
    s0b_t = jnp.swapaxes(state_bwd[:, 0], -1, -2)
    y_sample, _, _, _ = _run_group(lat, x_sample, mods_lat, p, _rope_tables(n_lat), k_ctx, v_ctx,
                                   s0f_t, s0b_t, tm=512)
    return (y_prompt, y_sample, new_k, new_v, new_sf, new_sb)
```

```python
import functools

import jax
import jax.numpy as jnp
from jax import lax
from jax.experimental import pallas as pl
from jax.experimental.pallas import tpu as pltpu

F32 = jnp.float32
BF16 = jnp.bfloat16

D_MODEL = 1024
MOD_CHUNKS = 6
EPS = 1e-6
CONV_DIM = 512
N_Q_HEADS = 8
N_KV_HEADS = 2
GQA_GROUP = 4
HEAD_DIM = 64
WINDOW = 128
ATTN_BLOCK = 128
GRID_W = 64
ROPE_BASE = 10000.0
QD = N_Q_HEADS * HEAD_DIM
KD = N_KV_HEADS * HEAD_DIM
AB_IN = 3 * CONV_DIM + QD + 2 * KD
GLA_HEADS = 4
GLA_DK = 128
GLA_DV = 256
GLA_RANK = 16
GLA_GATE_NORM = 16.0
GLA_TILE = 256
LOG2E = 1.4426950408889634
GLA_QK = GLA_HEADS * GLA_DK
GLA_VD = GLA_HEADS * GLA_DV
GLA_MAIN = 2 * GLA_QK + 2 * GLA_VD
D_FF = 4 * D_MODEL
NEG_INF = -1e30
LANES = 128
VMEM_LIMIT = 56 * 1024 * 1024

NT_DIMS = (((1,), (1,)), ((), ()))
TN_DIMS = (((0,), (0,)), ((), ()))


def _cparams(*sem):
    return pltpu.CompilerParams(dimension_semantics=sem, vmem_limit_bytes=VMEM_LIMIT)


def _bdot(a, b):
    return jnp.dot(a.astype(BF16), b.astype(BF16), preferred_element_type=F32)


def _bdot_nt(a, b):
    return lax.dot_general(a.astype(BF16), b.astype(BF16), NT_DIMS, preferred_element_type=F32)


def _rms(x, g):
    ms = jnp.mean(x * x, axis=-1, keepdims=True)
    return x * lax.rsqrt(ms + EPS) * g


def _mod_chunk(mod_ref, i):
    return mod_ref[:, i * D_MODEL:(i + 1) * D_MODEL]


def _const_spec(shape):
    return pl.BlockSpec(shape, lambda *_: (0,) * len(shape))


def _mod_kernel(cond_ref, w_ref, b_ref, o_ref):
    cnd = cond_ref[...]
    s = cnd / (1.0 + jnp.exp(-cnd))
    o_ref[...] = _bdot(s, w_ref[...]) + b_ref[...]


def _modulation(cond8, mod_w, mod_b):
    depth = mod_w.shape[0]
    n = mod_w.shape[2]
    tn = 1536
    return pl.pallas_call(
        _mod_kernel,
        grid=(depth, n // tn),
        in_specs=[
            pl.BlockSpec((8, D_MODEL), lambda l, j: (0, 0)),
            pl.BlockSpec((None, D_MODEL, tn), lambda l, j: (l, 0, j)),
            pl.BlockSpec((None, 1, tn), lambda l, j: (l, 0, j)),
        ],
        out_specs=pl.BlockSpec((None, 8, tn), lambda l, j: (l, 0, j)),
        out_shape=jax.ShapeDtypeStruct((depth, 8, n), F32),
        compiler_params=_cparams("arbitrary", "arbitrary"),
        name="modulation",
    )(cond8, mod_w, mod_b.reshape(depth, 1, n))


class _Group:
    def __init__(self, b, n, per_seq_mod):
        self.b, self.n, self.t = b, n, b * n
        self.per_seq_mod = per_seq_mod

    def mod_spec(self, tm):
        if self.per_seq_mod:
            assert self.n % tm == 0
            per = self.n // tm
            return pl.BlockSpec((None, 1, MOD_CHUNKS * D_MODEL), lambda t: (t // per, 0, 0))
        return pl.BlockSpec((None, 1, MOD_CHUNKS * D_MODEL), lambda t: (0, 0, 0))


def _row_spec(tm, width):
    return pl.BlockSpec((tm, width), lambda t: (t, 0))


def _rope(x, cos, sin_lo, sin_hi):
    return (x * cos + pltpu.roll(x, LANES - 16, axis=1) * sin_lo
            + pltpu.roll(x, 16, axis=1) * sin_hi)


def _premix0_kernel(*refs, rope):
    if rope:
        x_ref, mod_ref, g_ref, w_ref, cos_ref, slo_ref, shi_ref, zc_ref, q_ref, kv_ref = refs
    else:
        x_ref, mod_ref, g_ref, w_ref, zc_ref, q_ref, kv_ref = refs
    sh1, sc1 = _mod_chunk(mod_ref, 0), _mod_chunk(mod_ref, 1)
    h = _rms(x_ref[...], g_ref[0:1, :]) * (1.0 + sc1) + sh1
    z = _bdot(h, w_ref[...])
    c3 = 3 * CONV_DIM
    zc_ref[...] = z[:, :c3]
    scale = HEAD_DIM ** -0.5 * LOG2E
    if rope:
        cos, slo, shi = cos_ref[...], slo_ref[...], shi_ref[...]
        for j in range(QD // LANES):
            qs = z[:, c3 + j * LANES:c3 + (j + 1) * LANES]
            q_ref[:, j * LANES:(j + 1) * LANES] = (_rope(qs, cos, slo, shi) * scale).astype(BF16)
        kv_ref[:, :KD] = _rope(z[:, c3 + QD:c3 + QD + KD], cos, slo, shi)
    else:
        q_ref[...] = (z[:, c3:c3 + QD] * scale).astype(BF16)
        kv_ref[:, :KD] = z[:, c3 + QD:c3 + QD + KD]
    kv_ref[:, KD:] = z[:, c3 + QD + KD:]


def _premix0(grp, x2d, mod, norm_g, w_in, rope_tabs, tm):
    rope = rope_tabs is not None
    in_specs = [_row_spec(tm, D_MODEL), grp.mod_spec(tm), _const_spec((4, D_MODEL)),
                _const_spec((D_MODEL, AB_IN))]
    args = [x2d, mod, norm_g, w_in]
    if rope:
        per = grp.n // tm
        in_specs += [pl.BlockSpec((tm, LANES), lambda t: (t % per, 0))] * 3
        args += list(rope_tabs)
    return pl.pallas_call(
        functools.partial(_premix0_kernel, rope=rope),
        grid=(grp.t // tm,),
        in_specs=in_specs,
        out_specs=[_row_spec(tm, 3 * CONV_DIM), _row_spec(tm, QD), _row_spec(tm, 2 * KD)],
        out_shape=[jax.ShapeDtypeStruct((grp.t, 3 * CONV_DIM), F32),
                   jax.ShapeDtypeStruct((grp.t, QD), BF16),
                   jax.ShapeDtypeStruct((grp.t, 2 * KD), F32)],
        compiler_params=_cparams("arbitrary"),
        name="premix0_rope" if rope else "premix0",
    )(*args)


def _rope_tables(n):
    rows = n // GRID_W
    pos_r = jnp.repeat(jnp.arange(rows), GRID_W)
    pos_c = jnp.tile(jnp.arange(GRID_W), rows)
    half = HEAD_DIM // 2
    quarter = half // 2
    inv = ROPE_BASE ** (-(jnp.arange(quarter, dtype=F32) * 2.0 / half))

    def cs(pos):
        ang = pos.astype(F32)[:, None] * inv[None, :]
        return jnp.cos(ang), jnp.sin(ang)

    cr, sr = cs(pos_r)
    cc, sc = cs(pos_c)
    zero = jnp.zeros_like(sr)
    cos = jnp.concatenate([cr, cr, cc, cc], axis=1)
    sin_lo = jnp.concatenate([-sr, zero, -sc, zero], axis=1)
    sin_hi = jnp.concatenate([zero, sr, zero, sc], axis=1)
    rep = LANES // HEAD_DIM
    return tuple(jnp.tile(t, (1, rep)) for t in (cos, sin_lo, sin_hi))


def _sink_attention(sink_ref, q_ref, k_all, bias_t, v_all, o_ref):
    assert KD == LANES == 2 * HEAD_DIM and GQA_GROUP == 4
    m = q_ref.shape[0]
    lane = lax.broadcasted_iota(jnp.int32, (1, LANES), 1)
    sub = lax.broadcasted_iota(jnp.int32, (LANES, 1), 0)
    v_t = v_all.T
    k_swapped = pltpu.roll(k_all, HEAD_DIM, axis=1)
    o_t = {}
    for g in range(N_KV_HEADS):
        k_low, k_high = (k_all, k_swapped) if g == 0 else (k_swapped, k_all)
        kz_even = jnp.where(lane < HEAD_DIM, k_low, 0.0).astype(BF16)
        kz_odd = jnp.where(lane >= HEAD_DIM, k_high, 0.0).astype(BF16)
        qq = jnp.concatenate([q_ref[:, (2 * g) * LANES:(2 * g + 1) * LANES],
                              q_ref[:, (2 * g + 1) * LANES:(2 * g + 2) * LANES]], axis=0)
        s = jnp.concatenate([_bdot_nt(kz_even, qq), _bdot_nt(kz_odd, qq)], axis=1)
        heads = [4 * g, 4 * g + 2, 4 * g + 1, 4 * g + 3]
        if bias_t is not None:
            nb = bias_t.shape[0]
            s = jnp.concatenate([s[:nb] + jnp.concatenate([bias_t] * GQA_GROUP, axis=1), s[nb:]],
                                axis=0)
        sink = jnp.concatenate([jnp.full((1, m), sink_ref[h] * LOG2E, F32) for h in heads], axis=1)
        mx = jnp.maximum(jnp.max(s, axis=0, keepdims=True), sink)
        p = jnp.exp2(s - mx).astype(BF16)
        own = (sub < HEAD_DIM) if g == 0 else (sub >= HEAD_DIM)
        v_ext_t = jnp.where(own, v_t, 1.0).astype(BF16)
        oe = jnp.dot(v_ext_t, p, preferred_element_type=F32)
        other = (1 - g) * HEAD_DIM
        den = oe[other:other + 1] + jnp.exp2(sink - mx)
        o_g = oe[g * HEAD_DIM:(g + 1) * HEAD_DIM] / den
        for i, h in enumerate(heads):
            o_t[h] = o_g[:, i * m:(i + 1) * m]
    for j in range(N_Q_HEADS // 2):
        pair_t = jnp.concatenate([o_t[2 * j], o_t[2 * j + 1]], axis=0)
        o_ref[:, j * LANES:(j + 1) * LANES] = pair_t.T.astype(BF16)


def _ctx_attn_kernel(sink_ref, q_ref, kv_ref, o_ref):
    _sink_attention(sink_ref, q_ref, kv_ref[:, :KD], None, kv_ref[:, KD:], o_ref)


def _ctx_attention(grp, q, kv, sink):
    n = grp.n
    return pl.pallas_call(
        _ctx_attn_kernel,
        grid=(grp.b,),
        in_specs=[pl.BlockSpec(memory_space=pltpu.SMEM), _row_spec(n, QD), _row_spec(n, 2 * KD)],
        out_specs=_row_spec(n, QD),
        out_shape=jax.ShapeDtypeStruct((grp.t, QD), BF16),
        compiler_params=_cparams("arbitrary"),
        name="ctx_attention",
    )(sink, q, kv)


def _lat_attn_kernel(sink_ref, q_ref, kvp_ref, kvc_ref, kvn_ref, kc_ref, vc_ref, o_ref, *, n):
    blk = pl.program_id(1)
    kv = jnp.concatenate([kvp_ref[...], kvc_ref[...], kvn_ref[...]], axis=0)
    kj = lax.broadcasted_iota(jnp.int32, (3 * ATTN_BLOCK, ATTN_BLOCK), 0)
    qi = lax.broadcasted_iota(jnp.int32, (3 * ATTN_BLOCK, ATTN_BLOCK), 1)
    rel = kj - ATTN_BLOCK - qi
    kpos = (blk - 1) * ATTN_BLOCK + kj
    valid = (jnp.abs(rel) <= WINDOW) & (kpos >= 0) & (kpos < n)
    bias_t = jnp.where(valid, 0.0, NEG_INF)
    k_all = jnp.concatenate([kv[:, :KD], kc_ref[...]], axis=0)
    v_all = jnp.concatenate([kv[:, KD:], vc_ref[...]], axis=0)
    _sink_attention(sink_ref, q_ref, k_all, bias_t, v_all, o_ref)


def _lat_attention(grp, q, kv, k_ctx, v_ctx, sink):
    nb = grp.n // ATTN_BLOCK
    past = k_ctx.shape[1]

    def band(off):
        return pl.BlockSpec((ATTN_BLOCK, 2 * KD),
                            lambda b, i: (b * nb + jnp.clip(i + off, 0, nb - 1), 0))

    ctx_spec = pl.BlockSpec((None, past, KD), lambda b, i: (b, 0, 0))
    return pl.pallas_call(
        functools.partial(_lat_attn_kernel, n=grp.n),
        grid=(grp.b, nb),
        in_specs=[pl.BlockSpec(memory_space=pltpu.SMEM),
                  pl.BlockSpec((ATTN_BLOCK, QD), lambda b, i: (b * nb + i, 0)),
                  band(-1), band(0), band(1), ctx_spec, ctx_spec],
        out_specs=pl.BlockSpec((ATTN_BLOCK, QD), lambda b, i: (b * nb + i, 0)),
        out_shape=jax.ShapeDtypeStruct((grp.t, QD), BF16),
        compiler_params=_cparams("arbitrary", "arbitrary"),
        name="lat_attention",
    )(sink, q, kv, kv, kv, k_ctx, v_ctx)


def _mlp_tail(x, mod_ref, g_ref, w1_ref, w2_ref):
    fchunk = 1024
    sh2, sc2, gt2 = _mod_chunk(mod_ref, 3), _mod_chunk(mod_ref, 4), _mod_chunk(mod_ref, 5)
    hb = (_rms(x, g_ref[2:3, :]) * (1.0 + sc2) + sh2).astype(BF16)
    acc = None
    for j in range(D_FF // fchunk):
        a = jnp.dot(hb, w1_ref[:, j * fchunk:(j + 1) * fchunk], preferred_element_type=F32)
        a = jnp.maximum(a, 0.0)
        part = _bdot(a * a, w2_ref[j * fchunk:(j + 1) * fchunk, :])
        acc = part if acc is None else acc + part
    return x + gt2 * _rms(acc, g_ref[3:4, :])


def _mlp_weight_specs(layer):
    single = pl.Buffered(1)
    return [pl.BlockSpec((None, D_MODEL, D_FF), lambda t: (layer, 0, 0), pipeline_mode=single),
            pl.BlockSpec((None, D_FF, D_MODEL), lambda t: (layer, 0, 0), pipeline_mode=single)]


def _postmix0_kernel(zc_ref, zp_ref, zn_ref, at_ref, x_ref, mod_ref, g_ref, cw_ref, w_ref,
                     w1_ref, w2_ref, o_ref, *, n, tm):
    c = CONV_DIM
    zc = zc_ref[...]
    u = zc[:, c:2 * c] * zc[:, 2 * c:]
    u_before = zp_ref[7:8, c:2 * c] * zp_ref[7:8, 2 * c:]
    u_after = zn_ref[0:1, c:2 * c] * zn_ref[0:1, 2 * c:]
    row = lax.broadcasted_iota(jnp.int32, (tm, 1), 0)
    pos = (pl.program_id(0) * tm + row) % n
    u_prev = jnp.where(row == 0, u_before, pltpu.roll(u, 1, axis=0))
    u_prev = jnp.where(pos == 0, 0.0, u_prev)
    u_next = jnp.where(row == tm - 1, u_after, pltpu.roll(u, tm - 1, axis=0))
    u_next = jnp.where(pos == n - 1, 0.0, u_next)
    conv = u_prev * cw_ref[0:1, :] + u * cw_ref[1:2, :] + u_next * cw_ref[2:3, :]
    mix = _bdot(zc[:, :c] * conv, w_ref[:c, :]) + _bdot(at_ref[...], w_ref[c:, :])
    gt1 = _mod_chunk(mod_ref, 2)
    x1 = x_ref[...] + gt1 * _rms(mix, g_ref[1:2, :])
    o_ref[...] = _mlp_tail(x1, mod_ref, g_ref, w1_ref, w2_ref)


def _postmix0(grp, zc, attn, x2d, mod, norm_g, conv_w, w_out, w1, w2, tm):
    r8 = tm // 8
    last8 = grp.t // 8 - 1
    return pl.pallas_call(
        functools.partial(_postmix0_kernel, n=grp.n, tm=tm),
        grid=(grp.t // tm,),
        in_specs=[_row_spec(tm, 3 * CONV_DIM),
                  pl.BlockSpec((8, 3 * CONV_DIM), lambda t: (jnp.maximum(t * r8 - 1, 0), 0)),
                  pl.BlockSpec((8, 3 * CONV_DIM), lambda t: (jnp.minimum((t + 1) * r8, last8), 0)),
                  _row_spec(tm, QD), _row_spec(tm, D_MODEL), grp.mod_spec(tm),
                  _const_spec((4, D_MODEL)), _const_spec((3, CONV_DIM)),
                  pl.BlockSpec((CONV_DIM + QD, D_MODEL), lambda t: (0, 0),
                               pipeline_mode=pl.Buffered(1))] + _mlp_weight_specs(0),
        out_specs=_row_spec(tm, D_MODEL),
        out_shape=jax.ShapeDtypeStruct((grp.t, D_MODEL), F32),
        compiler_params=_cparams("arbitrary"),
        name="postmix0_mlp",
    )(zc, zc, zc, attn, x2d, mod, norm_g, conv_w, w_out, w1, w2)


def _premix1_kernel(x_ref, mod_ref, g_ref, w_ref, wg_ref, gb_ref,
                    q_ref, k_ref, v_ref, og_ref, gf_ref, gbk_ref):
    sh1, sc1 = _mod_chunk(mod_ref, 0), _mod_chunk(mod_ref, 1)
    hb = (_rms(x_ref[...], g_ref[0:1, :]) * (1.0 + sc1) + sh1).astype(BF16)
    z = jnp.dot(hb, w_ref[...], preferred_element_type=F32)
    q_ref[...] = z[:, :GLA_QK] * (GLA_DK ** -0.5)
    k_ref[...] = z[:, GLA_QK:2 * GLA_QK]
    v_ref[...] = z[:, 2 * GLA_QK:2 * GLA_QK + GLA_VD].astype(BF16)
    og_ref[...] = z[:, 2 * GLA_QK + GLA_VD:GLA_MAIN]
    r = z[:, GLA_MAIN:]
    pre = _bdot(r, wg_ref[...]) + gb_ref[...]
    gate = ((jnp.minimum(pre, 0.0) - jnp.log1p(jnp.exp(-jnp.abs(pre))))
            * (LOG2E / GLA_GATE_NORM))
    gf_ref[...] = gate[:, :GLA_QK]
    gbk_ref[...] = gate[:, GLA_QK:]


def _premix1(grp, x2d, mod, norm_g, w_in, w_gate, gate_bias, tm):
    t = grp.t
    return pl.pallas_call(
        _premix1_kernel,
        grid=(t // tm,),
        in_specs=[_row_spec(tm, D_MODEL), grp.mod_spec(tm), _const_spec((4, D_MODEL)),
                  _const_spec((D_MODEL, GLA_MAIN + LANES)), _const_spec((LANES, 2 * GLA_QK)),
                  _const_spec((1, 2 * GLA_QK))],
        out_specs=[_row_spec(tm, GLA_QK), _row_spec(tm, GLA_QK), _row_spec(tm, GLA_VD),
                   _row_spec(tm, GLA_VD), _row_spec(tm, GLA_QK), _row_spec(tm, GLA_QK)],
        out_shape=[jax.ShapeDtypeStruct((t, GLA_QK), F32), jax.ShapeDtypeStruct((t, GLA_QK), F32),
                   jax.ShapeDtypeStruct((t, GLA_VD), BF16),
                   jax.ShapeDtypeStruct((t, GLA_VD), F32), jax.ShapeDtypeStruct((t, GLA_QK), F32),
                   jax.ShapeDtypeStruct((t, GLA_QK), F32)],
        compiler_params=_cparams("arbitrary"),
        name="premix1",
    )(x2d, mod, norm_g, w_in, w_gate, gate_bias)


def _split3(x):
    hi = x.astype(BF16)
    r1 = x - hi.astype(F32)
    mid = r1.astype(BF16)
    lo = (r1 - mid.astype(F32)).astype(BF16)
    return hi, mid, lo


def _level_exponent(b, g, s, reverse):
    if s == 1:
        row = lax.broadcasted_iota(jnp.int32, (GLA_TILE, 1), 0)
        return jnp.where(row % 2 == (0 if reverse else 1), g, 0.0)
    idx = s if reverse else s - 1
    if s >= 8:
        n = GLA_TILE // (2 * s)
        b4 = b.reshape(n, 2, s, GLA_DK)
        first, second = b4[:, 0:1], b4[:, 1:2]
        r = (second[:, :, 0:1] if reverse else first[:, :, s - 1:s])
        parts = [first - r, r - second] if reverse else [r - first, second - r]
        return jnp.concatenate(parts, axis=1).reshape(GLA_TILE, GLA_DK)
    b8 = b.reshape(GLA_TILE // 8, 8, GLA_DK)
    sub = lax.broadcasted_iota(jnp.int32, (1, 8, 1), 1)
    if s == 4:
        r = b8[:, idx:idx + 1, :]
    else:
        assert s == 2
        r = jnp.where(sub < 4, b8[:, idx:idx + 1, :], b8[:, 4 + idx:5 + idx, :])
    in_second = (sub // s) % 2 == 1
    sign = jnp.where(in_second != reverse, 1.0, -1.0)
    return ((b8 - r) * sign).reshape(GLA_TILE, GLA_DK)


def _gla_tile(q, k, g, b, v, lvl, s_ref, reverse):
    half = GLA_TILE // 2
    edge = 0 if reverse else GLA_TILE - 1
    b_last = b[edge:edge + 1, :]
    qe = (q * jnp.exp2(b)).astype(BF16)
    ke = (k * jnp.exp2(b_last - b)).astype(BF16)

    lo, hi = slice(0, half), slice(half, GLA_TILE)
    qb, kb = q.astype(BF16), k.astype(BF16)
    pairs = _bdot_nt(qb, kb)
    blocks = [jnp.where(lvl == 0, pairs[lo, lo], 0.0), jnp.where(lvl == 0, pairs[hi, hi], 0.0)]
    cross = None
    s, level = 1, 1
    while s < GLA_TILE:
        f = jnp.exp2(_level_exponent(b, g, s, reverse)).astype(BF16)
        pairs = _bdot_nt(qb * f, kb * f)
        if s == half:
            cross = pairs[lo, hi] if reverse else pairs[hi, lo]
        else:
            blocks = [jnp.where(lvl == level, pairs[lo, lo], blocks[0]),
                      jnp.where(lvl == level, pairs[hi, hi], blocks[1])]
        s, level = 2 * s, level + 1

    zero = jnp.zeros((half, half), BF16)
    a_lo, a_hi, a_x = blocks[0].astype(BF16), blocks[1].astype(BF16), cross.astype(BF16)
    if reverse:
        att = jnp.concatenate([jnp.concatenate([a_lo, a_x], axis=1),
                               jnp.concatenate([zero, a_hi], axis=1)], axis=0)
    else:
        att = jnp.concatenate([jnp.concatenate([a_lo, zero], axis=1),
                               jnp.concatenate([a_x, a_hi], axis=1)], axis=0)
    st = s_ref[...]
    o = jnp.dot(att, v, preferred_element_type=F32) + _bdot_nt(qe, st)
    s_ref[...] = st * jnp.exp2(b_last) + lax.dot_general(
        v, ke, TN_DIMS, preferred_element_type=F32)
    return o


def _gla_kernel(*refs, zero_init, nt):
    if zero_init:
        (tri_ref, lvl_ref, q_ref, k_ref, gf_ref, gb_ref, v_ref,
         o_ref, sf_ref, sb_ref) = refs
        sf_ref[...] = jnp.zeros_like(sf_ref)
        sb_ref[...] = jnp.zeros_like(sb_ref)
    else:
        (tri_ref, lvl_ref, q_ref, k_ref, gf_ref, gb_ref, v_ref, s0f_ref, s0b_ref,
         o_ref, sf_ref, sb_ref) = refs
        sf_ref[...] = s0f_ref[...]
        sb_ref[...] = s0b_ref[...]
    o_ref[...] = jnp.zeros_like(o_ref)

    def tile_rows(tile):
        r0 = tile * GLA_TILE
        return pl.ds(r0 if isinstance(r0, int) else pl.multiple_of(r0, GLA_TILE), GLA_TILE)

    def step(t, carry):
        rows_f, rows_b = tile_rows(t), tile_rows(nt - 1 - t)
        g_f, g_b = gf_ref[rows_f, :], gb_ref[rows_b, :]
        c = None
        for part in _split3(jnp.concatenate([g_f, g_b], axis=1)):
            term = jnp.dot(tri_ref[...], part, preferred_element_type=F32)
            c = term if c is None else c + term
        c_b = c[:, GLA_DK:]
        b_f = c[:, :GLA_DK]
        b_b = (c_b[GLA_TILE - 1:GLA_TILE, :] - c_b) + g_b
        for d, (rows, g, b, s_ref) in enumerate(((rows_f, g_f, b_f, sf_ref),
                                                 (rows_b, g_b, b_b, sb_ref))):
            o = _gla_tile(q_ref[rows, :], k_ref[rows, :], g, b, v_ref[rows, :],
                          lvl_ref[d], s_ref, bool(d))
            o_ref[rows, :] += o
        return carry

    if nt == 1:
        step(0, 0)
    else:
        lax.fori_loop(0, nt, step, 0)


def _gla_scan(grp, q, k, v, gf, gb, s0f_t, s0b_t, tri, lvl):
    n = grp.n
    nt = n // GLA_TILE
    zero_init = s0f_t is None
    half = GLA_TILE // 2
    seq_dk = pl.BlockSpec((n, GLA_DK), lambda b, h: (b, h))
    seq_dv = pl.BlockSpec((n, GLA_DV), lambda b, h: (b, h))
    state_spec = pl.BlockSpec((None, None, GLA_DV, GLA_DK), lambda b, h: (b, h, 0, 0))
    in_specs = [_const_spec((GLA_TILE, GLA_TILE)), _const_spec((2, half, half)),
                seq_dk, seq_dk, seq_dk, seq_dk, seq_dv]
    args = [tri, lvl, q, k, gf, gb, v]
    if not zero_init:
        in_specs += [state_spec, state_spec]
        args += [s0f_t, s0b_t]
    state_shape = jax.ShapeDtypeStruct((grp.b, GLA_HEADS, GLA_DV, GLA_DK), F32)
    return pl.pallas_call(
        functools.partial(_gla_kernel, zero_init=zero_init, nt=nt),
        grid=(grp.b, GLA_HEADS),
        in_specs=in_specs,
        out_specs=[seq_dv, state_spec, state_spec],
        out_shape=[jax.ShapeDtypeStruct((grp.t, GLA_VD), F32), state_shape, state_shape],
        compiler_params=_cparams("arbitrary", "arbitrary"),
        name="gla_scan",
    )(*args)


def _gla_constants():
    half = GLA_TILE // 2
    i = jnp.arange(GLA_TILE)[:, None]
    j = jnp.arange(GLA_TILE)[None, :]
    tri = (j <= i).astype(BF16)
    ih, jh = i[:half], j[:, :half]
    x = jnp.bitwise_xor(ih, jh)
    level = sum((x >= (1 << p)).astype(jnp.int32) for p in range(half.bit_length() - 1))
    lvl = jnp.stack([jnp.where(jh <= ih, level, -1), jnp.where(jh >= ih, level, -1)])
    return tri, lvl


def _postmix1_kernel(o_ref, og_ref, x_ref, mod_ref, g_ref, gn_ref, w_ref, w1_ref, w2_ref,
                     out_ref):
    gn = gn_ref[...]
    mix = None
    for h in range(GLA_HEADS):
        cols = slice(h * GLA_DV, (h + 1) * GLA_DV)
        o = _rms(o_ref[:, cols], gn)
        og = og_ref[:, cols]
        y = o * (og / (1.0 + jnp.exp(-og)))
        part = _bdot(y, w_ref[cols, :])
        mix = part if mix is None else mix + part
    gt1 = _mod_chunk(mod_ref, 2)
    x1 = x_ref[...] + gt1 * _rms(mix, g_ref[1:2, :])
    out_ref[...] = _mlp_tail(x1, mod_ref, g_ref, w1_ref, w2_ref)


def _postmix1(grp, o, og, x2d, mod, norm_g, gla_norm_g, w_out, w1, w2, tm):
    return pl.pallas_call(
        _postmix1_kernel,
        grid=(grp.t // tm,),
        in_specs=[_row_spec(tm, GLA_VD), _row_spec(tm, GLA_VD),
                  _row_spec(tm, D_MODEL), grp.mod_spec(tm), _const_spec((4, D_MODEL)),
                  _const_spec((1, GLA_DV)),
                  pl.BlockSpec((GLA_VD, D_MODEL), lambda t: (0, 0),
                               pipeline_mode=pl.Buffered(1))] + _mlp_weight_specs(1),
        out_specs=_row_spec(tm, D_MODEL),
        out_shape=jax.ShapeDtypeStruct((grp.t, D_MODEL), F32),
        compiler_params=_cparams("arbitrary"),
        name="postmix1_mlp",
    )(o, og, x2d, mod, norm_g, gla_norm_g, w_out, w1, w2)


def _run_group(grp, x, mods, p, rope_tabs, k_ctx, v_ctx, s0f_t, s0b_t, tm):
    x2d = x.reshape(grp.t, D_MODEL)
    zc, q, kv = _premix0(grp, x2d, mods[0], p["norm_g"][0], p["ab_w_in"], rope_tabs, tm)
    if k_ctx is None:
        attn = _ctx_attention(grp, q, kv, p["sink"])
    else:
        attn = _lat_attention(grp, q, kv, k_ctx, v_ctx, p["sink"])
    x2d = _postmix0(grp, zc, attn, x2d, mods[0], p["norm_g"][0], p["conv_w"], p["ab_w_out"],
                    p["mlp_w1"], p["mlp_w2"], tm)
    gq, gk, gv, og, gf, gb = _premix1(grp, x2d, mods[1], p["norm_g"][1], p["gla_w_in"],
                                      p["gla_w_gate"], p["gla_gate_bias"], tm)
    o, sf_t, sb_t = _gla_scan(grp, gq, gk, gv, gf, gb, s0f_t, s0b_t, p["tri"], p["lvl"])
    x2d = _postmix1(grp, o, og, x2d, mods[1], p["norm_g"][1], p["gla_norm_g"], p["gla_w_out"],
                    p["mlp_w1"], p["mlp_w2"], tm)
    return x2d.reshape(x.shape), kv, sf_t, sb_t


def kernel(x_prompt, x_sample, cache_k, cache_v, state_fwd, state_bwd, c, c_ctx, mod_w, mod_b,
           norm_g, ab_w_in, conv_w, attn_sink, ab_w_out, gla_w_in, gla_gate_w, gla_gate_b,
           gla_norm_g, gla_w_out, mlp_w1, mlp_w2):
    b_ctx, n_ctx, _ = x_prompt.shape
    b_lat, n_lat, _ = x_sample.shape
    assert mod_w.shape[0] == 2 and ab_w_in.shape[0] == 1 and gla_w_in.shape[0] == 1
    assert 1 + b_lat <= 8

    cond8 = jnp.zeros((8, D_MODEL), F32).at[0].set(c_ctx).at[1:1 + b_lat].set(c)
    mod = _modulation(cond8, mod_w, mod_b)
    mods_ctx = [mod[l, 0:1].reshape(1, 1, -1) for l in range(2)]
    mods_lat = [mod[l, 1:1 + b_lat].reshape(b_lat, 1, -1) for l in range(2)]

    gw = gla_w_in[0]
    w_gate = jnp.zeros((LANES, 2 * GLA_QK), F32)
    w_gate = w_gate.at[:GLA_RANK, :GLA_QK].set(gla_gate_w[0, 0])
    w_gate = w_gate.at[GLA_RANK:2 * GLA_RANK, GLA_QK:].set(gla_gate_w[0, 1])
    tri, lvl = _gla_constants()
    p = {
        "norm_g": norm_g,
        "ab_w_in": ab_w_in[0].astype(BF16),
        "conv_w": conv_w[0],
        "sink": attn_sink[0],
        "ab_w_out": ab_w_out[0].astype(BF16),
        "mlp_w1": mlp_w1.astype(BF16),
        "mlp_w2": mlp_w2.astype(BF16),
        "gla_w_in": jnp.pad(gw, ((0, 0), (0, LANES - 2 * GLA_RANK))).astype(BF16),
        "gla_w_gate": w_gate.astype(BF16),
        "gla_gate_bias": gla_gate_b[0].reshape(1, 2 * GLA_QK),
        "gla_norm_g": gla_norm_g[0].reshape(1, GLA_DV),
        "gla_w_out": gla_w_out[0].astype(BF16),
        "tri": tri,
        "lvl": lvl,
    }

    ctx = _Group(b_ctx, n_ctx, per_seq_mod=False)
    lat = _Group(b_lat, n_lat, per_seq_mod=True)

    y_prompt, kv_ctx, sf_t, sb_t = _run_group(ctx, x_prompt, mods_ctx, p, None, None, None,
                                              None, None, tm=512)
    new_k = kv_ctx[:, :KD].reshape(b_ctx, 1, n_ctx, N_KV_HEADS, HEAD_DIM)
    new_v = kv_ctx[:, KD:].reshape(b_ctx, 1, n_ctx, N_KV_HEADS, HEAD_DIM)
    new_sf = jnp.swapaxes(sf_t, -1, -2)[:, None]
    new_sb = jnp.swapaxes(sb_t, -1, -2)[:, None]

    past = cache_k.shape[2]
    k_ctx = cache_k[:, 0].reshape(b_lat, past, KD)
    v_ctx = cache_v[:, 0].reshape(b_lat, past, KD)
    s0f_t = jnp.swapaxes(state_fwd[:, 0], -1, -2)
    s0b_t = jnp.swapaxes(state_bwd[:, 0], -1, -2)
    y_sample, _, _, _ = _run_group(lat, x_sample, mods_lat, p, _rope_tables(n_lat), k_ctx, v_ctx,
                                   s0f_t, s0b_t, tm=512)
    return (y_prompt, y_sample, new_k, new_v, new_sf, new_sb)
```

```python
import functools

import jax
import jax.numpy as jnp
from jax import lax
from jax.experimental import pallas as pl
from jax.experimental.pallas import tpu as pltpu

F32 = jnp.float32
BF16 = jnp.bfloat16

D_MODEL = 1024
MOD_CHUNKS = 6
EPS = 1e-6
CONV_DIM = 512
N_Q_HEADS = 8
N_KV_HEADS = 2
GQA_GROUP = 4
HEAD_DIM = 64
WINDOW = 128
ATTN_BLOCK = 128
GRID_W = 64
ROPE_BASE = 10000.0
QD = N_Q_HEADS * HEAD_DIM
KD = N_KV_HEADS * HEAD_DIM
AB_IN = 3 * CONV_DIM + QD + 2 * KD
GLA_HEADS = 4
GLA_DK = 128
GLA_DV = 256
GLA_RANK = 16
GLA_GATE_NORM = 16.0
GLA_TILE = 256
LOG2E = 1.4426950408889634
GLA_QK = GLA_HEADS * GLA_DK
GLA_VD = GLA_HEADS * GLA_DV
GLA_MAIN = 2 * GLA_QK + 2 * GLA_VD
D_FF = 4 * D_MODEL
NEG_INF = -1e30
LANES = 128
VMEM_LIMIT = 56 * 1024 * 1024

NT_DIMS = (((1,), (1,)), ((), ()))
TN_DIMS = (((0,), (0,)), ((), ()))


def _cparams(*sem):
    return pltpu.CompilerParams(dimension_semantics=sem, vmem_limit_bytes=VMEM_LIMIT)


def _bdot(a, b):
    return jnp.dot(a.astype(BF16), b.astype(BF16), preferred_element_type=F32)


def _bdot_nt(a, b):
    return lax.dot_general(a.astype(BF16), b.astype(BF16), NT_DIMS, preferred_element_type=F32)


def _rms(x, g):
    ms = jnp.mean(x * x, axis=-1, keepdims=True)
    return x * lax.rsqrt(ms + EPS) * g


def _mod_chunk(mod_ref, i):
    return mod_ref[:, i * D_MODEL:(i + 1) * D_MODEL]


def _const_spec(shape):
    return pl.BlockSpec(shape, lambda *_: (0,) * len(shape))


def _mod_kernel(cond_ref, w_ref, b_ref, o_ref):
    cnd = cond_ref[...]
    s = cnd / (1.0 + jnp.exp(-cnd))
    o_ref[...] = _bdot(s, w_ref[...]) + b_ref[...]


def _modulation(cond8, mod_w, mod_b):
    depth = mod_w.shape[0]
    n = mod_w.shape[2]
    tn = 1536
    return pl.pallas_call(
        _mod_kernel,
        grid=(depth, n // tn),
        in_specs=[
            pl.BlockSpec((8, D_MODEL), lambda l, j: (0, 0)),
            pl.BlockSpec((None, D_MODEL, tn), lambda l, j: (l, 0, j)),
            pl.BlockSpec((None, 1, tn), lambda l, j: (l, 0, j)),
        ],
        out_specs=pl.BlockSpec((None, 8, tn), lambda l, j: (l, 0, j)),
        out_shape=jax.ShapeDtypeStruct((depth, 8, n), F32),
        compiler_params=_cparams("arbitrary", "arbitrary"),
        name="modulation",
    )(cond8, mod_w, mod_b.reshape(depth, 1, n))


class _Group:
    def __init__(self, b, n, per_seq_mod):
        self.b, self.n, self.t = b, n, b * n
        self.per_seq_mod = per_seq_mod

    def mod_spec(self, tm):
        if self.per_seq_mod:
            assert self.n % tm == 0
            per = self.n // tm
            return pl.BlockSpec((None, 1, MOD_CHUNKS * D_MODEL), lambda t: (t // per, 0, 0))
        return pl.BlockSpec((None, 1, MOD_CHUNKS * D_MODEL), lambda t: (0, 0, 0))


def _row_spec(tm, width):
    return pl.BlockSpec((tm, width), lambda t: (t, 0))


def _rope(x, cos, sin_lo, sin_hi):
    return (x * cos + pltpu.roll(x, LANES - 16, axis=1) * sin_lo
            + pltpu.roll(x, 16, axis=1) * sin_hi)


def _premix0_kernel(*refs, rope):
    if rope:
        x_ref, mod_ref, g_ref, w_ref, cos_ref, slo_ref, shi_ref, zc_ref, q_ref, kv_ref = refs
    else:
        x_ref, mod_ref, g_ref, w_ref, zc_ref, q_ref, kv_ref, kt_ref, vt_ref = refs
    sh1, sc1 = _mod_chunk(mod_ref, 0), _mod_chunk(mod_ref, 1)
    h = _rms(x_ref[...], g_ref[0:1, :]) * (1.0 + sc1) + sh1
    z = _bdot(h, w_ref[...])
    c3 = 3 * CONV_DIM
    zc_ref[...] = z[:, :c3]
    scale = HEAD_DIM ** -0.5 * LOG2E
    if rope:
        cos, slo, shi = cos_ref[...], slo_ref[...], shi_ref[...]
        for j in range(QD // LANES):
            qs = z[:, c3 + j * LANES:c3 + (j + 1) * LANES]
            q_ref[:, j * LANES:(j + 1) * LANES] = (_rope(qs, cos, slo, shi) * scale).astype(BF16)
        kv_ref[:, :KD] = _rope(z[:, c3 + QD:c3 + QD + KD], cos, slo, shi)
    else:
        q_ref[...] = (z[:, c3:c3 + QD] * scale).astype(BF16)
        kv_ref[:, :KD] = z[:, c3 + QD:c3 + QD + KD]
        n = kt_ref.shape[2]
        for j in range(kt_ref.shape[0]):
            kt_ref[j] = z[j * n:(j + 1) * n, c3 + QD:c3 + QD + KD].T
            vt_ref[j] = z[j * n:(j + 1) * n, c3 + QD + KD:].T
    kv_ref[:, KD:] = z[:, c3 + QD + KD:]


def _premix0(grp, x2d, mod, norm_g, w_in, rope_tabs, tm):
    rope = rope_tabs is not None
    in_specs = [_row_spec(tm, D_MODEL), grp.mod_spec(tm), _const_spec((4, D_MODEL)),
                _const_spec((D_MODEL, AB_IN))]
    args = [x2d, mod, norm_g, w_in]
    if rope:
        per = grp.n // tm
        in_specs += [pl.BlockSpec((tm, LANES), lambda t: (t % per, 0))] * 3
        args += list(rope_tabs)
    out_specs = [_row_spec(tm, 3 * CONV_DIM), _row_spec(tm, QD), _row_spec(tm, 2 * KD)]
    out_shape = [jax.ShapeDtypeStruct((grp.t, 3 * CONV_DIM), F32),
                 jax.ShapeDtypeStruct((grp.t, QD), BF16),
                 jax.ShapeDtypeStruct((grp.t, 2 * KD), F32)]
    if not rope:
        assert tm % grp.n == 0
        out_specs += [pl.BlockSpec((tm // grp.n, KD, grp.n), lambda t: (t, 0, 0))] * 2
        out_shape += [jax.ShapeDtypeStruct((grp.b, KD, grp.n), F32)] * 2
    return pl.pallas_call(
        functools.partial(_premix0_kernel, rope=rope),
        grid=(grp.t // tm,),
        in_specs=in_specs,
        out_specs=out_specs,
        out_shape=out_shape,
        compiler_params=_cparams("arbitrary"),
        name="premix0_rope" if rope else "premix0",
    )(*args)


def _rope_tables(n):
    rows = n // GRID_W
    pos_r = jnp.repeat(jnp.arange(rows), GRID_W)
    pos_c = jnp.tile(jnp.arange(GRID_W), rows)
    half = HEAD_DIM // 2
    quarter = half // 2
    inv = ROPE_BASE ** (-(jnp.arange(quarter, dtype=F32) * 2.0 / half))

    def cs(pos):
        ang = pos.astype(F32)[:, None] * inv[None, :]
        return jnp.cos(ang), jnp.sin(ang)

    cr, sr = cs(pos_r)
    cc, sc = cs(pos_c)
    zero = jnp.zeros_like(sr)
    cos = jnp.concatenate([cr, cr, cc, cc], axis=1)
    sin_lo = jnp.concatenate([-sr, zero, -sc, zero], axis=1)
    sin_hi = jnp.concatenate([zero, sr, zero, sc], axis=1)
    rep = LANES // HEAD_DIM
    return tuple(jnp.tile(t, (1, rep)) for t in (cos, sin_lo, sin_hi))


def _sink_attention(sink_ref, q_ref, k_all, bias_t, v_all, o_ref):
    assert KD == LANES == 2 * HEAD_DIM and GQA_GROUP == 4
    m = q_ref.shape[0]
    lane = lax.broadcasted_iota(jnp.int32, (1, LANES), 1)
    sub = lax.broadcasted_iota(jnp.int32, (LANES, 1), 0)
    v_t = v_all.T
    k_swapped = pltpu.roll(k_all, HEAD_DIM, axis=1)
    o_t = {}
    for g in range(N_KV_HEADS):
        k_low, k_high = (k_all, k_swapped) if g == 0 else (k_swapped, k_all)
        kz_even = jnp.where(lane < HEAD_DIM, k_low, 0.0).astype(BF16)
        kz_odd = jnp.where(lane >= HEAD_DIM, k_high, 0.0).astype(BF16)
        qq = jnp.concatenate([q_ref[:, (2 * g) * LANES:(2 * g + 1) * LANES],
                              q_ref[:, (2 * g + 1) * LANES:(2 * g + 2) * LANES]], axis=0)
        s = jnp.concatenate([_bdot_nt(kz_even, qq), _bdot_nt(kz_odd, qq)], axis=1)
        heads = [4 * g, 4 * g + 2, 4 * g + 1, 4 * g + 3]
        if bias_t is not None:
            nb = bias_t.shape[0]
            s = jnp.concatenate([s[:nb] + jnp.concatenate([bias_t] * GQA_GROUP, axis=1), s[nb:]],
                                axis=0)
        sink = jnp.concatenate([jnp.full((1, m), sink_ref[h] * LOG2E, F32) for h in heads], axis=1)
        mx = jnp.maximum(jnp.max(s, axis=0, keepdims=True), sink)
        p = jnp.exp2(s - mx).astype(BF16)
        own = (sub < HEAD_DIM) if g == 0 else (sub >= HEAD_DIM)
        v_ext_t = jnp.where(own, v_t, 1.0).astype(BF16)
        oe = jnp.dot(v_ext_t, p, preferred_element_type=F32)
        other = (1 - g) * HEAD_DIM
        den = oe[other:other + 1] + jnp.exp2(sink - mx)
        o_g = oe[g * HEAD_DIM:(g + 1) * HEAD_DIM] / den
        for i, h in enumerate(heads):
            o_t[h] = o_g[:, i * m:(i + 1) * m]
    for j in range(N_Q_HEADS // 2):
        pair_t = jnp.concatenate([o_t[2 * j], o_t[2 * j + 1]], axis=0)
        o_ref[:, j * LANES:(j + 1) * LANES] = pair_t.T.astype(BF16)


def _ctx_attn_kernel(sink_ref, q_ref, kv_ref, o_ref):
    _sink_attention(sink_ref, q_ref, kv_ref[:, :KD], None, kv_ref[:, KD:], o_ref)


def _ctx_attention(grp, q, kv, sink):
    n = grp.n
    return pl.pallas_call(
        _ctx_attn_kernel,
        grid=(grp.b,),
        in_specs=[pl.BlockSpec(memory_space=pltpu.SMEM), _row_spec(n, QD), _row_spec(n, 2 * KD)],
        out_specs=_row_spec(n, QD),
        out_shape=jax.ShapeDtypeStruct((grp.t, QD), BF16),
        compiler_params=_cparams("arbitrary"),
        name="ctx_attention",
    )(sink, q, kv)


def _lat_attn_kernel(sink_ref, q_ref, kvp_ref, kvc_ref, kvn_ref, kc_ref, vc_ref, o_ref, *, n):
    blk = pl.program_id(1)
    kv = jnp.concatenate([kvp_ref[...], kvc_ref[...], kvn_ref[...]], axis=0)
    kj = lax.broadcasted_iota(jnp.int32, (3 * ATTN_BLOCK, ATTN_BLOCK), 0)
    qi = lax.broadcasted_iota(jnp.int32, (3 * ATTN_BLOCK, ATTN_BLOCK), 1)
    rel = kj - ATTN_BLOCK - qi
    kpos = (blk - 1) * ATTN_BLOCK + kj
    valid = (jnp.abs(rel) <= WINDOW) & (kpos >= 0) & (kpos < n)
    bias_t = jnp.where(valid, 0.0, NEG_INF)
    k_all = jnp.concatenate([kv[:, :KD], kc_ref[...]], axis=0)
    v_all = jnp.concatenate([kv[:, KD:], vc_ref[...]], axis=0)
    _sink_attention(sink_ref, q_ref, k_all, bias_t, v_all, o_ref)


def _lat_attention(grp, q, kv, k_ctx, v_ctx, sink):
    nb = grp.n // ATTN_BLOCK
    past = k_ctx.shape[1]

    def band(off):
        return pl.BlockSpec((ATTN_BLOCK, 2 * KD),
                            lambda b, i: (b * nb + jnp.clip(i + off, 0, nb - 1), 0))

    ctx_spec = pl.BlockSpec((None, past, KD), lambda b, i: (b, 0, 0))
    return pl.pallas_call(
        functools.partial(_lat_attn_kernel, n=grp.n),
        grid=(grp.b, nb),
        in_specs=[pl.BlockSpec(memory_space=pltpu.SMEM),
                  pl.BlockSpec((ATTN_BLOCK, QD), lambda b, i: (b * nb + i, 0)),
                  band(-1), band(0), band(1), ctx_spec, ctx_spec],
        out_specs=pl.BlockSpec((ATTN_BLOCK, QD), lambda b, i: (b * nb + i, 0)),
        out_shape=jax.ShapeDtypeStruct((grp.t, QD), BF16),
        compiler_params=_cparams("arbitrary", "arbitrary"),
        name="lat_attention",
    )(sink, q, kv, kv, kv, k_ctx, v_ctx)


def _mlp_tail(x, mod_ref, g_ref, w1_ref, w2_ref):
    fchunk = 1024
    sh2, sc2, gt2 = _mod_chunk(mod_ref, 3), _mod_chunk(mod_ref, 4), _mod_chunk(mod_ref, 5)
    hb = (_rms(x, g_ref[2:3, :]) * (1.0 + sc2) + sh2).astype(BF16)
    acc = None
    for j in range(D_FF // fchunk):
        a = jnp.dot(hb, w1_ref[:, j * fchunk:(j + 1) * fchunk], preferred_element_type=F32)
        a = jnp.maximum(a, 0.0)
        part = _bdot(a * a, w2_ref[j * fchunk:(j + 1) * fchunk, :])
        acc = part if acc is None else acc + part
    return x + gt2 * _rms(acc, g_ref[3:4, :])


def _mlp_weight_specs(layer):
    single = pl.Buffered(1)
    return [pl.BlockSpec((None, D_MODEL, D_FF), lambda t: (layer, 0, 0), pipeline_mode=single),
            pl.BlockSpec((None, D_FF, D_MODEL), lambda t: (layer, 0, 0), pipeline_mode=single)]


def _postmix0_kernel(zc_ref, zp_ref, zn_ref, at_ref, x_ref, mod_ref, g_ref, cw_ref, w_ref,
                     w1_ref, w2_ref, o_ref, *, n, tm):
    c = CONV_DIM
    zc = zc_ref[...]
    u = zc[:, c:2 * c] * zc[:, 2 * c:]
    u_before = zp_ref[7:8, c:2 * c] * zp_ref[7:8, 2 * c:]
    u_after = zn_ref[0:1, c:2 * c] * zn_ref[0:1, 2 * c:]
    row = lax.broadcasted_iota(jnp.int32, (tm, 1), 0)
    pos = (pl.program_id(0) * tm + row) % n
    u_prev = jnp.where(row == 0, u_before, pltpu.roll(u, 1, axis=0))
    u_prev = jnp.where(pos == 0, 0.0, u_prev)
    u_next = jnp.where(row == tm - 1, u_after, pltpu.roll(u, tm - 1, axis=0))
    u_next = jnp.where(pos == n - 1, 0.0, u_next)
    conv = u_prev * cw_ref[0:1, :] + u * cw_ref[1:2, :] + u_next * cw_ref[2:3, :]
    mix = _bdot(zc[:, :c] * conv, w_ref[:c, :]) + _bdot(at_ref[...], w_ref[c:, :])
    gt1 = _mod_chunk(mod_ref, 2)
    x1 = x_ref[...] + gt1 * _rms(mix, g_ref[1:2, :])
    o_ref[...] = _mlp_tail(x1, mod_ref, g_ref, w1_ref, w2_ref)


def _postmix0(grp, zc, attn, x2d, mod, norm_g, conv_w, w_out, w1, w2, tm):
    r8 = tm // 8
    last8 = grp.t // 8 - 1
    return pl.pallas_call(
        functools.partial(_postmix0_kernel, n=grp.n, tm=tm),
        grid=(grp.t // tm,),
        in_specs=[_row_spec(tm, 3 * CONV_DIM),
                  pl.BlockSpec((8, 3 * CONV_DIM), lambda t: (jnp.maximum(t * r8 - 1, 0), 0)),
                  pl.BlockSpec((8, 3 * CONV_DIM), lambda t: (jnp.minimum((t + 1) * r8, last8), 0)),
                  _row_spec(tm, QD), _row_spec(tm, D_MODEL), grp.mod_spec(tm),
                  _const_spec((4, D_MODEL)), _const_spec((3, CONV_DIM)),
                  pl.BlockSpec((CONV_DIM + QD, D_MODEL), lambda t: (0, 0),
                               pipeline_mode=pl.Buffered(1))] + _mlp_weight_specs(0),
        out_specs=_row_spec(tm, D_MODEL),
        out_shape=jax.ShapeDtypeStruct((grp.t, D_MODEL), F32),
        compiler_params=_cparams("arbitrary"),
        name="postmix0_mlp",
    )(zc, zc, zc, attn, x2d, mod, norm_g, conv_w, w_out, w1, w2)


def _premix1_kernel(x_ref, mod_ref, g_ref, w_ref, wg_ref, gb_ref,
                    q_ref, k_ref, v_ref, og_ref, gf_ref, gbk_ref):
    sh1, sc1 = _mod_chunk(mod_ref, 0), _mod_chunk(mod_ref, 1)
    hb = (_rms(x_ref[...], g_ref[0:1, :]) * (1.0 + sc1) + sh1).astype(BF16)
    z = jnp.dot(hb, w_ref[...], preferred_element_type=F32)
    q_ref[...] = z[:, :GLA_QK] * (GLA_DK ** -0.5)
    k_ref[...] = z[:, GLA_QK:2 * GLA_QK]
    v_ref[...] = z[:, 2 * GLA_QK:2 * GLA_QK + GLA_VD].astype(BF16)
    og_ref[...] = z[:, 2 * GLA_QK + GLA_VD:GLA_MAIN]
    r = z[:, GLA_MAIN:]
    pre = _bdot(r, wg_ref[...]) + gb_ref[...]
    gate = ((jnp.minimum(pre, 0.0) - jnp.log1p(jnp.exp(-jnp.abs(pre))))
            * (LOG2E / GLA_GATE_NORM))
    gf_ref[...] = gate[:, :GLA_QK]
    gbk_ref[...] = gate[:, GLA_QK:]


def _premix1(grp, x2d, mod, norm_g, w_in, w_gate, gate_bias, tm):
    t = grp.t
    return pl.pallas_call(
        _premix1_kernel,
        grid=(t // tm,),
        in_specs=[_row_spec(tm, D_MODEL), grp.mod_spec(tm), _const_spec((4, D_MODEL)),
                  _const_spec((D_MODEL, GLA_MAIN + LANES)), _const_spec((LANES, 2 * GLA_QK)),
                  _const_spec((1, 2 * GLA_QK))],
        out_specs=[_row_spec(tm, GLA_QK), _row_spec(tm, GLA_QK), _row_spec(tm, GLA_VD),
                   _row_spec(tm, GLA_VD), _row_spec(tm, GLA_QK), _row_spec(tm, GLA_QK)],
        out_shape=[jax.ShapeDtypeStruct((t, GLA_QK), F32), jax.ShapeDtypeStruct((t, GLA_QK), F32),
                   jax.ShapeDtypeStruct((t, GLA_VD), BF16),
                   jax.ShapeDtypeStruct((t, GLA_VD), F32), jax.ShapeDtypeStruct((t, GLA_QK), F32),
                   jax.ShapeDtypeStruct((t, GLA_QK), F32)],
        compiler_params=_cparams("arbitrary"),
        name="premix1",
    )(x2d, mod, norm_g, w_in, w_gate, gate_bias)


def _split3(x):
    hi = x.astype(BF16)
    r1 = x - hi.astype(F32)
    mid = r1.astype(BF16)
    lo = (r1 - mid.astype(F32)).astype(BF16)
    return hi, mid, lo


def _level_exponent(b, g, s, reverse):
    if s == 1:
        row = lax.broadcasted_iota(jnp.int32, (GLA_TILE, 1), 0)
        return jnp.where(row % 2 == (0 if reverse else 1), g, 0.0)
    idx = s if reverse else s - 1
    if s >= 8:
        n = GLA_TILE // (2 * s)
        b4 = b.reshape(n, 2, s, GLA_DK)
        first, second = b4[:, 0:1], b4[:, 1:2]
        r = (second[:, :, 0:1] if reverse else first[:, :, s - 1:s])
        parts = [first - r, r - second] if reverse else [r - first, second - r]
        return jnp.concatenate(parts, axis=1).reshape(GLA_TILE, GLA_DK)
    b8 = b.reshape(GLA_TILE // 8, 8, GLA_DK)
    sub = lax.broadcasted_iota(jnp.int32, (1, 8, 1), 1)
    if s == 4:
        r = b8[:, idx:idx + 1, :]
    else:
        assert s == 2
        r = jnp.where(sub < 4, b8[:, idx:idx + 1, :], b8[:, 4 + idx:5 + idx, :])
    in_second = (sub // s) % 2 == 1
    sign = jnp.where(in_second != reverse, 1.0, -1.0)
    return ((b8 - r) * sign).reshape(GLA_TILE, GLA_DK)


def _gla_tile(q, k, g, b, v, lvl, s_ref, reverse):
    half = GLA_TILE // 2
    edge = 0 if reverse else GLA_TILE - 1
    b_last = b[edge:edge + 1, :]
    qe = (q * jnp.exp2(b)).astype(BF16)
    ke = (k * jnp.exp2(b_last - b)).astype(BF16)

    lo, hi = slice(0, half), slice(half, GLA_TILE)
    qb, kb = q.astype(BF16), k.astype(BF16)
    pairs = _bdot_nt(qb, kb)
    blocks = [jnp.where(lvl == 0, pairs[lo, lo], 0.0), jnp.where(lvl == 0, pairs[hi, hi], 0.0)]
    cross = None
    s, level = 1, 1
    while s < GLA_TILE:
        f = jnp.exp2(_level_exponent(b, g, s, reverse)).astype(BF16)
        pairs = _bdot_nt(qb * f, kb * f)
        if s == half:
            cross = pairs[lo, hi] if reverse else pairs[hi, lo]
        else:
            blocks = [jnp.where(lvl == level, pairs[lo, lo], blocks[0]),
                      jnp.where(lvl == level, pairs[hi, hi], blocks[1])]
        s, level = 2 * s, level + 1

    zero = jnp.zeros((half, half), BF16)
    a_lo, a_hi, a_x = blocks[0].astype(BF16), blocks[1].astype(BF16), cross.astype(BF16)
    if reverse:
        att = jnp.concatenate([jnp.concatenate([a_lo, a_x], axis=1),
                               jnp.concatenate([zero, a_hi], axis=1)], axis=0)
    else:
        att = jnp.concatenate([jnp.concatenate([a_lo, zero], axis=1),
                               jnp.concatenate([a_x, a_hi], axis=1)], axis=0)
    st = s_ref[...]
    o = jnp.dot(att, v, preferred_element_type=F32) + _bdot_nt(qe, st)
    s_ref[...] = st * jnp.exp2(b_last) + lax.dot_general(
        v, ke, TN_DIMS, preferred_element_type=F32)
    return o


def _gla_kernel(*refs, zero_init, nt):
    if zero_init:
        (tri_ref, lvl_ref, q_ref, k_ref, gf_ref, gb_ref, v_ref,
         o_ref, sfo_ref, sbo_ref, sf_ref, sb_ref) = refs
        sf_ref[...] = jnp.zeros_like(sf_ref)
        sb_ref[...] = jnp.zeros_like(sb_ref)
    else:
        (tri_ref, lvl_ref, q_ref, k_ref, gf_ref, gb_ref, v_ref, s0f_ref, s0b_ref,
         o_ref, sfo_ref, sbo_ref, sf_ref, sb_ref) = refs
        sf_ref[...] = s0f_ref[...].T
        sb_ref[...] = s0b_ref[...].T
    o_ref[...] = jnp.zeros_like(o_ref)

    def tile_rows(tile):
        r0 = tile * GLA_TILE
        return pl.ds(r0 if isinstance(r0, int) else pl.multiple_of(r0, GLA_TILE), GLA_TILE)

    def step(t, carry):
        rows_f, rows_b = tile_rows(t), tile_rows(nt - 1 - t)
        g_f, g_b = gf_ref[rows_f, :], gb_ref[rows_b, :]
        c = None
        for part in _split3(jnp.concatenate([g_f, g_b], axis=1)):
            term = jnp.dot(tri_ref[...], part, preferred_element_type=F32)
            c = term if c is None else c + term
        c_b = c[:, GLA_DK:]
        b_f = c[:, :GLA_DK]
        b_b = (c_b[GLA_TILE - 1:GLA_TILE, :] - c_b) + g_b
        for d, (rows, g, b, s_ref) in enumerate(((rows_f, g_f, b_f, sf_ref),
                                                 (rows_b, g_b, b_b, sb_ref))):
            o = _gla_tile(q_ref[rows, :], k_ref[rows, :], g, b, v_ref[rows, :],
                          lvl_ref[d], s_ref, bool(d))
            o_ref[rows, :] += o
        return carry

    if nt == 1:
        step(0, 0)
    else:
        lax.fori_loop(0, nt, step, 0)
    sfo_ref[...] = sf_ref[...].T
    sbo_ref[...] = sb_ref[...].T


def _gla_scan(grp, q, k, v, gf, gb, s0f, s0b, tri, lvl):
    n = grp.n
    nt = n // GLA_TILE
    zero_init = s0f is None
    half = GLA_TILE // 2
    seq_dk = pl.BlockSpec((n, GLA_DK), lambda b, h: (b, h))
    seq_dv = pl.BlockSpec((n, GLA_DV), lambda b, h: (b, h))
    state_spec = pl.BlockSpec((None, None, GLA_DK, GLA_DV), lambda b, h: (b, h, 0, 0))
    in_specs = [_const_spec((GLA_TILE, GLA_TILE)), _const_spec((2, half, half)),
                seq_dk, seq_dk, seq_dk, seq_dk, seq_dv]
    args = [tri, lvl, q, k, gf, gb, v]
    if not zero_init:
        in_specs += [state_spec, state_spec]
        args += [s0f, s0b]
    state_shape = jax.ShapeDtypeStruct((grp.b, GLA_HEADS, GLA_DK, GLA_DV), F32)
    return pl.pallas_call(
        functools.partial(_gla_kernel, zero_init=zero_init, nt=nt),
        grid=(grp.b, GLA_HEADS),
        in_specs=in_specs,
        out_specs=[seq_dv, state_spec, state_spec],
        out_shape=[jax.ShapeDtypeStruct((grp.t, GLA_VD), F32), state_shape, state_shape],
        scratch_shapes=[pltpu.VMEM((GLA_DV, GLA_DK), F32), pltpu.VMEM((GLA_DV, GLA_DK), F32)],
        compiler_params=_cparams("arbitrary", "arbitrary"),
        name="gla_scan",
    )(*args)


def _gla_constants():
    half = GLA_TILE // 2
    i = jnp.arange(GLA_TILE)[:, None]
    j = jnp.arange(GLA_TILE)[None, :]
    tri = (j <= i).astype(BF16)
    ih, jh = i[:half], j[:, :half]
    x = jnp.bitwise_xor(ih, jh)
    level = sum((x >= (1 << p)).astype(jnp.int32) for p in range(half.bit_length() - 1))
    lvl = jnp.stack([jnp.where(jh <= ih, level, -1), jnp.where(jh >= ih, level, -1)])
    return tri, lvl


def _postmix1_kernel(o_ref, og_ref, x_ref, mod_ref, g_ref, gn_ref, w_ref, w1_ref, w2_ref,
                     out_ref):
    gn = gn_ref[...]
    mix = None
    for h in range(GLA_HEADS):
        cols = slice(h * GLA_DV, (h + 1) * GLA_DV)
        o = _rms(o_ref[:, cols], gn)
        og = og_ref[:, cols]
        y = o * (og / (1.0 + jnp.exp(-og)))
        part = _bdot(y, w_ref[cols, :])
        mix = part if mix is None else mix + part
    gt1 = _mod_chunk(mod_ref, 2)
    x1 = x_ref[...] + gt1 * _rms(mix, g_ref[1:2, :])
    out_ref[...] = _mlp_tail(x1, mod_ref, g_ref, w1_ref, w2_ref)


def _postmix1(grp, o, og, x2d, mod, norm_g, gla_norm_g, w_out, w1, w2, tm):
    return pl.pallas_call(
        _postmix1_kernel,
        grid=(grp.t // tm,),
        in_specs=[_row_spec(tm, GLA_VD), _row_spec(tm, GLA_VD),
                  _row_spec(tm, D_MODEL), grp.mod_spec(tm), _const_spec((4, D_MODEL)),
                  _const_spec((1, GLA_DV)),
                  pl.BlockSpec((GLA_VD, D_MODEL), lambda t: (0, 0),
                               pipeline_mode=pl.Buffered(1))] + _mlp_weight_specs(1),
        out_specs=_row_spec(tm, D_MODEL),
        out_shape=jax.ShapeDtypeStruct((grp.t, D_MODEL), F32),
        compiler_params=_cparams("arbitrary"),
        name="postmix1_mlp",
    )(o, og, x2d, mod, norm_g, gla_norm_g, w_out, w1, w2)


def _run_group(grp, x, mods, p, rope_tabs, k_ctx, v_ctx, s0f, s0b, tm):
    x2d = x.reshape(grp.t, D_MODEL)
    zc, q, kv, *cache_t = _premix0(grp, x2d, mods[0], p["norm_g"][0], p["ab_w_in"], rope_tabs, tm)
    if k_ctx is None:
        attn = _ctx_attention(grp, q, kv, p["sink"])
    else:
        attn = _lat_attention(grp, q, kv, k_ctx, v_ctx, p["sink"])
    x2d = _postmix0(grp, zc, attn, x2d, mods[0], p["norm_g"][0], p["conv_w"], p["ab_w_out"],
                    p["mlp_w1"], p["mlp_w2"], tm)
    gq, gk, gv, og, gf, gb = _premix1(grp, x2d, mods[1], p["norm_g"][1], p["gla_w_in"],
                                      p["gla_w_gate"], p["gla_gate_bias"], tm)
    o, sf, sb = _gla_scan(grp, gq, gk, gv, gf, gb, s0f, s0b, p["tri"], p["lvl"])
    x2d = _postmix1(grp, o, og, x2d, mods[1], p["norm_g"][1], p["gla_norm_g"], p["gla_w_out"],
                    p["mlp_w1"], p["mlp_w2"], tm)
    return x2d.reshape(x.shape), cache_t, sf, sb


def kernel(x_prompt, x_sample, cache_k, cache_v, state_fwd, state_bwd, c, c_ctx, mod_w, mod_b,
           norm_g, ab_w_in, conv_w, attn_sink, ab_w_out, gla_w_in, gla_gate_w, gla_gate_b,
           gla_norm_g, gla_w_out, mlp_w1, mlp_w2):
    b_ctx, n_ctx, _ = x_prompt.shape
    b_lat, n_lat, _ = x_sample.shape
    assert mod_w.shape[0] == 2 and ab_w_in.shape[0] == 1 and gla_w_in.shape[0] == 1
    assert 1 + b_lat <= 8

    cond8 = jnp.zeros((8, D_MODEL), F32).at[0].set(c_ctx).at[1:1 + b_lat].set(c)
    mod = _modulation(cond8, mod_w, mod_b)
    mods_ctx = [mod[l, 0:1].reshape(1, 1, -1) for l in range(2)]
    mods_lat = [mod[l, 1:1 + b_lat].reshape(b_lat, 1, -1) for l in range(2)]

    gw = gla_w_in[0]
    w_gate = jnp.zeros((LANES, 2 * GLA_QK), F32)
    w_gate = w_gate.at[:GLA_RANK, :GLA_QK].set(gla_gate_w[0, 0])
    w_gate = w_gate.at[GLA_RANK:2 * GLA_RANK, GLA_QK:].set(gla_gate_w[0, 1])
    tri, lvl = _gla_constants()
    p = {
        "norm_g": norm_g,
        "ab_w_in": ab_w_in[0].astype(BF16),
        "conv_w": conv_w[0],
        "sink": attn_sink[0],
        "ab_w_out": ab_w_out[0].astype(BF16),
        "mlp_w1": mlp_w1.astype(BF16),
        "mlp_w2": mlp_w2.astype(BF16),
        "gla_w_in": jnp.pad(gw, ((0, 0), (0, LANES - 2 * GLA_RANK))).astype(BF16),
        "gla_w_gate": w_gate.astype(BF16),
        "gla_gate_bias": gla_gate_b[0].reshape(1, 2 * GLA_QK),
        "gla_norm_g": gla_norm_g[0].reshape(1, GLA_DV),
        "gla_w_out": gla_w_out[0].astype(BF16),
        "tri": tri,
        "lvl": lvl,
    }

    ctx = _Group(b_ctx, n_ctx, per_seq_mod=False)
    lat = _Group(b_lat, n_lat, per_seq_mod=True)

    y_prompt, (k_t, v_t), sf, sb = _run_group(ctx, x_prompt, mods_ctx, p, None, None, None,
                                              None, None, tm=512)

    def cache_layout(t):
        t = t.reshape(b_ctx, 1, N_KV_HEADS, HEAD_DIM, n_ctx)
        return jnp.transpose(t, (0, 1, 4, 2, 3))

    new_k, new_v = cache_layout(k_t), cache_layout(v_t)
    new_sf, new_sb = sf[:, None], sb[:, None]

    past = cache_k.shape[2]
    k_ctx = cache_k[:, 0].reshape(b_lat, past, KD)
    v_ctx = cache_v[:, 0].reshape(b_lat, past, KD)
    y_sample, _, _, _ = _run_group(lat, x_sample, mods_lat, p, _rope_tables(n_lat), k_ctx, v_ctx,
                                   state_fwd[:, 0], state_bwd[:, 0], tm=512)
    return (y_prompt, y_sample, new_k, new_v, new_sf, new_sb)
```

```python
import functools

import jax
import jax.numpy as jnp
from jax import lax
from jax.experimental import pallas as pl
from jax.experimental.pallas import tpu as pltpu

F32 = jnp.float32
BF16 = jnp.bfloat16

D_MODEL = 1024
MOD_CHUNKS = 6
EPS = 1e-6
CONV_DIM = 512
N_Q_HEADS = 8
N_KV_HEADS = 2
GQA_GROUP = 4
HEAD_DIM = 64
WINDOW = 128
ATTN_BLOCK = 128
GRID_W = 64
ROPE_BASE = 10000.0
QD = N_Q_HEADS * HEAD_DIM
KD = N_KV_HEADS * HEAD_DIM
AB_IN = 3 * CONV_DIM + QD + 2 * KD
GLA_HEADS = 4
GLA_DK = 128
GLA_DV = 256
GLA_RANK = 16
GLA_GATE_NORM = 16.0
GLA_TILE = 256
GLA_HEADS_PER_STEP = 2
LOG2E = 1.4426950408889634
GLA_QK = GLA_HEADS * GLA_DK
GLA_VD = GLA_HEADS * GLA_DV
GLA_MAIN = 2 * GLA_QK + 2 * GLA_VD
D_FF = 4 * D_MODEL
NEG_INF = -1e30
LANES = 128
VMEM_LIMIT = 56 * 1024 * 1024

NT_DIMS = (((1,), (1,)), ((), ()))
TN_DIMS = (((0,), (0,)), ((), ()))


def _cparams(*sem):
    return pltpu.CompilerParams(dimension_semantics=sem, vmem_limit_bytes=VMEM_LIMIT)


def _bdot(a, b):
    return jnp.dot(a.astype(BF16), b.astype(BF16), preferred_element_type=F32)


def _bdot_nt(a, b):
    return lax.dot_general(a.astype(BF16), b.astype(BF16), NT_DIMS, preferred_element_type=F32)


def _rms(x, g):
    ms = jnp.mean(x * x, axis=-1, keepdims=True)
    return x * lax.rsqrt(ms + EPS) * g


def _mod_chunk(mod_ref, i):
    return mod_ref[:, i * D_MODEL:(i + 1) * D_MODEL]


def _const_spec(shape):
    return pl.BlockSpec(shape, lambda *_: (0,) * len(shape))


def _mod_kernel(cond_ref, w_ref, b_ref, o_ref):
    cnd = cond_ref[...]
    s = cnd / (1.0 + jnp.exp(-cnd))
    o_ref[...] = _bdot(s, w_ref[...]) + b_ref[...]


def _modulation(cond8, mod_w, mod_b):
    depth = mod_w.shape[0]
    n = mod_w.shape[2]
    tn = 1536
    return pl.pallas_call(
        _mod_kernel,
        grid=(depth, n // tn),
        in_specs=[
            pl.BlockSpec((8, D_MODEL), lambda l, j: (0, 0)),
            pl.BlockSpec((None, D_MODEL, tn), lambda l, j: (l, 0, j)),
            pl.BlockSpec((None, 1, tn), lambda l, j: (l, 0, j)),
        ],
        out_specs=pl.BlockSpec((None, 8, tn), lambda l, j: (l, 0, j)),
        out_shape=jax.ShapeDtypeStruct((depth, 8, n), F32),
        compiler_params=_cparams("arbitrary", "arbitrary"),
        name="modulation",
    )(cond8, mod_w, mod_b.reshape(depth, 1, n))


class _Group:
    def __init__(self, b, n, per_seq_mod):
        self.b, self.n, self.t = b, n, b * n
        self.per_seq_mod = per_seq_mod

    def mod_spec(self, tm):
        if self.per_seq_mod:
            assert self.n % tm == 0
            per = self.n // tm
            return pl.BlockSpec((None, 1, MOD_CHUNKS * D_MODEL), lambda t: (t // per, 0, 0))
        return pl.BlockSpec((None, 1, MOD_CHUNKS * D_MODEL), lambda t: (0, 0, 0))


def _row_spec(tm, width):
    return pl.BlockSpec((tm, width), lambda t: (t, 0))


def _rope(x, cos, sin_lo, sin_hi):
    return (x * cos + pltpu.roll(x, LANES - 16, axis=1) * sin_lo
            + pltpu.roll(x, 16, axis=1) * sin_hi)


def _premix0_kernel(*refs, rope):
    if rope:
        x_ref, mod_ref, g_ref, w_ref, cos_ref, slo_ref, shi_ref, zc_ref, q_ref, kv_ref = refs
    else:
        x_ref, mod_ref, g_ref, w_ref, zc_ref, q_ref, kv_ref, kt_ref, vt_ref = refs
    sh1, sc1 = _mod_chunk(mod_ref, 0), _mod_chunk(mod_ref, 1)
    h = _rms(x_ref[...], g_ref[0:1, :]) * (1.0 + sc1) + sh1
    z = _bdot(h, w_ref[...])
    c3 = 3 * CONV_DIM
    zc_ref[...] = z[:, :c3]
    scale = HEAD_DIM ** -0.5 * LOG2E
    if rope:
        cos, slo, shi = cos_ref[...], slo_ref[...], shi_ref[...]
        for j in range(QD // LANES):
            qs = z[:, c3 + j * LANES:c3 + (j + 1) * LANES]
            q_ref[:, j * LANES:(j + 1) * LANES] = (_rope(qs, cos, slo, shi) * scale).astype(BF16)
        kv_ref[:, :KD] = _rope(z[:, c3 + QD:c3 + QD + KD], cos, slo, shi)
    else:
        q_ref[...] = (z[:, c3:c3 + QD] * scale).astype(BF16)
        kv_ref[:, :KD] = z[:, c3 + QD:c3 + QD + KD]
        n = kt_ref.shape[2]
        for j in range(kt_ref.shape[0]):
            kt_ref[j] = z[j * n:(j + 1) * n, c3 + QD:c3 + QD + KD].T
            vt_ref[j] = z[j * n:(j + 1) * n, c3 + QD + KD:].T
    kv_ref[:, KD:] = z[:, c3 + QD + KD:]


def _premix0(grp, x2d, mod, norm_g, w_in, rope_tabs, tm):
    rope = rope_tabs is not None
    in_specs = [_row_spec(tm, D_MODEL), grp.mod_spec(tm), _const_spec((4, D_MODEL)),
                _const_spec((D_MODEL, AB_IN))]
    args = [x2d, mod, norm_g, w_in]
    if rope:
        per = grp.n // tm
        in_specs += [pl.BlockSpec((tm, LANES), lambda t: (t % per, 0))] * 3
        args += list(rope_tabs)
    out_specs = [_row_spec(tm, 3 * CONV_DIM), _row_spec(tm, QD), _row_spec(tm, 2 * KD)]
    out_shape = [jax.ShapeDtypeStruct((grp.t, 3 * CONV_DIM), F32),
                 jax.ShapeDtypeStruct((grp.t, QD), BF16),
                 jax.ShapeDtypeStruct((grp.t, 2 * KD), F32)]
    if not rope:
        assert tm % grp.n == 0
        out_specs += [pl.BlockSpec((tm // grp.n, KD, grp.n), lambda t: (t, 0, 0))] * 2
        out_shape += [jax.ShapeDtypeStruct((grp.b, KD, grp.n), F32)] * 2
    return pl.pallas_call(
        functools.partial(_premix0_kernel, rope=rope),
        grid=(grp.t // tm,),
        in_specs=in_specs,
        out_specs=out_specs,
        out_shape=out_shape,
        compiler_params=_cparams("arbitrary"),
        name="premix0_rope" if rope else "premix0",
    )(*args)


def _rope_tables(n):
    rows = n // GRID_W
    pos_r = jnp.repeat(jnp.arange(rows), GRID_W)
    pos_c = jnp.tile(jnp.arange(GRID_W), rows)
    half = HEAD_DIM // 2
    quarter = half // 2
    inv = ROPE_BASE ** (-(jnp.arange(quarter, dtype=F32) * 2.0 / half))

    def cs(pos):
        ang = pos.astype(F32)[:, None] * inv[None, :]
        return jnp.cos(ang), jnp.sin(ang)

    cr, sr = cs(pos_r)
    cc, sc = cs(pos_c)
    zero = jnp.zeros_like(sr)
    cos = jnp.concatenate([cr, cr, cc, cc], axis=1)
    sin_lo = jnp.concatenate([-sr, zero, -sc, zero], axis=1)
    sin_hi = jnp.concatenate([zero, sr, zero, sc], axis=1)
    rep = LANES // HEAD_DIM
    return tuple(jnp.tile(t, (1, rep)) for t in (cos, sin_lo, sin_hi))


def _sink_attention(sink_ref, q_ref, k_all, bias_t, v_all, o_ref):
    assert KD == LANES == 2 * HEAD_DIM and GQA_GROUP == 4
    m = q_ref.shape[0]
    lane = lax.broadcasted_iota(jnp.int32, (1, LANES), 1)
    sub = lax.broadcasted_iota(jnp.int32, (LANES, 1), 0)
    v_t = v_all.T
    k_swapped = pltpu.roll(k_all, HEAD_DIM, axis=1)
    o_t = {}
    for g in range(N_KV_HEADS):
        k_low, k_high = (k_all, k_swapped) if g == 0 else (k_swapped, k_all)
        kz_even = jnp.where(lane < HEAD_DIM, k_low, 0.0).astype(BF16)
        kz_odd = jnp.where(lane >= HEAD_DIM, k_high, 0.0).astype(BF16)
        qq = jnp.concatenate([q_ref[:, (2 * g) * LANES:(2 * g + 1) * LANES],
                              q_ref[:, (2 * g + 1) * LANES:(2 * g + 2) * LANES]], axis=0)
        s = jnp.concatenate([_bdot_nt(kz_even, qq), _bdot_nt(kz_odd, qq)], axis=1)
        heads = [4 * g, 4 * g + 2, 4 * g + 1, 4 * g + 3]
        if bias_t is not None:
            nb = bias_t.shape[0]
            s = jnp.concatenate([s[:nb] + jnp.concatenate([bias_t] * GQA_GROUP, axis=1), s[nb:]],
                                axis=0)
        sink = jnp.concatenate([jnp.full((1, m), sink_ref[h] * LOG2E, F32) for h in heads], axis=1)
        mx = jnp.maximum(jnp.max(s, axis=0, keepdims=True), sink)
        p = jnp.exp2(s - mx).astype(BF16)
        own = (sub < HEAD_DIM) if g == 0 else (sub >= HEAD_DIM)
        v_ext_t = jnp.where(own, v_t, 1.0).astype(BF16)
        oe = jnp.dot(v_ext_t, p, preferred_element_type=F32)
        other = (1 - g) * HEAD_DIM
        den = oe[other:other + 1] + jnp.exp2(sink - mx)
        o_g = oe[g * HEAD_DIM:(g + 1) * HEAD_DIM] / den
        for i, h in enumerate(heads):
            o_t[h] = o_g[:, i * m:(i + 1) * m]
    for j in range(N_Q_HEADS // 2):
        pair_t = jnp.concatenate([o_t[2 * j], o_t[2 * j + 1]], axis=0)
        o_ref[:, j * LANES:(j + 1) * LANES] = pair_t.T.astype(BF16)


def _ctx_attn_kernel(sink_ref, q_ref, kv_ref, o_ref):
    _sink_attention(sink_ref, q_ref, kv_ref[:, :KD], None, kv_ref[:, KD:], o_ref)


def _ctx_attention(grp, q, kv, sink):
    n = grp.n
    return pl.pallas_call(
        _ctx_attn_kernel,
        grid=(grp.b,),
        in_specs=[pl.BlockSpec(memory_space=pltpu.SMEM), _row_spec(n, QD), _row_spec(n, 2 * KD)],
        out_specs=_row_spec(n, QD),
        out_shape=jax.ShapeDtypeStruct((grp.t, QD), BF16),
        compiler_params=_cparams("arbitrary"),
        name="ctx_attention",
    )(sink, q, kv)


def _lat_attn_kernel(sink_ref, q_ref, kvp_ref, kvc_ref, kvn_ref, kc_ref, vc_ref, o_ref, *, n):
    blk = pl.program_id(1)
    kv = jnp.concatenate([kvp_ref[...], kvc_ref[...], kvn_ref[...]], axis=0)
    kj = lax.broadcasted_iota(jnp.int32, (3 * ATTN_BLOCK, ATTN_BLOCK), 0)
    qi = lax.broadcasted_iota(jnp.int32, (3 * ATTN_BLOCK, ATTN_BLOCK), 1)
    rel = kj - ATTN_BLOCK - qi
    kpos = (blk - 1) * ATTN_BLOCK + kj
    valid = (jnp.abs(rel) <= WINDOW) & (kpos >= 0) & (kpos < n)
    bias_t = jnp.where(valid, 0.0, NEG_INF)
    k_all = jnp.concatenate([kv[:, :KD], kc_ref[...]], axis=0)
    v_all = jnp.concatenate([kv[:, KD:], vc_ref[...]], axis=0)
    _sink_attention(sink_ref, q_ref, k_all, bias_t, v_all, o_ref)


def _lat_attention(grp, q, kv, k_ctx, v_ctx, sink):
    nb = grp.n // ATTN_BLOCK
    past = k_ctx.shape[1]

    def band(off):
        return pl.BlockSpec((ATTN_BLOCK, 2 * KD),
                            lambda b, i: (b * nb + jnp.clip(i + off, 0, nb - 1), 0))

    ctx_spec = pl.BlockSpec((None, past, KD), lambda b, i: (b, 0, 0))
    return pl.pallas_call(
        functools.partial(_lat_attn_kernel, n=grp.n),
        grid=(grp.b, nb),
        in_specs=[pl.BlockSpec(memory_space=pltpu.SMEM),
                  pl.BlockSpec((ATTN_BLOCK, QD), lambda b, i: (b * nb + i, 0)),
                  band(-1), band(0), band(1), ctx_spec, ctx_spec],
        out_specs=pl.BlockSpec((ATTN_BLOCK, QD), lambda b, i: (b * nb + i, 0)),
        out_shape=jax.ShapeDtypeStruct((grp.t, QD), BF16),
        compiler_params=_cparams("arbitrary", "arbitrary"),
        name="lat_attention",
    )(sink, q, kv, kv, kv, k_ctx, v_ctx)


def _mlp_tail(x, mod_ref, g_ref, w1_ref, w2_ref):
    fchunk = 1024
    sh2, sc2, gt2 = _mod_chunk(mod_ref, 3), _mod_chunk(mod_ref, 4), _mod_chunk(mod_ref, 5)
    hb = (_rms(x, g_ref[2:3, :]) * (1.0 + sc2) + sh2).astype(BF16)
    acc = None
    for j in range(D_FF // fchunk):
        a = jnp.dot(hb, w1_ref[:, j * fchunk:(j + 1) * fchunk], preferred_element_type=F32)
        a = jnp.maximum(a, 0.0)
        part = _bdot(a * a, w2_ref[j * fchunk:(j + 1) * fchunk, :])
        acc = part if acc is None else acc + part
    return x + gt2 * _rms(acc, g_ref[3:4, :])


def _mlp_weight_specs(layer):
    single = pl.Buffered(1)
    return [pl.BlockSpec((None, D_MODEL, D_FF), lambda t: (layer, 0, 0), pipeline_mode=single),
            pl.BlockSpec((None, D_FF, D_MODEL), lambda t: (layer, 0, 0), pipeline_mode=single)]


def _postmix0_kernel(zc_ref, zp_ref, zn_ref, at_ref, x_ref, mod_ref, g_ref, cw_ref, w_ref,
                     w1_ref, w2_ref, o_ref, *, n, tm):
    c = CONV_DIM
    zc = zc_ref[...]
    u = zc[:, c:2 * c] * zc[:, 2 * c:]
    u_before = zp_ref[7:8, c:2 * c] * zp_ref[7:8, 2 * c:]
    u_after = zn_ref[0:1, c:2 * c] * zn_ref[0:1, 2 * c:]
    row = lax.broadcasted_iota(jnp.int32, (tm, 1), 0)
    pos = (pl.program_id(0) * tm + row) % n
    u_prev = jnp.where(row == 0, u_before, pltpu.roll(u, 1, axis=0))
    u_prev = jnp.where(pos == 0, 0.0, u_prev)
    u_next = jnp.where(row == tm - 1, u_after, pltpu.roll(u, tm - 1, axis=0))
    u_next = jnp.where(pos == n - 1, 0.0, u_next)
    conv = u_prev * cw_ref[0:1, :] + u * cw_ref[1:2, :] + u_next * cw_ref[2:3, :]
    mix = _bdot(zc[:, :c] * conv, w_ref[:c, :]) + _bdot(at_ref[...], w_ref[c:, :])
    gt1 = _mod_chunk(mod_ref, 2)
    x1 = x_ref[...] + gt1 * _rms(mix, g_ref[1:2, :])
    o_ref[...] = _mlp_tail(x1, mod_ref, g_ref, w1_ref, w2_ref)


def _postmix0(grp, zc, attn, x2d, mod, norm_g, conv_w, w_out, w1, w2, tm):
    r8 = tm // 8
    last8 = grp.t // 8 - 1
    return pl.pallas_call(
        functools.partial(_postmix0_kernel, n=grp.n, tm=tm),
        grid=(grp.t // tm,),
        in_specs=[_row_spec(tm, 3 * CONV_DIM),
                  pl.BlockSpec((8, 3 * CONV_DIM), lambda t: (jnp.maximum(t * r8 - 1, 0), 0)),
                  pl.BlockSpec((8, 3 * CONV_DIM), lambda t: (jnp.minimum((t + 1) * r8, last8), 0)),
                  _row_spec(tm, QD), _row_spec(tm, D_MODEL), grp.mod_spec(tm),
                  _const_spec((4, D_MODEL)), _const_spec((3, CONV_DIM)),
                  pl.BlockSpec((CONV_DIM + QD, D_MODEL), lambda t: (0, 0),
                               pipeline_mode=pl.Buffered(1))] + _mlp_weight_specs(0),
        out_specs=_row_spec(tm, D_MODEL),
        out_shape=jax.ShapeDtypeStruct((grp.t, D_MODEL), F32),
        compiler_params=_cparams("arbitrary"),
        name="postmix0_mlp",
    )(zc, zc, zc, attn, x2d, mod, norm_g, conv_w, w_out, w1, w2)


def _premix1_kernel(x_ref, mod_ref, g_ref, w_ref, wg_ref, gb_ref,
                    q_ref, k_ref, v_ref, og_ref, gf_ref, gbk_ref):
    sh1, sc1 = _mod_chunk(mod_ref, 0), _mod_chunk(mod_ref, 1)
    half = x_ref.shape[0] // 2
    for rows in (slice(0, half), slice(half, 2 * half)):
        hb = (_rms(x_ref[rows, :], g_ref[0:1, :]) * (1.0 + sc1) + sh1).astype(BF16)
        z = jnp.dot(hb, w_ref[...], preferred_element_type=F32)
        q_ref[rows, :] = z[:, :GLA_QK] * (GLA_DK ** -0.5)
        k_ref[rows, :] = z[:, GLA_QK:2 * GLA_QK]
        v_ref[rows, :] = z[:, 2 * GLA_QK:2 * GLA_QK + GLA_VD].astype(BF16)
        og_ref[rows, :] = z[:, 2 * GLA_QK + GLA_VD:GLA_MAIN]
        r = z[:, GLA_MAIN:]
        pre = _bdot(r, wg_ref[...]) + gb_ref[...]
        gate = ((jnp.minimum(pre, 0.0) - jnp.log1p(jnp.exp(-jnp.abs(pre))))
                * (LOG2E / GLA_GATE_NORM))
        gf_ref[rows, :] = gate[:, :GLA_QK]
        gbk_ref[rows, :] = gate[:, GLA_QK:]


def _premix1(grp, x2d, mod, norm_g, w_in, w_gate, gate_bias, tm):
    t = grp.t
    return pl.pallas_call(
        _premix1_kernel,
        grid=(t // tm,),
        in_specs=[_row_spec(tm, D_MODEL), grp.mod_spec(tm), _const_spec((4, D_MODEL)),
                  _const_spec((D_MODEL, GLA_MAIN + LANES)), _const_spec((LANES, 2 * GLA_QK)),
                  _const_spec((1, 2 * GLA_QK))],
        out_specs=[_row_spec(tm, GLA_QK), _row_spec(tm, GLA_QK), _row_spec(tm, GLA_VD),
                   _row_spec(tm, GLA_VD), _row_spec(tm, GLA_QK), _row_spec(tm, GLA_QK)],
        out_shape=[jax.ShapeDtypeStruct((t, GLA_QK), F32), jax.ShapeDtypeStruct((t, GLA_QK), F32),
                   jax.ShapeDtypeStruct((t, GLA_VD), BF16),
                   jax.ShapeDtypeStruct((t, GLA_VD), F32), jax.ShapeDtypeStruct((t, GLA_QK), F32),
                   jax.ShapeDtypeStruct((t, GLA_QK), F32)],
        compiler_params=_cparams("arbitrary"),
        name="premix1",
    )(x2d, mod, norm_g, w_in, w_gate, gate_bias)


def _split3(x):
    hi = x.astype(BF16)
    r1 = x - hi.astype(F32)
    mid = r1.astype(BF16)
    lo = (r1 - mid.astype(F32)).astype(BF16)
    return hi, mid, lo


def _level_exponent(b, s, reverse):
    idx = s if reverse else s - 1
    if s >= 8:
        n = GLA_TILE // (2 * s)
        b4 = b.reshape(n, 2, s, GLA_DK)
        first, second = b4[:, 0:1], b4[:, 1:2]
        r = (second[:, :, 0:1] if reverse else first[:, :, s - 1:s])
        parts = [first - r, r - second] if reverse else [r - first, second - r]
        return jnp.concatenate(parts, axis=1).reshape(GLA_TILE, GLA_DK)
    b8 = b.reshape(GLA_TILE // 8, 8, GLA_DK)
    sub = lax.broadcasted_iota(jnp.int32, (1, 8, 1), 1)
    if s == 4:
        r = b8[:, idx:idx + 1, :]
    else:
        assert s == 2
        r = jnp.where(sub < 4, b8[:, idx:idx + 1, :], b8[:, 4 + idx:5 + idx, :])
    in_second = (sub // s) % 2 == 1
    sign = jnp.where(in_second != reverse, 1.0, -1.0)
    return ((b8 - r) * sign).reshape(GLA_TILE, GLA_DK)


def _gla_tile(q, k, g, b, v, lvl, s_ref, reverse):
    half = GLA_TILE // 2
    edge = 0 if reverse else GLA_TILE - 1
    b_last = b[edge:edge + 1, :]
    qe = (q * jnp.exp2(b)).astype(BF16)
    ke = (k * jnp.exp2(b_last - b)).astype(BF16)

    lo, hi = slice(0, half), slice(half, GLA_TILE)
    qb, kb = q.astype(BF16), k.astype(BF16)
    k_adj = pltpu.roll(k, GLA_TILE - 1 if reverse else 1, axis=0)
    own = jnp.sum(q * k, axis=-1, keepdims=True)
    adj = jnp.sum(q * jnp.exp2(g) * k_adj, axis=-1, keepdims=True)
    blocks = [jnp.where(lvl == 0, own[rows], jnp.where(lvl == 1, adj[rows], 0.0))
              for rows in (lo, hi)]
    cross = None
    s, level = 2, 2
    while s < GLA_TILE:
        f = jnp.exp2(_level_exponent(b, s, reverse)).astype(BF16)
        pairs = _bdot_nt(qb * f, kb * f)
        if s == half:
            cross = pairs[lo, hi] if reverse else pairs[hi, lo]
        else:
            blocks = [jnp.where(lvl == level, pairs[lo, lo], blocks[0]),
                      jnp.where(lvl == level, pairs[hi, hi], blocks[1])]
        s, level = 2 * s, level + 1

    zero = jnp.zeros((half, half), BF16)
    a_lo, a_hi, a_x = blocks[0].astype(BF16), blocks[1].astype(BF16), cross.astype(BF16)
    if reverse:
        att = jnp.concatenate([jnp.concatenate([a_lo, a_x], axis=1),
                               jnp.concatenate([zero, a_hi], axis=1)], axis=0)
    else:
        att = jnp.concatenate([jnp.concatenate([a_lo, zero], axis=1),
                               jnp.concatenate([a_x, a_hi], axis=1)], axis=0)
    st = s_ref[...]
    o = jnp.dot(att, v, preferred_element_type=F32) + _bdot_nt(qe, st)
    s_ref[...] = st * jnp.exp2(b_last) + lax.dot_general(
        v, ke, TN_DIMS, preferred_element_type=F32)
    return o


def _gla_kernel(*refs, zero_init, nt):
    if zero_init:
        (tri_ref, lvl_ref, q_ref, k_ref, gf_ref, gb_ref, v_ref,
         o_ref, sfo_ref, sbo_ref, sf_ref, sb_ref) = refs
        sf_ref[...] = jnp.zeros_like(sf_ref)
        sb_ref[...] = jnp.zeros_like(sb_ref)
    else:
        (tri_ref, lvl_ref, q_ref, k_ref, gf_ref, gb_ref, v_ref, s0f_ref, s0b_ref,
         o_ref, sfo_ref, sbo_ref, sf_ref, sb_ref) = refs
        for hh in range(GLA_HEADS_PER_STEP):
            sf_ref[hh] = s0f_ref[hh].T
            sb_ref[hh] = s0b_ref[hh].T
    o_ref[...] = jnp.zeros_like(o_ref)

    def tile_rows(tile):
        r0 = tile * GLA_TILE
        return pl.ds(r0 if isinstance(r0, int) else pl.multiple_of(r0, GLA_TILE), GLA_TILE)

    def step(t, carry):
        rows_f, rows_b = tile_rows(t), tile_rows(nt - 1 - t)
        g_f, g_b = gf_ref[rows_f, :], gb_ref[rows_b, :]
        c = None
        for part in _split3(jnp.concatenate([g_f, g_b], axis=1)):
            term = jnp.dot(tri_ref[...], part, preferred_element_type=F32)
            c = term if c is None else c + term
        width = GLA_HEADS_PER_STEP * GLA_DK
        c_b = c[:, width:]
        b_f = c[:, :width]
        b_b = (c_b[GLA_TILE - 1:GLA_TILE, :] - c_b) + g_b
        for hh in range(GLA_HEADS_PER_STEP):
            dk = slice(hh * GLA_DK, (hh + 1) * GLA_DK)
            dv = slice(hh * GLA_DV, (hh + 1) * GLA_DV)
            for d, (rows, g, b, s_ref) in enumerate(((rows_f, g_f, b_f, sf_ref),
                                                     (rows_b, g_b, b_b, sb_ref))):
                o = _gla_tile(q_ref[rows, dk], k_ref[rows, dk], g[:, dk], b[:, dk], v_ref[rows, dv],
                              lvl_ref[d], s_ref.at[hh], bool(d))
                o_ref[rows, dv] += o
        return carry

    if nt == 1:
        step(0, 0)
    else:
        lax.fori_loop(0, nt, step, 0)
    for hh in range(GLA_HEADS_PER_STEP):
        sfo_ref[hh] = sf_ref[hh].T
        sbo_ref[hh] = sb_ref[hh].T


def _gla_scan(grp, q, k, v, gf, gb, s0f, s0b, tri, lvl):
    n = grp.n
    nt = n // GLA_TILE
    hp = GLA_HEADS_PER_STEP
    zero_init = s0f is None
    half = GLA_TILE // 2
    seq_dk = pl.BlockSpec((n, hp * GLA_DK), lambda b, h: (b, h))
    seq_dv = pl.BlockSpec((n, hp * GLA_DV), lambda b, h: (b, h))
    state_spec = pl.BlockSpec((None, hp, GLA_DK, GLA_DV), lambda b, h: (b, h, 0, 0))
    in_specs = [_const_spec((GLA_TILE, GLA_TILE)), _const_spec((2, half, half)),
                seq_dk, seq_dk, seq_dk, seq_dk, seq_dv]
    args = [tri, lvl, q, k, gf, gb, v]
    if not zero_init:
        in_specs += [state_spec, state_spec]
        args += [s0f, s0b]
    state_shape = jax.ShapeDtypeStruct((grp.b, GLA_HEADS, GLA_DK, GLA_DV), F32)
    return pl.pallas_call(
        functools.partial(_gla_kernel, zero_init=zero_init, nt=nt),
        grid=(grp.b, GLA_HEADS // hp),
        in_specs=in_specs,
        out_specs=[seq_dv, state_spec, state_spec],
        out_shape=[jax.ShapeDtypeStruct((grp.t, GLA_VD), F32), state_shape, state_shape],
        scratch_shapes=[pltpu.VMEM((hp, GLA_DV, GLA_DK), F32), pltpu.VMEM((hp, GLA_DV, GLA_DK), F32)],
        compiler_params=_cparams("arbitrary", "arbitrary"),
        name="gla_scan",
    )(*args)


def _gla_constants():
    half = GLA_TILE // 2
    i = jnp.arange(GLA_TILE)[:, None]
    j = jnp.arange(GLA_TILE)[None, :]
    tri = (j <= i).astype(BF16)
    ih, jh = i[:half], j[:, :half]
    x = jnp.bitwise_xor(ih, jh)
    level = sum((x >= (1 << p)).astype(jnp.int32) for p in range(half.bit_length() - 1))
    lvl = jnp.stack([jnp.where(jh <= ih, level, -1), jnp.where(jh >= ih, level, -1)])
    return tri, lvl


def _postmix1_kernel(o_ref, og_ref, x_ref, mod_ref, g_ref, gn_ref, w_ref, w1_ref, w2_ref,
                     out_ref):
    gn = gn_ref[...]
    mix = None
    for h in range(GLA_HEADS):
        cols = slice(h * GLA_DV, (h + 1) * GLA_DV)
        o = _rms(o_ref[:, cols], gn)
        og = og_ref[:, cols]
        y = o * (og / (1.0 + jnp.exp(-og)))
        part = _bdot(y, w_ref[cols, :])
        mix = part if mix is None else mix + part
    gt1 = _mod_chunk(mod_ref, 2)
    x1 = x_ref[...] + gt1 * _rms(mix, g_ref[1:2, :])
    out_ref[...] = _mlp_tail(x1, mod_ref, g_ref, w1_ref, w2_ref)


def _postmix1(grp, o, og, x2d, mod, norm_g, gla_norm_g, w_out, w1, w2, tm):
    return pl.pallas_call(
        _postmix1_kernel,
        grid=(grp.t // tm,),
        in_specs=[_row_spec(tm, GLA_VD), _row_spec(tm, GLA_VD),
                  _row_spec(tm, D_MODEL), grp.mod_spec(tm), _const_spec((4, D_MODEL)),
                  _const_spec((1, GLA_DV)),
                  pl.BlockSpec((GLA_VD, D_MODEL), lambda t: (0, 0),
                               pipeline_mode=pl.Buffered(1))] + _mlp_weight_specs(1),
        out_specs=_row_spec(tm, D_MODEL),
        out_shape=jax.ShapeDtypeStruct((grp.t, D_MODEL), F32),
        compiler_params=_cparams("arbitrary"),
        name="postmix1_mlp",
    )(o, og, x2d, mod, norm_g, gla_norm_g, w_out, w1, w2)


def _run_group(grp, x, mods, p, rope_tabs, k_ctx, v_ctx, s0f, s0b, tm):
    x2d = x.reshape(grp.t, D_MODEL)
    zc, q, kv, *cache_t = _premix0(grp, x2d, mods[0], p["norm_g"][0], p["ab_w_in"], rope_tabs, tm)
    if k_ctx is None:
        attn = _ctx_attention(grp, q, kv, p["sink"])
    else:
        attn = _lat_attention(grp, q, kv, k_ctx, v_ctx, p["sink"])
    x2d = _postmix0(grp, zc, attn, x2d, mods[0], p["norm_g"][0], p["conv_w"], p["ab_w_out"],
                    p["mlp_w1"], p["mlp_w2"], tm)
    gq, gk, gv, og, gf, gb = _premix1(grp, x2d, mods[1], p["norm_g"][1], p["gla_w_in"],
                                      p["gla_w_gate"], p["gla_gate_bias"], 2 * tm)
    o, sf, sb = _gla_scan(grp, gq, gk, gv, gf, gb, s0f, s0b, p["tri"], p["lvl"])
    x2d = _postmix1(grp, o, og, x2d, mods[1], p["norm_g"][1], p["gla_norm_g"], p["gla_w_out"],
                    p["mlp_w1"], p["mlp_w2"], tm)
    return x2d.reshape(x.shape), cache_t, sf, sb


def kernel(x_prompt, x_sample, cache_k, cache_v, state_fwd, state_bwd, c, c_ctx, mod_w, mod_b,
           norm_g, ab_w_in, conv_w, attn_sink, ab_w_out, gla_w_in, gla_gate_w, gla_gate_b,
           gla_norm_g, gla_w_out, mlp_w1, mlp_w2):
    b_ctx, n_ctx, _ = x_prompt.shape
    b_lat, n_lat, _ = x_sample.shape
    assert mod_w.shape[0] == 2 and ab_w_in.shape[0] == 1 and gla_w_in.shape[0] == 1
    assert 1 + b_lat <= 8

    cond8 = jnp.zeros((8, D_MODEL), F32).at[0].set(c_ctx).at[1:1 + b_lat].set(c)
    mod = _modulation(cond8, mod_w, mod_b)
    mods_ctx = [mod[l, 0:1].reshape(1, 1, -1) for l in range(2)]
    mods_lat = [mod[l, 1:1 + b_lat].reshape(b_lat, 1, -1) for l in range(2)]

    gw = gla_w_in[0]
    w_gate = jnp.zeros((LANES, 2 * GLA_QK), F32)
    w_gate = w_gate.at[:GLA_RANK, :GLA_QK].set(gla_gate_w[0, 0])
    w_gate = w_gate.at[GLA_RANK:2 * GLA_RANK, GLA_QK:].set(gla_gate_w[0, 1])
    tri, lvl = _gla_constants()
    p = {
        "norm_g": norm_g,
        "ab_w_in": ab_w_in[0].astype(BF16),
        "conv_w": conv_w[0],
        "sink": attn_sink[0],
        "ab_w_out": ab_w_out[0].astype(BF16),
        "mlp_w1": mlp_w1.astype(BF16),
        "mlp_w2": mlp_w2.astype(BF16),
        "gla_w_in": jnp.pad(gw, ((0, 0), (0, LANES - 2 * GLA_RANK))).astype(BF16),
        "gla_w_gate": w_gate.astype(BF16),
        "gla_gate_bias": gla_gate_b[0].reshape(1, 2 * GLA_QK),
        "gla_norm_g": gla_norm_g[0].reshape(1, GLA_DV),
        "gla_w_out": gla_w_out[0].astype(BF16),
        "tri": tri,
        "lvl": lvl,
    }

    ctx = _Group(b_ctx, n_ctx, per_seq_mod=False)
    lat = _Group(b_lat, n_lat, per_seq_mod=True)

    y_prompt, (k_t, v_t), sf, sb = _run_group(ctx, x_prompt, mods_ctx, p, None, None, None,
                                              None, None, tm=512)

    def cache_layout(t):
        t = t.reshape(b_ctx, 1, N_KV_HEADS, HEAD_DIM, n_ctx)
        return jnp.transpose(t, (0, 1, 4, 2, 3))

    new_k, new_v = cache_layout(k_t), cache_layout(v_t)
    new_sf, new_sb = sf[:, None], sb[:, None]

    past = cache_k.shape[2]
    k_ctx = cache_k[:, 0].reshape(b_lat, past, KD)
    v_ctx = cache_v[:, 0].reshape(b_lat, past, KD)
    y_sample, _, _, _ = _run_group(lat, x_sample, mods_lat, p, _rope_tables(n_lat), k_ctx, v_ctx,
                                   state_fwd[:, 0], state_bwd[:, 0], tm=512)
    return (y_prompt, y_sample, new_k, new_v, new_sf, new_sb)
```

```python
import functools

import jax
import jax.numpy as jnp
from jax import lax
from jax.experimental import pallas as pl
from jax.experimental.pallas import tpu as pltpu

F32 = jnp.float32
BF16 = jnp.bfloat16

D_MODEL = 1024
MOD_CHUNKS = 6
EPS = 1e-6
CONV_DIM = 512
N_Q_HEADS = 8
N_KV_HEADS = 2
GQA_GROUP = 4
HEAD_DIM = 64
WINDOW = 128
ATTN_BLOCK = 128
ATTN_BLOCKS_PER_STEP = 4
GRID_W = 64
ROPE_BASE = 10000.0
QD = N_Q_HEADS * HEAD_DIM
KD = N_KV_HEADS * HEAD_DIM
AB_IN = 3 * CONV_DIM + QD + 2 * KD
GLA_HEADS = 4
GLA_DK = 128
GLA_DV = 256
GLA_RANK = 16
GLA_GATE_NORM = 16.0
GLA_TILE = 256
GLA_HEADS_PER_STEP = 2
LOG2E = 1.4426950408889634
GLA_QK = GLA_HEADS * GLA_DK
GLA_VD = GLA_HEADS * GLA_DV
GLA_MAIN = 2 * GLA_QK + 2 * GLA_VD
D_FF = 4 * D_MODEL
MLP_CHUNK = 512
MLP_CHUNKS = D_FF // MLP_CHUNK
NEG_INF = -1e30
LANES = 128
VMEM_LIMIT = 56 * 1024 * 1024

NT_DIMS = (((1,), (1,)), ((), ()))
TN_DIMS = (((0,), (0,)), ((), ()))


def _cparams(*sem):
    return pltpu.CompilerParams(dimension_semantics=sem, vmem_limit_bytes=VMEM_LIMIT)


def _bdot(a, b):
    return jnp.dot(a.astype(BF16), b.astype(BF16), preferred_element_type=F32)


def _bdot_nt(a, b):
    return lax.dot_general(a.astype(BF16), b.astype(BF16), NT_DIMS, preferred_element_type=F32)


def _rms(x, g):
    ms = jnp.mean(x * x, axis=-1, keepdims=True)
    return x * lax.rsqrt(ms + EPS) * g


def _mod_chunk(mod_ref, i):
    return mod_ref[:, i * D_MODEL:(i + 1) * D_MODEL]


def _const_spec(shape):
    return pl.BlockSpec(shape, lambda *_: (0,) * len(shape))


def _f32_weight_spec(shape):
    assert shape[0] == 1
    return pl.BlockSpec((None,) + tuple(shape[1:]), lambda *_: (0, 0, 0),
                        pipeline_mode=pl.Buffered(1))


def _cast_once(w_ref, wb_ref):
    @pl.when(pl.program_id(0) == 0)
    def _():
        wb_ref[...] = w_ref[...].astype(BF16)


def _mod_kernel(cond_ref, w_ref, b_ref, o_ref):
    cnd = cond_ref[...]
    s = cnd / (1.0 + jnp.exp(-cnd))
    o_ref[...] = _bdot(s, w_ref[...]) + b_ref[...]


def _modulation(cond8, mod_w, mod_b):
    depth = mod_w.shape[0]
    n = mod_w.shape[2]
    tn = 1536
    return pl.pallas_call(
        _mod_kernel,
        grid=(depth, n // tn),
        in_specs=[
            pl.BlockSpec((8, D_MODEL), lambda l, j: (0, 0)),
            pl.BlockSpec((None, D_MODEL, tn), lambda l, j: (l, 0, j)),
            pl.BlockSpec((None, 1, tn), lambda l, j: (l, 0, j)),
        ],
        out_specs=pl.BlockSpec((None, 8, tn), lambda l, j: (l, 0, j)),
        out_shape=jax.ShapeDtypeStruct((depth, 8, n), F32),
        compiler_params=_cparams("arbitrary", "arbitrary"),
        name="modulation",
    )(cond8, mod_w, mod_b.reshape(depth, 1, n))


class _Group:
    def __init__(self, b, n, per_seq_mod):
        self.b, self.n, self.t = b, n, b * n
        self.per_seq_mod = per_seq_mod

    def mod_spec(self, tm):
        if self.per_seq_mod:
            assert self.n % tm == 0
            per = self.n // tm
            return pl.BlockSpec((None, 1, MOD_CHUNKS * D_MODEL), lambda t: (t // per, 0, 0))
        return pl.BlockSpec((None, 1, MOD_CHUNKS * D_MODEL), lambda t: (0, 0, 0))


def _row_spec(tm, width):
    return pl.BlockSpec((tm, width), lambda t: (t, 0))


def _rope(x, cos, sin_lo, sin_hi):
    return (x * cos + pltpu.roll(x, LANES - 16, axis=1) * sin_lo
            + pltpu.roll(x, 16, axis=1) * sin_hi)


def _premix0_kernel(*refs, rope):
    if rope:
        (x_ref, mod_ref, g_ref, w_ref, cos_ref, slo_ref, shi_ref,
         zc_ref, q_ref, kv_ref, wb_ref) = refs
    else:
        x_ref, mod_ref, g_ref, w_ref, zc_ref, q_ref, kv_ref, kt_ref, vt_ref, wb_ref = refs
    _cast_once(w_ref, wb_ref)
    sh1, sc1 = _mod_chunk(mod_ref, 0), _mod_chunk(mod_ref, 1)
    h = _rms(x_ref[...], g_ref[0:1, :]) * (1.0 + sc1) + sh1
    z = _bdot(h, wb_ref[...])
    c3 = 3 * CONV_DIM
    zc_ref[...] = z[:, :c3]
    scale = HEAD_DIM ** -0.5 * LOG2E
    if rope:
        cos, slo, shi = cos_ref[...], slo_ref[...], shi_ref[...]
        for j in range(QD // LANES):
            qs = z[:, c3 + j * LANES:c3 + (j + 1) * LANES]
            q_ref[:, j * LANES:(j + 1) * LANES] = (_rope(qs, cos, slo, shi) * scale).astype(BF16)
        kv_ref[:, :KD] = _rope(z[:, c3 + QD:c3 + QD + KD], cos, slo, shi)
    else:
        q_ref[...] = (z[:, c3:c3 + QD] * scale).astype(BF16)
        kv_ref[:, :KD] = z[:, c3 + QD:c3 + QD + KD]
        n = kt_ref.shape[2]
        for j in range(kt_ref.shape[0]):
            kt_ref[j] = z[j * n:(j + 1) * n, c3 + QD:c3 + QD + KD].T
            vt_ref[j] = z[j * n:(j + 1) * n, c3 + QD + KD:].T
    kv_ref[:, KD:] = z[:, c3 + QD + KD:]


def _premix0(grp, x2d, mod, norm_g, w_in, rope_tabs, tm):
    rope = rope_tabs is not None
    in_specs = [_row_spec(tm, D_MODEL), grp.mod_spec(tm), _const_spec((4, D_MODEL)),
                _f32_weight_spec(w_in.shape)]
    args = [x2d, mod, norm_g, w_in]
    if rope:
        per = grp.n // tm
        in_specs += [pl.BlockSpec((tm, LANES), lambda t: (t % per, 0))] * 3
        args += list(rope_tabs)
    out_specs = [_row_spec(tm, 3 * CONV_DIM), _row_spec(tm, QD), _row_spec(tm, 2 * KD)]
    out_shape = [jax.ShapeDtypeStruct((grp.t, 3 * CONV_DIM), F32),
                 jax.ShapeDtypeStruct((grp.t, QD), BF16),
                 jax.ShapeDtypeStruct((grp.t, 2 * KD), F32)]
    if not rope:
        assert tm % grp.n == 0
        out_specs += [pl.BlockSpec((tm // grp.n, KD, grp.n), lambda t: (t, 0, 0))] * 2
        out_shape += [jax.ShapeDtypeStruct((grp.b, KD, grp.n), F32)] * 2
    return pl.pallas_call(
        functools.partial(_premix0_kernel, rope=rope),
        grid=(grp.t // tm,),
        in_specs=in_specs,
        out_specs=out_specs,
        out_shape=out_shape,
        scratch_shapes=[pltpu.VMEM(w_in.shape[1:], BF16)],
        compiler_params=_cparams("arbitrary"),
        name="premix0_rope" if rope else "premix0",
    )(*args)


def _rope_tables(n):
    rows = n // GRID_W
    pos_r = jnp.repeat(jnp.arange(rows), GRID_W)
    pos_c = jnp.tile(jnp.arange(GRID_W), rows)
    half = HEAD_DIM // 2
    quarter = half // 2
    inv = ROPE_BASE ** (-(jnp.arange(quarter, dtype=F32) * 2.0 / half))

    def cs(pos):
        ang = pos.astype(F32)[:, None] * inv[None, :]
        return jnp.cos(ang), jnp.sin(ang)

    cr, sr = cs(pos_r)
    cc, sc = cs(pos_c)
    zero = jnp.zeros_like(sr)
    cos = jnp.concatenate([cr, cr, cc, cc], axis=1)
    sin_lo = jnp.concatenate([-sr, zero, -sc, zero], axis=1)
    sin_hi = jnp.concatenate([zero, sr, zero, sc], axis=1)
    rep = LANES // HEAD_DIM
    return tuple(jnp.tile(t, (1, rep)) for t in (cos, sin_lo, sin_hi))


def _sink_attention(sink_ref, q_ref, k_all, bias_t, v_all, o_ref):
    assert KD == LANES == 2 * HEAD_DIM and GQA_GROUP == 4
    m = q_ref.shape[0]
    lane = lax.broadcasted_iota(jnp.int32, (1, LANES), 1)
    sub = lax.broadcasted_iota(jnp.int32, (LANES, 1), 0)
    v_t = v_all.T
    k_swapped = pltpu.roll(k_all, HEAD_DIM, axis=1)
    o_t = {}
    for g in range(N_KV_HEADS):
        k_low, k_high = (k_all, k_swapped) if g == 0 else (k_swapped, k_all)
        kz_even = jnp.where(lane < HEAD_DIM, k_low, 0.0).astype(BF16)
        kz_odd = jnp.where(lane >= HEAD_DIM, k_high, 0.0).astype(BF16)
        qq = jnp.concatenate([q_ref[:, (2 * g) * LANES:(2 * g + 1) * LANES],
                              q_ref[:, (2 * g + 1) * LANES:(2 * g + 2) * LANES]], axis=0)
        s = jnp.concatenate([_bdot_nt(kz_even, qq), _bdot_nt(kz_odd, qq)], axis=1)
        heads = [4 * g, 4 * g + 2, 4 * g + 1, 4 * g + 3]
        if bias_t is not None:
            nb = bias_t.shape[0]
            s = jnp.concatenate([s[:nb] + jnp.concatenate([bias_t] * GQA_GROUP, axis=1), s[nb:]],
                                axis=0)
        sink = jnp.concatenate([jnp.full((1, m), sink_ref[h] * LOG2E, F32) for h in heads], axis=1)
        mx = jnp.maximum(jnp.max(s, axis=0, keepdims=True), sink)
        p = jnp.exp2(s - mx).astype(BF16)
        own = (sub < HEAD_DIM) if g == 0 else (sub >= HEAD_DIM)
        v_ext_t = jnp.where(own, v_t, 1.0).astype(BF16)
        oe = jnp.dot(v_ext_t, p, preferred_element_type=F32)
        other = (1 - g) * HEAD_DIM
        den = oe[other:other + 1] + jnp.exp2(sink - mx)
        o_g = oe[g * HEAD_DIM:(g + 1) * HEAD_DIM] / den
        for i, h in enumerate(heads):
            o_t[h] = o_g[:, i * m:(i + 1) * m]
    for j in range(N_Q_HEADS // 2):
        pair_t = jnp.concatenate([o_t[2 * j], o_t[2 * j + 1]], axis=0)
        o_ref[:, j * LANES:(j + 1) * LANES] = pair_t.T.astype(BF16)


def _ctx_attn_kernel(sink_ref, q_ref, kv_ref, o_ref):
    _sink_attention(sink_ref, q_ref, kv_ref[:, :KD], None, kv_ref[:, KD:], o_ref)


def _ctx_attention(grp, q, kv, sink):
    n = grp.n
    return pl.pallas_call(
        _ctx_attn_kernel,
        grid=(grp.b,),
        in_specs=[pl.BlockSpec(memory_space=pltpu.SMEM), _row_spec(n, QD), _row_spec(n, 2 * KD)],
        out_specs=_row_spec(n, QD),
        out_shape=jax.ShapeDtypeStruct((grp.t, QD), BF16),
        compiler_params=_cparams("arbitrary"),
        name="ctx_attention",
    )(sink, q, kv)


def _lat_attn_kernel(sink_ref, q_ref, kvp_ref, kvc_ref, kvn_ref, kc_ref, vc_ref, o_ref, *, n):
    band = jnp.concatenate([kvp_ref[...], kvc_ref[...], kvn_ref[...]], axis=0)
    kj = lax.broadcasted_iota(jnp.int32, (3 * ATTN_BLOCK, ATTN_BLOCK), 0)
    qi = lax.broadcasted_iota(jnp.int32, (3 * ATTN_BLOCK, ATTN_BLOCK), 1)
    rel = kj - ATTN_BLOCK - qi
    for j in range(ATTN_BLOCKS_PER_STEP):
        blk = pl.program_id(1) * ATTN_BLOCKS_PER_STEP + j
        kv = band[j * ATTN_BLOCK:(j + 3) * ATTN_BLOCK]
        kpos = (blk - 1) * ATTN_BLOCK + kj
        valid = (jnp.abs(rel) <= WINDOW) & (kpos >= 0) & (kpos < n)
        bias_t = jnp.where(valid, 0.0, NEG_INF)
        k_all = jnp.concatenate([kv[:, :KD], kc_ref[...]], axis=0)
        v_all = jnp.concatenate([kv[:, KD:], vc_ref[...]], axis=0)
        rows = pl.ds(j * ATTN_BLOCK, ATTN_BLOCK)
        _sink_attention(sink_ref, q_ref.at[rows], k_all, bias_t, v_all, o_ref.at[rows])


def _lat_attention(grp, q, kv, k_ctx, v_ctx, sink):
    nb = grp.n // ATTN_BLOCK
    per = ATTN_BLOCKS_PER_STEP
    steps = nb // per
    past = k_ctx.shape[1]

    def edge(off):
        return pl.BlockSpec((ATTN_BLOCK, 2 * KD),
                            lambda b, i: (b * nb + jnp.clip(i * per + off, 0, nb - 1), 0))

    ctx_spec = pl.BlockSpec((None, past, KD), lambda b, i: (b, 0, 0))
    return pl.pallas_call(
        functools.partial(_lat_attn_kernel, n=grp.n),
        grid=(grp.b, steps),
        in_specs=[pl.BlockSpec(memory_space=pltpu.SMEM),
                  pl.BlockSpec((per * ATTN_BLOCK, QD), lambda b, i: (b * steps + i, 0)),
                  edge(-1),
                  pl.BlockSpec((per * ATTN_BLOCK, 2 * KD), lambda b, i: (b * steps + i, 0)),
                  edge(per), ctx_spec, ctx_spec],
        out_specs=pl.BlockSpec((per * ATTN_BLOCK, QD), lambda b, i: (b * steps + i, 0)),
        out_shape=jax.ShapeDtypeStruct((grp.t, QD), BF16),
        compiler_params=_cparams("arbitrary", "arbitrary"),
        name="lat_attention",
    )(sink, q, kv, kv, kv, k_ctx, v_ctx)


class _TailWeights:
    def __init__(self, layer, wo_hbm, w1_hbm, w2_hbm, wo_s, w1_s, w2_s, stage1, stage2, sem):
        self.layer = layer
        self.wo_hbm, self.w1_hbm, self.w2_hbm = wo_hbm, w1_hbm, w2_hbm
        self.wo_s, self.w1_s, self.w2_s = wo_s, w1_s, w2_s
        self.stage1, self.stage2, self.sem = stage1, stage2, sem

    def _w1_copy(self, j):
        src = self.w1_hbm.at[self.layer, :, pl.ds(j * MLP_CHUNK, MLP_CHUNK)]
        return pltpu.make_async_copy(src, self.stage1.at[j % 2], self.sem.at[0, j % 2])

    def _w2_copy(self, j):
        src = self.w2_hbm.at[self.layer, pl.ds(j * MLP_CHUNK, MLP_CHUNK), :]
        return pltpu.make_async_copy(src, self.stage2.at[j % 2], self.sem.at[1, j % 2])

    def _wo_copy(self, i):
        src = self.wo_hbm.at[0, pl.ds(i * MLP_CHUNK, MLP_CHUNK), :]
        return pltpu.make_async_copy(src, self.stage2.at[i], self.sem.at[1, i])

    def start(self):
        self._wo_copy(0).start()
        self._wo_copy(1).start()
        self._w1_copy(0).start()
        self._w1_copy(1).start()

    def fetch_out_proj(self):
        for i in range(2):
            self._wo_copy(i).wait()
            self.wo_s[i * MLP_CHUNK:(i + 1) * MLP_CHUNK, :] = self.stage2[i].astype(BF16)
            self._w2_copy(i).start()

    def fetch_mlp_chunk(self, j):
        self._w1_copy(j).wait()
        self.w1_s[j] = self.stage1[j % 2].astype(BF16)
        if j + 2 < MLP_CHUNKS:
            self._w1_copy(j + 2).start()
        self._w2_copy(j).wait()
        self.w2_s[j] = self.stage2[j % 2].astype(BF16)
        if j + 2 < MLP_CHUNKS:
            self._w2_copy(j + 2).start()


def _tail_weight_scratch():
    assert D_MODEL == 2 * MLP_CHUNK
    return [pltpu.VMEM((D_MODEL, D_MODEL), BF16),
            pltpu.VMEM((MLP_CHUNKS, D_MODEL, MLP_CHUNK), BF16),
            pltpu.VMEM((MLP_CHUNKS, MLP_CHUNK, D_MODEL), BF16),
            pltpu.VMEM((2, D_MODEL, MLP_CHUNK), F32),
            pltpu.VMEM((2, MLP_CHUNK, D_MODEL), F32),
            pltpu.SemaphoreType.DMA((2, 2))]


def _first_step_streams_weights(body, weights):
    first = pl.program_id(0) == 0

    @pl.when(first)
    def _():
        weights.start()
        body(True)

    @pl.when(jnp.logical_not(first))
    def _():
        body(False)


def _mlp_tail(x, mod_ref, g_ref, weights, streaming):
    sh2, sc2, gt2 = _mod_chunk(mod_ref, 3), _mod_chunk(mod_ref, 4), _mod_chunk(mod_ref, 5)
    hb = (_rms(x, g_ref[2:3, :]) * (1.0 + sc2) + sh2).astype(BF16)
    acc = None
    for j in range(MLP_CHUNKS):
        if streaming:
            weights.fetch_mlp_chunk(j)
        a = jnp.dot(hb, weights.w1_s[j], preferred_element_type=F32)
        a = jnp.maximum(a, 0.0)
        part = jnp.dot((a * a).astype(BF16), weights.w2_s[j], preferred_element_type=F32)
        acc = part if acc is None else acc + part
    return x + gt2 * _rms(acc, g_ref[3:4, :])


def _postmix0_kernel(zc_ref, zp_ref, zn_ref, at_ref, x_ref, mod_ref, g_ref, cw_ref,
                     wo_hbm, w1_hbm, w2_hbm, o_ref, *scratch, n, tm):
    weights = _TailWeights(0, wo_hbm, w1_hbm, w2_hbm, *scratch)
    _first_step_streams_weights(
        functools.partial(_postmix0_body, zc_ref, zp_ref, zn_ref, at_ref, x_ref, mod_ref, g_ref,
                          cw_ref, o_ref, weights, n, tm), weights)


def _postmix0_body(zc_ref, zp_ref, zn_ref, at_ref, x_ref, mod_ref, g_ref, cw_ref, o_ref,
                   weights, n, tm, streaming):
    c = CONV_DIM
    zc = zc_ref[...]
    u = zc[:, c:2 * c] * zc[:, 2 * c:]
    u_before = zp_ref[7:8, c:2 * c] * zp_ref[7:8, 2 * c:]
    u_after = zn_ref[0:1, c:2 * c] * zn_ref[0:1, 2 * c:]
    row = lax.broadcasted_iota(jnp.int32, (tm, 1), 0)
    pos = (pl.program_id(0) * tm + row) % n
    u_prev = jnp.where(row == 0, u_before, pltpu.roll(u, 1, axis=0))
    u_prev = jnp.where(pos == 0, 0.0, u_prev)
    u_next = jnp.where(row == tm - 1, u_after, pltpu.roll(u, tm - 1, axis=0))
    u_next = jnp.where(pos == n - 1, 0.0, u_next)
    conv = u_prev * cw_ref[0:1, :] + u * cw_ref[1:2, :] + u_next * cw_ref[2:3, :]
    if streaming:
        weights.fetch_out_proj()
    mix = (_bdot(zc[:, :c] * conv, weights.wo_s[:c, :])
           + jnp.dot(at_ref[...], weights.wo_s[c:, :], preferred_element_type=F32))
    gt1 = _mod_chunk(mod_ref, 2)
    x1 = x_ref[...] + gt1 * _rms(mix, g_ref[1:2, :])
    o_ref[...] = _mlp_tail(x1, mod_ref, g_ref, weights, streaming)


def _postmix0(grp, zc, attn, x2d, mod, norm_g, conv_w, w_out, w1, w2, tm):
    r8 = tm // 8
    last8 = grp.t // 8 - 1
    hbm = pl.BlockSpec(memory_space=pl.ANY)
    return pl.pallas_call(
        functools.partial(_postmix0_kernel, n=grp.n, tm=tm),
        grid=(grp.t // tm,),
        in_specs=[_row_spec(tm, 3 * CONV_DIM),
                  pl.BlockSpec((8, 3 * CONV_DIM), lambda t: (jnp.maximum(t * r8 - 1, 0), 0)),
                  pl.BlockSpec((8, 3 * CONV_DIM), lambda t: (jnp.minimum((t + 1) * r8, last8), 0)),
                  _row_spec(tm, QD), _row_spec(tm, D_MODEL), grp.mod_spec(tm),
                  _const_spec((4, D_MODEL)), _const_spec((3, CONV_DIM)), hbm, hbm, hbm],
        out_specs=_row_spec(tm, D_MODEL),
        out_shape=jax.ShapeDtypeStruct((grp.t, D_MODEL), F32),
        scratch_shapes=_tail_weight_scratch(),
        compiler_params=_cparams("arbitrary"),
        name="postmix0_mlp",
    )(zc, zc, zc, attn, x2d, mod, norm_g, conv_w, w_out, w1, w2)


def _premix1_kernel(x_ref, mod_ref, g_ref, w_ref, wg_ref, gb_ref,
                    q_ref, k_ref, v_ref, og_ref, gf_ref, gbk_ref, wb_ref):
    _cast_once(w_ref, wb_ref)
    sh1, sc1 = _mod_chunk(mod_ref, 0), _mod_chunk(mod_ref, 1)
    hb = (_rms(x_ref[...], g_ref[0:1, :]) * (1.0 + sc1) + sh1).astype(BF16)
    z = jnp.dot(hb, wb_ref[:, :GLA_MAIN], preferred_element_type=F32)
    q_ref[...] = z[:, :GLA_QK] * (GLA_DK ** -0.5)
    k_ref[...] = z[:, GLA_QK:2 * GLA_QK]
    v_ref[...] = z[:, 2 * GLA_QK:2 * GLA_QK + GLA_VD].astype(BF16)
    og_ref[...] = z[:, 2 * GLA_QK + GLA_VD:]
    r = jnp.dot(hb, wb_ref[:, GLA_MAIN:], preferred_element_type=F32)
    pre = _bdot(r, wg_ref[...]) + gb_ref[...]
    gate = ((jnp.minimum(pre, 0.0) - jnp.log1p(jnp.exp(-jnp.abs(pre))))
            * (LOG2E / GLA_GATE_NORM))
    gf_ref[...] = gate[:, :GLA_QK]
    gbk_ref[...] = gate[:, GLA_QK:]


def _premix1(grp, x2d, mod, norm_g, w_in, w_gate, gate_bias, tm):
    t = grp.t
    return pl.pallas_call(
        _premix1_kernel,
        grid=(t // tm,),
        in_specs=[_row_spec(tm, D_MODEL), grp.mod_spec(tm), _const_spec((4, D_MODEL)),
                  _f32_weight_spec(w_in.shape), _const_spec((2 * GLA_RANK, 2 * GLA_QK)),
                  _const_spec((1, 2 * GLA_QK))],
        out_specs=[_row_spec(tm, GLA_QK), _row_spec(tm, GLA_QK), _row_spec(tm, GLA_VD),
                   _row_spec(tm, GLA_VD), _row_spec(tm, GLA_QK), _row_spec(tm, GLA_QK)],
        out_shape=[jax.ShapeDtypeStruct((t, GLA_QK), F32), jax.ShapeDtypeStruct((t, GLA_QK), F32),
                   jax.ShapeDtypeStruct((t, GLA_VD), BF16),
                   jax.ShapeDtypeStruct((t, GLA_VD), F32), jax.ShapeDtypeStruct((t, GLA_QK), F32),
                   jax.ShapeDtypeStruct((t, GLA_QK), F32)],
        scratch_shapes=[pltpu.VMEM(w_in.shape[1:], BF16)],
        compiler_params=_cparams("arbitrary"),
        name="premix1",
    )(x2d, mod, norm_g, w_in, w_gate, gate_bias)


def _split3(x):
    hi = x.astype(BF16)
    r1 = x - hi.astype(F32)
    mid = r1.astype(BF16)
    lo = (r1 - mid.astype(F32)).astype(BF16)
    return hi, mid, lo


def _level_exponent(b, s, reverse):
    idx = s if reverse else s - 1
    if s >= 8:
        n = GLA_TILE // (2 * s)
        b4 = b.reshape(n, 2, s, GLA_DK)
        first, second = b4[:, 0:1], b4[:, 1:2]
        r = (second[:, :, 0:1] if reverse else first[:, :, s - 1:s])
        parts = [first - r, r - second] if reverse else [r - first, second - r]
        return jnp.concatenate(parts, axis=1).reshape(GLA_TILE, GLA_DK)
    b8 = b.reshape(GLA_TILE // 8, 8, GLA_DK)
    sub = lax.broadcasted_iota(jnp.int32, (1, 8, 1), 1)
    if s == 4:
        r = b8[:, idx:idx + 1, :]
    else:
        assert s == 2
        r = jnp.where(sub < 4, b8[:, idx:idx + 1, :], b8[:, 4 + idx:5 + idx, :])
    in_second = (sub // s) % 2 == 1
    sign = jnp.where(in_second != reverse, 1.0, -1.0)
    return ((b8 - r) * sign).reshape(GLA_TILE, GLA_DK)


def _gla_tile(q, k, g, b, v, lvl, s_ref, reverse):
    half = GLA_TILE // 2
    edge = 0 if reverse else GLA_TILE - 1
    b_last = b[edge:edge + 1, :]
    qe = (q * jnp.exp2(b)).astype(BF16)
    ke = (k * jnp.exp2(b_last - b)).astype(BF16)

    lo, hi = slice(0, half), slice(half, GLA_TILE)
    qb, kb = q.astype(BF16), k.astype(BF16)
    k_adj = pltpu.roll(k, GLA_TILE - 1 if reverse else 1, axis=0)
    own = jnp.sum(q * k, axis=-1, keepdims=True)
    adj = jnp.sum(q * jnp.exp2(g) * k_adj, axis=-1, keepdims=True)
    blocks = [jnp.where(lvl == 0, own[rows], jnp.where(lvl == 1, adj[rows], 0.0))
              for rows in (lo, hi)]
    cross = None
    s, level = 2, 2
    while s < GLA_TILE:
        f = jnp.exp2(_level_exponent(b, s, reverse)).astype(BF16)
        pairs = _bdot_nt(qb * f, kb * f)
        if s == half:
            cross = pairs[lo, hi] if reverse else pairs[hi, lo]
        else:
            blocks = [jnp.where(lvl == level, pairs[lo, lo], blocks[0]),
                      jnp.where(lvl == level, pairs[hi, hi], blocks[1])]
        s, level = 2 * s, level + 1

    zero = jnp.zeros((half, half), BF16)
    a_lo, a_hi, a_x = blocks[0].astype(BF16), blocks[1].astype(BF16), cross.astype(BF16)
    if reverse:
        att = jnp.concatenate([jnp.concatenate([a_lo, a_x], axis=1),
                               jnp.concatenate([zero, a_hi], axis=1)], axis=0)
    else:
        att = jnp.concatenate([jnp.concatenate([a_lo, zero], axis=1),
                               jnp.concatenate([a_x, a_hi], axis=1)], axis=0)
    st = s_ref[...]
    o = jnp.dot(att, v, preferred_element_type=F32) + _bdot_nt(qe, st)
    s_ref[...] = st * jnp.exp2(b_last) + lax.dot_general(
        v, ke, TN_DIMS, preferred_element_type=F32)
    return o


def _gla_kernel(*refs, zero_init, nt):
    if zero_init:
        (tri_ref, lvl_ref, q_ref, k_ref, gf_ref, gb_ref, v_ref,
         o_ref, sfo_ref, sbo_ref, sf_ref, sb_ref) = refs
        sf_ref[...] = jnp.zeros_like(sf_ref)
        sb_ref[...] = jnp.zeros_like(sb_ref)
    else:
        (tri_ref, lvl_ref, q_ref, k_ref, gf_ref, gb_ref, v_ref, s0f_ref, s0b_ref,
         o_ref, sfo_ref, sbo_ref, sf_ref, sb_ref) = refs
        for hh in range(GLA_HEADS_PER_STEP):
            sf_ref[hh] = s0f_ref[hh].T
            sb_ref[hh] = s0b_ref[hh].T
    o_ref[...] = jnp.zeros_like(o_ref)

    def tile_rows(tile):
        r0 = tile * GLA_TILE
        return pl.ds(r0 if isinstance(r0, int) else pl.multiple_of(r0, GLA_TILE), GLA_TILE)

    def step(t, carry):
        rows_f, rows_b = tile_rows(t), tile_rows(nt - 1 - t)
        g_f, g_b = gf_ref[rows_f, :], gb_ref[rows_b, :]
        c = None
        for part in _split3(jnp.concatenate([g_f, g_b], axis=1)):
            term = jnp.dot(tri_ref[...], part, preferred_element_type=F32)
            c = term if c is None else c + term
        width = GLA_HEADS_PER_STEP * GLA_DK
        c_b = c[:, width:]
        b_f = c[:, :width]
        b_b = (c_b[GLA_TILE - 1:GLA_TILE, :] - c_b) + g_b
        for hh in range(GLA_HEADS_PER_STEP):
            dk = slice(hh * GLA_DK, (hh + 1) * GLA_DK)
            dv = slice(hh * GLA_DV, (hh + 1) * GLA_DV)
            for d, (rows, g, b, s_ref) in enumerate(((rows_f, g_f, b_f, sf_ref),
                                                     (rows_b, g_b, b_b, sb_ref))):
                o = _gla_tile(q_ref[rows, dk], k_ref[rows, dk], g[:, dk], b[:, dk], v_ref[rows, dv],
                              lvl_ref[d], s_ref.at[hh], bool(d))
                o_ref[rows, dv] += o
        return carry

    if nt == 1:
        step(0, 0)
    else:
        lax.fori_loop(0, nt, step, 0)
    for hh in range(GLA_HEADS_PER_STEP):
        sfo_ref[hh] = sf_ref[hh].T
        sbo_ref[hh] = sb_ref[hh].T


def _gla_scan(grp, q, k, v, gf, gb, s0f, s0b, tri, lvl):
    n = grp.n
    nt = n // GLA_TILE
    hp = GLA_HEADS_PER_STEP
    zero_init = s0f is None
    half = GLA_TILE // 2
    seq_dk = pl.BlockSpec((n, hp * GLA_DK), lambda b, h: (b, h))
    seq_dv = pl.BlockSpec((n, hp * GLA_DV), lambda b, h: (b, h))
    state_spec = pl.BlockSpec((None, hp, GLA_DK, GLA_DV), lambda b, h: (b, h, 0, 0))
    in_specs = [_const_spec((GLA_TILE, GLA_TILE)), _const_spec((2, half, half)),
                seq_dk, seq_dk, seq_dk, seq_dk, seq_dv]
    args = [tri, lvl, q, k, gf, gb, v]
    if not zero_init:
        in_specs += [state_spec, state_spec]
        args += [s0f, s0b]
    state_shape = jax.ShapeDtypeStruct((grp.b, GLA_HEADS, GLA_DK, GLA_DV), F32)
    return pl.pallas_call(
        functools.partial(_gla_kernel, zero_init=zero_init, nt=nt),
        grid=(grp.b, GLA_HEADS // hp),
        in_specs=in_specs,
        out_specs=[seq_dv, state_spec, state_spec],
        out_shape=[jax.ShapeDtypeStruct((grp.t, GLA_VD), F32), state_shape, state_shape],
        scratch_shapes=[pltpu.VMEM((hp, GLA_DV, GLA_DK), F32), pltpu.VMEM((hp, GLA_DV, GLA_DK), F32)],
        compiler_params=_cparams("arbitrary", "arbitrary"),
        name="gla_scan",
    )(*args)


def _gla_constants():
    half = GLA_TILE // 2
    i = jnp.arange(GLA_TILE)[:, None]
    j = jnp.arange(GLA_TILE)[None, :]
    tri = (j <= i).astype(BF16)
    ih, jh = i[:half], j[:, :half]
    x = jnp.bitwise_xor(ih, jh)
    level = sum((x >= (1 << p)).astype(jnp.int32) for p in range(half.bit_length() - 1))
    lvl = jnp.stack([jnp.where(jh <= ih, level, -1), jnp.where(jh >= ih, level, -1)])
    return tri, lvl


def _postmix1_kernel(o_ref, og_ref, x_ref, mod_ref, g_ref, gn_ref, wo_hbm, w1_hbm, w2_hbm,
                     out_ref, *scratch):
    weights = _TailWeights(1, wo_hbm, w1_hbm, w2_hbm, *scratch)
    _first_step_streams_weights(
        functools.partial(_postmix1_body, o_ref, og_ref, x_ref, mod_ref, g_ref, gn_ref, out_ref,
                          weights), weights)


def _postmix1_body(o_ref, og_ref, x_ref, mod_ref, g_ref, gn_ref, out_ref, weights, streaming):
    gn = gn_ref[...]
    ys = []
    for h in range(GLA_HEADS):
        cols = slice(h * GLA_DV, (h + 1) * GLA_DV)
        o = _rms(o_ref[:, cols], gn)
        og = og_ref[:, cols]
        ys.append((o * (og / (1.0 + jnp.exp(-og)))).astype(BF16))
    if streaming:
        weights.fetch_out_proj()
    mix = None
    for h, y in enumerate(ys):
        part = jnp.dot(y, weights.wo_s[h * GLA_DV:(h + 1) * GLA_DV, :], preferred_element_type=F32)
        mix = part if mix is None else mix + part
    gt1 = _mod_chunk(mod_ref, 2)
    x1 = x_ref[...] + gt1 * _rms(mix, g_ref[1:2, :])
    out_ref[...] = _mlp_tail(x1, mod_ref, g_ref, weights, streaming)


def _postmix1(grp, o, og, x2d, mod, norm_g, gla_norm_g, w_out, w1, w2, tm):
    hbm = pl.BlockSpec(memory_space=pl.ANY)
    return pl.pallas_call(
        _postmix1_kernel,
        grid=(grp.t // tm,),
        in_specs=[_row_spec(tm, GLA_VD), _row_spec(tm, GLA_VD),
                  _row_spec(tm, D_MODEL), grp.mod_spec(tm), _const_spec((4, D_MODEL)),
                  _const_spec((1, GLA_DV)), hbm, hbm, hbm],
        out_specs=_row_spec(tm, D_MODEL),
        out_shape=jax.ShapeDtypeStruct((grp.t, D_MODEL), F32),
        scratch_shapes=_tail_weight_scratch(),
        compiler_params=_cparams("arbitrary"),
        name="postmix1_mlp",
    )(o, og, x2d, mod, norm_g, gla_norm_g, w_out, w1, w2)


def _run_group(grp, x, mods, p, rope_tabs, k_ctx, v_ctx, s0f, s0b, tm):
    x2d = x.reshape(grp.t, D_MODEL)
    zc, q, kv, *cache_t = _premix0(grp, x2d, mods[0], p["norm_g"][0], p["ab_w_in"], rope_tabs, tm)
    if k_ctx is None:
        attn = _ctx_attention(grp, q, kv, p["sink"])
    else:
        attn = _lat_attention(grp, q, kv, k_ctx, v_ctx, p["sink"])
    x2d = _postmix0(grp, zc, attn, x2d, mods[0], p["norm_g"][0], p["conv_w"], p["ab_w_out"],
                    p["mlp_w1"], p["mlp_w2"], tm)
    gq, gk, gv, og, gf, gb = _premix1(grp, x2d, mods[1], p["norm_g"][1], p["gla_w_in"],
                                      p["gla_w_gate"], p["gla_gate_bias"], tm)
    o, sf, sb = _gla_scan(grp, gq, gk, gv, gf, gb, s0f, s0b, p["tri"], p["lvl"])
    x2d = _postmix1(grp, o, og, x2d, mods[1], p["norm_g"][1], p["gla_norm_g"], p["gla_w_out"],
                    p["mlp_w1"], p["mlp_w2"], tm)
    return x2d.reshape(x.shape), cache_t, sf, sb


def kernel(x_prompt, x_sample, cache_k, cache_v, state_fwd, state_bwd, c, c_ctx, mod_w, mod_b,
           norm_g, ab_w_in, conv_w, attn_sink, ab_w_out, gla_w_in, gla_gate_w, gla_gate_b,
           gla_norm_g, gla_w_out, mlp_w1, mlp_w2):
    b_ctx, n_ctx, _ = x_prompt.shape
    b_lat, n_lat, _ = x_sample.shape
    assert mod_w.shape[0] == 2 and ab_w_in.shape[0] == 1 and gla_w_in.shape[0] == 1
    assert 1 + b_lat <= 8

    cond8 = jnp.zeros((8, D_MODEL), F32).at[0].set(c_ctx).at[1:1 + b_lat].set(c)
    mod = _modulation(cond8, mod_w, mod_b)
    mods_ctx = [mod[l, 0:1].reshape(1, 1, -1) for l in range(2)]
    mods_lat = [mod[l, 1:1 + b_lat].reshape(b_lat, 1, -1) for l in range(2)]

    w_gate = jnp.zeros((2 * GLA_RANK, 2 * GLA_QK), F32)
    w_gate = w_gate.at[:GLA_RANK, :GLA_QK].set(gla_gate_w[0, 0])
    w_gate = w_gate.at[GLA_RANK:, GLA_QK:].set(gla_gate_w[0, 1])
    tri, lvl = _gla_constants()
    p = {
        "norm_g": norm_g,
        "ab_w_in": ab_w_in,
        "conv_w": conv_w[0],
        "sink": attn_sink[0],
        "ab_w_out": ab_w_out,
        "mlp_w1": mlp_w1,
        "mlp_w2": mlp_w2,
        "gla_w_in": gla_w_in,
        "gla_w_gate": w_gate.astype(BF16),
        "gla_gate_bias": gla_gate_b[0].reshape(1, 2 * GLA_QK),
        "gla_norm_g": gla_norm_g[0].reshape(1, GLA_DV),
        "gla_w_out": gla_w_out,
        "tri": tri,
        "lvl": lvl,
    }

    ctx = _Group(b_ctx, n_ctx, per_seq_mod=False)
    lat = _Group(b_lat, n_lat, per_seq_mod=True)

    y_prompt, (k_t, v_t), sf, sb = _run_group(ctx, x_prompt, mods_ctx, p, None, None, None,
                                              None, None, tm=512)

    def cache_layout(t):
        t = t.reshape(b_ctx, 1, N_KV_HEADS, HEAD_DIM, n_ctx)
        return jnp.transpose(t, (0, 1, 4, 2, 3))

    new_k, new_v = cache_layout(k_t), cache_layout(v_t)
    new_sf, new_sb = sf[:, None], sb[:, None]

    past = cache_k.shape[2]
    k_ctx = cache_k[:, 0].reshape(b_lat, past, KD)
    v_ctx = cache_v[:, 0].reshape(b_lat, past, KD)
    y_sample, _, _, _ = _run_group(lat, x_sample, mods_lat, p, _rope_tables(n_lat), k_ctx, v_ctx,
                                   state_fwd[:, 0], state_bwd[:, 0], tm=512)
    return (y_prompt, y_sample, new_k, new_v, new_sf, new_sb)
```

```python
import functools

import jax
import jax.numpy as jnp
from jax import lax
from jax.experimental import pallas as pl
from jax.experimental.pallas import tpu as pltpu

F32 = jnp.float32
BF16 = jnp.bfloat16

D_MODEL = 1024
MOD_CHUNKS = 6
EPS = 1e-6
CONV_DIM = 512
N_Q_HEADS = 8
N_KV_HEADS = 2
GQA_GROUP = 4
HEAD_DIM = 64
WINDOW = 128
ATTN_BLOCK = 128
ATTN_BLOCKS_PER_STEP = 4
GRID_W = 64
ROPE_BASE = 10000.0
QD = N_Q_HEADS * HEAD_DIM
KD = N_KV_HEADS * HEAD_DIM
AB_IN = 3 * CONV_DIM + QD + 2 * KD
GLA_HEADS = 4
GLA_DK = 128
GLA_DV = 256
GLA_RANK = 16
GLA_GATE_NORM = 16.0
GLA_TILE = 256
GLA_HEADS_PER_STEP = 2
LOG2E = 1.4426950408889634
GLA_QK = GLA_HEADS * GLA_DK
GLA_VD = GLA_HEADS * GLA_DV
GLA_MAIN = 2 * GLA_QK + 2 * GLA_VD
D_FF = 4 * D_MODEL
MLP_CHUNK = 512
MLP_CHUNKS = D_FF // MLP_CHUNK
NEG_INF = -1e30
LANES = 128
VMEM_LIMIT = 56 * 1024 * 1024

NT_DIMS = (((1,), (1,)), ((), ()))
TN_DIMS = (((0,), (0,)), ((), ()))


def _cparams(*sem):
    return pltpu.CompilerParams(dimension_semantics=sem, vmem_limit_bytes=VMEM_LIMIT)


def _bdot(a, b):
    return jnp.dot(a.astype(BF16), b.astype(BF16), preferred_element_type=F32)


def _bdot_nt(a, b):
    return lax.dot_general(a.astype(BF16), b.astype(BF16), NT_DIMS, preferred_element_type=F32)


def _rms(x, g):
    ms = jnp.mean(x * x, axis=-1, keepdims=True)
    return x * lax.rsqrt(ms + EPS) * g


def _mod_chunk(mod_ref, i):
    return mod_ref[:, i * D_MODEL:(i + 1) * D_MODEL]


def _const_spec(shape):
    return pl.BlockSpec(shape, lambda *_: (0,) * len(shape))


def _f32_weight_spec(shape):
    assert shape[0] == 1
    return pl.BlockSpec((None,) + tuple(shape[1:]), lambda *_: (0, 0, 0),
                        pipeline_mode=pl.Buffered(1))


def _cast_once(w_ref, wb_ref):
    @pl.when(pl.program_id(0) == 0)
    def _():
        wb_ref[...] = w_ref[...].astype(BF16)


def _mod_kernel(cond_ref, w_ref, b_ref, o_ref):
    cnd = cond_ref[...]
    s = cnd / (1.0 + jnp.exp(-cnd))
    o_ref[...] = _bdot(s, w_ref[...]) + b_ref[...]


def _modulation(cond8, mod_w, mod_b):
    depth = mod_w.shape[0]
    n = mod_w.shape[2]
    tn = 1536
    return pl.pallas_call(
        _mod_kernel,
        grid=(depth, n // tn),
        in_specs=[
            pl.BlockSpec((8, D_MODEL), lambda l, j: (0, 0)),
            pl.BlockSpec((None, D_MODEL, tn), lambda l, j: (l, 0, j)),
            pl.BlockSpec((None, 1, tn), lambda l, j: (l, 0, j)),
        ],
        out_specs=pl.BlockSpec((None, 8, tn), lambda l, j: (l, 0, j)),
        out_shape=jax.ShapeDtypeStruct((depth, 8, n), F32),
        compiler_params=_cparams("arbitrary", "arbitrary"),
        name="modulation",
    )(cond8, mod_w, mod_b.reshape(depth, 1, n))


class _Group:
    def __init__(self, b, n, per_seq_mod):
        self.b, self.n, self.t = b, n, b * n
        self.per_seq_mod = per_seq_mod

    def mod_spec(self, tm):
        if self.per_seq_mod:
            assert self.n % tm == 0
            per = self.n // tm
            return pl.BlockSpec((None, 1, MOD_CHUNKS * D_MODEL), lambda t: (t // per, 0, 0))
        return pl.BlockSpec((None, 1, MOD_CHUNKS * D_MODEL), lambda t: (0, 0, 0))


def _row_spec(tm, width):
    return pl.BlockSpec((tm, width), lambda t: (t, 0))


def _rope(x, cos, sin_lo, sin_hi):
    return (x * cos + pltpu.roll(x, LANES - 16, axis=1) * sin_lo
            + pltpu.roll(x, 16, axis=1) * sin_hi)


def _premix0_kernel(*refs, rope):
    if rope:
        (x_ref, mod_ref, g_ref, w_ref, cos_ref, slo_ref, shi_ref,
         zc_ref, q_ref, kv_ref, wb_ref) = refs
    else:
        x_ref, mod_ref, g_ref, w_ref, zc_ref, q_ref, kv_ref, kt_ref, vt_ref, wb_ref = refs
    _cast_once(w_ref, wb_ref)
    sh1, sc1 = _mod_chunk(mod_ref, 0), _mod_chunk(mod_ref, 1)
    h = _rms(x_ref[...], g_ref[0:1, :]) * (1.0 + sc1) + sh1
    z = _bdot(h, wb_ref[...])
    c3 = 3 * CONV_DIM
    zc_ref[...] = z[:, :c3]
    scale = HEAD_DIM ** -0.5 * LOG2E
    if rope:
        cos, slo, shi = cos_ref[...], slo_ref[...], shi_ref[...]
        for j in range(QD // LANES):
            qs = z[:, c3 + j * LANES:c3 + (j + 1) * LANES]
            q_ref[:, j * LANES:(j + 1) * LANES] = (_rope(qs, cos, slo, shi) * scale).astype(BF16)
        kv_ref[:, :KD] = _rope(z[:, c3 + QD:c3 + QD + KD], cos, slo, shi)
    else:
        q_ref[...] = (z[:, c3:c3 + QD] * scale).astype(BF16)
        kv_ref[:, :KD] = z[:, c3 + QD:c3 + QD + KD]
        n = kt_ref.shape[2]
        for j in range(kt_ref.shape[0]):
            kt_ref[j] = z[j * n:(j + 1) * n, c3 + QD:c3 + QD + KD].T
            vt_ref[j] = z[j * n:(j + 1) * n, c3 + QD + KD:].T
    kv_ref[:, KD:] = z[:, c3 + QD + KD:]


def _premix0(grp, x2d, mod, norm_g, w_in, rope_tabs, tm):
    rope = rope_tabs is not None
    in_specs = [_row_spec(tm, D_MODEL), grp.mod_spec(tm), _const_spec((4, D_MODEL)),
                _f32_weight_spec(w_in.shape)]
    args = [x2d, mod, norm_g, w_in]
    if rope:
        per = grp.n // tm
        in_specs += [pl.BlockSpec((tm, LANES), lambda t: (t % per, 0))] * 3
        args += list(rope_tabs)
    out_specs = [_row_spec(tm, 3 * CONV_DIM), _row_spec(tm, QD), _row_spec(tm, 2 * KD)]
    out_shape = [jax.ShapeDtypeStruct((grp.t, 3 * CONV_DIM), F32),
                 jax.ShapeDtypeStruct((grp.t, QD), BF16),
                 jax.ShapeDtypeStruct((grp.t, 2 * KD), F32)]
    if not rope:
        assert tm % grp.n == 0
        out_specs += [pl.BlockSpec((tm // grp.n, KD, grp.n), lambda t: (t, 0, 0))] * 2
        out_shape += [jax.ShapeDtypeStruct((grp.b, KD, grp.n), F32)] * 2
    return pl.pallas_call(
        functools.partial(_premix0_kernel, rope=rope),
        grid=(grp.t // tm,),
        in_specs=in_specs,
        out_specs=out_specs,
        out_shape=out_shape,
        scratch_shapes=[pltpu.VMEM(w_in.shape[1:], BF16)],
        compiler_params=_cparams("arbitrary"),
        name="premix0_rope" if rope else "premix0",
    )(*args)


def _rope_tables(n):
    rows = n // GRID_W
    pos_r = jnp.repeat(jnp.arange(rows), GRID_W)
    pos_c = jnp.tile(jnp.arange(GRID_W), rows)
    half = HEAD_DIM // 2
    quarter = half // 2
    inv = ROPE_BASE ** (-(jnp.arange(quarter, dtype=F32) * 2.0 / half))

    def cs(pos):
        ang = pos.astype(F32)[:, None] * inv[None, :]
        return jnp.cos(ang), jnp.sin(ang)

    cr, sr = cs(pos_r)
    cc, sc = cs(pos_c)
    zero = jnp.zeros_like(sr)
    cos = jnp.concatenate([cr, cr, cc, cc], axis=1)
    sin_lo = jnp.concatenate([-sr, zero, -sc, zero], axis=1)
    sin_hi = jnp.concatenate([zero, sr, zero, sc], axis=1)
    rep = LANES // HEAD_DIM
    return tuple(jnp.tile(t, (1, rep)) for t in (cos, sin_lo, sin_hi))


def _attention_operands(k_all, v_all):
    assert KD == LANES == 2 * HEAD_DIM and GQA_GROUP == 4
    lane = lax.broadcasted_iota(jnp.int32, (1, LANES), 1)
    sub = lax.broadcasted_iota(jnp.int32, (LANES, 1), 0)
    v_t = v_all.T
    k_swapped = pltpu.roll(k_all, HEAD_DIM, axis=1)
    ops = []
    for g in range(N_KV_HEADS):
        k_low, k_high = (k_all, k_swapped) if g == 0 else (k_swapped, k_all)
        kz_even = jnp.where(lane < HEAD_DIM, k_low, 0.0).astype(BF16)
        kz_odd = jnp.where(lane >= HEAD_DIM, k_high, 0.0).astype(BF16)
        own = (sub < HEAD_DIM) if g == 0 else (sub >= HEAD_DIM)
        v_ext_t = jnp.where(own, v_t, 1.0).astype(BF16)
        ops.append((kz_even, kz_odd, v_ext_t))
    return ops


def _concat_operands(parts):
    return [(jnp.concatenate([p[g][0] for p in parts], axis=0),
             jnp.concatenate([p[g][1] for p in parts], axis=0),
             jnp.concatenate([p[g][2] for p in parts], axis=1)) for g in range(N_KV_HEADS)]


def _sink_attention(sink_ref, q_ref, operands, bias_t, o_ref):
    m = q_ref.shape[0]
    o_t = {}
    for g in range(N_KV_HEADS):
        kz_even, kz_odd, v_ext_t = operands[g]
        qq = jnp.concatenate([q_ref[:, (2 * g) * LANES:(2 * g + 1) * LANES],
                              q_ref[:, (2 * g + 1) * LANES:(2 * g + 2) * LANES]], axis=0)
        s = jnp.concatenate([_bdot_nt(kz_even, qq), _bdot_nt(kz_odd, qq)], axis=1)
        heads = [4 * g, 4 * g + 2, 4 * g + 1, 4 * g + 3]
        if bias_t is not None:
            nb = bias_t.shape[0]
            s = jnp.concatenate([s[:nb] + jnp.concatenate([bias_t] * GQA_GROUP, axis=1), s[nb:]],
                                axis=0)
        sink = jnp.concatenate([jnp.full((1, m), sink_ref[h] * LOG2E, F32) for h in heads], axis=1)
        mx = jnp.maximum(jnp.max(s, axis=0, keepdims=True), sink)
        p = jnp.exp2(s - mx).astype(BF16)
        oe = jnp.dot(v_ext_t, p, preferred_element_type=F32)
        other = (1 - g) * HEAD_DIM
        den = oe[other:other + 1] + jnp.exp2(sink - mx)
        o_g = oe[g * HEAD_DIM:(g + 1) * HEAD_DIM] / den
        for i, h in enumerate(heads):
            o_t[h] = o_g[:, i * m:(i + 1) * m]
    for j in range(N_Q_HEADS // 2):
        pair_t = jnp.concatenate([o_t[2 * j], o_t[2 * j + 1]], axis=0)
        o_ref[:, j * LANES:(j + 1) * LANES] = pair_t.T.astype(BF16)


def _ctx_attn_kernel(sink_ref, q_ref, kv_ref, o_ref):
    operands = _attention_operands(kv_ref[:, :KD], kv_ref[:, KD:])
    _sink_attention(sink_ref, q_ref, operands, None, o_ref)


def _ctx_attention(grp, q, kv, sink):
    n = grp.n
    return pl.pallas_call(
        _ctx_attn_kernel,
        grid=(grp.b,),
        in_specs=[pl.BlockSpec(memory_space=pltpu.SMEM), _row_spec(n, QD), _row_spec(n, 2 * KD)],
        out_specs=_row_spec(n, QD),
        out_shape=jax.ShapeDtypeStruct((grp.t, QD), BF16),
        compiler_params=_cparams("arbitrary"),
        name="ctx_attention",
    )(sink, q, kv)


def _lat_attn_kernel(sink_ref, q_ref, kvp_ref, kvc_ref, kvn_ref, kc_ref, vc_ref, o_ref, *, n):
    band = jnp.concatenate([kvp_ref[...], kvc_ref[...], kvn_ref[...]], axis=0)
    kj = lax.broadcasted_iota(jnp.int32, (3 * ATTN_BLOCK, ATTN_BLOCK), 0)
    qi = lax.broadcasted_iota(jnp.int32, (3 * ATTN_BLOCK, ATTN_BLOCK), 1)
    rel = kj - ATTN_BLOCK - qi
    ctx_ops = _attention_operands(kc_ref[...], vc_ref[...])
    band_ops = [_attention_operands(band[i * ATTN_BLOCK:(i + 1) * ATTN_BLOCK, :KD],
                                    band[i * ATTN_BLOCK:(i + 1) * ATTN_BLOCK, KD:])
                for i in range(ATTN_BLOCKS_PER_STEP + 2)]
    for j in range(ATTN_BLOCKS_PER_STEP):
        blk = pl.program_id(1) * ATTN_BLOCKS_PER_STEP + j
        kpos = (blk - 1) * ATTN_BLOCK + kj
        valid = (jnp.abs(rel) <= WINDOW) & (kpos >= 0) & (kpos < n)
        bias_t = jnp.where(valid, 0.0, NEG_INF)
        operands = _concat_operands(band_ops[j:j + 3] + [ctx_ops])
        rows = pl.ds(j * ATTN_BLOCK, ATTN_BLOCK)
        _sink_attention(sink_ref, q_ref.at[rows], operands, bias_t, o_ref.at[rows])


def _lat_attention(grp, q, kv, k_ctx, v_ctx, sink):
    nb = grp.n // ATTN_BLOCK
    per = ATTN_BLOCKS_PER_STEP
    steps = nb // per
    past = k_ctx.shape[1]

    def edge(off):
        return pl.BlockSpec((ATTN_BLOCK, 2 * KD),
                            lambda b, i: (b * nb + jnp.clip(i * per + off, 0, nb - 1), 0))

    ctx_spec = pl.BlockSpec((None, past, KD), lambda b, i: (b, 0, 0))
    return pl.pallas_call(
        functools.partial(_lat_attn_kernel, n=grp.n),
        grid=(grp.b, steps),
        in_specs=[pl.BlockSpec(memory_space=pltpu.SMEM),
                  pl.BlockSpec((per * ATTN_BLOCK, QD), lambda b, i: (b * steps + i, 0)),
                  edge(-1),
                  pl.BlockSpec((per * ATTN_BLOCK, 2 * KD), lambda b, i: (b * steps + i, 0)),
                  edge(per), ctx_spec, ctx_spec],
        out_specs=pl.BlockSpec((per * ATTN_BLOCK, QD), lambda b, i: (b * steps + i, 0)),
        out_shape=jax.ShapeDtypeStruct((grp.t, QD), BF16),
        compiler_params=_cparams("arbitrary", "arbitrary"),
        name="lat_attention",
    )(sink, q, kv, kv, kv, k_ctx, v_ctx)


class _TailWeights:
    def __init__(self, layer, wo_hbm, w1_hbm, w2_hbm, wo_s, w1_s, w2_s, stage1, stage2, sem):
        self.layer = layer
        self.wo_hbm, self.w1_hbm, self.w2_hbm = wo_hbm, w1_hbm, w2_hbm
        self.wo_s, self.w1_s, self.w2_s = wo_s, w1_s, w2_s
        self.stage1, self.stage2, self.sem = stage1, stage2, sem

    def _w1_copy(self, j):
        src = self.w1_hbm.at[self.layer, :, pl.ds(j * MLP_CHUNK, MLP_CHUNK)]
        return pltpu.make_async_copy(src, self.stage1.at[j % 2], self.sem.at[0, j % 2])

    def _w2_copy(self, j):
        src = self.w2_hbm.at[self.layer, pl.ds(j * MLP_CHUNK, MLP_CHUNK), :]
        return pltpu.make_async_copy(src, self.stage2.at[j % 2], self.sem.at[1, j % 2])

    def _wo_copy(self, i):
        src = self.wo_hbm.at[0, pl.ds(i * MLP_CHUNK, MLP_CHUNK), :]
        return pltpu.make_async_copy(src, self.stage2.at[i], self.sem.at[1, i])

    def start(self):
        self._wo_copy(0).start()
        self._wo_copy(1).start()
        self._w1_copy(0).start()
        self._w1_copy(1).start()

    def fetch_out_proj(self):
        for i in range(2):
            self._wo_copy(i).wait()
            self.wo_s[i * MLP_CHUNK:(i + 1) * MLP_CHUNK, :] = self.stage2[i].astype(BF16)
            self._w2_copy(i).start()

    def fetch_mlp_chunk(self, j):
        self._w1_copy(j).wait()
        self.w1_s[j] = self.stage1[j % 2].astype(BF16)
        if j + 2 < MLP_CHUNKS:
            self._w1_copy(j + 2).start()
        self._w2_copy(j).wait()
        self.w2_s[j] = self.stage2[j % 2].astype(BF16)
        if j + 2 < MLP_CHUNKS:
            self._w2_copy(j + 2).start()


def _tail_weight_scratch():
    assert D_MODEL == 2 * MLP_CHUNK
    return [pltpu.VMEM((D_MODEL, D_MODEL), BF16),
            pltpu.VMEM((MLP_CHUNKS, D_MODEL, MLP_CHUNK), BF16),
            pltpu.VMEM((MLP_CHUNKS, MLP_CHUNK, D_MODEL), BF16),
            pltpu.VMEM((2, D_MODEL, MLP_CHUNK), F32),
            pltpu.VMEM((2, MLP_CHUNK, D_MODEL), F32),
            pltpu.SemaphoreType.DMA((2, 2))]


def _first_step_streams_weights(body, weights):
    first = pl.program_id(0) == 0

    @pl.when(first)
    def _():
        weights.start()
        body(True)

    @pl.when(jnp.logical_not(first))
    def _():
        body(False)


def _mlp_tail(x, mod_ref, g_ref, weights, streaming):
    sh2, sc2, gt2 = _mod_chunk(mod_ref, 3), _mod_chunk(mod_ref, 4), _mod_chunk(mod_ref, 5)
    hb = (_rms(x, g_ref[2:3, :]) * (1.0 + sc2) + sh2).astype(BF16)
    acc = None
    for j in range(MLP_CHUNKS):
        if streaming:
            weights.fetch_mlp_chunk(j)
        a = jnp.dot(hb, weights.w1_s[j], preferred_element_type=F32)
        a = jnp.maximum(a, 0.0)
        part = jnp.dot((a * a).astype(BF16), weights.w2_s[j], preferred_element_type=F32)
        acc = part if acc is None else acc + part
    return x + gt2 * _rms(acc, g_ref[3:4, :])


def _postmix0_kernel(zc_ref, zp_ref, zn_ref, at_ref, x_ref, mod_ref, g_ref, cw_ref,
                     wo_hbm, w1_hbm, w2_hbm, o_ref, *scratch, n, tm):
    weights = _TailWeights(0, wo_hbm, w1_hbm, w2_hbm, *scratch)
    _first_step_streams_weights(
        functools.partial(_postmix0_body, zc_ref, zp_ref, zn_ref, at_ref, x_ref, mod_ref, g_ref,
                          cw_ref, o_ref, weights, n, tm), weights)


def _postmix0_body(zc_ref, zp_ref, zn_ref, at_ref, x_ref, mod_ref, g_ref, cw_ref, o_ref,
                   weights, n, tm, streaming):
    c = CONV_DIM
    zc = zc_ref[...]
    u = zc[:, c:2 * c] * zc[:, 2 * c:]
    u_before = zp_ref[7:8, c:2 * c] * zp_ref[7:8, 2 * c:]
    u_after = zn_ref[0:1, c:2 * c] * zn_ref[0:1, 2 * c:]
    row = lax.broadcasted_iota(jnp.int32, (tm, 1), 0)
    pos = (pl.program_id(0) * tm + row) % n
    u_prev = jnp.where(row == 0, u_before, pltpu.roll(u, 1, axis=0))
    u_prev = jnp.where(pos == 0, 0.0, u_prev)
    u_next = jnp.where(row == tm - 1, u_after, pltpu.roll(u, tm - 1, axis=0))
    u_next = jnp.where(pos == n - 1, 0.0, u_next)
    conv = u_prev * cw_ref[0:1, :] + u * cw_ref[1:2, :] + u_next * cw_ref[2:3, :]
    if streaming:
        weights.fetch_out_proj()
    mix = (_bdot(zc[:, :c] * conv, weights.wo_s[:c, :])
           + jnp.dot(at_ref[...], weights.wo_s[c:, :], preferred_element_type=F32))
    gt1 = _mod_chunk(mod_ref, 2)
    x1 = x_ref[...] + gt1 * _rms(mix, g_ref[1:2, :])
    o_ref[...] = _mlp_tail(x1, mod_ref, g_ref, weights, streaming)


def _postmix0(grp, zc, attn, x2d, mod, norm_g, conv_w, w_out, w1, w2, tm):
    r8 = tm // 8
    last8 = grp.t // 8 - 1
    hbm = pl.BlockSpec(memory_space=pl.ANY)
    return pl.pallas_call(
        functools.partial(_postmix0_kernel, n=grp.n, tm=tm),
        grid=(grp.t // tm,),
        in_specs=[_row_spec(tm, 3 * CONV_DIM),
                  pl.BlockSpec((8, 3 * CONV_DIM), lambda t: (jnp.maximum(t * r8 - 1, 0), 0)),
                  pl.BlockSpec((8, 3 * CONV_DIM), lambda t: (jnp.minimum((t + 1) * r8, last8), 0)),
                  _row_spec(tm, QD), _row_spec(tm, D_MODEL), grp.mod_spec(tm),
                  _const_spec((4, D_MODEL)), _const_spec((3, CONV_DIM)), hbm, hbm, hbm],
        out_specs=_row_spec(tm, D_MODEL),
        out_shape=jax.ShapeDtypeStruct((grp.t, D_MODEL), F32),
        scratch_shapes=_tail_weight_scratch(),
        compiler_params=_cparams("arbitrary"),
        name="postmix0_mlp",
    )(zc, zc, zc, attn, x2d, mod, norm_g, conv_w, w_out, w1, w2)


def _premix1_kernel(x_ref, mod_ref, g_ref, w_ref, wg_ref, gb_ref,
                    q_ref, k_ref, v_ref, og_ref, gf_ref, gbk_ref, wb_ref):
    _cast_once(w_ref, wb_ref)
    sh1, sc1 = _mod_chunk(mod_ref, 0), _mod_chunk(mod_ref, 1)
    hb = (_rms(x_ref[...], g_ref[0:1, :]) * (1.0 + sc1) + sh1).astype(BF16)
    z = jnp.dot(hb, wb_ref[:, :GLA_MAIN], preferred_element_type=F32)
    q_ref[...] = z[:, :GLA_QK] * (GLA_DK ** -0.5)
    k_ref[...] = z[:, GLA_QK:2 * GLA_QK]
    v_ref[...] = z[:, 2 * GLA_QK:2 * GLA_QK + GLA_VD].astype(BF16)
    og_ref[...] = z[:, 2 * GLA_QK + GLA_VD:]
    r = jnp.dot(hb, wb_ref[:, GLA_MAIN:], preferred_element_type=F32)
    pre = _bdot(r, wg_ref[...]) + gb_ref[...]
    gate = ((jnp.minimum(pre, 0.0) - jnp.log1p(jnp.exp(-jnp.abs(pre))))
            * (LOG2E / GLA_GATE_NORM))
    gf_ref[...] = gate[:, :GLA_QK]
    gbk_ref[...] = gate[:, GLA_QK:]


def _premix1(grp, x2d, mod, norm_g, w_in, w_gate, gate_bias, tm):
    t = grp.t
    return pl.pallas_call(
        _premix1_kernel,
        grid=(t // tm,),
        in_specs=[_row_spec(tm, D_MODEL), grp.mod_spec(tm), _const_spec((4, D_MODEL)),
                  _f32_weight_spec(w_in.shape), _const_spec((2 * GLA_RANK, 2 * GLA_QK)),
                  _const_spec((1, 2 * GLA_QK))],
        out_specs=[_row_spec(tm, GLA_QK), _row_spec(tm, GLA_QK), _row_spec(tm, GLA_VD),
                   _row_spec(tm, GLA_VD), _row_spec(tm, GLA_QK), _row_spec(tm, GLA_QK)],
        out_shape=[jax.ShapeDtypeStruct((t, GLA_QK), F32), jax.ShapeDtypeStruct((t, GLA_QK), F32),
                   jax.ShapeDtypeStruct((t, GLA_VD), BF16),
                   jax.ShapeDtypeStruct((t, GLA_VD), F32), jax.ShapeDtypeStruct((t, GLA_QK), F32),
                   jax.ShapeDtypeStruct((t, GLA_QK), F32)],
        scratch_shapes=[pltpu.VMEM(w_in.shape[1:], BF16)],
        compiler_params=_cparams("arbitrary"),
        name="premix1",
    )(x2d, mod, norm_g, w_in, w_gate, gate_bias)


def _split3(x):
    hi = x.astype(BF16)
    r1 = x - hi.astype(F32)
    mid = r1.astype(BF16)
    lo = (r1 - mid.astype(F32)).astype(BF16)
    return hi, mid, lo


def _level_exponent(b, s, reverse):
    idx = s if reverse else s - 1
    if s >= 8:
        n = GLA_TILE // (2 * s)
        b4 = b.reshape(n, 2, s, GLA_DK)
        first, second = b4[:, 0:1], b4[:, 1:2]
        r = (second[:, :, 0:1] if reverse else first[:, :, s - 1:s])
        parts = [first - r, r - second] if reverse else [r - first, second - r]
        return jnp.concatenate(parts, axis=1).reshape(GLA_TILE, GLA_DK)
    b8 = b.reshape(GLA_TILE // 8, 8, GLA_DK)
    sub = lax.broadcasted_iota(jnp.int32, (1, 8, 1), 1)
    if s == 4:
        r = b8[:, idx:idx + 1, :]
    else:
        assert s == 2
        r = jnp.where(sub < 4, b8[:, idx:idx + 1, :], b8[:, 4 + idx:5 + idx, :])
    in_second = (sub // s) % 2 == 1
    sign = jnp.where(in_second != reverse, 1.0, -1.0)
    return ((b8 - r) * sign).reshape(GLA_TILE, GLA_DK)


def _pair_matrix(q, k, g_f, g_b, b_f, b_b, lvl):
    half = GLA_TILE // 2
    lo, hi = slice(0, half), slice(half, GLA_TILE)
    qb, kb = q.astype(BF16), k.astype(BF16)
    row = lax.broadcasted_iota(jnp.int32, (GLA_TILE, 1), 0)
    odd = row % 2 == 1
    own = jnp.sum(q * k, axis=-1, keepdims=True)
    k_adj = jnp.where(odd, pltpu.roll(k, 1, axis=0), pltpu.roll(k, GLA_TILE - 1, axis=0))
    adj = jnp.sum(q * jnp.exp2(jnp.where(odd, g_f, g_b)) * k_adj, axis=-1, keepdims=True)
    blocks = [jnp.where(lvl == 0, 2.0 * own[rows], jnp.where(lvl == 1, adj[rows], 0.0))
              for rows in (lo, hi)]
    cross_f = cross_b = None
    s, level = 2, 2
    while s < GLA_TILE:
        if s >= 16:
            n = GLA_TILE // (2 * s)

            def halves(x):
                x4 = x.reshape(n, 2, s, GLA_DK)
                return x4[:, 0:1], x4[:, 1:2]

            def rows(first, second):
                return jnp.concatenate([first, second], axis=1).reshape(GLA_TILE, GLA_DK)

            (bf1, bf2), (bb1, bb2) = halves(b_f), halves(b_b)
            (q1, q2), (k1, k2) = halves(qb), halves(kb)
            r_f, r_b = bf1[:, :, s - 1:s], bb2[:, :, 0:1]
            zero = jnp.zeros_like(q1)
            lhs = jnp.concatenate(
                [rows(zero, q2 * jnp.exp2(bf2 - r_f).astype(BF16)),
                 rows(q1 * jnp.exp2(bb1 - r_b).astype(BF16), zero)], axis=1)
            rhs = jnp.concatenate(
                [rows(k1 * jnp.exp2(r_f - bf1).astype(BF16), zero),
                 rows(zero, k2 * jnp.exp2(r_b - bb2).astype(BF16))], axis=1)
        else:
            second = (row // s) % 2 == 1
            f_f = jnp.exp2(_level_exponent(b_f, s, False)).astype(BF16)
            f_b = jnp.exp2(_level_exponent(b_b, s, True)).astype(BF16)
            u = qb * jnp.where(second, f_f, f_b)
            w = kb * jnp.where(second, f_b, f_f)
            zero = jnp.zeros_like(u)
            lhs = jnp.concatenate([jnp.where(second, u, zero), jnp.where(second, zero, u)], axis=1)
            rhs = jnp.concatenate([jnp.where(second, zero, w), jnp.where(second, w, zero)], axis=1)
        pairs = _bdot_nt(lhs, rhs)
        if s == half:
            cross_f, cross_b = pairs[hi, lo], pairs[lo, hi]
        else:
            blocks = [jnp.where(lvl == level, pairs[lo, lo], blocks[0]),
                      jnp.where(lvl == level, pairs[hi, hi], blocks[1])]
        s, level = 2 * s, level + 1
    top = jnp.concatenate([blocks[0].astype(BF16), cross_b.astype(BF16)], axis=1)
    bottom = jnp.concatenate([cross_f.astype(BF16), blocks[1].astype(BF16)], axis=1)
    return jnp.concatenate([top, bottom], axis=0)


def _carry_state(q, k, b, v, s_ref, reverse):
    edge = 0 if reverse else GLA_TILE - 1
    b_last = b[edge:edge + 1, :]
    qe = (q * jnp.exp2(b)).astype(BF16)
    ke = (k * jnp.exp2(b_last - b)).astype(BF16)
    st = s_ref[...]
    s_ref[...] = st * jnp.exp2(b_last) + lax.dot_general(
        v, ke, TN_DIMS, preferred_element_type=F32)
    return _bdot_nt(qe, st)


def _gla_kernel(*refs, zero_init, nt):
    if zero_init:
        (tri_ref, lvl_ref, q_ref, k_ref, gf_ref, gb_ref, v_ref,
         o_ref, sfo_ref, sbo_ref, sf_ref, sb_ref, bb_ref) = refs
        sf_ref[...] = jnp.zeros_like(sf_ref)
        sb_ref[...] = jnp.zeros_like(sb_ref)
    else:
        (tri_ref, lvl_ref, q_ref, k_ref, gf_ref, gb_ref, v_ref, s0f_ref, s0b_ref,
         o_ref, sfo_ref, sbo_ref, sf_ref, sb_ref, bb_ref) = refs
        for hh in range(GLA_HEADS_PER_STEP):
            sf_ref[hh] = s0f_ref[hh].T
            sb_ref[hh] = s0b_ref[hh].T

    def tile_rows(tile):
        r0 = tile * GLA_TILE
        return pl.ds(r0 if isinstance(r0, int) else pl.multiple_of(r0, GLA_TILE), GLA_TILE)

    heads = [(slice(hh * GLA_DK, (hh + 1) * GLA_DK), slice(hh * GLA_DV, (hh + 1) * GLA_DV), hh)
             for hh in range(GLA_HEADS_PER_STEP)]

    def forward_sweep(t, carry):
        rows = tile_rows(t)
        g_f, g_b = gf_ref[rows, :], gb_ref[rows, :]
        c = None
        for part in _split3(jnp.concatenate([g_f, g_b], axis=1)):
            term = jnp.dot(tri_ref[...], part, preferred_element_type=F32)
            c = term if c is None else c + term
        width = GLA_HEADS_PER_STEP * GLA_DK
        c_b = c[:, width:]
        b_f = c[:, :width]
        b_b = (c_b[GLA_TILE - 1:GLA_TILE, :] - c_b) + g_b
        bb_ref[t] = b_b
        for dk, dv, hh in heads:
            q, k, v = q_ref[rows, dk], k_ref[rows, dk], v_ref[rows, dv]
            att = _pair_matrix(q, k, g_f[:, dk], g_b[:, dk], b_f[:, dk], b_b[:, dk], lvl_ref[...])
            o_ref[rows, dv] = (jnp.dot(att, v, preferred_element_type=F32)
                               + _carry_state(q, k, b_f[:, dk], v, sf_ref.at[hh], False))
        return carry

    def backward_sweep(i, carry):
        t = nt - 1 - i
        rows = tile_rows(t)
        b_b = bb_ref[t]
        for dk, dv, hh in heads:
            o_ref[rows, dv] += _carry_state(q_ref[rows, dk], k_ref[rows, dk], b_b[:, dk],
                                            v_ref[rows, dv], sb_ref.at[hh], True)
        return carry

    if nt == 1:
        forward_sweep(0, 0)
        backward_sweep(0, 0)
    else:
        lax.fori_loop(0, nt, forward_sweep, 0)
        lax.fori_loop(0, nt, backward_sweep, 0)
    for hh in range(GLA_HEADS_PER_STEP):
        sfo_ref[hh] = sf_ref[hh].T
        sbo_ref[hh] = sb_ref[hh].T


def _gla_scan(grp, q, k, v, gf, gb, s0f, s0b, tri, lvl):
    n = grp.n
    nt = n // GLA_TILE
    hp = GLA_HEADS_PER_STEP
    zero_init = s0f is None
    half = GLA_TILE // 2
    seq_dk = pl.BlockSpec((n, hp * GLA_DK), lambda b, h: (b, h))
    seq_dv = pl.BlockSpec((n, hp * GLA_DV), lambda b, h: (b, h))
    state_spec = pl.BlockSpec((None, hp, GLA_DK, GLA_DV), lambda b, h: (b, h, 0, 0))
    in_specs = [_const_spec((GLA_TILE, GLA_TILE)), _const_spec((half, half)),
                seq_dk, seq_dk, seq_dk, seq_dk, seq_dv]
    args = [tri, lvl, q, k, gf, gb, v]
    if not zero_init:
        in_specs += [state_spec, state_spec]
        args += [s0f, s0b]
    state_shape = jax.ShapeDtypeStruct((grp.b, GLA_HEADS, GLA_DK, GLA_DV), F32)
    return pl.pallas_call(
        functools.partial(_gla_kernel, zero_init=zero_init, nt=nt),
        grid=(grp.b, GLA_HEADS // hp),
        in_specs=in_specs,
        out_specs=[seq_dv, state_spec, state_spec],
        out_shape=[jax.ShapeDtypeStruct((grp.t, GLA_VD), F32), state_shape, state_shape],
        scratch_shapes=[pltpu.VMEM((hp, GLA_DV, GLA_DK), F32), pltpu.VMEM((hp, GLA_DV, GLA_DK), F32),
                        pltpu.VMEM((nt, GLA_TILE, hp * GLA_DK), F32)],
        compiler_params=_cparams("arbitrary", "arbitrary"),
        name="gla_scan",
    )(*args)


def _gla_constants():
    half = GLA_TILE // 2
    i = jnp.arange(GLA_TILE)[:, None]
    j = jnp.arange(GLA_TILE)[None, :]
    tri = (j <= i).astype(BF16)
    ih, jh = i[:half], j[:, :half]
    x = jnp.bitwise_xor(ih, jh)
    lvl = sum((x >= (1 << p)).astype(jnp.int32) for p in range(half.bit_length() - 1))
    return tri, lvl


def _postmix1_kernel(o_ref, og_ref, x_ref, mod_ref, g_ref, gn_ref, wo_hbm, w1_hbm, w2_hbm,
                     out_ref, *scratch):
    weights = _TailWeights(1, wo_hbm, w1_hbm, w2_hbm, *scratch)
    _first_step_streams_weights(
        functools.partial(_postmix1_body, o_ref, og_ref, x_ref, mod_ref, g_ref, gn_ref, out_ref,
                          weights), weights)


def _postmix1_body(o_ref, og_ref, x_ref, mod_ref, g_ref, gn_ref, out_ref, weights, streaming):
    gn = gn_ref[...]
    ys = []
    for h in range(GLA_HEADS):
        cols = slice(h * GLA_DV, (h + 1) * GLA_DV)
        o = _rms(o_ref[:, cols], gn)
        og = og_ref[:, cols]
        ys.append((o * (og / (1.0 + jnp.exp(-og)))).astype(BF16))
    if streaming:
        weights.fetch_out_proj()
    mix = None
    for h, y in enumerate(ys):
        part = jnp.dot(y, weights.wo_s[h * GLA_DV:(h + 1) * GLA_DV, :], preferred_element_type=F32)
        mix = part if mix is None else mix + part
    gt1 = _mod_chunk(mod_ref, 2)
    x1 = x_ref[...] + gt1 * _rms(mix, g_ref[1:2, :])
    out_ref[...] = _mlp_tail(x1, mod_ref, g_ref, weights, streaming)


def _postmix1(grp, o, og, x2d, mod, norm_g, gla_norm_g, w_out, w1, w2, tm):
    hbm = pl.BlockSpec(memory_space=pl.ANY)
    return pl.pallas_call(
        _postmix1_kernel,
        grid=(grp.t // tm,),
        in_specs=[_row_spec(tm, GLA_VD), _row_spec(tm, GLA_VD),
                  _row_spec(tm, D_MODEL), grp.mod_spec(tm), _const_spec((4, D_MODEL)),
                  _const_spec((1, GLA_DV)), hbm, hbm, hbm],
        out_specs=_row_spec(tm, D_MODEL),
        out_shape=jax.ShapeDtypeStruct((grp.t, D_MODEL), F32),
        scratch_shapes=_tail_weight_scratch(),
        compiler_params=_cparams("arbitrary"),
        name="postmix1_mlp",
    )(o, og, x2d, mod, norm_g, gla_norm_g, w_out, w1, w2)


def _run_group(grp, x, mods, p, rope_tabs, k_ctx, v_ctx, s0f, s0b, tm):
    x2d = x.reshape(grp.t, D_MODEL)
    zc, q, kv, *cache_t = _premix0(grp, x2d, mods[0], p["norm_g"][0], p["ab_w_in"], rope_tabs, tm)
    if k_ctx is None:
        attn = _ctx_attention(grp, q, kv, p["sink"])
    else:
        attn = _lat_attention(grp, q, kv, k_ctx, v_ctx, p["sink"])
    x2d = _postmix0(grp, zc, attn, x2d, mods[0], p["norm_g"][0], p["conv_w"], p["ab_w_out"],
                    p["mlp_w1"], p["mlp_w2"], tm)
    gq, gk, gv, og, gf, gb = _premix1(grp, x2d, mods[1], p["norm_g"][1], p["gla_w_in"],
                                      p["gla_w_gate"], p["gla_gate_bias"], tm)
    o, sf, sb = _gla_scan(grp, gq, gk, gv, gf, gb, s0f, s0b, p["tri"], p["lvl"])
    x2d = _postmix1(grp, o, og, x2d, mods[1], p["norm_g"][1], p["gla_norm_g"], p["gla_w_out"],
                    p["mlp_w1"], p["mlp_w2"], tm)
    return x2d.reshape(x.shape), cache_t, sf, sb


def kernel(x_prompt, x_sample, cache_k, cache_v, state_fwd, state_bwd, c, c_ctx, mod_w, mod_b,
           norm_g, ab_w_in, conv_w, attn_sink, ab_w_out, gla_w_in, gla_gate_w, gla_gate_b,
           gla_norm_g, gla_w_out, mlp_w1, mlp_w2):
    b_ctx, n_ctx, _ = x_prompt.shape
    b_lat, n_lat, _ = x_sample.shape
    assert mod_w.shape[0] == 2 and ab_w_in.shape[0] == 1 and gla_w_in.shape[0] == 1
    assert 1 + b_lat <= 8

    cond8 = jnp.zeros((8, D_MODEL), F32).at[0].set(c_ctx).at[1:1 + b_lat].set(c)
    mod = _modulation(cond8, mod_w, mod_b)
    mods_ctx = [mod[l, 0:1].reshape(1, 1, -1) for l in range(2)]
    mods_lat = [mod[l, 1:1 + b_lat].reshape(b_lat, 1, -1) for l in range(2)]

    w_gate = jnp.zeros((2 * GLA_RANK, 2 * GLA_QK), F32)
    w_gate = w_gate.at[:GLA_RANK, :GLA_QK].set(gla_gate_w[0, 0])
    w_gate = w_gate.at[GLA_RANK:, GLA_QK:].set(gla_gate_w[0, 1])
    tri, lvl = _gla_constants()
    p = {
        "norm_g": norm_g,
        "ab_w_in": ab_w_in,
        "conv_w": conv_w[0],
        "sink": attn_sink[0],
        "ab_w_out": ab_w_out,
        "mlp_w1": mlp_w1,
        "mlp_w2": mlp_w2,
        "gla_w_in": gla_w_in,
        "gla_w_gate": w_gate.astype(BF16),
        "gla_gate_bias": gla_gate_b[0].reshape(1, 2 * GLA_QK),
        "gla_norm_g": gla_norm_g[0].reshape(1, GLA_DV),
        "gla_w_out": gla_w_out,
        "tri": tri,
        "lvl": lvl,
    }

    ctx = _Group(b_ctx, n_ctx, per_seq_mod=False)
    lat = _Group(b_lat, n_lat, per_seq_mod=True)

    y_prompt, (k_t, v_t), sf, sb = _run_group(ctx, x_prompt, mods_ctx, p, None, None, None,
                                              None, None, tm=512)

    def cache_layout(t):
        t = t.reshape(b_ctx, 1, N_KV_HEADS, HEAD_DIM, n_ctx)
        return jnp.transpose(t, (0, 1, 4, 2, 3))

    new_k, new_v = cache_layout(k_t), cache_layout(v_t)
    new_sf, new_sb = sf[:, None], sb[:, None]

    past = cache_k.shape[2]
    k_ctx = cache_k[:, 0].reshape(b_lat, past, KD)
    v_ctx = cache_v[:, 0].reshape(b_lat, past, KD)
    y_sample, _, _, _ = _run_group(lat, x_sample, mods_lat, p, _rope_tables(n_lat), k_ctx, v_ctx,
                                   state_fwd[:, 0], state_bwd[:, 0], tm=512)
    return (y_prompt, y_sample, new_k, new_v, new_sf, new_sb)
```

```python
import functools

import jax
import jax.numpy as jnp
from jax import lax
from jax.experimental import pallas as pl
from jax.experimental.pallas import tpu as pltpu

F32 = jnp.float32
BF16 = jnp.bfloat16

D_MODEL = 1024
MOD_CHUNKS = 6
EPS = 1e-6
CONV_DIM = 512
N_Q_HEADS = 8
N_KV_HEADS = 2
GQA_GROUP = 4
HEAD_DIM = 64
WINDOW = 128
ATTN_BLOCK = 128
ATTN_BLOCKS_PER_STEP = 4
GRID_W = 64
ROPE_BASE = 10000.0
QD = N_Q_HEADS * HEAD_DIM
KD = N_KV_HEADS * HEAD_DIM
AB_IN = 3 * CONV_DIM + QD + 2 * KD
GLA_HEADS = 4
GLA_DK = 128
GLA_DV = 256
GLA_RANK = 16
GLA_GATE_NORM = 16.0
GLA_TILE = 256
GLA_HEADS_PER_STEP = 2
LOG2E = 1.4426950408889634
GLA_QK = GLA_HEADS * GLA_DK
GLA_VD = GLA_HEADS * GLA_DV
GLA_MAIN = 2 * GLA_QK + 2 * GLA_VD
D_FF = 4 * D_MODEL
MLP_CHUNK = 512
MLP_CHUNKS = D_FF // MLP_CHUNK
NEG_INF = -1e30
LANES = 128
VMEM_LIMIT = 56 * 1024 * 1024

NT_DIMS = (((1,), (1,)), ((), ()))
TN_DIMS = (((0,), (0,)), ((), ()))


def _cparams(*sem):
    return pltpu.CompilerParams(dimension_semantics=sem, vmem_limit_bytes=VMEM_LIMIT)


def _bdot(a, b):
    return jnp.dot(a.astype(BF16), b.astype(BF16), preferred_element_type=F32)


def _bdot_nt(a, b):
    return lax.dot_general(a.astype(BF16), b.astype(BF16), NT_DIMS, preferred_element_type=F32)


def _rms(x, g):
    ms = jnp.mean(x * x, axis=-1, keepdims=True)
    return x * lax.rsqrt(ms + EPS) * g


def _mod_chunk(mod_ref, i):
    return mod_ref[:, i * D_MODEL:(i + 1) * D_MODEL]


def _const_spec(shape):
    return pl.BlockSpec(shape, lambda *_: (0,) * len(shape))


def _f32_weight_spec(shape):
    assert shape[0] == 1
    return pl.BlockSpec((None,) + tuple(shape[1:]), lambda *_: (0, 0, 0),
                        pipeline_mode=pl.Buffered(1))


def _cast_once(w_ref, wb_ref):
    @pl.when(pl.program_id(0) == 0)
    def _():
        wb_ref[...] = w_ref[...].astype(BF16)


def _mod_kernel(cond_ref, w_ref, b_ref, o_ref):
    cnd = cond_ref[...]
    s = cnd / (1.0 + jnp.exp(-cnd))
    o_ref[...] = _bdot(s, w_ref[...]) + b_ref[...]


def _modulation(cond8, mod_w, mod_b):
    depth = mod_w.shape[0]
    n = mod_w.shape[2]
    tn = 1536
    return pl.pallas_call(
        _mod_kernel,
        grid=(depth, n // tn),
        in_specs=[
            pl.BlockSpec((8, D_MODEL), lambda l, j: (0, 0)),
            pl.BlockSpec((None, D_MODEL, tn), lambda l, j: (l, 0, j)),
            pl.BlockSpec((None, 1, tn), lambda l, j: (l, 0, j)),
        ],
        out_specs=pl.BlockSpec((None, 8, tn), lambda l, j: (l, 0, j)),
        out_shape=jax.ShapeDtypeStruct((depth, 8, n), F32),
        compiler_params=_cparams("arbitrary", "arbitrary"),
        name="modulation",
    )(cond8, mod_w, mod_b.reshape(depth, 1, n))


class _Group:
    def __init__(self, b, n, per_seq_mod):
        self.b, self.n, self.t = b, n, b * n
        self.per_seq_mod = per_seq_mod

    def mod_spec(self, tm):
        if self.per_seq_mod:
            assert self.n % tm == 0
            per = self.n // tm
            return pl.BlockSpec((None, 1, MOD_CHUNKS * D_MODEL), lambda t: (t // per, 0, 0))
        return pl.BlockSpec((None, 1, MOD_CHUNKS * D_MODEL), lambda t: (0, 0, 0))


def _row_spec(tm, width):
    return pl.BlockSpec((tm, width), lambda t: (t, 0))


def _rope(x, cos, sin_lo, sin_hi):
    return (x * cos + pltpu.roll(x, LANES - 16, axis=1) * sin_lo
            + pltpu.roll(x, 16, axis=1) * sin_hi)


def _premix0_kernel(*refs, rope):
    if rope:
        (x_ref, mod_ref, g_ref, w_ref, cos_ref, slo_ref, shi_ref,
         zc_ref, q_ref, kv_ref, wb_ref) = refs
    else:
        x_ref, mod_ref, g_ref, w_ref, zc_ref, q_ref, kv_ref, kt_ref, vt_ref, wb_ref = refs
    _cast_once(w_ref, wb_ref)
    sh1, sc1 = _mod_chunk(mod_ref, 0), _mod_chunk(mod_ref, 1)
    h = _rms(x_ref[...], g_ref[0:1, :]) * (1.0 + sc1) + sh1
    z = _bdot(h, wb_ref[...])
    c3 = 3 * CONV_DIM
    zc_ref[...] = z[:, :c3]
    scale = HEAD_DIM ** -0.5 * LOG2E
    if rope:
        cos, slo, shi = cos_ref[...], slo_ref[...], shi_ref[...]
        for j in range(QD // LANES):
            qs = z[:, c3 + j * LANES:c3 + (j + 1) * LANES]
            q_ref[:, j * LANES:(j + 1) * LANES] = (_rope(qs, cos, slo, shi) * scale).astype(BF16)
        kv_ref[:, :KD] = _rope(z[:, c3 + QD:c3 + QD + KD], cos, slo, shi)
    else:
        q_ref[...] = (z[:, c3:c3 + QD] * scale).astype(BF16)
        kv_ref[:, :KD] = z[:, c3 + QD:c3 + QD + KD]
        n = kt_ref.shape[2]
        for j in range(kt_ref.shape[0]):
            kt_ref[j] = z[j * n:(j + 1) * n, c3 + QD:c3 + QD + KD].T
            vt_ref[j] = z[j * n:(j + 1) * n, c3 + QD + KD:].T
    kv_ref[:, KD:] = z[:, c3 + QD + KD:]


def _premix0(grp, x2d, mod, norm_g, w_in, rope_tabs, tm):
    rope = rope_tabs is not None
    in_specs = [_row_spec(tm, D_MODEL), grp.mod_spec(tm), _const_spec((4, D_MODEL)),
                _f32_weight_spec(w_in.shape)]
    args = [x2d, mod, norm_g, w_in]
    if rope:
        per = grp.n // tm
        in_specs += [pl.BlockSpec((tm, LANES), lambda t: (t % per, 0))] * 3
        args += list(rope_tabs)
    out_specs = [_row_spec(tm, 3 * CONV_DIM), _row_spec(tm, QD), _row_spec(tm, 2 * KD)]
    out_shape = [jax.ShapeDtypeStruct((grp.t, 3 * CONV_DIM), F32),
                 jax.ShapeDtypeStruct((grp.t, QD), BF16),
                 jax.ShapeDtypeStruct((grp.t, 2 * KD), F32)]
    if not rope:
        assert tm % grp.n == 0
        out_specs += [pl.BlockSpec((tm // grp.n, KD, grp.n), lambda t: (t, 0, 0))] * 2
        out_shape += [jax.ShapeDtypeStruct((grp.b, KD, grp.n), F32)] * 2
    return pl.pallas_call(
        functools.partial(_premix0_kernel, rope=rope),
        grid=(grp.t // tm,),
        in_specs=in_specs,
        out_specs=out_specs,
        out_shape=out_shape,
        scratch_shapes=[pltpu.VMEM(w_in.shape[1:], BF16)],
        compiler_params=_cparams("arbitrary"),
        name="premix0_rope" if rope else "premix0",
    )(*args)


def _rope_tables(n):
    rows = n // GRID_W
    pos_r = jnp.repeat(jnp.arange(rows), GRID_W)
    pos_c = jnp.tile(jnp.arange(GRID_W), rows)
    half = HEAD_DIM // 2
    quarter = half // 2
    inv = ROPE_BASE ** (-(jnp.arange(quarter, dtype=F32) * 2.0 / half))

    def cs(pos):
        ang = pos.astype(F32)[:, None] * inv[None, :]
        return jnp.cos(ang), jnp.sin(ang)

    cr, sr = cs(pos_r)
    cc, sc = cs(pos_c)
    zero = jnp.zeros_like(sr)
    cos = jnp.concatenate([cr, cr, cc, cc], axis=1)
    sin_lo = jnp.concatenate([-sr, zero, -sc, zero], axis=1)
    sin_hi = jnp.concatenate([zero, sr, zero, sc], axis=1)
    rep = LANES // HEAD_DIM
    return tuple(jnp.tile(t, (1, rep)) for t in (cos, sin_lo, sin_hi))


def _attention_operands(k_all, v_all):
    assert KD == LANES == 2 * HEAD_DIM and GQA_GROUP == 4
    lane = lax.broadcasted_iota(jnp.int32, (1, LANES), 1)
    sub = lax.broadcasted_iota(jnp.int32, (LANES, 1), 0)
    v_t = v_all.T
    k_swapped = pltpu.roll(k_all, HEAD_DIM, axis=1)
    ops = []
    for g in range(N_KV_HEADS):
        k_low, k_high = (k_all, k_swapped) if g == 0 else (k_swapped, k_all)
        kz_even = jnp.where(lane < HEAD_DIM, k_low, 0.0).astype(BF16)
        kz_odd = jnp.where(lane >= HEAD_DIM, k_high, 0.0).astype(BF16)
        own = (sub < HEAD_DIM) if g == 0 else (sub >= HEAD_DIM)
        v_ext_t = jnp.where(own, v_t, 1.0).astype(BF16)
        ops.append((kz_even, kz_odd, v_ext_t))
    return ops


def _concat_operands(parts):
    return [(jnp.concatenate([p[g][0] for p in parts], axis=0),
             jnp.concatenate([p[g][1] for p in parts], axis=0),
             jnp.concatenate([p[g][2] for p in parts], axis=1)) for g in range(N_KV_HEADS)]


def _sink_attention(sink_ref, q_ref, operands, bias_t, o_ref):
    m = q_ref.shape[0]
    o_t = {}
    for g in range(N_KV_HEADS):
        kz_even, kz_odd, v_ext_t = operands[g]
        qq = jnp.concatenate([q_ref[:, (2 * g) * LANES:(2 * g + 1) * LANES],
                              q_ref[:, (2 * g + 1) * LANES:(2 * g + 2) * LANES]], axis=0)
        s = jnp.concatenate([_bdot_nt(kz_even, qq), _bdot_nt(kz_odd, qq)], axis=1)
        heads = [4 * g, 4 * g + 2, 4 * g + 1, 4 * g + 3]
        if bias_t is not None:
            nb = bias_t.shape[0]
            s = jnp.concatenate([s[:nb] + jnp.concatenate([bias_t] * GQA_GROUP, axis=1), s[nb:]],
                                axis=0)
        sink = jnp.concatenate([jnp.full((1, m), sink_ref[h] * LOG2E, F32) for h in heads], axis=1)
        mx = jnp.maximum(jnp.max(s, axis=0, keepdims=True), sink)
        p = jnp.exp2(s - mx).astype(BF16)
        oe = jnp.dot(v_ext_t, p, preferred_element_type=F32)
        other = (1 - g) * HEAD_DIM
        den = oe[other:other + 1] + jnp.exp2(sink - mx)
        o_g = oe[g * HEAD_DIM:(g + 1) * HEAD_DIM] / den
        for i, h in enumerate(heads):
            o_t[h] = o_g[:, i * m:(i + 1) * m]
    for j in range(N_Q_HEADS // 2):
        pair_t = jnp.concatenate([o_t[2 * j], o_t[2 * j + 1]], axis=0)
        o_ref[:, j * LANES:(j + 1) * LANES] = pair_t.T.astype(BF16)


def _ctx_attn_kernel(sink_ref, q_ref, kv_ref, o_ref):
    operands = _attention_operands(kv_ref[:, :KD], kv_ref[:, KD:])
    _sink_attention(sink_ref, q_ref, operands, None, o_ref)


def _ctx_attention(grp, q, kv, sink):
    n = grp.n
    return pl.pallas_call(
        _ctx_attn_kernel,
        grid=(grp.b,),
        in_specs=[pl.BlockSpec(memory_space=pltpu.SMEM), _row_spec(n, QD), _row_spec(n, 2 * KD)],
        out_specs=_row_spec(n, QD),
        out_shape=jax.ShapeDtypeStruct((grp.t, QD), BF16),
        compiler_params=_cparams("arbitrary"),
        name="ctx_attention",
    )(sink, q, kv)


def _lat_attn_kernel(sink_ref, q_ref, kvp_ref, kvc_ref, kvn_ref, kc_ref, vc_ref, o_ref, *, n):
    band = jnp.concatenate([kvp_ref[...], kvc_ref[...], kvn_ref[...]], axis=0)
    kj = lax.broadcasted_iota(jnp.int32, (3 * ATTN_BLOCK, ATTN_BLOCK), 0)
    qi = lax.broadcasted_iota(jnp.int32, (3 * ATTN_BLOCK, ATTN_BLOCK), 1)
    rel = kj - ATTN_BLOCK - qi
    ctx_ops = _attention_operands(kc_ref[...], vc_ref[...])
    band_ops = [_attention_operands(band[i * ATTN_BLOCK:(i + 1) * ATTN_BLOCK, :KD],
                                    band[i * ATTN_BLOCK:(i + 1) * ATTN_BLOCK, KD:])
                for i in range(ATTN_BLOCKS_PER_STEP + 2)]
    for j in range(ATTN_BLOCKS_PER_STEP):
        blk = pl.program_id(1) * ATTN_BLOCKS_PER_STEP + j
        kpos = (blk - 1) * ATTN_BLOCK + kj
        valid = (jnp.abs(rel) <= WINDOW) & (kpos >= 0) & (kpos < n)
        bias_t = jnp.where(valid, 0.0, NEG_INF)
        operands = _concat_operands(band_ops[j:j + 3] + [ctx_ops])
        rows = pl.ds(j * ATTN_BLOCK, ATTN_BLOCK)
        _sink_attention(sink_ref, q_ref.at[rows], operands, bias_t, o_ref.at[rows])


def _lat_attention(grp, q, kv, k_ctx, v_ctx, sink):
    nb = grp.n // ATTN_BLOCK
    per = ATTN_BLOCKS_PER_STEP
    steps = nb // per
    past = k_ctx.shape[1]

    def edge(off):
        return pl.BlockSpec((ATTN_BLOCK, 2 * KD),
                            lambda b, i: (b * nb + jnp.clip(i * per + off, 0, nb - 1), 0))

    ctx_spec = pl.BlockSpec((None, past, KD), lambda b, i: (b, 0, 0))
    return pl.pallas_call(
        functools.partial(_lat_attn_kernel, n=grp.n),
        grid=(grp.b, steps),
        in_specs=[pl.BlockSpec(memory_space=pltpu.SMEM),
                  pl.BlockSpec((per * ATTN_BLOCK, QD), lambda b, i: (b * steps + i, 0)),
                  edge(-1),
                  pl.BlockSpec((per * ATTN_BLOCK, 2 * KD), lambda b, i: (b * steps + i, 0)),
                  edge(per), ctx_spec, ctx_spec],
        out_specs=pl.BlockSpec((per * ATTN_BLOCK, QD), lambda b, i: (b * steps + i, 0)),
        out_shape=jax.ShapeDtypeStruct((grp.t, QD), BF16),
        compiler_params=_cparams("arbitrary", "arbitrary"),
        name="lat_attention",
    )(sink, q, kv, kv, kv, k_ctx, v_ctx)


class _TailWeights:
    def __init__(self, layer, wo_hbm, w1_hbm, w2_hbm, wo_s, w1_s, w2_s, stage1, stage2, sem):
        self.layer = layer
        self.wo_hbm, self.w1_hbm, self.w2_hbm = wo_hbm, w1_hbm, w2_hbm
        self.wo_s, self.w1_s, self.w2_s = wo_s, w1_s, w2_s
        self.stage1, self.stage2, self.sem = stage1, stage2, sem

    def _w1_copy(self, j):
        src = self.w1_hbm.at[self.layer, :, pl.ds(j * MLP_CHUNK, MLP_CHUNK)]
        return pltpu.make_async_copy(src, self.stage1.at[j % 2], self.sem.at[0, j % 2])

    def _w2_copy(self, j):
        src = self.w2_hbm.at[self.layer, pl.ds(j * MLP_CHUNK, MLP_CHUNK), :]
        return pltpu.make_async_copy(src, self.stage2.at[j % 2], self.sem.at[1, j % 2])

    def _wo_copy(self, i):
        src = self.wo_hbm.at[0, pl.ds(i * MLP_CHUNK, MLP_CHUNK), :]
        return pltpu.make_async_copy(src, self.stage2.at[i], self.sem.at[1, i])

    def start(self):
        self._wo_copy(0).start()
        self._wo_copy(1).start()
        self._w1_copy(0).start()
        self._w1_copy(1).start()

    def fetch_out_proj(self):
        for i in range(2):
            self._wo_copy(i).wait()
            self.wo_s[i * MLP_CHUNK:(i + 1) * MLP_CHUNK, :] = self.stage2[i].astype(BF16)
            self._w2_copy(i).start()

    def fetch_mlp_chunk(self, j):
        self._w1_copy(j).wait()
        self.w1_s[j] = self.stage1[j % 2].astype(BF16)
        if j + 2 < MLP_CHUNKS:
            self._w1_copy(j + 2).start()
        self._w2_copy(j).wait()
        self.w2_s[j] = self.stage2[j % 2].astype(BF16)
        if j + 2 < MLP_CHUNKS:
            self._w2_copy(j + 2).start()


def _tail_weight_scratch():
    assert D_MODEL == 2 * MLP_CHUNK
    return [pltpu.VMEM((D_MODEL, D_MODEL), BF16),
            pltpu.VMEM((MLP_CHUNKS, D_MODEL, MLP_CHUNK), BF16),
            pltpu.VMEM((MLP_CHUNKS, MLP_CHUNK, D_MODEL), BF16),
            pltpu.VMEM((2, D_MODEL, MLP_CHUNK), F32),
            pltpu.VMEM((2, MLP_CHUNK, D_MODEL), F32),
            pltpu.SemaphoreType.DMA((2, 2))]


def _first_step_streams_weights(body, weights):
    first = pl.program_id(0) == 0

    @pl.when(first)
    def _():
        weights.start()
        body(True)

    @pl.when(jnp.logical_not(first))
    def _():
        body(False)


def _mlp_tail(x, mod_ref, g_ref, weights, streaming):
    sh2, sc2, gt2 = _mod_chunk(mod_ref, 3), _mod_chunk(mod_ref, 4), _mod_chunk(mod_ref, 5)
    hb = (_rms(x, g_ref[2:3, :]) * (1.0 + sc2) + sh2).astype(BF16)
    acc = None
    for j in range(MLP_CHUNKS):
        if streaming:
            weights.fetch_mlp_chunk(j)
        a = jnp.dot(hb, weights.w1_s[j], preferred_element_type=F32)
        a = jnp.maximum(a, 0.0)
        part = jnp.dot((a * a).astype(BF16), weights.w2_s[j], preferred_element_type=F32)
        acc = part if acc is None else acc + part
    return x + gt2 * _rms(acc, g_ref[3:4, :])


def _postmix0_kernel(zc_ref, zp_ref, zn_ref, at_ref, x_ref, mod_ref, g_ref, cw_ref,
                     wo_hbm, w1_hbm, w2_hbm, o_ref, *scratch, n, tm):
    weights = _TailWeights(0, wo_hbm, w1_hbm, w2_hbm, *scratch)
    _first_step_streams_weights(
        functools.partial(_postmix0_body, zc_ref, zp_ref, zn_ref, at_ref, x_ref, mod_ref, g_ref,
                          cw_ref, o_ref, weights, n, tm), weights)


def _postmix0_body(zc_ref, zp_ref, zn_ref, at_ref, x_ref, mod_ref, g_ref, cw_ref, o_ref,
                   weights, n, tm, streaming):
    c = CONV_DIM
    zc = zc_ref[...]
    u = zc[:, c:2 * c] * zc[:, 2 * c:]
    u_before = zp_ref[7:8, c:2 * c] * zp_ref[7:8, 2 * c:]
    u_after = zn_ref[0:1, c:2 * c] * zn_ref[0:1, 2 * c:]
    row = lax.broadcasted_iota(jnp.int32, (tm, 1), 0)
    pos = (pl.program_id(0) * tm + row) % n
    u_prev = jnp.where(row == 0, u_before, pltpu.roll(u, 1, axis=0))
    u_prev = jnp.where(pos == 0, 0.0, u_prev)
    u_next = jnp.where(row == tm - 1, u_after, pltpu.roll(u, tm - 1, axis=0))
    u_next = jnp.where(pos == n - 1, 0.0, u_next)
    conv = u_prev * cw_ref[0:1, :] + u * cw_ref[1:2, :] + u_next * cw_ref[2:3, :]
    if streaming:
        weights.fetch_out_proj()
    mix = (_bdot(zc[:, :c] * conv, weights.wo_s[:c, :])
           + jnp.dot(at_ref[...], weights.wo_s[c:, :], preferred_element_type=F32))
    gt1 = _mod_chunk(mod_ref, 2)
    x1 = x_ref[...] + gt1 * _rms(mix, g_ref[1:2, :])
    o_ref[...] = _mlp_tail(x1, mod_ref, g_ref, weights, streaming)


def _postmix0(grp, zc, attn, x2d, mod, norm_g, conv_w, w_out, w1, w2, tm):
    r8 = tm // 8
    last8 = grp.t // 8 - 1
    hbm = pl.BlockSpec(memory_space=pl.ANY)
    return pl.pallas_call(
        functools.partial(_postmix0_kernel, n=grp.n, tm=tm),
        grid=(grp.t // tm,),
        in_specs=[_row_spec(tm, 3 * CONV_DIM),
                  pl.BlockSpec((8, 3 * CONV_DIM), lambda t: (jnp.maximum(t * r8 - 1, 0), 0)),
                  pl.BlockSpec((8, 3 * CONV_DIM), lambda t: (jnp.minimum((t + 1) * r8, last8), 0)),
                  _row_spec(tm, QD), _row_spec(tm, D_MODEL), grp.mod_spec(tm),
                  _const_spec((4, D_MODEL)), _const_spec((3, CONV_DIM)), hbm, hbm, hbm],
        out_specs=_row_spec(tm, D_MODEL),
        out_shape=jax.ShapeDtypeStruct((grp.t, D_MODEL), F32),
        scratch_shapes=_tail_weight_scratch(),
        compiler_params=_cparams("arbitrary"),
        name="postmix0_mlp",
    )(zc, zc, zc, attn, x2d, mod, norm_g, conv_w, w_out, w1, w2)


def _premix1_kernel(x_ref, mod_ref, g_ref, w_ref, wg_ref, gb_ref,
                    q_ref, k_ref, v_ref, og_ref, gf_ref, gbk_ref, wb_ref, wr_ref):
    @pl.when(pl.program_id(0) == 0)
    def _():
        for c in range(0, GLA_MAIN, GLA_QK):
            wb_ref[:, c:c + GLA_QK] = w_ref[c:c + GLA_QK, :].T.astype(BF16)
        wr_ref[...] = w_ref[GLA_MAIN:, :].astype(BF16)

    sh1, sc1 = _mod_chunk(mod_ref, 0), _mod_chunk(mod_ref, 1)
    hb = (_rms(x_ref[...], g_ref[0:1, :]) * (1.0 + sc1) + sh1).astype(BF16)
    z = jnp.dot(hb, wb_ref[...], preferred_element_type=F32)
    q_ref[...] = z[:, :GLA_QK] * (GLA_DK ** -0.5)
    k_ref[...] = z[:, GLA_QK:2 * GLA_QK]
    v_ref[...] = z[:, 2 * GLA_QK:2 * GLA_QK + GLA_VD].astype(BF16)
    og_ref[...] = z[:, 2 * GLA_QK + GLA_VD:]
    r = _bdot_nt(hb, wr_ref[...])
    pre = _bdot(r, wg_ref[...]) + gb_ref[...]
    gate = ((jnp.minimum(pre, 0.0) - jnp.log1p(jnp.exp(-jnp.abs(pre))))
            * (LOG2E / GLA_GATE_NORM))
    gf_ref[...] = gate[:, :GLA_QK]
    gbk_ref[...] = gate[:, GLA_QK:]


def _premix1(grp, x2d, mod, norm_g, w_in_t, w_gate, gate_bias, tm):
    t = grp.t
    w_in = w_in_t
    assert w_in.shape == (1, GLA_MAIN + 2 * GLA_RANK, D_MODEL)
    return pl.pallas_call(
        _premix1_kernel,
        grid=(t // tm,),
        in_specs=[_row_spec(tm, D_MODEL), grp.mod_spec(tm), _const_spec((4, D_MODEL)),
                  _f32_weight_spec(w_in.shape), _const_spec((2 * GLA_RANK, 2 * GLA_QK)),
                  _const_spec((1, 2 * GLA_QK))],
        out_specs=[_row_spec(tm, GLA_QK), _row_spec(tm, GLA_QK), _row_spec(tm, GLA_VD),
                   _row_spec(tm, GLA_VD), _row_spec(tm, GLA_QK), _row_spec(tm, GLA_QK)],
        out_shape=[jax.ShapeDtypeStruct((t, GLA_QK), F32), jax.ShapeDtypeStruct((t, GLA_QK), F32),
                   jax.ShapeDtypeStruct((t, GLA_VD), BF16),
                   jax.ShapeDtypeStruct((t, GLA_VD), F32), jax.ShapeDtypeStruct((t, GLA_QK), F32),
                   jax.ShapeDtypeStruct((t, GLA_QK), F32)],
        scratch_shapes=[pltpu.VMEM((D_MODEL, GLA_MAIN), BF16),
                        pltpu.VMEM((2 * GLA_RANK, D_MODEL), BF16)],
        compiler_params=_cparams("arbitrary"),
        name="premix1",
    )(x2d, mod, norm_g, w_in, w_gate, gate_bias)


def _split3(x):
    hi = x.astype(BF16)
    r1 = x - hi.astype(F32)
    mid = r1.astype(BF16)
    lo = (r1 - mid.astype(F32)).astype(BF16)
    return hi, mid, lo


def _level_exponent(b, s, reverse):
    idx = s if reverse else s - 1
    if s >= 8:
        n = GLA_TILE // (2 * s)
        b4 = b.reshape(n, 2, s, GLA_DK)
        first, second = b4[:, 0:1], b4[:, 1:2]
        r = (second[:, :, 0:1] if reverse else first[:, :, s - 1:s])
        parts = [first - r, r - second] if reverse else [r - first, second - r]
        return jnp.concatenate(parts, axis=1).reshape(GLA_TILE, GLA_DK)
    b8 = b.reshape(GLA_TILE // 8, 8, GLA_DK)
    sub = lax.broadcasted_iota(jnp.int32, (1, 8, 1), 1)
    if s == 4:
        r = b8[:, idx:idx + 1, :]
    else:
        assert s == 2
        r = jnp.where(sub < 4, b8[:, idx:idx + 1, :], b8[:, 4 + idx:5 + idx, :])
    in_second = (sub // s) % 2 == 1
    sign = jnp.where(in_second != reverse, 1.0, -1.0)
    return ((b8 - r) * sign).reshape(GLA_TILE, GLA_DK)


def _pair_matrix(q, k, g_f, g_b, b_f, b_b, lvl):
    half = GLA_TILE // 2
    lo, hi = slice(0, half), slice(half, GLA_TILE)
    qb, kb = q.astype(BF16), k.astype(BF16)
    row = lax.broadcasted_iota(jnp.int32, (GLA_TILE, 1), 0)
    odd = row % 2 == 1
    own = jnp.sum(q * k, axis=-1, keepdims=True)
    k_adj = jnp.where(odd, pltpu.roll(k, 1, axis=0), pltpu.roll(k, GLA_TILE - 1, axis=0))
    adj = jnp.sum(q * jnp.exp2(jnp.where(odd, g_f, g_b)) * k_adj, axis=-1, keepdims=True)
    blocks = [jnp.where(lvl == 0, 2.0 * own[rows], jnp.where(lvl == 1, adj[rows], 0.0))
              for rows in (lo, hi)]
    cross_f = cross_b = None
    s, level = 2, 2
    while s < GLA_TILE:
        if s >= 16:
            n = GLA_TILE // (2 * s)

            def halves(x):
                x4 = x.reshape(n, 2, s, GLA_DK)
                return x4[:, 0:1], x4[:, 1:2]

            def rows(first, second):
                return jnp.concatenate([first, second], axis=1).reshape(GLA_TILE, GLA_DK)

            (bf1, bf2), (bb1, bb2) = halves(b_f), halves(b_b)
            (q1, q2), (k1, k2) = halves(qb), halves(kb)
            r_f, r_b = bf1[:, :, s - 1:s], bb2[:, :, 0:1]
            zero = jnp.zeros_like(q1)
            lhs = jnp.concatenate(
                [rows(zero, q2 * jnp.exp2(bf2 - r_f).astype(BF16)),
                 rows(q1 * jnp.exp2(bb1 - r_b).astype(BF16), zero)], axis=1)
            rhs = jnp.concatenate(
                [rows(k1 * jnp.exp2(r_f - bf1).astype(BF16), zero),
                 rows(zero, k2 * jnp.exp2(r_b - bb2).astype(BF16))], axis=1)
        else:
            second = (row // s) % 2 == 1
            f_f = jnp.exp2(_level_exponent(b_f, s, False)).astype(BF16)
            f_b = jnp.exp2(_level_exponent(b_b, s, True)).astype(BF16)
            u = qb * jnp.where(second, f_f, f_b)
            w = kb * jnp.where(second, f_b, f_f)
            zero = jnp.zeros_like(u)
            lhs = jnp.concatenate([jnp.where(second, u, zero), jnp.where(second, zero, u)], axis=1)
            rhs = jnp.concatenate([jnp.where(second, zero, w), jnp.where(second, w, zero)], axis=1)
        pairs = _bdot_nt(lhs, rhs)
        if s == half:
            cross_f, cross_b = pairs[hi, lo], pairs[lo, hi]
        else:
            blocks = [jnp.where(lvl == level, pairs[lo, lo], blocks[0]),
                      jnp.where(lvl == level, pairs[hi, hi], blocks[1])]
        s, level = 2 * s, level + 1
    top = jnp.concatenate([blocks[0].astype(BF16), cross_b.astype(BF16)], axis=1)
    bottom = jnp.concatenate([cross_f.astype(BF16), blocks[1].astype(BF16)], axis=1)
    return jnp.concatenate([top, bottom], axis=0)


def _carry_state(q, k, b, v, s_ref, reverse):
    edge = 0 if reverse else GLA_TILE - 1
    b_last = b[edge:edge + 1, :]
    qe = (q * jnp.exp2(b)).astype(BF16)
    ke = (k * jnp.exp2(b_last - b)).astype(BF16)
    st = s_ref[...]
    s_ref[...] = st * jnp.exp2(b_last) + lax.dot_general(
        v, ke, TN_DIMS, preferred_element_type=F32)
    return _bdot_nt(qe, st)


def _gla_kernel(*refs, zero_init, nt):
    if zero_init:
        (tri_ref, lvl_ref, q_ref, k_ref, gf_ref, gb_ref, v_ref,
         o_ref, sfo_ref, sbo_ref, sf_ref, sb_ref, bb_ref) = refs
        sf_ref[...] = jnp.zeros_like(sf_ref)
        sb_ref[...] = jnp.zeros_like(sb_ref)
    else:
        (tri_ref, lvl_ref, q_ref, k_ref, gf_ref, gb_ref, v_ref, s0f_ref, s0b_ref,
         o_ref, sfo_ref, sbo_ref, sf_ref, sb_ref, bb_ref) = refs
        for hh in range(GLA_HEADS_PER_STEP):
            sf_ref[hh] = s0f_ref[hh].T
            sb_ref[hh] = s0b_ref[hh].T

    def tile_rows(tile):
        r0 = tile * GLA_TILE
        return pl.ds(r0 if isinstance(r0, int) else pl.multiple_of(r0, GLA_TILE), GLA_TILE)

    heads = [(slice(hh * GLA_DK, (hh + 1) * GLA_DK), slice(hh * GLA_DV, (hh + 1) * GLA_DV), hh)
             for hh in range(GLA_HEADS_PER_STEP)]

    def forward_sweep(t, carry):
        rows = tile_rows(t)
        g_f, g_b = gf_ref[rows, :], gb_ref[rows, :]
        c = None
        for part in _split3(jnp.concatenate([g_f, g_b], axis=1)):
            term = jnp.dot(tri_ref[...], part, preferred_element_type=F32)
            c = term if c is None else c + term
        width = GLA_HEADS_PER_STEP * GLA_DK
        c_b = c[:, width:]
        b_f = c[:, :width]
        b_b = (c_b[GLA_TILE - 1:GLA_TILE, :] - c_b) + g_b
        bb_ref[t] = b_b
        for dk, dv, hh in heads:
            q, k, v = q_ref[rows, dk], k_ref[rows, dk], v_ref[rows, dv]
            att = _pair_matrix(q, k, g_f[:, dk], g_b[:, dk], b_f[:, dk], b_b[:, dk], lvl_ref[...])
            o_ref[rows, dv] = (jnp.dot(att, v, preferred_element_type=F32)
                               + _carry_state(q, k, b_f[:, dk], v, sf_ref.at[hh], False))
        return carry

    def backward_sweep(i, carry):
        t = nt - 1 - i
        rows = tile_rows(t)
        b_b = bb_ref[t]
        for dk, dv, hh in heads:
            o_ref[rows, dv] += _carry_state(q_ref[rows, dk], k_ref[rows, dk], b_b[:, dk],
                                            v_ref[rows, dv], sb_ref.at[hh], True)
        return carry

    if nt == 1:
        forward_sweep(0, 0)
        backward_sweep(0, 0)
    else:
        lax.fori_loop(0, nt, forward_sweep, 0)
        lax.fori_loop(0, nt, backward_sweep, 0)
    for hh in range(GLA_HEADS_PER_STEP):
        sfo_ref[hh] = sf_ref[hh].T
        sbo_ref[hh] = sb_ref[hh].T


def _gla_scan(grp, q, k, v, gf, gb, s0f, s0b, tri, lvl):
    n = grp.n
    nt = n // GLA_TILE
    hp = GLA_HEADS_PER_STEP
    zero_init = s0f is None
    half = GLA_TILE // 2
    seq_dk = pl.BlockSpec((n, hp * GLA_DK), lambda b, h: (b, h))
    seq_dv = pl.BlockSpec((n, hp * GLA_DV), lambda b, h: (b, h))
    state_spec = pl.BlockSpec((None, hp, GLA_DK, GLA_DV), lambda b, h: (b, h, 0, 0))
    in_specs = [_const_spec((GLA_TILE, GLA_TILE)), _const_spec((half, half)),
                seq_dk, seq_dk, seq_dk, seq_dk, seq_dv]
    args = [tri, lvl, q, k, gf, gb, v]
    if not zero_init:
        in_specs += [state_spec, state_spec]
        args += [s0f, s0b]
    state_shape = jax.ShapeDtypeStruct((grp.b, GLA_HEADS, GLA_DK, GLA_DV), F32)
    return pl.pallas_call(
        functools.partial(_gla_kernel, zero_init=zero_init, nt=nt),
        grid=(grp.b, GLA_HEADS // hp),
        in_specs=in_specs,
        out_specs=[seq_dv, state_spec, state_spec],
        out_shape=[jax.ShapeDtypeStruct((grp.t, GLA_VD), F32), state_shape, state_shape],
        scratch_shapes=[pltpu.VMEM((hp, GLA_DV, GLA_DK), F32), pltpu.VMEM((hp, GLA_DV, GLA_DK), F32),
                        pltpu.VMEM((nt, GLA_TILE, hp * GLA_DK), F32)],
        compiler_params=_cparams("arbitrary", "arbitrary"),
        name="gla_scan",
    )(*args)


def _gla_constants():
    half = GLA_TILE // 2
    i = jnp.arange(GLA_TILE)[:, None]
    j = jnp.arange(GLA_TILE)[None, :]
    tri = (j <= i).astype(BF16)
    ih, jh = i[:half], j[:, :half]
    x = jnp.bitwise_xor(ih, jh)
    lvl = sum((x >= (1 << p)).astype(jnp.int32) for p in range(half.bit_length() - 1))
    return tri, lvl


def _postmix1_kernel(o_ref, og_ref, x_ref, mod_ref, g_ref, gn_ref, wo_hbm, w1_hbm, w2_hbm,
                     out_ref, *scratch):
    weights = _TailWeights(1, wo_hbm, w1_hbm, w2_hbm, *scratch)
    _first_step_streams_weights(
        functools.partial(_postmix1_body, o_ref, og_ref, x_ref, mod_ref, g_ref, gn_ref, out_ref,
                          weights), weights)


def _postmix1_body(o_ref, og_ref, x_ref, mod_ref, g_ref, gn_ref, out_ref, weights, streaming):
    gn = gn_ref[...]
    ys = []
    for h in range(GLA_HEADS):
        cols = slice(h * GLA_DV, (h + 1) * GLA_DV)
        o = _rms(o_ref[:, cols], gn)
        og = og_ref[:, cols]
        ys.append((o * (og / (1.0 + jnp.exp(-og)))).astype(BF16))
    if streaming:
        weights.fetch_out_proj()
    mix = None
    for h, y in enumerate(ys):
        part = jnp.dot(y, weights.wo_s[h * GLA_DV:(h + 1) * GLA_DV, :], preferred_element_type=F32)
        mix = part if mix is None else mix + part
    gt1 = _mod_chunk(mod_ref, 2)
    x1 = x_ref[...] + gt1 * _rms(mix, g_ref[1:2, :])
    out_ref[...] = _mlp_tail(x1, mod_ref, g_ref, weights, streaming)


def _postmix1(grp, o, og, x2d, mod, norm_g, gla_norm_g, w_out, w1, w2, tm):
    hbm = pl.BlockSpec(memory_space=pl.ANY)
    return pl.pallas_call(
        _postmix1_kernel,
        grid=(grp.t // tm,),
        in_specs=[_row_spec(tm, GLA_VD), _row_spec(tm, GLA_VD),
                  _row_spec(tm, D_MODEL), grp.mod_spec(tm), _const_spec((4, D_MODEL)),
                  _const_spec((1, GLA_DV)), hbm, hbm, hbm],
        out_specs=_row_spec(tm, D_MODEL),
        out_shape=jax.ShapeDtypeStruct((grp.t, D_MODEL), F32),
        scratch_shapes=_tail_weight_scratch(),
        compiler_params=_cparams("arbitrary"),
        name="postmix1_mlp",
    )(o, og, x2d, mod, norm_g, gla_norm_g, w_out, w1, w2)


def _run_group(grp, x, mods, p, rope_tabs, k_ctx, v_ctx, s0f, s0b, tm):
    x2d = x.reshape(grp.t, D_MODEL)
    zc, q, kv, *cache_t = _premix0(grp, x2d, mods[0], p["norm_g"][0], p["ab_w_in"], rope_tabs, tm)
    if k_ctx is None:
        attn = _ctx_attention(grp, q, kv, p["sink"])
    else:
        attn = _lat_attention(grp, q, kv, k_ctx, v_ctx, p["sink"])
    x2d = _postmix0(grp, zc, attn, x2d, mods[0], p["norm_g"][0], p["conv_w"], p["ab_w_out"],
                    p["mlp_w1"], p["mlp_w2"], tm)
    gq, gk, gv, og, gf, gb = _premix1(grp, x2d, mods[1], p["norm_g"][1], p["gla_w_in"],
                                      p["gla_w_gate"], p["gla_gate_bias"], tm)
    o, sf, sb = _gla_scan(grp, gq, gk, gv, gf, gb, s0f, s0b, p["tri"], p["lvl"])
    x2d = _postmix1(grp, o, og, x2d, mods[1], p["norm_g"][1], p["gla_norm_g"], p["gla_w_out"],
                    p["mlp_w1"], p["mlp_w2"], tm)
    return x2d.reshape(x.shape), cache_t, sf, sb


def kernel(x_prompt, x_sample, cache_k, cache_v, state_fwd, state_bwd, c, c_ctx, mod_w, mod_b,
           norm_g, ab_w_in, conv_w, attn_sink, ab_w_out, gla_w_in, gla_gate_w, gla_gate_b,
           gla_norm_g, gla_w_out, mlp_w1, mlp_w2):
    b_ctx, n_ctx, _ = x_prompt.shape
    b_lat, n_lat, _ = x_sample.shape
    assert mod_w.shape[0] == 2 and ab_w_in.shape[0] == 1 and gla_w_in.shape[0] == 1
    assert 1 + b_lat <= 8

    cond8 = jnp.zeros((8, D_MODEL), F32).at[0].set(c_ctx).at[1:1 + b_lat].set(c)
    mod = _modulation(cond8, mod_w, mod_b)
    mods_ctx = [mod[l, 0:1].reshape(1, 1, -1) for l in range(2)]
    mods_lat = [mod[l, 1:1 + b_lat].reshape(b_lat, 1, -1) for l in range(2)]

    w_gate = jnp.zeros((2 * GLA_RANK, 2 * GLA_QK), F32)
    w_gate = w_gate.at[:GLA_RANK, :GLA_QK].set(gla_gate_w[0, 0])
    w_gate = w_gate.at[GLA_RANK:, GLA_QK:].set(gla_gate_w[0, 1])
    tri, lvl = _gla_constants()
    p = {
        "norm_g": norm_g,
        "ab_w_in": ab_w_in,
        "conv_w": conv_w[0],
        "sink": attn_sink[0],
        "ab_w_out": ab_w_out,
        "mlp_w1": mlp_w1,
        "mlp_w2": mlp_w2,
        "gla_w_in": jnp.swapaxes(gla_w_in, 1, 2),
        "gla_w_gate": w_gate.astype(BF16),
        "gla_gate_bias": gla_gate_b[0].reshape(1, 2 * GLA_QK),
        "gla_norm_g": gla_norm_g[0].reshape(1, GLA_DV),
        "gla_w_out": gla_w_out,
        "tri": tri,
        "lvl": lvl,
    }

    ctx = _Group(b_ctx, n_ctx, per_seq_mod=False)
    lat = _Group(b_lat, n_lat, per_seq_mod=True)

    y_prompt, (k_t, v_t), sf, sb = _run_group(ctx, x_prompt, mods_ctx, p, None, None, None,
                                              None, None, tm=512)

    def cache_layout(t):
        t = t.reshape(b_ctx, 1, N_KV_HEADS, HEAD_DIM, n_ctx)
        return jnp.transpose(t, (0, 1, 4, 2, 3))

    new_k, new_v = cache_layout(k_t), cache_layout(v_t)
    new_sf, new_sb = sf[:, None], sb[:, None]

    past = cache_k.shape[2]
    k_ctx = cache_k[:, 0].reshape(b_lat, past, KD)
    v_ctx = cache_v[:, 0].reshape(b_lat, past, KD)
    y_sample, _, _, _ = _run_group(lat, x_sample, mods_lat, p, _rope_tables(n_lat), k_ctx, v_ctx,
                                   state_fwd[:, 0], state_bwd[:, 0], tm=512)
    return (y_prompt, y_sample, new_k, new_v, new_sf, new_sb)
```

```python
import functools

import jax
import jax.numpy as jnp
from jax import lax
from jax.experimental import pallas as pl
from jax.experimental.pallas import tpu as pltpu

F32 = jnp.float32
BF16 = jnp.bfloat16

D_MODEL = 1024
MOD_CHUNKS = 6
EPS = 1e-6
CONV_DIM = 512
N_Q_HEADS = 8
N_KV_HEADS = 2
GQA_GROUP = 4
HEAD_DIM = 64
WINDOW = 128
ATTN_BLOCK = 128
ATTN_BLOCKS_PER_STEP = 4
GRID_W = 64
ROPE_BASE = 10000.0
QD = N_Q_HEADS * HEAD_DIM
KD = N_KV_HEADS * HEAD_DIM
AB_IN = 3 * CONV_DIM + QD + 2 * KD
GLA_HEADS = 4
GLA_DK = 128
GLA_DV = 256
GLA_RANK = 16
GLA_GATE_NORM = 16.0
GLA_TILE = 256
GLA_HEADS_PER_STEP = 2
LOG2E = 1.4426950408889634
GLA_QK = GLA_HEADS * GLA_DK
GLA_VD = GLA_HEADS * GLA_DV
GLA_MAIN = 2 * GLA_QK + 2 * GLA_VD
D_FF = 4 * D_MODEL
MLP_CHUNK = 512
MLP_CHUNKS = D_FF // MLP_CHUNK
WEIGHT_PIECE = 256
NEG_INF = -1e30
LANES = 128
VMEM_LIMIT = 60 * 1024 * 1024

NT_DIMS = (((1,), (1,)), ((), ()))
TN_DIMS = (((0,), (0,)), ((), ()))


def _cparams(*sem):
    return pltpu.CompilerParams(dimension_semantics=sem, vmem_limit_bytes=VMEM_LIMIT)


def _bdot(a, b):
    return jnp.dot(a.astype(BF16), b.astype(BF16), preferred_element_type=F32)


def _bdot_nt(a, b):
    return lax.dot_general(a.astype(BF16), b.astype(BF16), NT_DIMS, preferred_element_type=F32)


def _rms(x, g):
    ms = jnp.mean(x * x, axis=-1, keepdims=True)
    return x * lax.rsqrt(ms + EPS) * g


def _mod_chunk(mod_ref, i):
    return mod_ref[:, i * D_MODEL:(i + 1) * D_MODEL]


def _const_spec(shape):
    return pl.BlockSpec(shape, lambda *_: (0,) * len(shape))


def _f32_weight_spec(shape):
    assert shape[0] == 1
    return pl.BlockSpec((None,) + tuple(shape[1:]), lambda *_: (0, 0, 0),
                        pipeline_mode=pl.Buffered(1))


def _cast_once(w_ref, wb_ref):
    @pl.when(pl.program_id(0) == 0)
    def _():
        wb_ref[...] = w_ref[...].astype(BF16)


def _mod_kernel(cond_ref, w_ref, b_ref, o_ref):
    cnd = cond_ref[...]
    s = cnd / (1.0 + jnp.exp(-cnd))
    o_ref[...] = _bdot(s, w_ref[...]) + b_ref[...]


def _modulation(cond8, mod_w, mod_b):
    depth = mod_w.shape[0]
    n = mod_w.shape[2]
    tn = 1536
    return pl.pallas_call(
        _mod_kernel,
        grid=(depth, n // tn),
        in_specs=[
            pl.BlockSpec((8, D_MODEL), lambda l, j: (0, 0)),
            pl.BlockSpec((None, D_MODEL, tn), lambda l, j: (l, 0, j)),
            pl.BlockSpec((None, 1, tn), lambda l, j: (l, 0, j)),
        ],
        out_specs=pl.BlockSpec((None, 8, tn), lambda l, j: (l, 0, j)),
        out_shape=jax.ShapeDtypeStruct((depth, 8, n), F32),
        compiler_params=_cparams("arbitrary", "arbitrary"),
        name="modulation",
    )(cond8, mod_w, mod_b.reshape(depth, 1, n))


class _Group:
    def __init__(self, b, n):
        self.b, self.n, self.t = b, n, b * n


class _Stream:
    def __init__(self, ctx, lat, tm):
        assert ctx.t % tm == 0 and lat.n % tm == 0 and tm % ctx.n == 0
        self.ctx, self.lat, self.tm = ctx, lat, tm
        self.t = ctx.t + lat.t
        self.ctx_tiles = ctx.t // tm
        self.tiles = self.t // tm

    def is_lat(self):
        return pl.program_id(0) >= self.ctx_tiles

    def spec(self, width):
        return pl.BlockSpec((self.tm, width), lambda t: (t, 0))

    def ctx_spec(self, width):
        last = self.ctx_tiles - 1
        return pl.BlockSpec((self.tm, width), lambda t: (jnp.minimum(t, last), 0))

    def lat_spec(self, width):
        first = self.ctx_tiles
        return pl.BlockSpec((self.tm, width), lambda t: (jnp.maximum(t - first, 0), 0))

    def mod_spec(self):
        first, per = self.ctx_tiles, self.lat.n // self.tm
        return pl.BlockSpec((None, 1, MOD_CHUNKS * D_MODEL),
                            lambda t: (jnp.where(t < first, 0, 1 + (t - first) // per), 0, 0))

    def seq_len(self):
        return jnp.where(self.is_lat(), self.lat.n, self.ctx.n)


def _row_spec(tm, width):
    return pl.BlockSpec((tm, width), lambda t: (t, 0))


def _rope(x, cos, sin_lo, sin_hi):
    return (x * cos + pltpu.roll(x, LANES - 16, axis=1) * sin_lo
            + pltpu.roll(x, 16, axis=1) * sin_hi)


def _premix0_kernel(xc_ref, xl_ref, mod_ref, g_ref, w_ref, cos_ref, slo_ref, shi_ref,
                    zc_ref, q_ref, kv_ref, kt_ref, vt_ref, wb_ref, *, stream):
    _cast_once(w_ref, wb_ref)
    is_lat = stream.is_lat()
    x = jnp.where(is_lat, xl_ref[...], xc_ref[...])
    sh1, sc1 = _mod_chunk(mod_ref, 0), _mod_chunk(mod_ref, 1)
    h = _rms(x, g_ref[0:1, :]) * (1.0 + sc1) + sh1
    z = _bdot(h, wb_ref[...])
    c3 = 3 * CONV_DIM
    zc_ref[...] = z[:, :c3]
    scale = HEAD_DIM ** -0.5 * LOG2E
    cos, slo, shi = cos_ref[...], slo_ref[...], shi_ref[...]
    for j in range(QD // LANES):
        qs = z[:, c3 + j * LANES:c3 + (j + 1) * LANES]
        q_ref[:, j * LANES:(j + 1) * LANES] = (_rope(qs, cos, slo, shi) * scale).astype(BF16)
    kv_ref[:, :KD] = _rope(z[:, c3 + QD:c3 + QD + KD], cos, slo, shi)
    kv_ref[:, KD:] = z[:, c3 + QD + KD:]

    @pl.when(jnp.logical_not(is_lat))
    def _():
        n = kt_ref.shape[2]
        for j in range(kt_ref.shape[0]):
            kt_ref[j] = z[j * n:(j + 1) * n, c3 + QD:c3 + QD + KD].T
            vt_ref[j] = z[j * n:(j + 1) * n, c3 + QD + KD:].T


def _premix0(stream, x_ctx, x_lat, mod, norm_g, w_in, rope_tabs):
    tm, ctx, lat = stream.tm, stream.ctx, stream.lat
    first, per, last = stream.ctx_tiles, lat.n // tm, stream.ctx_tiles - 1
    rope_spec = pl.BlockSpec((tm, LANES),
                             lambda t: (jnp.where(t < first, 0, 1 + (t - first) % per), 0))
    cache_spec = pl.BlockSpec((tm // ctx.n, KD, ctx.n), lambda t: (jnp.minimum(t, last), 0, 0))
    return pl.pallas_call(
        functools.partial(_premix0_kernel, stream=stream),
        grid=(stream.tiles,),
        in_specs=[stream.ctx_spec(D_MODEL), stream.lat_spec(D_MODEL), stream.mod_spec(),
                  _const_spec((4, D_MODEL)), _f32_weight_spec(w_in.shape)] + [rope_spec] * 3,
        out_specs=[stream.spec(3 * CONV_DIM), stream.spec(QD), stream.spec(2 * KD),
                   cache_spec, cache_spec],
        out_shape=[jax.ShapeDtypeStruct((stream.t, 3 * CONV_DIM), F32),
                   jax.ShapeDtypeStruct((stream.t, QD), BF16),
                   jax.ShapeDtypeStruct((stream.t, 2 * KD), F32),
                   jax.ShapeDtypeStruct((ctx.b, KD, ctx.n), F32),
                   jax.ShapeDtypeStruct((ctx.b, KD, ctx.n), F32)],
        scratch_shapes=[pltpu.VMEM(w_in.shape[1:], BF16)],
        compiler_params=_cparams("arbitrary"),
        name="premix0",
    )(x_ctx, x_lat, mod, norm_g, w_in, *rope_tabs)


def _rope_tables(n, identity_rows):
    rows = n // GRID_W
    pos_r = jnp.repeat(jnp.arange(rows), GRID_W)
    pos_c = jnp.tile(jnp.arange(GRID_W), rows)
    half = HEAD_DIM // 2
    quarter = half // 2
    inv = ROPE_BASE ** (-(jnp.arange(quarter, dtype=F32) * 2.0 / half))

    def cs(pos):
        ang = pos.astype(F32)[:, None] * inv[None, :]
        return jnp.cos(ang), jnp.sin(ang)

    cr, sr = cs(pos_r)
    cc, sc = cs(pos_c)
    zero = jnp.zeros_like(sr)
    cos = jnp.concatenate([cr, cr, cc, cc], axis=1)
    sin_lo = jnp.concatenate([-sr, zero, -sc, zero], axis=1)
    sin_hi = jnp.concatenate([zero, sr, zero, sc], axis=1)
    rep = LANES // HEAD_DIM
    tables = []
    for t, ident in ((cos, 1.0), (sin_lo, 0.0), (sin_hi, 0.0)):
        head = jnp.full((identity_rows, LANES), ident, F32)
        tables.append(jnp.concatenate([head, jnp.tile(t, (1, rep))], axis=0))
    return tuple(tables)


def _attention_operands(k_all, v_all):
    assert KD == LANES == 2 * HEAD_DIM and GQA_GROUP == 4
    lane = lax.broadcasted_iota(jnp.int32, (1, LANES), 1)
    sub = lax.broadcasted_iota(jnp.int32, (LANES, 1), 0)
    v_t = v_all.T
    k_swapped = pltpu.roll(k_all, HEAD_DIM, axis=1)
    ops = []
    for g in range(N_KV_HEADS):
        k_low, k_high = (k_all, k_swapped) if g == 0 else (k_swapped, k_all)
        kz_even = jnp.where(lane < HEAD_DIM, k_low, 0.0).astype(BF16)
        kz_odd = jnp.where(lane >= HEAD_DIM, k_high, 0.0).astype(BF16)
        own = (sub < HEAD_DIM) if g == 0 else (sub >= HEAD_DIM)
        v_ext_t = jnp.where(own, v_t, 1.0).astype(BF16)
        ops.append((kz_even, kz_odd, v_ext_t))
    return ops


def _concat_operands(parts):
    return [(jnp.concatenate([p[g][0] for p in parts], axis=0),
             jnp.concatenate([p[g][1] for p in parts], axis=0),
             jnp.concatenate([p[g][2] for p in parts], axis=1)) for g in range(N_KV_HEADS)]


def _sink_attention(sink_ref, q_ref, operands, bias_t, o_ref):
    m = q_ref.shape[0]
    o_t = {}
    for g in range(N_KV_HEADS):
        kz_even, kz_odd, v_ext_t = operands[g]
        qq = jnp.concatenate([q_ref[:, (2 * g) * LANES:(2 * g + 1) * LANES],
                              q_ref[:, (2 * g + 1) * LANES:(2 * g + 2) * LANES]], axis=0)
        s = jnp.concatenate([_bdot_nt(kz_even, qq), _bdot_nt(kz_odd, qq)], axis=1)
        heads = [4 * g, 4 * g + 2, 4 * g + 1, 4 * g + 3]
        if bias_t is not None:
            nb = bias_t.shape[0]
            s = jnp.concatenate([s[:nb] + jnp.concatenate([bias_t] * GQA_GROUP, axis=1), s[nb:]],
                                axis=0)
        sink = jnp.concatenate([jnp.full((1, m), sink_ref[h] * LOG2E, F32) for h in heads], axis=1)
        mx = jnp.maximum(jnp.max(s, axis=0, keepdims=True), sink)
        p = jnp.exp2(s - mx).astype(BF16)
        oe = jnp.dot(v_ext_t, p, preferred_element_type=F32)
        other = (1 - g) * HEAD_DIM
        den = oe[other:other + 1] + jnp.exp2(sink - mx)
        o_g = oe[g * HEAD_DIM:(g + 1) * HEAD_DIM] / den
        for i, h in enumerate(heads):
            o_t[h] = o_g[:, i * m:(i + 1) * m]
    for j in range(N_Q_HEADS // 2):
        pair_t = jnp.concatenate([o_t[2 * j], o_t[2 * j + 1]], axis=0)
        o_ref[:, j * LANES:(j + 1) * LANES] = pair_t.T.astype(BF16)


def _ctx_attn_kernel(sink_ref, q_ref, kv_ref, o_ref):
    operands = _attention_operands(kv_ref[:, :KD], kv_ref[:, KD:])
    _sink_attention(sink_ref, q_ref, operands, None, o_ref)


def _ctx_attention(grp, q, kv, sink):
    n = grp.n
    return pl.pallas_call(
        _ctx_attn_kernel,
        grid=(grp.b,),
        in_specs=[pl.BlockSpec(memory_space=pltpu.SMEM), _row_spec(n, QD), _row_spec(n, 2 * KD)],
        out_specs=_row_spec(n, QD),
        out_shape=jax.ShapeDtypeStruct((grp.t, QD), BF16),
        compiler_params=_cparams("arbitrary"),
        name="ctx_attention",
    )(sink, q, kv)


def _lat_attn_kernel(sink_ref, q_ref, kvp_ref, kvc_ref, kvn_ref, kc_ref, vc_ref, o_ref, *, n):
    band = jnp.concatenate([kvp_ref[...], kvc_ref[...], kvn_ref[...]], axis=0)
    kj = lax.broadcasted_iota(jnp.int32, (3 * ATTN_BLOCK, ATTN_BLOCK), 0)
    qi = lax.broadcasted_iota(jnp.int32, (3 * ATTN_BLOCK, ATTN_BLOCK), 1)
    rel = kj - ATTN_BLOCK - qi
    ctx_ops = _attention_operands(kc_ref[...], vc_ref[...])
    band_ops = [_attention_operands(band[i * ATTN_BLOCK:(i + 1) * ATTN_BLOCK, :KD],
                                    band[i * ATTN_BLOCK:(i + 1) * ATTN_BLOCK, KD:])
                for i in range(ATTN_BLOCKS_PER_STEP + 2)]
    for j in range(ATTN_BLOCKS_PER_STEP):
        blk = pl.program_id(1) * ATTN_BLOCKS_PER_STEP + j
        kpos = (blk - 1) * ATTN_BLOCK + kj
        valid = (jnp.abs(rel) <= WINDOW) & (kpos >= 0) & (kpos < n)
        bias_t = jnp.where(valid, 0.0, NEG_INF)
        operands = _concat_operands(band_ops[j:j + 3] + [ctx_ops])
        rows = pl.ds(j * ATTN_BLOCK, ATTN_BLOCK)
        _sink_attention(sink_ref, q_ref.at[rows], operands, bias_t, o_ref.at[rows])


def _lat_attention(grp, row0, q, kv, k_ctx, v_ctx, sink):
    nb = grp.n // ATTN_BLOCK
    per = ATTN_BLOCKS_PER_STEP
    steps = nb // per
    past = k_ctx.shape[1]
    assert row0 % (per * ATTN_BLOCK) == 0
    blk0, step0 = row0 // ATTN_BLOCK, row0 // (per * ATTN_BLOCK)

    def edge(off):
        return pl.BlockSpec((ATTN_BLOCK, 2 * KD),
                            lambda b, i: (blk0 + b * nb + jnp.clip(i * per + off, 0, nb - 1), 0))

    ctx_spec = pl.BlockSpec((None, past, KD), lambda b, i: (b, 0, 0))
    return pl.pallas_call(
        functools.partial(_lat_attn_kernel, n=grp.n),
        grid=(grp.b, steps),
        in_specs=[pl.BlockSpec(memory_space=pltpu.SMEM),
                  pl.BlockSpec((per * ATTN_BLOCK, QD), lambda b, i: (step0 + b * steps + i, 0)),
                  edge(-1),
                  pl.BlockSpec((per * ATTN_BLOCK, 2 * KD), lambda b, i: (step0 + b * steps + i, 0)),
                  edge(per), ctx_spec, ctx_spec],
        out_specs=pl.BlockSpec((per * ATTN_BLOCK, QD), lambda b, i: (b * steps + i, 0)),
        out_shape=jax.ShapeDtypeStruct((grp.t, QD), BF16),
        compiler_params=_cparams("arbitrary", "arbitrary"),
        name="lat_attention",
    )(sink, q, kv, kv, kv, k_ctx, v_ctx)


class _TailWeights:
    def __init__(self, layer, wo_hbm, w1_hbm, w2_hbm, wo_s, w1_s, w2_s, stage1, stage2, sem):
        self.layer = layer
        self.wo_hbm, self.w1_hbm, self.w2_hbm = wo_hbm, w1_hbm, w2_hbm
        self.wo_s, self.w1_s, self.w2_s = wo_s, w1_s, w2_s
        self.stage1, self.stage2, self.sem = stage1, stage2, sem

    def _w1_copy(self, p):
        src = self.w1_hbm.at[self.layer, :, pl.ds(p * WEIGHT_PIECE, WEIGHT_PIECE)]
        return pltpu.make_async_copy(src, self.stage1.at[p % 2], self.sem.at[0, p % 2])

    def _w2_copy(self, p):
        src = self.w2_hbm.at[self.layer, pl.ds(p * WEIGHT_PIECE, WEIGHT_PIECE), :]
        return pltpu.make_async_copy(src, self.stage2.at[p % 2], self.sem.at[1, p % 2])

    def _wo_copy(self, p):
        src = self.wo_hbm.at[0, pl.ds(p * WEIGHT_PIECE, WEIGHT_PIECE), :]
        return pltpu.make_async_copy(src, self.stage2.at[p % 2], self.sem.at[1, p % 2])

    def start(self):
        self._wo_copy(0).start()
        self._wo_copy(1).start()
        self._w1_copy(0).start()
        self._w1_copy(1).start()

    def fetch_out_proj(self):
        pieces = D_MODEL // WEIGHT_PIECE
        for p in range(pieces):
            self._wo_copy(p).wait()
            self.wo_s[p * WEIGHT_PIECE:(p + 1) * WEIGHT_PIECE, :] = self.stage2[p % 2].astype(BF16)
            if p + 2 < pieces:
                self._wo_copy(p + 2).start()
            else:
                self._w2_copy(p + 2 - pieces).start()

    def fetch_mlp_chunk(self, j):
        per = MLP_CHUNK // WEIGHT_PIECE
        total = MLP_CHUNKS * per
        for p in range(j * per, (j + 1) * per):
            part = slice((p % per) * WEIGHT_PIECE, (p % per + 1) * WEIGHT_PIECE)
            self._w1_copy(p).wait()
            self.w1_s[j, :, part] = self.stage1[p % 2].astype(BF16)
            if p + 2 < total:
                self._w1_copy(p + 2).start()
            self._w2_copy(p).wait()
            self.w2_s[j, part, :] = self.stage2[p % 2].astype(BF16)
            if p + 2 < total:
                self._w2_copy(p + 2).start()


def _tail_weight_scratch():
    assert MLP_CHUNK % WEIGHT_PIECE == 0 and D_MODEL // WEIGHT_PIECE >= 2
    return [pltpu.VMEM((D_MODEL, D_MODEL), BF16),
            pltpu.VMEM((MLP_CHUNKS, D_MODEL, MLP_CHUNK), BF16),
            pltpu.VMEM((MLP_CHUNKS, MLP_CHUNK, D_MODEL), BF16),
            pltpu.VMEM((2, D_MODEL, WEIGHT_PIECE), F32),
            pltpu.VMEM((2, WEIGHT_PIECE, D_MODEL), F32),
            pltpu.SemaphoreType.DMA((2, 2))]


def _first_step_streams_weights(body, weights):
    first = pl.program_id(0) == 0

    @pl.when(first)
    def _():
        weights.start()
        body(True)

    @pl.when(jnp.logical_not(first))
    def _():
        body(False)


def _mlp_tail(x, mod_ref, g_ref, weights, streaming):
    sh2, sc2, gt2 = _mod_chunk(mod_ref, 3), _mod_chunk(mod_ref, 4), _mod_chunk(mod_ref, 5)
    hb = (_rms(x, g_ref[2:3, :]) * (1.0 + sc2) + sh2).astype(BF16)
    acc = None
    for j in range(MLP_CHUNKS):
        if streaming:
            weights.fetch_mlp_chunk(j)
        a = jnp.dot(hb, weights.w1_s[j], preferred_element_type=F32)
        a = jnp.maximum(a, 0.0)
        part = jnp.dot((a * a).astype(BF16), weights.w2_s[j], preferred_element_type=F32)
        acc = part if acc is None else acc + part
    return x + gt2 * _rms(acc, g_ref[3:4, :])


def _postmix0_kernel(zc_ref, zp_ref, zn_ref, atc_ref, atl_ref, xc_ref, xl_ref, mod_ref, g_ref,
                     cw_ref, wo_hbm, w1_hbm, w2_hbm, o_ref, *scratch, stream):
    weights = _TailWeights(0, wo_hbm, w1_hbm, w2_hbm, *scratch)
    _first_step_streams_weights(
        functools.partial(_postmix0_body, zc_ref, zp_ref, zn_ref, atc_ref, atl_ref, xc_ref, xl_ref,
                          mod_ref, g_ref, cw_ref, o_ref, weights, stream), weights)


def _postmix0_body(zc_ref, zp_ref, zn_ref, atc_ref, atl_ref, xc_ref, xl_ref, mod_ref, g_ref,
                   cw_ref, o_ref, weights, stream, streaming):
    c, tm = CONV_DIM, stream.tm
    is_lat, n = stream.is_lat(), stream.seq_len()
    zc = zc_ref[...]
    u = zc[:, c:2 * c] * zc[:, 2 * c:]
    u_before = zp_ref[7:8, c:2 * c] * zp_ref[7:8, 2 * c:]
    u_after = zn_ref[0:1, c:2 * c] * zn_ref[0:1, 2 * c:]
    row = lax.broadcasted_iota(jnp.int32, (tm, 1), 0)
    pos = (pl.program_id(0) * tm + row) & (n - 1)
    u_prev = jnp.where(row == 0, u_before, pltpu.roll(u, 1, axis=0))
    u_prev = jnp.where(pos == 0, 0.0, u_prev)
    u_next = jnp.where(row == tm - 1, u_after, pltpu.roll(u, tm - 1, axis=0))
    u_next = jnp.where(pos == n - 1, 0.0, u_next)
    conv = u_prev * cw_ref[0:1, :] + u * cw_ref[1:2, :] + u_next * cw_ref[2:3, :]
    if streaming:
        weights.fetch_out_proj()
    attn = jnp.where(is_lat, atl_ref[...], atc_ref[...])
    mix = (_bdot(zc[:, :c] * conv, weights.wo_s[:c, :])
           + jnp.dot(attn, weights.wo_s[c:, :], preferred_element_type=F32))
    gt1 = _mod_chunk(mod_ref, 2)
    x1 = jnp.where(is_lat, xl_ref[...], xc_ref[...]) + gt1 * _rms(mix, g_ref[1:2, :])
    o_ref[...] = _mlp_tail(x1, mod_ref, g_ref, weights, streaming)


def _postmix0(stream, zc, attn_ctx, attn_lat, x_ctx, x_lat, mod, norm_g, conv_w, w_out, w1, w2):
    tm = stream.tm
    for n in (stream.ctx.n, stream.lat.n):
        assert n & (n - 1) == 0 and stream.ctx.t % n == 0
    r8 = tm // 8
    last8 = stream.t // 8 - 1
    hbm = pl.BlockSpec(memory_space=pl.ANY)
    return pl.pallas_call(
        functools.partial(_postmix0_kernel, stream=stream),
        grid=(stream.tiles,),
        in_specs=[stream.spec(3 * CONV_DIM),
                  pl.BlockSpec((8, 3 * CONV_DIM), lambda t: (jnp.maximum(t * r8 - 1, 0), 0)),
                  pl.BlockSpec((8, 3 * CONV_DIM), lambda t: (jnp.minimum((t + 1) * r8, last8), 0)),
                  stream.ctx_spec(QD), stream.lat_spec(QD),
                  stream.ctx_spec(D_MODEL), stream.lat_spec(D_MODEL), stream.mod_spec(),
                  _const_spec((4, D_MODEL)), _const_spec((3, CONV_DIM)), hbm, hbm, hbm],
        out_specs=stream.spec(D_MODEL),
        out_shape=jax.ShapeDtypeStruct((stream.t, D_MODEL), F32),
        scratch_shapes=_tail_weight_scratch(),
        compiler_params=_cparams("arbitrary"),
        name="postmix0_mlp",
    )(zc, zc, zc, attn_ctx, attn_lat, x_ctx, x_lat, mod, norm_g, conv_w, w_out, w1, w2)


def _premix1_kernel(x_ref, mod_ref, g_ref, w_ref, wg_ref, gb_ref,
                    q_ref, k_ref, v_ref, og_ref, gf_ref, gbk_ref, wb_ref, wr_ref):
    @pl.when(pl.program_id(0) == 0)
    def _():
        for c in range(0, GLA_MAIN, GLA_QK):
            wb_ref[:, c:c + GLA_QK] = w_ref[c:c + GLA_QK, :].T.astype(BF16)
        wr_ref[...] = w_ref[GLA_MAIN:, :].astype(BF16)

    sh1, sc1 = _mod_chunk(mod_ref, 0), _mod_chunk(mod_ref, 1)
    hb = (_rms(x_ref[...], g_ref[0:1, :]) * (1.0 + sc1) + sh1).astype(BF16)
    z = jnp.dot(hb, wb_ref[...], preferred_element_type=F32)
    q_ref[...] = z[:, :GLA_QK] * (GLA_DK ** -0.5)
    k_ref[...] = z[:, GLA_QK:2 * GLA_QK]
    v_ref[...] = z[:, 2 * GLA_QK:2 * GLA_QK + GLA_VD].astype(BF16)
    og_ref[...] = z[:, 2 * GLA_QK + GLA_VD:]
    r = _bdot_nt(hb, wr_ref[...])
    pre = _bdot(r, wg_ref[...]) + gb_ref[...]
    gate = ((jnp.minimum(pre, 0.0) - jnp.log1p(jnp.exp(-jnp.abs(pre))))
            * (LOG2E / GLA_GATE_NORM))
    gf_ref[...] = gate[:, :GLA_QK]
    gbk_ref[...] = gate[:, GLA_QK:]


def _premix1(stream, x2d, mod, norm_g, w_in_t, w_gate, gate_bias):
    t, tm = stream.t, stream.tm
    w_in = w_in_t
    assert w_in.shape == (1, GLA_MAIN + 2 * GLA_RANK, D_MODEL)
    return pl.pallas_call(
        _premix1_kernel,
        grid=(t // tm,),
        in_specs=[_row_spec(tm, D_MODEL), stream.mod_spec(), _const_spec((4, D_MODEL)),
                  _f32_weight_spec(w_in.shape), _const_spec((2 * GLA_RANK, 2 * GLA_QK)),
                  _const_spec((1, 2 * GLA_QK))],
        out_specs=[_row_spec(tm, GLA_QK), _row_spec(tm, GLA_QK), _row_spec(tm, GLA_VD),
                   _row_spec(tm, GLA_VD), _row_spec(tm, GLA_QK), _row_spec(tm, GLA_QK)],
        out_shape=[jax.ShapeDtypeStruct((t, GLA_QK), F32), jax.ShapeDtypeStruct((t, GLA_QK), F32),
                   jax.ShapeDtypeStruct((t, GLA_VD), BF16),
                   jax.ShapeDtypeStruct((t, GLA_VD), F32), jax.ShapeDtypeStruct((t, GLA_QK), F32),
                   jax.ShapeDtypeStruct((t, GLA_QK), F32)],
        scratch_shapes=[pltpu.VMEM((D_MODEL, GLA_MAIN), BF16),
                        pltpu.VMEM((2 * GLA_RANK, D_MODEL), BF16)],
        compiler_params=_cparams("arbitrary"),
        name="premix1",
    )(x2d, mod, norm_g, w_in, w_gate, gate_bias)


def _split3(x):
    hi = x.astype(BF16)
    r1 = x - hi.astype(F32)
    mid = r1.astype(BF16)
    lo = (r1 - mid.astype(F32)).astype(BF16)
    return hi, mid, lo


def _level_exponent(b, s, reverse):
    idx = s if reverse else s - 1
    if s >= 8:
        n = GLA_TILE // (2 * s)
        b4 = b.reshape(n, 2, s, GLA_DK)
        first, second = b4[:, 0:1], b4[:, 1:2]
        r = (second[:, :, 0:1] if reverse else first[:, :, s - 1:s])
        parts = [first - r, r - second] if reverse else [r - first, second - r]
        return jnp.concatenate(parts, axis=1).reshape(GLA_TILE, GLA_DK)
    b8 = b.reshape(GLA_TILE // 8, 8, GLA_DK)
    sub = lax.broadcasted_iota(jnp.int32, (1, 8, 1), 1)
    if s == 4:
        r = b8[:, idx:idx + 1, :]
    else:
        assert s == 2
        r = jnp.where(sub < 4, b8[:, idx:idx + 1, :], b8[:, 4 + idx:5 + idx, :])
    in_second = (sub // s) % 2 == 1
    sign = jnp.where(in_second != reverse, 1.0, -1.0)
    return ((b8 - r) * sign).reshape(GLA_TILE, GLA_DK)


def _pair_matrix(q, k, g_f, g_b, b_f, b_b, lvl):
    half = GLA_TILE // 2
    lo, hi = slice(0, half), slice(half, GLA_TILE)
    qb, kb = q.astype(BF16), k.astype(BF16)
    row = lax.broadcasted_iota(jnp.int32, (GLA_TILE, 1), 0)
    odd = row % 2 == 1
    own = jnp.sum(q * k, axis=-1, keepdims=True)
    k_adj = jnp.where(odd, pltpu.roll(k, 1, axis=0), pltpu.roll(k, GLA_TILE - 1, axis=0))
    adj = jnp.sum(q * jnp.exp2(jnp.where(odd, g_f, g_b)) * k_adj, axis=-1, keepdims=True)
    blocks = [jnp.where(lvl == 0, 2.0 * own[rows], jnp.where(lvl == 1, adj[rows], 0.0))
              for rows in (lo, hi)]
    cross_f = cross_b = None
    s, level = 2, 2
    while s < GLA_TILE:
        if s >= 16:
            n = GLA_TILE // (2 * s)

            def halves(x):
                x4 = x.reshape(n, 2, s, GLA_DK)
                return x4[:, 0:1], x4[:, 1:2]

            def rows(first, second):
                return jnp.concatenate([first, second], axis=1).reshape(GLA_TILE, GLA_DK)

            (bf1, bf2), (bb1, bb2) = halves(b_f), halves(b_b)
            (q1, q2), (k1, k2) = halves(qb), halves(kb)
            r_f, r_b = bf1[:, :, s - 1:s], bb2[:, :, 0:1]
            zero = jnp.zeros_like(q1)
            lhs = jnp.concatenate(
                [rows(zero, q2 * jnp.exp2(bf2 - r_f).astype(BF16)),
                 rows(q1 * jnp.exp2(bb1 - r_b).astype(BF16), zero)], axis=1)
            rhs = jnp.concatenate(
                [rows(k1 * jnp.exp2(r_f - bf1).astype(BF16), zero),
                 rows(zero, k2 * jnp.exp2(r_b - bb2).astype(BF16))], axis=1)
        else:
            second = (row // s) % 2 == 1
            f_f = jnp.exp2(_level_exponent(b_f, s, False)).astype(BF16)
            f_b = jnp.exp2(_level_exponent(b_b, s, True)).astype(BF16)
            u = qb * jnp.where(second, f_f, f_b)
            w = kb * jnp.where(second, f_b, f_f)
            zero = jnp.zeros_like(u)
            lhs = jnp.concatenate([jnp.where(second, u, zero), jnp.where(second, zero, u)], axis=1)
            rhs = jnp.concatenate([jnp.where(second, zero, w), jnp.where(second, w, zero)], axis=1)
        pairs = _bdot_nt(lhs, rhs)
        if s == half:
            cross_f, cross_b = pairs[hi, lo], pairs[lo, hi]
        else:
            blocks = [jnp.where(lvl == level, pairs[lo, lo], blocks[0]),
                      jnp.where(lvl == level, pairs[hi, hi], blocks[1])]
        s, level = 2 * s, level + 1
    top = jnp.concatenate([blocks[0].astype(BF16), cross_b.astype(BF16)], axis=1)
    bottom = jnp.concatenate([cross_f.astype(BF16), blocks[1].astype(BF16)], axis=1)
    return jnp.concatenate([top, bottom], axis=0)


def _carry_state(q, k, b, v, s_ref, reverse):
    edge = 0 if reverse else GLA_TILE - 1
    b_last = b[edge:edge + 1, :]
    qe = (q * jnp.exp2(b)).astype(BF16)
    ke = (k * jnp.exp2(b_last - b)).astype(BF16)
    st = s_ref[...]
    s_ref[...] = st * jnp.exp2(b_last) + lax.dot_general(
        v, ke, TN_DIMS, preferred_element_type=F32)
    return _bdot_nt(qe, st)


def _gla_kernel(*refs, zero_init, nt):
    if zero_init:
        (tri_ref, lvl_ref, q_ref, k_ref, gf_ref, gb_ref, v_ref,
         o_ref, sfo_ref, sbo_ref, sf_ref, sb_ref, bb_ref) = refs
        sf_ref[...] = jnp.zeros_like(sf_ref)
        sb_ref[...] = jnp.zeros_like(sb_ref)
    else:
        (tri_ref, lvl_ref, q_ref, k_ref, gf_ref, gb_ref, v_ref, s0f_ref, s0b_ref,
         o_ref, sfo_ref, sbo_ref, sf_ref, sb_ref, bb_ref) = refs
        for hh in range(GLA_HEADS_PER_STEP):
            sf_ref[hh] = s0f_ref[hh].T
            sb_ref[hh] = s0b_ref[hh].T

    def tile_rows(tile):
        r0 = tile * GLA_TILE
        return pl.ds(r0 if isinstance(r0, int) else pl.multiple_of(r0, GLA_TILE), GLA_TILE)

    heads = [(slice(hh * GLA_DK, (hh + 1) * GLA_DK), slice(hh * GLA_DV, (hh + 1) * GLA_DV), hh)
             for hh in range(GLA_HEADS_PER_STEP)]

    def forward_sweep(t, carry):
        rows = tile_rows(t)
        g_f, g_b = gf_ref[rows, :], gb_ref[rows, :]
        c = None
        for part in _split3(jnp.concatenate([g_f, g_b], axis=1)):
            term = jnp.dot(tri_ref[...], part, preferred_element_type=F32)
            c = term if c is None else c + term
        width = GLA_HEADS_PER_STEP * GLA_DK
        c_b = c[:, width:]
        b_f = c[:, :width]
        b_b = (c_b[GLA_TILE - 1:GLA_TILE, :] - c_b) + g_b
        bb_ref[t] = b_b
        for dk, dv, hh in heads:
            q, k, v = q_ref[rows, dk], k_ref[rows, dk], v_ref[rows, dv]
            att = _pair_matrix(q, k, g_f[:, dk], g_b[:, dk], b_f[:, dk], b_b[:, dk], lvl_ref[...])
            o_ref[rows, dv] = (jnp.dot(att, v, preferred_element_type=F32)
                               + _carry_state(q, k, b_f[:, dk], v, sf_ref.at[hh], False))
        return carry

    def backward_sweep(i, carry):
        t = nt - 1 - i
        rows = tile_rows(t)
        b_b = bb_ref[t]
        for dk, dv, hh in heads:
            o_ref[rows, dv] += _carry_state(q_ref[rows, dk], k_ref[rows, dk], b_b[:, dk],
                                            v_ref[rows, dv], sb_ref.at[hh], True)
        return carry

    if nt == 1:
        forward_sweep(0, 0)
        backward_sweep(0, 0)
    else:
        lax.fori_loop(0, nt, forward_sweep, 0)
        lax.fori_loop(0, nt, backward_sweep, 0)
    for hh in range(GLA_HEADS_PER_STEP):
        sfo_ref[hh] = sf_ref[hh].T
        sbo_ref[hh] = sb_ref[hh].T


def _gla_scan(grp, row0, q, k, v, gf, gb, s0f, s0b, tri, lvl):
    n = grp.n
    nt = n // GLA_TILE
    hp = GLA_HEADS_PER_STEP
    zero_init = s0f is None
    half = GLA_TILE // 2
    assert row0 % n == 0
    seq0 = row0 // n
    in_dk = pl.BlockSpec((n, hp * GLA_DK), lambda b, h: (seq0 + b, h))
    in_dv = pl.BlockSpec((n, hp * GLA_DV), lambda b, h: (seq0 + b, h))
    seq_dv = pl.BlockSpec((n, hp * GLA_DV), lambda b, h: (b, h))
    state_spec = pl.BlockSpec((None, hp, GLA_DK, GLA_DV), lambda b, h: (b, h, 0, 0))
    in_specs = [_const_spec((GLA_TILE, GLA_TILE)), _const_spec((half, half)),
                in_dk, in_dk, in_dk, in_dk, in_dv]
    args = [tri, lvl, q, k, gf, gb, v]
    if not zero_init:
        in_specs += [state_spec, state_spec]
        args += [s0f, s0b]
    state_shape = jax.ShapeDtypeStruct((grp.b, GLA_HEADS, GLA_DK, GLA_DV), F32)
    return pl.pallas_call(
        functools.partial(_gla_kernel, zero_init=zero_init, nt=nt),
        grid=(grp.b, GLA_HEADS // hp),
        in_specs=in_specs,
        out_specs=[seq_dv, state_spec, state_spec],
        out_shape=[jax.ShapeDtypeStruct((grp.t, GLA_VD), F32), state_shape, state_shape],
        scratch_shapes=[pltpu.VMEM((hp, GLA_DV, GLA_DK), F32), pltpu.VMEM((hp, GLA_DV, GLA_DK), F32),
                        pltpu.VMEM((nt, GLA_TILE, hp * GLA_DK), F32)],
        compiler_params=_cparams("arbitrary", "arbitrary"),
        name="gla_scan",
    )(*args)


def _gla_constants():
    half = GLA_TILE // 2
    i = jnp.arange(GLA_TILE)[:, None]
    j = jnp.arange(GLA_TILE)[None, :]
    tri = (j <= i).astype(BF16)
    ih, jh = i[:half], j[:, :half]
    x = jnp.bitwise_xor(ih, jh)
    lvl = sum((x >= (1 << p)).astype(jnp.int32) for p in range(half.bit_length() - 1))
    return tri, lvl


def _postmix1_kernel(oc_ref, ol_ref, og_ref, x_ref, mod_ref, g_ref, gn_ref, wo_hbm, w1_hbm, w2_hbm,
                     yc_ref, yl_ref, *scratch, stream):
    weights = _TailWeights(1, wo_hbm, w1_hbm, w2_hbm, *scratch)
    _first_step_streams_weights(
        functools.partial(_postmix1_body, oc_ref, ol_ref, og_ref, x_ref, mod_ref, g_ref, gn_ref,
                          yc_ref, yl_ref, weights, stream), weights)


def _postmix1_body(oc_ref, ol_ref, og_ref, x_ref, mod_ref, g_ref, gn_ref, yc_ref, yl_ref,
                   weights, stream, streaming):
    is_lat = stream.is_lat()
    gn = gn_ref[...]
    ys = []
    for h in range(GLA_HEADS):
        cols = slice(h * GLA_DV, (h + 1) * GLA_DV)
        o = _rms(jnp.where(is_lat, ol_ref[:, cols], oc_ref[:, cols]), gn)
        og = og_ref[:, cols]
        ys.append((o * (og / (1.0 + jnp.exp(-og)))).astype(BF16))
    if streaming:
        weights.fetch_out_proj()
    mix = None
    for h, y in enumerate(ys):
        part = jnp.dot(y, weights.wo_s[h * GLA_DV:(h + 1) * GLA_DV, :], preferred_element_type=F32)
        mix = part if mix is None else mix + part
    gt1 = _mod_chunk(mod_ref, 2)
    x1 = x_ref[...] + gt1 * _rms(mix, g_ref[1:2, :])
    y = _mlp_tail(x1, mod_ref, g_ref, weights, streaming)

    @pl.when(jnp.logical_not(is_lat))
    def _():
        yc_ref[...] = y

    @pl.when(is_lat)
    def _():
        yl_ref[...] = y


def _postmix1(stream, o_ctx, o_lat, og, x2d, mod, norm_g, gla_norm_g, w_out, w1, w2):
    hbm = pl.BlockSpec(memory_space=pl.ANY)
    return pl.pallas_call(
        functools.partial(_postmix1_kernel, stream=stream),
        grid=(stream.tiles,),
        in_specs=[stream.ctx_spec(GLA_VD), stream.lat_spec(GLA_VD), stream.spec(GLA_VD),
                  stream.spec(D_MODEL), stream.mod_spec(), _const_spec((4, D_MODEL)),
                  _const_spec((1, GLA_DV)), hbm, hbm, hbm],
        out_specs=[stream.ctx_spec(D_MODEL), stream.lat_spec(D_MODEL)],
        out_shape=[jax.ShapeDtypeStruct((stream.ctx.t, D_MODEL), F32),
                   jax.ShapeDtypeStruct((stream.lat.t, D_MODEL), F32)],
        scratch_shapes=_tail_weight_scratch(),
        compiler_params=_cparams("arbitrary"),
        name="postmix1_mlp",
    )(o_ctx, o_lat, og, x2d, mod, norm_g, gla_norm_g, w_out, w1, w2)


TOKEN_TILE = 512


def kernel(x_prompt, x_sample, cache_k, cache_v, state_fwd, state_bwd, c, c_ctx, mod_w, mod_b,
           norm_g, ab_w_in, conv_w, attn_sink, ab_w_out, gla_w_in, gla_gate_w, gla_gate_b,
           gla_norm_g, gla_w_out, mlp_w1, mlp_w2):
    b_ctx, n_ctx, _ = x_prompt.shape
    b_lat, n_lat, _ = x_sample.shape
    assert mod_w.shape[0] == 2 and ab_w_in.shape[0] == 1 and gla_w_in.shape[0] == 1
    assert 1 + b_lat <= 8

    cond8 = jnp.zeros((8, D_MODEL), F32).at[0].set(c_ctx).at[1:1 + b_lat].set(c)
    mod = _modulation(cond8, mod_w, mod_b)
    mods = [mod[l].reshape(8, 1, -1) for l in range(2)]

    w_gate = jnp.zeros((2 * GLA_RANK, 2 * GLA_QK), F32)
    w_gate = w_gate.at[:GLA_RANK, :GLA_QK].set(gla_gate_w[0, 0])
    w_gate = w_gate.at[GLA_RANK:, GLA_QK:].set(gla_gate_w[0, 1])
    tri, lvl = _gla_constants()
    p = {
        "conv_w": conv_w[0],
        "sink": attn_sink[0],
        "gla_w_in": jnp.swapaxes(gla_w_in, 1, 2),
        "gla_w_gate": w_gate.astype(BF16),
        "gla_gate_bias": gla_gate_b[0].reshape(1, 2 * GLA_QK),
        "gla_norm_g": gla_norm_g[0].reshape(1, GLA_DV),
    }

    ctx, lat = _Group(b_ctx, n_ctx), _Group(b_lat, n_lat)
    stream = _Stream(ctx, lat, TOKEN_TILE)
    x_ctx, x_lat = x_prompt.reshape(ctx.t, D_MODEL), x_sample.reshape(lat.t, D_MODEL)
    past = cache_k.shape[2]
    k_ctx = cache_k[:, 0].reshape(b_lat, past, KD)
    v_ctx = cache_v[:, 0].reshape(b_lat, past, KD)

    zc, q, kv, k_t, v_t = _premix0(stream, x_ctx, x_lat, mods[0], norm_g[0], ab_w_in,
                                   _rope_tables(n_lat, TOKEN_TILE))
    attn_ctx = _ctx_attention(ctx, q, kv, p["sink"])
    attn_lat = _lat_attention(lat, ctx.t, q, kv, k_ctx, v_ctx, p["sink"])
    x1 = _postmix0(stream, zc, attn_ctx, attn_lat, x_ctx, x_lat, mods[0], norm_g[0], p["conv_w"],
                   ab_w_out, mlp_w1, mlp_w2)
    gq, gk, gv, og, gf, gb = _premix1(stream, x1, mods[1], norm_g[1], p["gla_w_in"],
                                      p["gla_w_gate"], p["gla_gate_bias"])
    o_ctx, sf, sb = _gla_scan(ctx, 0, gq, gk, gv, gf, gb, None, None, tri, lvl)
    o_lat, _, _ = _gla_scan(lat, ctx.t, gq, gk, gv, gf, gb, state_fwd[:, 0], state_bwd[:, 0],
                            tri, lvl)
    y_ctx, y_lat = _postmix1(stream, o_ctx, o_lat, og, x1, mods[1], norm_g[1], p["gla_norm_g"],
                             gla_w_out, mlp_w1, mlp_w2)

    def cache_layout(t):
        t = t.reshape(b_ctx, 1, N_KV_HEADS, HEAD_DIM, n_ctx)
        return jnp.transpose(t, (0, 1, 4, 2, 3))

    return (y_ctx.reshape(x_prompt.shape), y_lat.reshape(x_sample.shape),
            cache_layout(k_t), cache_layout(v_t), sf[:, None], sb[:, None])
```

```python
import functools

import jax
import jax.numpy as jnp
from jax import lax
from jax.experimental import pallas as pl
from jax.experimental.pallas import tpu as pltpu

F32 = jnp.float32
BF16 = jnp.bfloat16

D_MODEL = 1024
MOD_CHUNKS = 6
EPS = 1e-6
CONV_DIM = 512
N_Q_HEADS = 8
N_KV_HEADS = 2
GQA_GROUP = 4
HEAD_DIM = 64
WINDOW = 128
ATTN_BLOCK = 128
ATTN_BLOCKS_PER_STEP = 4
GRID_W = 64
ROPE_BASE = 10000.0
QD = N_Q_HEADS * HEAD_DIM
KD = N_KV_HEADS * HEAD_DIM
AB_IN = 3 * CONV_DIM + QD + 2 * KD
GLA_HEADS = 4
GLA_DK = 128
GLA_DV = 256
GLA_RANK = 16
GLA_GATE_NORM = 16.0
GLA_TILE = 256
GLA_HEADS_PER_STEP = 2
LOG2E = 1.4426950408889634
GLA_QK = GLA_HEADS * GLA_DK
GLA_VD = GLA_HEADS * GLA_DV
GLA_MAIN = 2 * GLA_QK + 2 * GLA_VD
D_FF = 4 * D_MODEL
MLP_CHUNK = 512
MLP_CHUNKS = D_FF // MLP_CHUNK
WEIGHT_PIECE = 256
NEG_INF = -1e30
LANES = 128
VMEM_LIMIT = 60 * 1024 * 1024

NT_DIMS = (((1,), (1,)), ((), ()))
TN_DIMS = (((0,), (0,)), ((), ()))


def _cparams(*sem):
    return pltpu.CompilerParams(dimension_semantics=sem, vmem_limit_bytes=VMEM_LIMIT)


def _bdot(a, b):
    return jnp.dot(a.astype(BF16), b.astype(BF16), preferred_element_type=F32)


def _bdot_nt(a, b):
    return lax.dot_general(a.astype(BF16), b.astype(BF16), NT_DIMS, preferred_element_type=F32)


def _rms(x, g):
    ms = jnp.mean(x * x, axis=-1, keepdims=True)
    return x * lax.rsqrt(ms + EPS) * g


def _mod_chunk(mod_ref, i):
    return mod_ref[:, i * D_MODEL:(i + 1) * D_MODEL]


def _const_spec(shape):
    return pl.BlockSpec(shape, lambda *_: (0,) * len(shape))


def _f32_weight_spec(shape):
    assert shape[0] == 1
    return pl.BlockSpec((None,) + tuple(shape[1:]), lambda *_: (0, 0, 0),
                        pipeline_mode=pl.Buffered(1))


def _cast_once(w_ref, wb_ref):
    @pl.when(pl.program_id(0) == 0)
    def _():
        wb_ref[...] = w_ref[...].astype(BF16)


def _mod_kernel(cond_ref, w_ref, b_ref, o_ref):
    cnd = cond_ref[...]
    s = cnd / (1.0 + jnp.exp(-cnd))
    o_ref[...] = _bdot(s, w_ref[...]) + b_ref[...]


def _modulation(cond8, mod_w, mod_b):
    depth = mod_w.shape[0]
    n = mod_w.shape[2]
    tn = 1536
    return pl.pallas_call(
        _mod_kernel,
        grid=(depth, n // tn),
        in_specs=[
            pl.BlockSpec((8, D_MODEL), lambda l, j: (0, 0)),
            pl.BlockSpec((None, D_MODEL, tn), lambda l, j: (l, 0, j)),
            pl.BlockSpec((None, 1, tn), lambda l, j: (l, 0, j)),
        ],
        out_specs=pl.BlockSpec((None, 8, tn), lambda l, j: (l, 0, j)),
        out_shape=jax.ShapeDtypeStruct((depth, 8, n), F32),
        compiler_params=_cparams("arbitrary", "arbitrary"),
        name="modulation",
    )(cond8, mod_w, mod_b.reshape(depth, 1, n))


class _Group:
    def __init__(self, b, n):
        self.b, self.n, self.t = b, n, b * n


class _Stream:
    def __init__(self, ctx, lat, tm):
        assert ctx.t % tm == 0 and lat.n % tm == 0 and tm % ctx.n == 0
        self.ctx, self.lat, self.tm = ctx, lat, tm
        self.t = ctx.t + lat.t
        self.ctx_tiles = ctx.t // tm
        self.tiles = self.t // tm

    def is_lat(self):
        return pl.program_id(0) >= self.ctx_tiles

    def spec(self, width):
        return pl.BlockSpec((self.tm, width), lambda t: (t, 0))

    def ctx_spec(self, width):
        last = self.ctx_tiles - 1
        return pl.BlockSpec((self.tm, width), lambda t: (jnp.minimum(t, last), 0))

    def lat_spec(self, width):
        first = self.ctx_tiles
        return pl.BlockSpec((self.tm, width), lambda t: (jnp.maximum(t - first, 0), 0))

    def mod_spec(self):
        first, per = self.ctx_tiles, self.lat.n // self.tm
        return pl.BlockSpec((None, 1, MOD_CHUNKS * D_MODEL),
                            lambda t: (jnp.where(t < first, 0, 1 + (t - first) // per), 0, 0))

    def seq_len(self):
        return jnp.where(self.is_lat(), self.lat.n, self.ctx.n)


def _row_spec(tm, width):
    return pl.BlockSpec((tm, width), lambda t: (t, 0))


def _rope(x, cos, sin_lo, sin_hi):
    return (x * cos + pltpu.roll(x, LANES - 16, axis=1) * sin_lo
            + pltpu.roll(x, 16, axis=1) * sin_hi)


def _premix0_kernel(xc_ref, xl_ref, mod_ref, g_ref, w_ref, cos_ref, slo_ref, shi_ref,
                    zc_ref, q_ref, kv_ref, kt_ref, vt_ref, wb_ref, *, stream):
    _cast_once(w_ref, wb_ref)
    is_lat = stream.is_lat()
    x = jnp.where(is_lat, xl_ref[...], xc_ref[...])
    sh1, sc1 = _mod_chunk(mod_ref, 0), _mod_chunk(mod_ref, 1)
    h = _rms(x, g_ref[0:1, :]) * (1.0 + sc1) + sh1
    z = _bdot(h, wb_ref[...])
    c3 = 3 * CONV_DIM
    zc_ref[...] = z[:, :c3]
    scale = HEAD_DIM ** -0.5 * LOG2E
    cos, slo, shi = cos_ref[...], slo_ref[...], shi_ref[...]
    for j in range(QD // LANES):
        qs = z[:, c3 + j * LANES:c3 + (j + 1) * LANES]
        q_ref[:, j * LANES:(j + 1) * LANES] = (_rope(qs, cos, slo, shi) * scale).astype(BF16)
    kv_ref[:, :KD] = _rope(z[:, c3 + QD:c3 + QD + KD], cos, slo, shi)
    kv_ref[:, KD:] = z[:, c3 + QD + KD:]

    @pl.when(jnp.logical_not(is_lat))
    def _():
        n = kt_ref.shape[2]
        for j in range(kt_ref.shape[0]):
            kt_ref[j] = z[j * n:(j + 1) * n, c3 + QD:c3 + QD + KD].T
            vt_ref[j] = z[j * n:(j + 1) * n, c3 + QD + KD:].T


def _premix0(stream, x_ctx, x_lat, mod, norm_g, w_in, rope_tabs):
    tm, ctx, lat = stream.tm, stream.ctx, stream.lat
    first, per, last = stream.ctx_tiles, lat.n // tm, stream.ctx_tiles - 1
    rope_spec = pl.BlockSpec((tm, LANES),
                             lambda t: (jnp.where(t < first, 0, 1 + (t - first) % per), 0))
    cache_spec = pl.BlockSpec((tm // ctx.n, KD, ctx.n), lambda t: (jnp.minimum(t, last), 0, 0))
    return pl.pallas_call(
        functools.partial(_premix0_kernel, stream=stream),
        grid=(stream.tiles,),
        in_specs=[stream.ctx_spec(D_MODEL), stream.lat_spec(D_MODEL), stream.mod_spec(),
                  _const_spec((4, D_MODEL)), _f32_weight_spec(w_in.shape)] + [rope_spec] * 3,
        out_specs=[stream.spec(3 * CONV_DIM), stream.spec(QD), stream.spec(2 * KD),
                   cache_spec, cache_spec],
        out_shape=[jax.ShapeDtypeStruct((stream.t, 3 * CONV_DIM), F32),
                   jax.ShapeDtypeStruct((stream.t, QD), BF16),
                   jax.ShapeDtypeStruct((stream.t, 2 * KD), F32),
                   jax.ShapeDtypeStruct((ctx.b, KD, ctx.n), F32),
                   jax.ShapeDtypeStruct((ctx.b, KD, ctx.n), F32)],
        scratch_shapes=[pltpu.VMEM(w_in.shape[1:], BF16)],
        compiler_params=_cparams("arbitrary"),
        name="premix0",
    )(x_ctx, x_lat, mod, norm_g, w_in, *rope_tabs)


def _rope_tables(n, identity_rows):
    rows = n // GRID_W
    pos_r = jnp.repeat(jnp.arange(rows), GRID_W)
    pos_c = jnp.tile(jnp.arange(GRID_W), rows)
    half = HEAD_DIM // 2
    quarter = half // 2
    inv = ROPE_BASE ** (-(jnp.arange(quarter, dtype=F32) * 2.0 / half))

    def cs(pos):
        ang = pos.astype(F32)[:, None] * inv[None, :]
        return jnp.cos(ang), jnp.sin(ang)

    cr, sr = cs(pos_r)
    cc, sc = cs(pos_c)
    zero = jnp.zeros_like(sr)
    cos = jnp.concatenate([cr, cr, cc, cc], axis=1)
    sin_lo = jnp.concatenate([-sr, zero, -sc, zero], axis=1)
    sin_hi = jnp.concatenate([zero, sr, zero, sc], axis=1)
    rep = LANES // HEAD_DIM
    tables = []
    for t, ident in ((cos, 1.0), (sin_lo, 0.0), (sin_hi, 0.0)):
        head = jnp.full((identity_rows, LANES), ident, F32)
        tables.append(jnp.concatenate([head, jnp.tile(t, (1, rep))], axis=0))
    return tuple(tables)


def _attention_operands(k_all, v_all):
    assert KD == LANES == 2 * HEAD_DIM and GQA_GROUP == 4
    lane = lax.broadcasted_iota(jnp.int32, (1, LANES), 1)
    sub = lax.broadcasted_iota(jnp.int32, (LANES, 1), 0)
    v_t = v_all.T
    k_swapped = pltpu.roll(k_all, HEAD_DIM, axis=1)
    ops = []
    for g in range(N_KV_HEADS):
        k_low, k_high = (k_all, k_swapped) if g == 0 else (k_swapped, k_all)
        kz_even = jnp.where(lane < HEAD_DIM, k_low, 0.0).astype(BF16)
        kz_odd = jnp.where(lane >= HEAD_DIM, k_high, 0.0).astype(BF16)
        own = (sub < HEAD_DIM) if g == 0 else (sub >= HEAD_DIM)
        v_ext_t = jnp.where(own, v_t, 1.0).astype(BF16)
        ops.append((kz_even, kz_odd, v_ext_t))
    return ops


def _concat_operands(parts):
    return [(jnp.concatenate([p[g][0] for p in parts], axis=0),
             jnp.concatenate([p[g][1] for p in parts], axis=0),
             jnp.concatenate([p[g][2] for p in parts], axis=1)) for g in range(N_KV_HEADS)]


def _sink_attention(sink_ref, q_ref, operands, bias_t, o_ref):
    m = q_ref.shape[0]
    o_t = {}
    for g in range(N_KV_HEADS):
        kz_even, kz_odd, v_ext_t = operands[g]
        qq = jnp.concatenate([q_ref[:, (2 * g) * LANES:(2 * g + 1) * LANES],
                              q_ref[:, (2 * g + 1) * LANES:(2 * g + 2) * LANES]], axis=0)
        s = jnp.concatenate([_bdot_nt(kz_even, qq), _bdot_nt(kz_odd, qq)], axis=1)
        heads = [4 * g, 4 * g + 2, 4 * g + 1, 4 * g + 3]
        if bias_t is not None:
            nb = bias_t.shape[0]
            s = jnp.concatenate([s[:nb] + jnp.concatenate([bias_t] * GQA_GROUP, axis=1), s[nb:]],
                                axis=0)
        sink = jnp.concatenate([jnp.full((1, m), sink_ref[h] * LOG2E, F32) for h in heads], axis=1)
        mx = jnp.maximum(jnp.max(s, axis=0, keepdims=True), sink)
        p = jnp.exp2(s - mx).astype(BF16)
        oe = jnp.dot(v_ext_t, p, preferred_element_type=F32)
        other = (1 - g) * HEAD_DIM
        den = oe[other:other + 1] + jnp.exp2(sink - mx)
        o_g = oe[g * HEAD_DIM:(g + 1) * HEAD_DIM] / den
        for i, h in enumerate(heads):
            o_t[h] = o_g[:, i * m:(i + 1) * m]
    for j in range(N_Q_HEADS // 2):
        pair_t = jnp.concatenate([o_t[2 * j], o_t[2 * j + 1]], axis=0)
        o_ref[:, j * LANES:(j + 1) * LANES] = pair_t.T.astype(BF16)


def _ctx_attn_kernel(sink_ref, q_ref, kv_ref, o_ref):
    operands = _attention_operands(kv_ref[:, :KD], kv_ref[:, KD:])
    _sink_attention(sink_ref, q_ref, operands, None, o_ref)


def _ctx_attention(grp, q, kv, sink):
    n = grp.n
    return pl.pallas_call(
        _ctx_attn_kernel,
        grid=(grp.b,),
        in_specs=[pl.BlockSpec(memory_space=pltpu.SMEM), _row_spec(n, QD), _row_spec(n, 2 * KD)],
        out_specs=_row_spec(n, QD),
        out_shape=jax.ShapeDtypeStruct((grp.t, QD), BF16),
        compiler_params=_cparams("arbitrary"),
        name="ctx_attention",
    )(sink, q, kv)


def _lat_attn_kernel(sink_ref, q_ref, kvp_ref, kvc_ref, kvn_ref, kc_ref, vc_ref, o_ref, *, n):
    band = jnp.concatenate([kvp_ref[...], kvc_ref[...], kvn_ref[...]], axis=0)
    kj = lax.broadcasted_iota(jnp.int32, (3 * ATTN_BLOCK, ATTN_BLOCK), 0)
    qi = lax.broadcasted_iota(jnp.int32, (3 * ATTN_BLOCK, ATTN_BLOCK), 1)
    rel = kj - ATTN_BLOCK - qi
    ctx_ops = _attention_operands(kc_ref[...], vc_ref[...])
    band_ops = [_attention_operands(band[i * ATTN_BLOCK:(i + 1) * ATTN_BLOCK, :KD],
                                    band[i * ATTN_BLOCK:(i + 1) * ATTN_BLOCK, KD:])
                for i in range(ATTN_BLOCKS_PER_STEP + 2)]
    for j in range(ATTN_BLOCKS_PER_STEP):
        blk = pl.program_id(1) * ATTN_BLOCKS_PER_STEP + j
        kpos = (blk - 1) * ATTN_BLOCK + kj
        valid = (jnp.abs(rel) <= WINDOW) & (kpos >= 0) & (kpos < n)
        bias_t = jnp.where(valid, 0.0, NEG_INF)
        operands = _concat_operands(band_ops[j:j + 3] + [ctx_ops])
        rows = pl.ds(j * ATTN_BLOCK, ATTN_BLOCK)
        _sink_attention(sink_ref, q_ref.at[rows], operands, bias_t, o_ref.at[rows])


def _lat_attention(grp, row0, q, kv, k_ctx, v_ctx, sink):
    nb = grp.n // ATTN_BLOCK
    per = ATTN_BLOCKS_PER_STEP
    steps = nb // per
    past = k_ctx.shape[1]
    assert row0 % (per * ATTN_BLOCK) == 0
    blk0, step0 = row0 // ATTN_BLOCK, row0 // (per * ATTN_BLOCK)

    def edge(off):
        return pl.BlockSpec((ATTN_BLOCK, 2 * KD),
                            lambda b, i: (blk0 + b * nb + jnp.clip(i * per + off, 0, nb - 1), 0))

    ctx_spec = pl.BlockSpec((None, past, KD), lambda b, i: (b, 0, 0))
    return pl.pallas_call(
        functools.partial(_lat_attn_kernel, n=grp.n),
        grid=(grp.b, steps),
        in_specs=[pl.BlockSpec(memory_space=pltpu.SMEM),
                  pl.BlockSpec((per * ATTN_BLOCK, QD), lambda b, i: (step0 + b * steps + i, 0)),
                  edge(-1),
                  pl.BlockSpec((per * ATTN_BLOCK, 2 * KD), lambda b, i: (step0 + b * steps + i, 0)),
                  edge(per), ctx_spec, ctx_spec],
        out_specs=pl.BlockSpec((per * ATTN_BLOCK, QD), lambda b, i: (b * steps + i, 0)),
        out_shape=jax.ShapeDtypeStruct((grp.t, QD), BF16),
        compiler_params=_cparams("arbitrary", "arbitrary"),
        name="lat_attention",
    )(sink, q, kv, kv, kv, k_ctx, v_ctx)


class _TailWeights:
    def __init__(self, layer, wo_hbm, w1_hbm, w2_hbm, wo_s, w1_s, w2_s, stage1, stage2, sem):
        self.layer = layer
        self.wo_hbm, self.w1_hbm, self.w2_hbm = wo_hbm, w1_hbm, w2_hbm
        self.wo_s, self.w1_s, self.w2_s = wo_s, w1_s, w2_s
        self.stage1, self.stage2, self.sem = stage1, stage2, sem

    def _w1_copy(self, p):
        src = self.w1_hbm.at[self.layer, :, pl.ds(p * WEIGHT_PIECE, WEIGHT_PIECE)]
        return pltpu.make_async_copy(src, self.stage1.at[p % 2], self.sem.at[0, p % 2])

    def _w2_copy(self, p):
        src = self.w2_hbm.at[self.layer, pl.ds(p * WEIGHT_PIECE, WEIGHT_PIECE), :]
        return pltpu.make_async_copy(src, self.stage2.at[p % 2], self.sem.at[1, p % 2])

    def _wo_copy(self, p):
        src = self.wo_hbm.at[0, pl.ds(p * WEIGHT_PIECE, WEIGHT_PIECE), :]
        return pltpu.make_async_copy(src, self.stage2.at[p % 2], self.sem.at[1, p % 2])

    def start(self):
        self._wo_copy(0).start()
        self._wo_copy(1).start()
        self._w1_copy(0).start()
        self._w1_copy(1).start()

    def fetch_out_proj(self):
        pieces = D_MODEL // WEIGHT_PIECE
        for p in range(pieces):
            self._wo_copy(p).wait()
            self.wo_s[p * WEIGHT_PIECE:(p + 1) * WEIGHT_PIECE, :] = self.stage2[p % 2].astype(BF16)
            if p + 2 < pieces:
                self._wo_copy(p + 2).start()
            else:
                self._w2_copy(p + 2 - pieces).start()

    def fetch_mlp_chunk(self, j):
        per = MLP_CHUNK // WEIGHT_PIECE
        total = MLP_CHUNKS * per
        for p in range(j * per, (j + 1) * per):
            part = slice((p % per) * WEIGHT_PIECE, (p % per + 1) * WEIGHT_PIECE)
            self._w1_copy(p).wait()
            self.w1_s[j, :, part] = self.stage1[p % 2].astype(BF16)
            if p + 2 < total:
                self._w1_copy(p + 2).start()
            self._w2_copy(p).wait()
            self.w2_s[j, part, :] = self.stage2[p % 2].astype(BF16)
            if p + 2 < total:
                self._w2_copy(p + 2).start()


def _tail_weight_scratch():
    assert MLP_CHUNK % WEIGHT_PIECE == 0 and D_MODEL // WEIGHT_PIECE >= 2
    return [pltpu.VMEM((D_MODEL, D_MODEL), BF16),
            pltpu.VMEM((MLP_CHUNKS, D_MODEL, MLP_CHUNK), BF16),
            pltpu.VMEM((MLP_CHUNKS, MLP_CHUNK, D_MODEL), BF16),
            pltpu.VMEM((2, D_MODEL, WEIGHT_PIECE), F32),
            pltpu.VMEM((2, WEIGHT_PIECE, D_MODEL), F32),
            pltpu.SemaphoreType.DMA((2, 2))]


def _first_step_streams_weights(body, weights):
    first = pl.program_id(0) == 0

    @pl.when(first)
    def _():
        weights.start()
        body(True)

    @pl.when(jnp.logical_not(first))
    def _():
        body(False)


def _mlp_tail(x, mod_ref, g_ref, weights, streaming):
    sh2, sc2, gt2 = _mod_chunk(mod_ref, 3), _mod_chunk(mod_ref, 4), _mod_chunk(mod_ref, 5)
    hb = (_rms(x, g_ref[2:3, :]) * (1.0 + sc2) + sh2).astype(BF16)
    acc = None
    for j in range(MLP_CHUNKS):
        if streaming:
            weights.fetch_mlp_chunk(j)
        a = jnp.dot(hb, weights.w1_s[j], preferred_element_type=F32)
        a = jnp.maximum(a, 0.0)
        part = jnp.dot((a * a).astype(BF16), weights.w2_s[j], preferred_element_type=F32)
        acc = part if acc is None else acc + part
    return x + gt2 * _rms(acc, g_ref[3:4, :])


def _postmix0_kernel(zc_ref, zp_ref, zn_ref, atc_ref, atl_ref, xc_ref, xl_ref, mod_ref, g_ref,
                     cw_ref, wo_hbm, w1_hbm, w2_hbm, o_ref, *scratch, stream):
    weights = _TailWeights(0, wo_hbm, w1_hbm, w2_hbm, *scratch)
    _first_step_streams_weights(
        functools.partial(_postmix0_body, zc_ref, zp_ref, zn_ref, atc_ref, atl_ref, xc_ref, xl_ref,
                          mod_ref, g_ref, cw_ref, o_ref, weights, stream), weights)


def _postmix0_body(zc_ref, zp_ref, zn_ref, atc_ref, atl_ref, xc_ref, xl_ref, mod_ref, g_ref,
                   cw_ref, o_ref, weights, stream, streaming):
    c, tm = CONV_DIM, stream.tm
    is_lat, n = stream.is_lat(), stream.seq_len()
    zc = zc_ref[...]
    u = zc[:, c:2 * c] * zc[:, 2 * c:]
    u_before = zp_ref[7:8, c:2 * c] * zp_ref[7:8, 2 * c:]
    u_after = zn_ref[0:1, c:2 * c] * zn_ref[0:1, 2 * c:]
    row = lax.broadcasted_iota(jnp.int32, (tm, 1), 0)
    pos = (pl.program_id(0) * tm + row) & (n - 1)
    u_prev = jnp.where(row == 0, u_before, pltpu.roll(u, 1, axis=0))
    u_prev = jnp.where(pos == 0, 0.0, u_prev)
    u_next = jnp.where(row == tm - 1, u_after, pltpu.roll(u, tm - 1, axis=0))
    u_next = jnp.where(pos == n - 1, 0.0, u_next)
    conv = u_prev * cw_ref[0:1, :] + u * cw_ref[1:2, :] + u_next * cw_ref[2:3, :]
    if streaming:
        weights.fetch_out_proj()
    attn = jnp.where(is_lat, atl_ref[...], atc_ref[...])
    mix = (_bdot(zc[:, :c] * conv, weights.wo_s[:c, :])
           + jnp.dot(attn, weights.wo_s[c:, :], preferred_element_type=F32))
    gt1 = _mod_chunk(mod_ref, 2)
    x1 = jnp.where(is_lat, xl_ref[...], xc_ref[...]) + gt1 * _rms(mix, g_ref[1:2, :])
    o_ref[...] = _mlp_tail(x1, mod_ref, g_ref, weights, streaming)


def _postmix0(stream, zc, attn_ctx, attn_lat, x_ctx, x_lat, mod, norm_g, conv_w, w_out, w1, w2):
    tm = stream.tm
    for n in (stream.ctx.n, stream.lat.n):
        assert n & (n - 1) == 0 and stream.ctx.t % n == 0
    r8 = tm // 8
    last8 = stream.t // 8 - 1
    hbm = pl.BlockSpec(memory_space=pl.ANY)
    return pl.pallas_call(
        functools.partial(_postmix0_kernel, stream=stream),
        grid=(stream.tiles,),
        in_specs=[stream.spec(3 * CONV_DIM),
                  pl.BlockSpec((8, 3 * CONV_DIM), lambda t: (jnp.maximum(t * r8 - 1, 0), 0)),
                  pl.BlockSpec((8, 3 * CONV_DIM), lambda t: (jnp.minimum((t + 1) * r8, last8), 0)),
                  stream.ctx_spec(QD), stream.lat_spec(QD),
                  stream.ctx_spec(D_MODEL), stream.lat_spec(D_MODEL), stream.mod_spec(),
                  _const_spec((4, D_MODEL)), _const_spec((3, CONV_DIM)), hbm, hbm, hbm],
        out_specs=stream.spec(D_MODEL),
        out_shape=jax.ShapeDtypeStruct((stream.t, D_MODEL), F32),
        scratch_shapes=_tail_weight_scratch(),
        compiler_params=_cparams("arbitrary"),
        name="postmix0_mlp",
    )(zc, zc, zc, attn_ctx, attn_lat, x_ctx, x_lat, mod, norm_g, conv_w, w_out, w1, w2)


def _premix1_kernel(x_ref, mod_ref, g_ref, w_ref, wg_ref, gb_ref,
                    q_ref, k_ref, v_ref, og_ref, gf_ref, gbk_ref, wb_ref, wr_ref):
    @pl.when(pl.program_id(0) == 0)
    def _():
        for c in range(0, GLA_MAIN, GLA_QK):
            wb_ref[:, c:c + GLA_QK] = w_ref[c:c + GLA_QK, :].T.astype(BF16)
        wr_ref[...] = w_ref[GLA_MAIN:, :].astype(BF16)

    sh1, sc1 = _mod_chunk(mod_ref, 0), _mod_chunk(mod_ref, 1)
    hb = (_rms(x_ref[...], g_ref[0:1, :]) * (1.0 + sc1) + sh1).astype(BF16)
    z = jnp.dot(hb, wb_ref[...], preferred_element_type=F32)
    q_ref[...] = z[:, :GLA_QK] * (GLA_DK ** -0.5)
    k_ref[...] = z[:, GLA_QK:2 * GLA_QK]
    v_ref[...] = z[:, 2 * GLA_QK:2 * GLA_QK + GLA_VD].astype(BF16)
    og_ref[...] = z[:, 2 * GLA_QK + GLA_VD:]
    r = _bdot_nt(hb, wr_ref[...])
    pre = _bdot(r, wg_ref[...]) + gb_ref[...]
    soft = jnp.log2(1.0 + jnp.exp2(jnp.abs(pre) * (-LOG2E)))
    gate = (jnp.minimum(pre, 0.0) * LOG2E - soft) * (1.0 / GLA_GATE_NORM)
    gf_ref[...] = gate[:, :GLA_QK]
    gbk_ref[...] = gate[:, GLA_QK:]


def _premix1(stream, x2d, mod, norm_g, w_in_t, w_gate, gate_bias):
    t, tm = stream.t, stream.tm
    w_in = w_in_t
    assert w_in.shape == (1, GLA_MAIN + 2 * GLA_RANK, D_MODEL)
    return pl.pallas_call(
        _premix1_kernel,
        grid=(t // tm,),
        in_specs=[_row_spec(tm, D_MODEL), stream.mod_spec(), _const_spec((4, D_MODEL)),
                  _f32_weight_spec(w_in.shape), _const_spec((2 * GLA_RANK, 2 * GLA_QK)),
                  _const_spec((1, 2 * GLA_QK))],
        out_specs=[_row_spec(tm, GLA_QK), _row_spec(tm, GLA_QK), _row_spec(tm, GLA_VD),
                   _row_spec(tm, GLA_VD), _row_spec(tm, GLA_QK), _row_spec(tm, GLA_QK)],
        out_shape=[jax.ShapeDtypeStruct((t, GLA_QK), F32), jax.ShapeDtypeStruct((t, GLA_QK), F32),
                   jax.ShapeDtypeStruct((t, GLA_VD), BF16),
                   jax.ShapeDtypeStruct((t, GLA_VD), F32), jax.ShapeDtypeStruct((t, GLA_QK), F32),
                   jax.ShapeDtypeStruct((t, GLA_QK), F32)],
        scratch_shapes=[pltpu.VMEM((D_MODEL, GLA_MAIN), BF16),
                        pltpu.VMEM((2 * GLA_RANK, D_MODEL), BF16)],
        compiler_params=_cparams("arbitrary"),
        name="premix1",
    )(x2d, mod, norm_g, w_in, w_gate, gate_bias)


def _split3(x):
    hi = x.astype(BF16)
    r1 = x - hi.astype(F32)
    mid = r1.astype(BF16)
    lo = (r1 - mid.astype(F32)).astype(BF16)
    return hi, mid, lo


def _level_exponent(b, s, reverse):
    idx = s if reverse else s - 1
    if s >= 8:
        n = GLA_TILE // (2 * s)
        b4 = b.reshape(n, 2, s, GLA_DK)
        first, second = b4[:, 0:1], b4[:, 1:2]
        r = (second[:, :, 0:1] if reverse else first[:, :, s - 1:s])
        parts = [first - r, r - second] if reverse else [r - first, second - r]
        return jnp.concatenate(parts, axis=1).reshape(GLA_TILE, GLA_DK)
    b8 = b.reshape(GLA_TILE // 8, 8, GLA_DK)
    sub = lax.broadcasted_iota(jnp.int32, (1, 8, 1), 1)
    if s == 4:
        r = b8[:, idx:idx + 1, :]
    else:
        assert s == 2
        r = jnp.where(sub < 4, b8[:, idx:idx + 1, :], b8[:, 4 + idx:5 + idx, :])
    in_second = (sub // s) % 2 == 1
    sign = jnp.where(in_second != reverse, 1.0, -1.0)
    return ((b8 - r) * sign).reshape(GLA_TILE, GLA_DK)


def _pair_matrix(q, k, g_f, g_b, b_f, b_b, lvl):
    half = GLA_TILE // 2
    lo, hi = slice(0, half), slice(half, GLA_TILE)
    qb, kb = q.astype(BF16), k.astype(BF16)
    row = lax.broadcasted_iota(jnp.int32, (GLA_TILE, 1), 0)
    odd = row % 2 == 1
    own = jnp.sum(q * k, axis=-1, keepdims=True)
    k_adj = jnp.where(odd, pltpu.roll(k, 1, axis=0), pltpu.roll(k, GLA_TILE - 1, axis=0))
    adj = jnp.sum(q * jnp.exp2(jnp.where(odd, g_f, g_b)) * k_adj, axis=-1, keepdims=True)
    blocks = [jnp.where(lvl == 0, 2.0 * own[rows], jnp.where(lvl == 1, adj[rows], 0.0))
              for rows in (lo, hi)]
    cross_f = cross_b = None
    s, level = 2, 2
    while s < GLA_TILE:
        if s >= 16:
            n = GLA_TILE // (2 * s)

            def halves(x):
                x4 = x.reshape(n, 2, s, GLA_DK)
                return x4[:, 0:1], x4[:, 1:2]

            def rows(first, second):
                return jnp.concatenate([first, second], axis=1).reshape(GLA_TILE, GLA_DK)

            (bf1, bf2), (bb1, bb2) = halves(b_f), halves(b_b)
            (q1, q2), (k1, k2) = halves(qb), halves(kb)
            r_f, r_b = bf1[:, :, s - 1:s], bb2[:, :, 0:1]
            zero = jnp.zeros_like(q1)
            lhs = jnp.concatenate(
                [rows(zero, q2 * jnp.exp2(bf2 - r_f).astype(BF16)),
                 rows(q1 * jnp.exp2(bb1 - r_b).astype(BF16), zero)], axis=1)
            rhs = jnp.concatenate(
                [rows(k1 * jnp.exp2(r_f - bf1).astype(BF16), zero),
                 rows(zero, k2 * jnp.exp2(r_b - bb2).astype(BF16))], axis=1)
        else:
            second = (row // s) % 2 == 1
            f_f = jnp.exp2(_level_exponent(b_f, s, False)).astype(BF16)
            f_b = jnp.exp2(_level_exponent(b_b, s, True)).astype(BF16)
            u = qb * jnp.where(second, f_f, f_b)
            w = kb * jnp.where(second, f_b, f_f)
            zero = jnp.zeros_like(u)
            lhs = jnp.concatenate([jnp.where(second, u, zero), jnp.where(second, zero, u)], axis=1)
            rhs = jnp.concatenate([jnp.where(second, zero, w), jnp.where(second, w, zero)], axis=1)
        pairs = _bdot_nt(lhs, rhs)
        if s == half:
            cross_f, cross_b = pairs[hi, lo], pairs[lo, hi]
        else:
            blocks = [jnp.where(lvl == level, pairs[lo, lo], blocks[0]),
                      jnp.where(lvl == level, pairs[hi, hi], blocks[1])]
        s, level = 2 * s, level + 1
    top = jnp.concatenate([blocks[0].astype(BF16), cross_b.astype(BF16)], axis=1)
    bottom = jnp.concatenate([cross_f.astype(BF16), blocks[1].astype(BF16)], axis=1)
    return jnp.concatenate([top, bottom], axis=0)


def _carry_state(q, k, b, v, s_ref, reverse):
    edge = 0 if reverse else GLA_TILE - 1
    b_last = b[edge:edge + 1, :]
    qe = (q * jnp.exp2(b)).astype(BF16)
    ke = (k * jnp.exp2(b_last - b)).astype(BF16)
    st = s_ref[...]
    s_ref[...] = st * jnp.exp2(b_last) + lax.dot_general(
        v, ke, TN_DIMS, preferred_element_type=F32)
    return _bdot_nt(qe, st)


def _gla_kernel(*refs, zero_init, nt):
    if zero_init:
        (tri_ref, lvl_ref, q_ref, k_ref, gf_ref, gb_ref, v_ref,
         o_ref, sfo_ref, sbo_ref, sf_ref, sb_ref, bb_ref) = refs
        sf_ref[...] = jnp.zeros_like(sf_ref)
        sb_ref[...] = jnp.zeros_like(sb_ref)
    else:
        (tri_ref, lvl_ref, q_ref, k_ref, gf_ref, gb_ref, v_ref, s0f_ref, s0b_ref,
         o_ref, sfo_ref, sbo_ref, sf_ref, sb_ref, bb_ref) = refs
        for hh in range(GLA_HEADS_PER_STEP):
            sf_ref[hh] = s0f_ref[hh].T
            sb_ref[hh] = s0b_ref[hh].T

    def tile_rows(tile):
        r0 = tile * GLA_TILE
        return pl.ds(r0 if isinstance(r0, int) else pl.multiple_of(r0, GLA_TILE), GLA_TILE)

    heads = [(slice(hh * GLA_DK, (hh + 1) * GLA_DK), slice(hh * GLA_DV, (hh + 1) * GLA_DV), hh)
             for hh in range(GLA_HEADS_PER_STEP)]

    def forward_sweep(t, carry):
        rows = tile_rows(t)
        g_f, g_b = gf_ref[rows, :], gb_ref[rows, :]
        c = None
        for part in _split3(jnp.concatenate([g_f, g_b], axis=1)):
            term = jnp.dot(tri_ref[...], part, preferred_element_type=F32)
            c = term if c is None else c + term
        width = GLA_HEADS_PER_STEP * GLA_DK
        c_b = c[:, width:]
        b_f = c[:, :width]
        b_b = (c_b[GLA_TILE - 1:GLA_TILE, :] - c_b) + g_b
        bb_ref[t] = b_b
        for dk, dv, hh in heads:
            q, k, v = q_ref[rows, dk], k_ref[rows, dk], v_ref[rows, dv]
            att = _pair_matrix(q, k, g_f[:, dk], g_b[:, dk], b_f[:, dk], b_b[:, dk], lvl_ref[...])
            o_ref[rows, dv] = (jnp.dot(att, v, preferred_element_type=F32)
                               + _carry_state(q, k, b_f[:, dk], v, sf_ref.at[hh], False))
        return carry

    def backward_sweep(i, carry):
        t = nt - 1 - i
        rows = tile_rows(t)
        b_b = bb_ref[t]
        for dk, dv, hh in heads:
            o_ref[rows, dv] += _carry_state(q_ref[rows, dk], k_ref[rows, dk], b_b[:, dk],
                                            v_ref[rows, dv], sb_ref.at[hh], True)
        return carry

    if nt == 1:
        forward_sweep(0, 0)
        backward_sweep(0, 0)
    else:
        lax.fori_loop(0, nt, forward_sweep, 0)
        lax.fori_loop(0, nt, backward_sweep, 0)
    for hh in range(GLA_HEADS_PER_STEP):
        sfo_ref[hh] = sf_ref[hh].T
        sbo_ref[hh] = sb_ref[hh].T


def _gla_scan(grp, row0, q, k, v, gf, gb, s0f, s0b, tri, lvl):
    n = grp.n
    nt = n // GLA_TILE
    hp = GLA_HEADS_PER_STEP
    zero_init = s0f is None
    half = GLA_TILE // 2
    assert row0 % n == 0
    seq0 = row0 // n
    in_dk = pl.BlockSpec((n, hp * GLA_DK), lambda b, h: (seq0 + b, h))
    in_dv = pl.BlockSpec((n, hp * GLA_DV), lambda b, h: (seq0 + b, h))
    seq_dv = pl.BlockSpec((n, hp * GLA_DV), lambda b, h: (b, h))
    state_spec = pl.BlockSpec((None, hp, GLA_DK, GLA_DV), lambda b, h: (b, h, 0, 0))
    in_specs = [_const_spec((GLA_TILE, GLA_TILE)), _const_spec((half, half)),
                in_dk, in_dk, in_dk, in_dk, in_dv]
    args = [tri, lvl, q, k, gf, gb, v]
    if not zero_init:
        in_specs += [state_spec, state_spec]
        args += [s0f, s0b]
    state_shape = jax.ShapeDtypeStruct((grp.b, GLA_HEADS, GLA_DK, GLA_DV), F32)
    return pl.pallas_call(
        functools.partial(_gla_kernel, zero_init=zero_init, nt=nt),
        grid=(grp.b, GLA_HEADS // hp),
        in_specs=in_specs,
        out_specs=[seq_dv, state_spec, state_spec],
        out_shape=[jax.ShapeDtypeStruct((grp.t, GLA_VD), F32), state_shape, state_shape],
        scratch_shapes=[pltpu.VMEM((hp, GLA_DV, GLA_DK), F32), pltpu.VMEM((hp, GLA_DV, GLA_DK), F32),
                        pltpu.VMEM((nt, GLA_TILE, hp * GLA_DK), F32)],
        compiler_params=_cparams("arbitrary", "arbitrary"),
        name="gla_scan",
    )(*args)


def _gla_constants():
    half = GLA_TILE // 2
    i = jnp.arange(GLA_TILE)[:, None]
    j = jnp.arange(GLA_TILE)[None, :]
    tri = (j <= i).astype(BF16)
    ih, jh = i[:half], j[:, :half]
    x = jnp.bitwise_xor(ih, jh)
    lvl = sum((x >= (1 << p)).astype(jnp.int32) for p in range(half.bit_length() - 1))
    return tri, lvl


def _postmix1_kernel(oc_ref, ol_ref, og_ref, x_ref, mod_ref, g_ref, gn_ref, wo_hbm, w1_hbm, w2_hbm,
                     yc_ref, yl_ref, *scratch, stream):
    weights = _TailWeights(1, wo_hbm, w1_hbm, w2_hbm, *scratch)
    _first_step_streams_weights(
        functools.partial(_postmix1_body, oc_ref, ol_ref, og_ref, x_ref, mod_ref, g_ref, gn_ref,
                          yc_ref, yl_ref, weights, stream), weights)


def _postmix1_body(oc_ref, ol_ref, og_ref, x_ref, mod_ref, g_ref, gn_ref, yc_ref, yl_ref,
                   weights, stream, streaming):
    is_lat = stream.is_lat()
    gn = gn_ref[...]
    ys = []
    for h in range(GLA_HEADS):
        cols = slice(h * GLA_DV, (h + 1) * GLA_DV)
        o = _rms(jnp.where(is_lat, ol_ref[:, cols], oc_ref[:, cols]), gn)
        og = og_ref[:, cols]
        ys.append((o * (og / (1.0 + jnp.exp(-og)))).astype(BF16))
    if streaming:
        weights.fetch_out_proj()
    mix = None
    for h, y in enumerate(ys):
        part = jnp.dot(y, weights.wo_s[h * GLA_DV:(h + 1) * GLA_DV, :], preferred_element_type=F32)
        mix = part if mix is None else mix + part
    gt1 = _mod_chunk(mod_ref, 2)
    x1 = x_ref[...] + gt1 * _rms(mix, g_ref[1:2, :])
    y = _mlp_tail(x1, mod_ref, g_ref, weights, streaming)

    @pl.when(jnp.logical_not(is_lat))
    def _():
        yc_ref[...] = y

    @pl.when(is_lat)
    def _():
        yl_ref[...] = y


def _postmix1(stream, o_ctx, o_lat, og, x2d, mod, norm_g, gla_norm_g, w_out, w1, w2):
    hbm = pl.BlockSpec(memory_space=pl.ANY)
    return pl.pallas_call(
        functools.partial(_postmix1_kernel, stream=stream),
        grid=(stream.tiles,),
        in_specs=[stream.ctx_spec(GLA_VD), stream.lat_spec(GLA_VD), stream.spec(GLA_VD),
                  stream.spec(D_MODEL), stream.mod_spec(), _const_spec((4, D_MODEL)),
                  _const_spec((1, GLA_DV)), hbm, hbm, hbm],
        out_specs=[stream.ctx_spec(D_MODEL), stream.lat_spec(D_MODEL)],
        out_shape=[jax.ShapeDtypeStruct((stream.ctx.t, D_MODEL), F32),
                   jax.ShapeDtypeStruct((stream.lat.t, D_MODEL), F32)],
        scratch_shapes=_tail_weight_scratch(),
        compiler_params=_cparams("arbitrary"),
        name="postmix1_mlp",
    )(o_ctx, o_lat, og, x2d, mod, norm_g, gla_norm_g, w_out, w1, w2)


TOKEN_TILE = 512


def kernel(x_prompt, x_sample, cache_k, cache_v, state_fwd, state_bwd, c, c_ctx, mod_w, mod_b,
           norm_g, ab_w_in, conv_w, attn_sink, ab_w_out, gla_w_in, gla_gate_w, gla_gate_b,
           gla_norm_g, gla_w_out, mlp_w1, mlp_w2):
    b_ctx, n_ctx, _ = x_prompt.shape
    b_lat, n_lat, _ = x_sample.shape
    assert mod_w.shape[0] == 2 and ab_w_in.shape[0] == 1 and gla_w_in.shape[0] == 1
    assert 1 + b_lat <= 8

    cond8 = jnp.zeros((8, D_MODEL), F32).at[0].set(c_ctx).at[1:1 + b_lat].set(c)
    mod = _modulation(cond8, mod_w, mod_b)
    mods = [mod[l].reshape(8, 1, -1) for l in range(2)]

    w_gate = jnp.zeros((2 * GLA_RANK, 2 * GLA_QK), F32)
    w_gate = w_gate.at[:GLA_RANK, :GLA_QK].set(gla_gate_w[0, 0])
    w_gate = w_gate.at[GLA_RANK:, GLA_QK:].set(gla_gate_w[0, 1])
    tri, lvl = _gla_constants()
    p = {
        "conv_w": conv_w[0],
        "sink": attn_sink[0],
        "gla_w_in": jnp.swapaxes(gla_w_in, 1, 2),
        "gla_w_gate": w_gate.astype(BF16),
        "gla_gate_bias": gla_gate_b[0].reshape(1, 2 * GLA_QK),
        "gla_norm_g": gla_norm_g[0].reshape(1, GLA_DV),
    }

    ctx, lat = _Group(b_ctx, n_ctx), _Group(b_lat, n_lat)
    stream = _Stream(ctx, lat, TOKEN_TILE)
    x_ctx, x_lat = x_prompt.reshape(ctx.t, D_MODEL), x_sample.reshape(lat.t, D_MODEL)
    past = cache_k.shape[2]
    k_ctx = cache_k[:, 0].reshape(b_lat, past, KD)
    v_ctx = cache_v[:, 0].reshape(b_lat, past, KD)

    zc, q, kv, k_t, v_t = _premix0(stream, x_ctx, x_lat, mods[0], norm_g[0], ab_w_in,
                                   _rope_tables(n_lat, TOKEN_TILE))
    attn_ctx = _ctx_attention(ctx, q, kv, p["sink"])
    attn_lat = _lat_attention(lat, ctx.t, q, kv, k_ctx, v_ctx, p["sink"])
    x1 = _postmix0(stream, zc, attn_ctx, attn_lat, x_ctx, x_lat, mods[0], norm_g[0], p["conv_w"],
                   ab_w_out, mlp_w1, mlp_w2)
    gq, gk, gv, og, gf, gb = _premix1(stream, x1, mods[1], norm_g[1], p["gla_w_in"],
                                      p["gla_w_gate"], p["gla_gate_bias"])
    o_ctx, sf, sb = _gla_scan(ctx, 0, gq, gk, gv, gf, gb, None, None, tri, lvl)
    o_lat, _, _ = _gla_scan(lat, ctx.t, gq, gk, gv, gf, gb, state_fwd[:, 0], state_bwd[:, 0],
                            tri, lvl)
    y_ctx, y_lat = _postmix1(stream, o_ctx, o_lat, og, x1, mods[1], norm_g[1], p["gla_norm_g"],
                             gla_w_out, mlp_w1, mlp_w2)

    def cache_layout(t):
        t = t.reshape(b_ctx, 1, N_KV_HEADS, HEAD_DIM, n_ctx)
        return jnp.transpose(t, (0, 1, 4, 2, 3))

    return (y_ctx.reshape(x_prompt.shape), y_lat.reshape(x_sample.shape),
            cache_layout(k_t), cache_layout(v_t), sf[:, None], sb[:, None])
```

```python
import functools

import jax
import jax.numpy as jnp
from jax import lax
from jax.experimental import pallas as pl
from jax.experimental.pallas import tpu as pltpu

F32 = jnp.float32
BF16 = jnp.bfloat16

D_MODEL = 1024
MOD_CHUNKS = 6
EPS = 1e-6
CONV_DIM = 512
N_Q_HEADS = 8
N_KV_HEADS = 2
GQA_GROUP = 4
HEAD_DIM = 64
WINDOW = 128
ATTN_BLOCK = 128
ATTN_BLOCKS_PER_STEP = 4
GRID_W = 64
ROPE_BASE = 10000.0
QD = N_Q_HEADS * HEAD_DIM
KD = N_KV_HEADS * HEAD_DIM
AB_IN = 3 * CONV_DIM + QD + 2 * KD
GLA_HEADS = 4
GLA_DK = 128
GLA_DV = 256
GLA_RANK = 16
GLA_GATE_NORM = 16.0
GLA_TILE = 256
GLA_HEADS_PER_STEP = 2
LOG2E = 1.4426950408889634
GLA_QK = GLA_HEADS * GLA_DK
GLA_VD = GLA_HEADS * GLA_DV
GLA_MAIN = 2 * GLA_QK + 2 * GLA_VD
D_FF = 4 * D_MODEL
MLP_CHUNK = 512
MLP_CHUNKS = D_FF // MLP_CHUNK
WEIGHT_PIECE = 256
NEG_INF = -1e30
LANES = 128
VMEM_LIMIT = 60 * 1024 * 1024

NT_DIMS = (((1,), (1,)), ((), ()))
TN_DIMS = (((0,), (0,)), ((), ()))


def _cparams(*sem):
    return pltpu.CompilerParams(dimension_semantics=sem, vmem_limit_bytes=VMEM_LIMIT)


def _bdot(a, b):
    return jnp.dot(a.astype(BF16), b.astype(BF16), preferred_element_type=F32)


def _bdot_nt(a, b):
    return lax.dot_general(a.astype(BF16), b.astype(BF16), NT_DIMS, preferred_element_type=F32)


def _rms(x, g):
    ms = jnp.mean(x * x, axis=-1, keepdims=True)
    return x * lax.rsqrt(ms + EPS) * g


def _mod_chunk(mod_ref, i):
    return mod_ref[:, i * D_MODEL:(i + 1) * D_MODEL]


def _const_spec(shape):
    return pl.BlockSpec(shape, lambda *_: (0,) * len(shape))


def _f32_weight_spec(shape):
    assert shape[0] == 1
    return pl.BlockSpec((None,) + tuple(shape[1:]), lambda *_: (0, 0, 0),
                        pipeline_mode=pl.Buffered(1))


def _cast_once(w_ref, wb_ref):
    @pl.when(pl.program_id(0) == 0)
    def _():
        wb_ref[...] = w_ref[...].astype(BF16)


def _mod_kernel(cond_ref, w_ref, b_ref, o_ref):
    cnd = cond_ref[...]
    s = cnd / (1.0 + jnp.exp(-cnd))
    o_ref[...] = _bdot(s, w_ref[...]) + b_ref[...]


def _modulation(cond8, mod_w, mod_b):
    depth = mod_w.shape[0]
    n = mod_w.shape[2]
    tn = 1536
    return pl.pallas_call(
        _mod_kernel,
        grid=(depth, n // tn),
        in_specs=[
            pl.BlockSpec((8, D_MODEL), lambda l, j: (0, 0)),
            pl.BlockSpec((None, D_MODEL, tn), lambda l, j: (l, 0, j)),
            pl.BlockSpec((None, 1, tn), lambda l, j: (l, 0, j)),
        ],
        out_specs=pl.BlockSpec((None, 8, tn), lambda l, j: (l, 0, j)),
        out_shape=jax.ShapeDtypeStruct((depth, 8, n), F32),
        compiler_params=_cparams("arbitrary", "arbitrary"),
        name="modulation",
    )(cond8, mod_w, mod_b.reshape(depth, 1, n))


class _Group:
    def __init__(self, b, n):
        self.b, self.n, self.t = b, n, b * n


class _Stream:
    def __init__(self, ctx, lat, tm):
        assert ctx.t % tm == 0 and lat.n % tm == 0 and tm % ctx.n == 0
        self.ctx, self.lat, self.tm = ctx, lat, tm
        self.t = ctx.t + lat.t
        self.ctx_tiles = ctx.t // tm
        self.tiles = self.t // tm

    def is_lat(self):
        return pl.program_id(0) >= self.ctx_tiles

    def spec(self, width):
        return pl.BlockSpec((self.tm, width), lambda t: (t, 0))

    def ctx_spec(self, width):
        last = self.ctx_tiles - 1
        return pl.BlockSpec((self.tm, width), lambda t: (jnp.minimum(t, last), 0))

    def lat_spec(self, width):
        first = self.ctx_tiles
        return pl.BlockSpec((self.tm, width), lambda t: (jnp.maximum(t - first, 0), 0))

    def mod_spec(self):
        first, per = self.ctx_tiles, self.lat.n // self.tm
        return pl.BlockSpec((None, 1, MOD_CHUNKS * D_MODEL),
                            lambda t: (jnp.where(t < first, 0, 1 + (t - first) // per), 0, 0))

    def seq_len(self):
        return jnp.where(self.is_lat(), self.lat.n, self.ctx.n)


def _row_spec(tm, width):
    return pl.BlockSpec((tm, width), lambda t: (t, 0))


def _rope(x, cos, sin_lo, sin_hi):
    return (x * cos + pltpu.roll(x, LANES - 16, axis=1) * sin_lo
            + pltpu.roll(x, 16, axis=1) * sin_hi)


def _premix0_kernel(xc_ref, xl_ref, mod_ref, g_ref, w_ref, cos_ref, slo_ref, shi_ref,
                    zc_ref, q_ref, kv_ref, kt_ref, vt_ref, wb_ref, *, stream):
    _cast_once(w_ref, wb_ref)
    is_lat = stream.is_lat()
    x = jnp.where(is_lat, xl_ref[...], xc_ref[...])
    sh1, sc1 = _mod_chunk(mod_ref, 0), _mod_chunk(mod_ref, 1)
    h = _rms(x, g_ref[0:1, :]) * (1.0 + sc1) + sh1
    z = _bdot(h, wb_ref[...])
    c3 = 3 * CONV_DIM
    zc_ref[...] = z[:, :c3]
    scale = HEAD_DIM ** -0.5 * LOG2E
    cos, slo, shi = cos_ref[...], slo_ref[...], shi_ref[...]
    for j in range(QD // LANES):
        qs = z[:, c3 + j * LANES:c3 + (j + 1) * LANES]
        q_ref[:, j * LANES:(j + 1) * LANES] = (_rope(qs, cos, slo, shi) * scale).astype(BF16)
    kv_ref[:, :KD] = _rope(z[:, c3 + QD:c3 + QD + KD], cos, slo, shi)
    kv_ref[:, KD:] = z[:, c3 + QD + KD:]

    @pl.when(jnp.logical_not(is_lat))
    def _():
        n = kt_ref.shape[2]
        for j in range(kt_ref.shape[0]):
            kt_ref[j] = z[j * n:(j + 1) * n, c3 + QD:c3 + QD + KD].T
            vt_ref[j] = z[j * n:(j + 1) * n, c3 + QD + KD:].T


def _premix0(stream, x_ctx, x_lat, mod, norm_g, w_in, rope_tabs):
    tm, ctx, lat = stream.tm, stream.ctx, stream.lat
    first, per, last = stream.ctx_tiles, lat.n // tm, stream.ctx_tiles - 1
    rope_spec = pl.BlockSpec((tm, LANES),
                             lambda t: (jnp.where(t < first, 0, 1 + (t - first) % per), 0))
    cache_spec = pl.BlockSpec((tm // ctx.n, KD, ctx.n), lambda t: (jnp.minimum(t, last), 0, 0))
    return pl.pallas_call(
        functools.partial(_premix0_kernel, stream=stream),
        grid=(stream.tiles,),
        in_specs=[stream.ctx_spec(D_MODEL), stream.lat_spec(D_MODEL), stream.mod_spec(),
                  _const_spec((4, D_MODEL)), _f32_weight_spec(w_in.shape)] + [rope_spec] * 3,
        out_specs=[stream.spec(3 * CONV_DIM), stream.spec(QD), stream.spec(2 * KD),
                   cache_spec, cache_spec],
        out_shape=[jax.ShapeDtypeStruct((stream.t, 3 * CONV_DIM), F32),
                   jax.ShapeDtypeStruct((stream.t, QD), BF16),
                   jax.ShapeDtypeStruct((stream.t, 2 * KD), F32),
                   jax.ShapeDtypeStruct((ctx.b, KD, ctx.n), F32),
                   jax.ShapeDtypeStruct((ctx.b, KD, ctx.n), F32)],
        scratch_shapes=[pltpu.VMEM(w_in.shape[1:], BF16)],
        compiler_params=_cparams("arbitrary"),
        name="premix0",
    )(x_ctx, x_lat, mod, norm_g, w_in, *rope_tabs)


def _rope_tables(n, identity_rows):
    rows = n // GRID_W
    pos_r = jnp.repeat(jnp.arange(rows), GRID_W)
    pos_c = jnp.tile(jnp.arange(GRID_W), rows)
    half = HEAD_DIM // 2
    quarter = half // 2
    inv = ROPE_BASE ** (-(jnp.arange(quarter, dtype=F32) * 2.0 / half))

    def cs(pos):
        ang = pos.astype(F32)[:, None] * inv[None, :]
        return jnp.cos(ang), jnp.sin(ang)

    cr, sr = cs(pos_r)
    cc, sc = cs(pos_c)
    zero = jnp.zeros_like(sr)
    cos = jnp.concatenate([cr, cr, cc, cc], axis=1)
    sin_lo = jnp.concatenate([-sr, zero, -sc, zero], axis=1)
    sin_hi = jnp.concatenate([zero, sr, zero, sc], axis=1)
    rep = LANES // HEAD_DIM
    tables = []
    for t, ident in ((cos, 1.0), (sin_lo, 0.0), (sin_hi, 0.0)):
        head = jnp.full((identity_rows, LANES), ident, F32)
        tables.append(jnp.concatenate([head, jnp.tile(t, (1, rep))], axis=0))
    return tuple(tables)


def _attention_operands(k_all, v_all):
    assert KD == LANES == 2 * HEAD_DIM and GQA_GROUP == 4
    lane = lax.broadcasted_iota(jnp.int32, (1, LANES), 1)
    sub = lax.broadcasted_iota(jnp.int32, (LANES, 1), 0)
    v_t = v_all.T
    k_swapped = pltpu.roll(k_all, HEAD_DIM, axis=1)
    ops = []
    for g in range(N_KV_HEADS):
        k_low, k_high = (k_all, k_swapped) if g == 0 else (k_swapped, k_all)
        kz_even = jnp.where(lane < HEAD_DIM, k_low, 0.0).astype(BF16)
        kz_odd = jnp.where(lane >= HEAD_DIM, k_high, 0.0).astype(BF16)
        own = (sub < HEAD_DIM) if g == 0 else (sub >= HEAD_DIM)
        v_ext_t = jnp.where(own, v_t, 1.0).astype(BF16)
        ops.append((kz_even, kz_odd, v_ext_t))
    return ops


def _concat_operands(parts):
    return [(jnp.concatenate([p[g][0] for p in parts], axis=0),
             jnp.concatenate([p[g][1] for p in parts], axis=0),
             jnp.concatenate([p[g][2] for p in parts], axis=1)) for g in range(N_KV_HEADS)]


def _group_heads(g):
    return [4 * g, 4 * g + 2, 4 * g + 1, 4 * g + 3]


def _scores(sink_ref, q_ref, operands, bias_t, g):
    m = q_ref.shape[0]
    kz_even, kz_odd, _ = operands[g]
    qq = jnp.concatenate([q_ref[:, (2 * g) * LANES:(2 * g + 1) * LANES],
                          q_ref[:, (2 * g + 1) * LANES:(2 * g + 2) * LANES]], axis=0)
    s = jnp.concatenate([_bdot_nt(kz_even, qq), _bdot_nt(kz_odd, qq)], axis=1)
    if bias_t is not None:
        nb = bias_t.shape[0]
        s = jnp.concatenate([s[:nb] + jnp.concatenate([bias_t] * GQA_GROUP, axis=1), s[nb:]],
                            axis=0)
    sink = jnp.concatenate([jnp.full((1, m), sink_ref[h] * LOG2E, F32) for h in _group_heads(g)],
                           axis=1)
    mx = jnp.maximum(jnp.max(s, axis=0, keepdims=True), sink)
    return s, sink, mx


def _weighted_values(s, sink, mx, operands, g, m):
    v_ext_t = operands[g][2]
    p = jnp.exp2(s - mx).astype(BF16)
    oe = jnp.dot(v_ext_t, p, preferred_element_type=F32)
    other = (1 - g) * HEAD_DIM
    den = oe[other:other + 1] + jnp.exp2(sink - mx)
    o_g = oe[g * HEAD_DIM:(g + 1) * HEAD_DIM] / den
    return {h: o_g[:, i * m:(i + 1) * m] for i, h in enumerate(_group_heads(g))}


def _sink_attention(sink_ref, blocks):
    chains = [(blk, g) for blk in range(len(blocks)) for g in range(N_KV_HEADS)]
    outs = [dict() for _ in blocks]

    def scores(chain):
        blk, g = chain
        q_ref, operands, bias_t, _ = blocks[blk]
        return _scores(sink_ref, q_ref, operands, bias_t, g)

    ahead = scores(chains[0])
    for i, (blk, g) in enumerate(chains):
        current = ahead
        if i + 1 < len(chains):
            ahead = scores(chains[i + 1])
        q_ref, operands, _, o_ref = blocks[blk]
        outs[blk].update(_weighted_values(*current, operands, g, q_ref.shape[0]))
        if g == N_KV_HEADS - 1:
            for j in range(N_Q_HEADS // 2):
                pair_t = jnp.concatenate([outs[blk][2 * j], outs[blk][2 * j + 1]], axis=0)
                o_ref[:, j * LANES:(j + 1) * LANES] = pair_t.T.astype(BF16)


def _ctx_attn_kernel(sink_ref, q_ref, kv_ref, o_ref):
    operands = _attention_operands(kv_ref[:, :KD], kv_ref[:, KD:])
    _sink_attention(sink_ref, [(q_ref, operands, None, o_ref)])


def _ctx_attention(grp, q, kv, sink):
    n = grp.n
    return pl.pallas_call(
        _ctx_attn_kernel,
        grid=(grp.b,),
        in_specs=[pl.BlockSpec(memory_space=pltpu.SMEM), _row_spec(n, QD), _row_spec(n, 2 * KD)],
        out_specs=_row_spec(n, QD),
        out_shape=jax.ShapeDtypeStruct((grp.t, QD), BF16),
        compiler_params=_cparams("arbitrary"),
        name="ctx_attention",
    )(sink, q, kv)


def _lat_attn_kernel(sink_ref, q_ref, kvp_ref, kvc_ref, kvn_ref, kc_ref, vc_ref, o_ref, *, n):
    band = jnp.concatenate([kvp_ref[...], kvc_ref[...], kvn_ref[...]], axis=0)
    kj = lax.broadcasted_iota(jnp.int32, (3 * ATTN_BLOCK, ATTN_BLOCK), 0)
    qi = lax.broadcasted_iota(jnp.int32, (3 * ATTN_BLOCK, ATTN_BLOCK), 1)
    rel = kj - ATTN_BLOCK - qi
    ctx_ops = _attention_operands(kc_ref[...], vc_ref[...])
    band_ops = [_attention_operands(band[i * ATTN_BLOCK:(i + 1) * ATTN_BLOCK, :KD],
                                    band[i * ATTN_BLOCK:(i + 1) * ATTN_BLOCK, KD:])
                for i in range(ATTN_BLOCKS_PER_STEP + 2)]
    blocks = []
    for j in range(ATTN_BLOCKS_PER_STEP):
        blk = pl.program_id(1) * ATTN_BLOCKS_PER_STEP + j
        kpos = (blk - 1) * ATTN_BLOCK + kj
        valid = (jnp.abs(rel) <= WINDOW) & (kpos >= 0) & (kpos < n)
        bias_t = jnp.where(valid, 0.0, NEG_INF)
        operands = _concat_operands(band_ops[j:j + 3] + [ctx_ops])
        rows = pl.ds(j * ATTN_BLOCK, ATTN_BLOCK)
        blocks.append((q_ref.at[rows], operands, bias_t, o_ref.at[rows]))
    _sink_attention(sink_ref, blocks)


def _lat_attention(grp, row0, q, kv, k_ctx, v_ctx, sink):
    nb = grp.n // ATTN_BLOCK
    per = ATTN_BLOCKS_PER_STEP
    steps = nb // per
    past = k_ctx.shape[1]
    assert row0 % (per * ATTN_BLOCK) == 0
    blk0, step0 = row0 // ATTN_BLOCK, row0 // (per * ATTN_BLOCK)

    def edge(off):
        return pl.BlockSpec((ATTN_BLOCK, 2 * KD),
                            lambda b, i: (blk0 + b * nb + jnp.clip(i * per + off, 0, nb - 1), 0))

    ctx_spec = pl.BlockSpec((None, past, KD), lambda b, i: (b, 0, 0))
    return pl.pallas_call(
        functools.partial(_lat_attn_kernel, n=grp.n),
        grid=(grp.b, steps),
        in_specs=[pl.BlockSpec(memory_space=pltpu.SMEM),
                  pl.BlockSpec((per * ATTN_BLOCK, QD), lambda b, i: (step0 + b * steps + i, 0)),
                  edge(-1),
                  pl.BlockSpec((per * ATTN_BLOCK, 2 * KD), lambda b, i: (step0 + b * steps + i, 0)),
                  edge(per), ctx_spec, ctx_spec],
        out_specs=pl.BlockSpec((per * ATTN_BLOCK, QD), lambda b, i: (b * steps + i, 0)),
        out_shape=jax.ShapeDtypeStruct((grp.t, QD), BF16),
        compiler_params=_cparams("arbitrary", "arbitrary"),
        name="lat_attention",
    )(sink, q, kv, kv, kv, k_ctx, v_ctx)


class _TailWeights:
    def __init__(self, layer, wo_hbm, w1_hbm, w2_hbm, wo_s, w1_s, w2_s, stage1, stage2, sem):
        self.layer = layer
        self.wo_hbm, self.w1_hbm, self.w2_hbm = wo_hbm, w1_hbm, w2_hbm
        self.wo_s, self.w1_s, self.w2_s = wo_s, w1_s, w2_s
        self.stage1, self.stage2, self.sem = stage1, stage2, sem

    def _w1_copy(self, p):
        src = self.w1_hbm.at[self.layer, :, pl.ds(p * WEIGHT_PIECE, WEIGHT_PIECE)]
        return pltpu.make_async_copy(src, self.stage1.at[p % 2], self.sem.at[0, p % 2])

    def _w2_copy(self, p):
        src = self.w2_hbm.at[self.layer, pl.ds(p * WEIGHT_PIECE, WEIGHT_PIECE), :]
        return pltpu.make_async_copy(src, self.stage2.at[p % 2], self.sem.at[1, p % 2])

    def _wo_copy(self, p):
        src = self.wo_hbm.at[0, pl.ds(p * WEIGHT_PIECE, WEIGHT_PIECE), :]
        return pltpu.make_async_copy(src, self.stage2.at[p % 2], self.sem.at[1, p % 2])

    def start(self):
        self._wo_copy(0).start()
        self._wo_copy(1).start()
        self._w1_copy(0).start()
        self._w1_copy(1).start()

    def fetch_out_proj(self):
        pieces = D_MODEL // WEIGHT_PIECE
        for p in range(pieces):
            self._wo_copy(p).wait()
            self.wo_s[p * WEIGHT_PIECE:(p + 1) * WEIGHT_PIECE, :] = self.stage2[p % 2].astype(BF16)
            if p + 2 < pieces:
                self._wo_copy(p + 2).start()
            else:
                self._w2_copy(p + 2 - pieces).start()

    def fetch_mlp_chunk(self, j):
        per = MLP_CHUNK // WEIGHT_PIECE
        total = MLP_CHUNKS * per
        for p in range(j * per, (j + 1) * per):
            part = slice((p % per) * WEIGHT_PIECE, (p % per + 1) * WEIGHT_PIECE)
            self._w1_copy(p).wait()
            self.w1_s[j, :, part] = self.stage1[p % 2].astype(BF16)
            if p + 2 < total:
                self._w1_copy(p + 2).start()
            self._w2_copy(p).wait()
            self.w2_s[j, part, :] = self.stage2[p % 2].astype(BF16)
            if p + 2 < total:
                self._w2_copy(p + 2).start()


def _tail_weight_scratch():
    assert MLP_CHUNK % WEIGHT_PIECE == 0 and D_MODEL // WEIGHT_PIECE >= 2
    return [pltpu.VMEM((D_MODEL, D_MODEL), BF16),
            pltpu.VMEM((MLP_CHUNKS, D_MODEL, MLP_CHUNK), BF16),
            pltpu.VMEM((MLP_CHUNKS, MLP_CHUNK, D_MODEL), BF16),
            pltpu.VMEM((2, D_MODEL, WEIGHT_PIECE), F32),
            pltpu.VMEM((2, WEIGHT_PIECE, D_MODEL), F32),
            pltpu.SemaphoreType.DMA((2, 2))]


def _first_step_streams_weights(body, weights):
    first = pl.program_id(0) == 0

    @pl.when(first)
    def _():
        weights.start()
        body(True)

    @pl.when(jnp.logical_not(first))
    def _():
        body(False)


def _mlp_tail(x, mod_ref, g_ref, weights, streaming):
    sh2, sc2, gt2 = _mod_chunk(mod_ref, 3), _mod_chunk(mod_ref, 4), _mod_chunk(mod_ref, 5)
    hb = (_rms(x, g_ref[2:3, :]) * (1.0 + sc2) + sh2).astype(BF16)
    acc = None
    for j in range(MLP_CHUNKS):
        if streaming:
            weights.fetch_mlp_chunk(j)
        a = jnp.dot(hb, weights.w1_s[j], preferred_element_type=F32)
        a = jnp.maximum(a, 0.0)
        part = jnp.dot((a * a).astype(BF16), weights.w2_s[j], preferred_element_type=F32)
        acc = part if acc is None else acc + part
    return x + gt2 * _rms(acc, g_ref[3:4, :])


def _postmix0_kernel(zc_ref, zp_ref, zn_ref, atc_ref, atl_ref, xc_ref, xl_ref, mod_ref, g_ref,
                     cw_ref, wo_hbm, w1_hbm, w2_hbm, o_ref, *scratch, stream):
    weights = _TailWeights(0, wo_hbm, w1_hbm, w2_hbm, *scratch)
    _first_step_streams_weights(
        functools.partial(_postmix0_body, zc_ref, zp_ref, zn_ref, atc_ref, atl_ref, xc_ref, xl_ref,
                          mod_ref, g_ref, cw_ref, o_ref, weights, stream), weights)


def _postmix0_body(zc_ref, zp_ref, zn_ref, atc_ref, atl_ref, xc_ref, xl_ref, mod_ref, g_ref,
                   cw_ref, o_ref, weights, stream, streaming):
    c, tm = CONV_DIM, stream.tm
    is_lat, n = stream.is_lat(), stream.seq_len()
    zc = zc_ref[...]
    u = zc[:, c:2 * c] * zc[:, 2 * c:]
    u_before = zp_ref[7:8, c:2 * c] * zp_ref[7:8, 2 * c:]
    u_after = zn_ref[0:1, c:2 * c] * zn_ref[0:1, 2 * c:]
    row = lax.broadcasted_iota(jnp.int32, (tm, 1), 0)
    pos = (pl.program_id(0) * tm + row) & (n - 1)
    u_prev = jnp.where(row == 0, u_before, pltpu.roll(u, 1, axis=0))
    u_prev = jnp.where(pos == 0, 0.0, u_prev)
    u_next = jnp.where(row == tm - 1, u_after, pltpu.roll(u, tm - 1, axis=0))
    u_next = jnp.where(pos == n - 1, 0.0, u_next)
    conv = u_prev * cw_ref[0:1, :] + u * cw_ref[1:2, :] + u_next * cw_ref[2:3, :]
    if streaming:
        weights.fetch_out_proj()
    attn = jnp.where(is_lat, atl_ref[...], atc_ref[...])
    mix = (_bdot(zc[:, :c] * conv, weights.wo_s[:c, :])
           + jnp.dot(attn, weights.wo_s[c:, :], preferred_element_type=F32))
    gt1 = _mod_chunk(mod_ref, 2)
    x1 = jnp.where(is_lat, xl_ref[...], xc_ref[...]) + gt1 * _rms(mix, g_ref[1:2, :])
    o_ref[...] = _mlp_tail(x1, mod_ref, g_ref, weights, streaming)


def _postmix0(stream, zc, attn_ctx, attn_lat, x_ctx, x_lat, mod, norm_g, conv_w, w_out, w1, w2):
    tm = stream.tm
    for n in (stream.ctx.n, stream.lat.n):
        assert n & (n - 1) == 0 and stream.ctx.t % n == 0
    r8 = tm // 8
    last8 = stream.t // 8 - 1
    hbm = pl.BlockSpec(memory_space=pl.ANY)
    return pl.pallas_call(
        functools.partial(_postmix0_kernel, stream=stream),
        grid=(stream.tiles,),
        in_specs=[stream.spec(3 * CONV_DIM),
                  pl.BlockSpec((8, 3 * CONV_DIM), lambda t: (jnp.maximum(t * r8 - 1, 0), 0)),
                  pl.BlockSpec((8, 3 * CONV_DIM), lambda t: (jnp.minimum((t + 1) * r8, last8), 0)),
                  stream.ctx_spec(QD), stream.lat_spec(QD),
                  stream.ctx_spec(D_MODEL), stream.lat_spec(D_MODEL), stream.mod_spec(),
                  _const_spec((4, D_MODEL)), _const_spec((3, CONV_DIM)), hbm, hbm, hbm],
        out_specs=stream.spec(D_MODEL),
        out_shape=jax.ShapeDtypeStruct((stream.t, D_MODEL), F32),
        scratch_shapes=_tail_weight_scratch(),
        compiler_params=_cparams("arbitrary"),
        name="postmix0_mlp",
    )(zc, zc, zc, attn_ctx, attn_lat, x_ctx, x_lat, mod, norm_g, conv_w, w_out, w1, w2)


def _premix1_kernel(x_ref, mod_ref, g_ref, w_ref, wg_ref, gb_ref,
                    q_ref, k_ref, v_ref, og_ref, gf_ref, gbk_ref, wb_ref, wr_ref):
    @pl.when(pl.program_id(0) == 0)
    def _():
        for c in range(0, GLA_MAIN, GLA_QK):
            wb_ref[:, c:c + GLA_QK] = w_ref[c:c + GLA_QK, :].T.astype(BF16)
        wr_ref[...] = w_ref[GLA_MAIN:, :].astype(BF16)

    sh1, sc1 = _mod_chunk(mod_ref, 0), _mod_chunk(mod_ref, 1)
    hb = (_rms(x_ref[...], g_ref[0:1, :]) * (1.0 + sc1) + sh1).astype(BF16)
    z = jnp.dot(hb, wb_ref[...], preferred_element_type=F32)
    q_ref[...] = z[:, :GLA_QK] * (GLA_DK ** -0.5)
    k_ref[...] = z[:, GLA_QK:2 * GLA_QK]
    v_ref[...] = z[:, 2 * GLA_QK:2 * GLA_QK + GLA_VD].astype(BF16)
    og_ref[...] = z[:, 2 * GLA_QK + GLA_VD:]
    r = _bdot_nt(hb, wr_ref[...])
    pre = _bdot(r, wg_ref[...]) + gb_ref[...]
    soft = jnp.log2(1.0 + jnp.exp2(jnp.abs(pre) * (-LOG2E)))
    gate = (jnp.minimum(pre, 0.0) * LOG2E - soft) * (1.0 / GLA_GATE_NORM)
    gf_ref[...] = gate[:, :GLA_QK]
    gbk_ref[...] = gate[:, GLA_QK:]


def _premix1(stream, x2d, mod, norm_g, w_in_t, w_gate, gate_bias):
    t, tm = stream.t, stream.tm
    w_in = w_in_t
    assert w_in.shape == (1, GLA_MAIN + 2 * GLA_RANK, D_MODEL)
    return pl.pallas_call(
        _premix1_kernel,
        grid=(t // tm,),
        in_specs=[_row_spec(tm, D_MODEL), stream.mod_spec(), _const_spec((4, D_MODEL)),
                  _f32_weight_spec(w_in.shape), _const_spec((2 * GLA_RANK, 2 * GLA_QK)),
                  _const_spec((1, 2 * GLA_QK))],
        out_specs=[_row_spec(tm, GLA_QK), _row_spec(tm, GLA_QK), _row_spec(tm, GLA_VD),
                   _row_spec(tm, GLA_VD), _row_spec(tm, GLA_QK), _row_spec(tm, GLA_QK)],
        out_shape=[jax.ShapeDtypeStruct((t, GLA_QK), F32), jax.ShapeDtypeStruct((t, GLA_QK), F32),
                   jax.ShapeDtypeStruct((t, GLA_VD), BF16),
                   jax.ShapeDtypeStruct((t, GLA_VD), F32), jax.ShapeDtypeStruct((t, GLA_QK), F32),
                   jax.ShapeDtypeStruct((t, GLA_QK), F32)],
        scratch_shapes=[pltpu.VMEM((D_MODEL, GLA_MAIN), BF16),
                        pltpu.VMEM((2 * GLA_RANK, D_MODEL), BF16)],
        compiler_params=_cparams("arbitrary"),
        name="premix1",
    )(x2d, mod, norm_g, w_in, w_gate, gate_bias)


def _split3(x):
    hi = x.astype(BF16)
    r1 = x - hi.astype(F32)
    mid = r1.astype(BF16)
    lo = (r1 - mid.astype(F32)).astype(BF16)
    return hi, mid, lo


def _level_exponent(b, s, reverse):
    idx = s if reverse else s - 1
    if s >= 8:
        n = GLA_TILE // (2 * s)
        b4 = b.reshape(n, 2, s, GLA_DK)
        first, second = b4[:, 0:1], b4[:, 1:2]
        r = (second[:, :, 0:1] if reverse else first[:, :, s - 1:s])
        parts = [first - r, r - second] if reverse else [r - first, second - r]
        return jnp.concatenate(parts, axis=1).reshape(GLA_TILE, GLA_DK)
    b8 = b.reshape(GLA_TILE // 8, 8, GLA_DK)
    sub = lax.broadcasted_iota(jnp.int32, (1, 8, 1), 1)
    if s == 4:
        r = b8[:, idx:idx + 1, :]
    else:
        assert s == 2
        r = jnp.where(sub < 4, b8[:, idx:idx + 1, :], b8[:, 4 + idx:5 + idx, :])
    in_second = (sub // s) % 2 == 1
    sign = jnp.where(in_second != reverse, 1.0, -1.0)
    return ((b8 - r) * sign).reshape(GLA_TILE, GLA_DK)


def _pair_matrix(q, k, g_f, g_b, b_f, b_b, lvl):
    half = GLA_TILE // 2
    lo, hi = slice(0, half), slice(half, GLA_TILE)
    qb, kb = q.astype(BF16), k.astype(BF16)
    row = lax.broadcasted_iota(jnp.int32, (GLA_TILE, 1), 0)
    odd = row % 2 == 1
    own = jnp.sum(q * k, axis=-1, keepdims=True)
    k_adj = jnp.where(odd, pltpu.roll(k, 1, axis=0), pltpu.roll(k, GLA_TILE - 1, axis=0))
    adj = jnp.sum(q * jnp.exp2(jnp.where(odd, g_f, g_b)) * k_adj, axis=-1, keepdims=True)
    blocks = [jnp.where(lvl == 0, 2.0 * own[rows], jnp.where(lvl == 1, adj[rows], 0.0))
              for rows in (lo, hi)]
    cross_f = cross_b = None
    s, level = 2, 2
    while s < GLA_TILE:
        if s >= 16:
            n = GLA_TILE // (2 * s)

            def halves(x):
                x4 = x.reshape(n, 2, s, GLA_DK)
                return x4[:, 0:1], x4[:, 1:2]

            def rows(first, second):
                return jnp.concatenate([first, second], axis=1).reshape(GLA_TILE, GLA_DK)

            (bf1, bf2), (bb1, bb2) = halves(b_f), halves(b_b)
            (q1, q2), (k1, k2) = halves(qb), halves(kb)
            r_f, r_b = bf1[:, :, s - 1:s], bb2[:, :, 0:1]
            zero = jnp.zeros_like(q1)
            lhs = jnp.concatenate(
                [rows(zero, q2 * jnp.exp2(bf2 - r_f).astype(BF16)),
                 rows(q1 * jnp.exp2(bb1 - r_b).astype(BF16), zero)], axis=1)
            rhs = jnp.concatenate(
                [rows(k1 * jnp.exp2(r_f - bf1).astype(BF16), zero),
                 rows(zero, k2 * jnp.exp2(r_b - bb2).astype(BF16))], axis=1)
        else:
            second = (row // s) % 2 == 1
            f_f = jnp.exp2(_level_exponent(b_f, s, False)).astype(BF16)
            f_b = jnp.exp2(_level_exponent(b_b, s, True)).astype(BF16)
            u = qb * jnp.where(second, f_f, f_b)
            w = kb * jnp.where(second, f_b, f_f)
            zero = jnp.zeros_like(u)
            lhs = jnp.concatenate([jnp.where(second, u, zero), jnp.where(second, zero, u)], axis=1)
            rhs = jnp.concatenate([jnp.where(second, zero, w), jnp.where(second, w, zero)], axis=1)
        pairs = _bdot_nt(lhs, rhs)
        if s == half:
            cross_f, cross_b = pairs[hi, lo], pairs[lo, hi]
        else:
            blocks = [jnp.where(lvl == level, pairs[lo, lo], blocks[0]),
                      jnp.where(lvl == level, pairs[hi, hi], blocks[1])]
        s, level = 2 * s, level + 1
    top = jnp.concatenate([blocks[0].astype(BF16), cross_b.astype(BF16)], axis=1)
    bottom = jnp.concatenate([cross_f.astype(BF16), blocks[1].astype(BF16)], axis=1)
    return jnp.concatenate([top, bottom], axis=0)


def _carry_state(q, k, b, v, s_ref, reverse):
    edge = 0 if reverse else GLA_TILE - 1
    b_last = b[edge:edge + 1, :]
    qe = (q * jnp.exp2(b)).astype(BF16)
    ke = (k * jnp.exp2(b_last - b)).astype(BF16)
    st = s_ref[...]
    s_ref[...] = st * jnp.exp2(b_last) + lax.dot_general(
        v, ke, TN_DIMS, preferred_element_type=F32)
    return _bdot_nt(qe, st)


def _gla_kernel(*refs, zero_init, nt):
    if zero_init:
        (tri_ref, lvl_ref, q_ref, k_ref, gf_ref, gb_ref, v_ref,
         o_ref, sfo_ref, sbo_ref, sf_ref, sb_ref, bb_ref) = refs
        sf_ref[...] = jnp.zeros_like(sf_ref)
        sb_ref[...] = jnp.zeros_like(sb_ref)
    else:
        (tri_ref, lvl_ref, q_ref, k_ref, gf_ref, gb_ref, v_ref, s0f_ref, s0b_ref,
         o_ref, sfo_ref, sbo_ref, sf_ref, sb_ref, bb_ref) = refs
        for hh in range(GLA_HEADS_PER_STEP):
            sf_ref[hh] = s0f_ref[hh].T
            sb_ref[hh] = s0b_ref[hh].T

    def tile_rows(tile):
        r0 = tile * GLA_TILE
        return pl.ds(r0 if isinstance(r0, int) else pl.multiple_of(r0, GLA_TILE), GLA_TILE)

    heads = [(slice(hh * GLA_DK, (hh + 1) * GLA_DK), slice(hh * GLA_DV, (hh + 1) * GLA_DV), hh)
             for hh in range(GLA_HEADS_PER_STEP)]

    def forward_sweep(t, carry):
        rows = tile_rows(t)
        g_f, g_b = gf_ref[rows, :], gb_ref[rows, :]
        c = None
        for part in _split3(jnp.concatenate([g_f, g_b], axis=1)):
            term = jnp.dot(tri_ref[...], part, preferred_element_type=F32)
            c = term if c is None else c + term
        width = GLA_HEADS_PER_STEP * GLA_DK
        c_b = c[:, width:]
        b_f = c[:, :width]
        b_b = (c_b[GLA_TILE - 1:GLA_TILE, :] - c_b) + g_b
        bb_ref[t] = b_b
        for dk, dv, hh in heads:
            q, k, v = q_ref[rows, dk], k_ref[rows, dk], v_ref[rows, dv]
            att = _pair_matrix(q, k, g_f[:, dk], g_b[:, dk], b_f[:, dk], b_b[:, dk], lvl_ref[...])
            o_ref[rows, dv] = (jnp.dot(att, v, preferred_element_type=F32)
                               + _carry_state(q, k, b_f[:, dk], v, sf_ref.at[hh], False))
        return carry

    def backward_sweep(i, carry):
        t = nt - 1 - i
        rows = tile_rows(t)
        b_b = bb_ref[t]
        for dk, dv, hh in heads:
            o_ref[rows, dv] += _carry_state(q_ref[rows, dk], k_ref[rows, dk], b_b[:, dk],
                                            v_ref[rows, dv], sb_ref.at[hh], True)
        return carry

    if nt == 1:
        forward_sweep(0, 0)
        backward_sweep(0, 0)
    else:
        lax.fori_loop(0, nt, forward_sweep, 0)
        lax.fori_loop(0, nt, backward_sweep, 0)
    for hh in range(GLA_HEADS_PER_STEP):
        sfo_ref[hh] = sf_ref[hh].T
        sbo_ref[hh] = sb_ref[hh].T


def _gla_scan(grp, row0, q, k, v, gf, gb, s0f, s0b, tri, lvl):
    n = grp.n
    nt = n // GLA_TILE
    hp = GLA_HEADS_PER_STEP
    zero_init = s0f is None
    half = GLA_TILE // 2
    assert row0 % n == 0
    seq0 = row0 // n
    in_dk = pl.BlockSpec((n, hp * GLA_DK), lambda b, h: (seq0 + b, h))
    in_dv = pl.BlockSpec((n, hp * GLA_DV), lambda b, h: (seq0 + b, h))
    seq_dv = pl.BlockSpec((n, hp * GLA_DV), lambda b, h: (b, h))
    state_spec = pl.BlockSpec((None, hp, GLA_DK, GLA_DV), lambda b, h: (b, h, 0, 0))
    in_specs = [_const_spec((GLA_TILE, GLA_TILE)), _const_spec((half, half)),
                in_dk, in_dk, in_dk, in_dk, in_dv]
    args = [tri, lvl, q, k, gf, gb, v]
    if not zero_init:
        in_specs += [state_spec, state_spec]
        args += [s0f, s0b]
    state_shape = jax.ShapeDtypeStruct((grp.b, GLA_HEADS, GLA_DK, GLA_DV), F32)
    return pl.pallas_call(
        functools.partial(_gla_kernel, zero_init=zero_init, nt=nt),
        grid=(grp.b, GLA_HEADS // hp),
        in_specs=in_specs,
        out_specs=[seq_dv, state_spec, state_spec],
        out_shape=[jax.ShapeDtypeStruct((grp.t, GLA_VD), F32), state_shape, state_shape],
        scratch_shapes=[pltpu.VMEM((hp, GLA_DV, GLA_DK), F32), pltpu.VMEM((hp, GLA_DV, GLA_DK), F32),
                        pltpu.VMEM((nt, GLA_TILE, hp * GLA_DK), F32)],
        compiler_params=_cparams("arbitrary", "arbitrary"),
        name="gla_scan",
    )(*args)


def _gla_constants():
    half = GLA_TILE // 2
    i = jnp.arange(GLA_TILE)[:, None]
    j = jnp.arange(GLA_TILE)[None, :]
    tri = (j <= i).astype(BF16)
    ih, jh = i[:half], j[:, :half]
    x = jnp.bitwise_xor(ih, jh)
    lvl = sum((x >= (1 << p)).astype(jnp.int32) for p in range(half.bit_length() - 1))
    return tri, lvl


def _postmix1_kernel(oc_ref, ol_ref, og_ref, x_ref, mod_ref, g_ref, gn_ref, wo_hbm, w1_hbm, w2_hbm,
                     yc_ref, yl_ref, *scratch, stream):
    weights = _TailWeights(1, wo_hbm, w1_hbm, w2_hbm, *scratch)
    _first_step_streams_weights(
        functools.partial(_postmix1_body, oc_ref, ol_ref, og_ref, x_ref, mod_ref, g_ref, gn_ref,
                          yc_ref, yl_ref, weights, stream), weights)


def _postmix1_body(oc_ref, ol_ref, og_ref, x_ref, mod_ref, g_ref, gn_ref, yc_ref, yl_ref,
                   weights, stream, streaming):
    is_lat = stream.is_lat()
    gn = gn_ref[...]
    ys = []
    for h in range(GLA_HEADS):
        cols = slice(h * GLA_DV, (h + 1) * GLA_DV)
        o = _rms(jnp.where(is_lat, ol_ref[:, cols], oc_ref[:, cols]), gn)
        og = og_ref[:, cols]
        ys.append((o * (og / (1.0 + jnp.exp(-og)))).astype(BF16))
    if streaming:
        weights.fetch_out_proj()
    mix = None
    for h, y in enumerate(ys):
        part = jnp.dot(y, weights.wo_s[h * GLA_DV:(h + 1) * GLA_DV, :], preferred_element_type=F32)
        mix = part if mix is None else mix + part
    gt1 = _mod_chunk(mod_ref, 2)
    x1 = x_ref[...] + gt1 * _rms(mix, g_ref[1:2, :])
    y = _mlp_tail(x1, mod_ref, g_ref, weights, streaming)

    @pl.when(jnp.logical_not(is_lat))
    def _():
        yc_ref[...] = y

    @pl.when(is_lat)
    def _():
        yl_ref[...] = y


def _postmix1(stream, o_ctx, o_lat, og, x2d, mod, norm_g, gla_norm_g, w_out, w1, w2):
    hbm = pl.BlockSpec(memory_space=pl.ANY)
    return pl.pallas_call(
        functools.partial(_postmix1_kernel, stream=stream),
        grid=(stream.tiles,),
        in_specs=[stream.ctx_spec(GLA_VD), stream.lat_spec(GLA_VD), stream.spec(GLA_VD),
                  stream.spec(D_MODEL), stream.mod_spec(), _const_spec((4, D_MODEL)),
                  _const_spec((1, GLA_DV)), hbm, hbm, hbm],
        out_specs=[stream.ctx_spec(D_MODEL), stream.lat_spec(D_MODEL)],
        out_shape=[jax.ShapeDtypeStruct((stream.ctx.t, D_MODEL), F32),
                   jax.ShapeDtypeStruct((stream.lat.t, D_MODEL), F32)],
        scratch_shapes=_tail_weight_scratch(),
        compiler_params=_cparams("arbitrary"),
        name="postmix1_mlp",
    )(o_ctx, o_lat, og, x2d, mod, norm_g, gla_norm_g, w_out, w1, w2)


TOKEN_TILE = 512


def kernel(x_prompt, x_sample, cache_k, cache_v, state_fwd, state_bwd, c, c_ctx, mod_w, mod_b,
           norm_g, ab_w_in, conv_w, attn_sink, ab_w_out, gla_w_in, gla_gate_w, gla_gate_b,
           gla_norm_g, gla_w_out, mlp_w1, mlp_w2):
    b_ctx, n_ctx, _ = x_prompt.shape
    b_lat, n_lat, _ = x_sample.shape
    assert mod_w.shape[0] == 2 and ab_w_in.shape[0] == 1 and gla_w_in.shape[0] == 1
    assert 1 + b_lat <= 8

    cond8 = jnp.zeros((8, D_MODEL), F32).at[0].set(c_ctx).at[1:1 + b_lat].set(c)
    mod = _modulation(cond8, mod_w, mod_b)
    mods = [mod[l].reshape(8, 1, -1) for l in range(2)]

    w_gate = jnp.zeros((2 * GLA_RANK, 2 * GLA_QK), F32)
    w_gate = w_gate.at[:GLA_RANK, :GLA_QK].set(gla_gate_w[0, 0])
    w_gate = w_gate.at[GLA_RANK:, GLA_QK:].set(gla_gate_w[0, 1])
    tri, lvl = _gla_constants()
    p = {
        "conv_w": conv_w[0],
        "sink": attn_sink[0],
        "gla_w_in": jnp.swapaxes(gla_w_in, 1, 2),
        "gla_w_gate": w_gate.astype(BF16),
        "gla_gate_bias": gla_gate_b[0].reshape(1, 2 * GLA_QK),
        "gla_norm_g": gla_norm_g[0].reshape(1, GLA_DV),
    }

    ctx, lat = _Group(b_ctx, n_ctx), _Group(b_lat, n_lat)
    stream = _Stream(ctx, lat, TOKEN_TILE)
    x_ctx, x_lat = x_prompt.reshape(ctx.t, D_MODEL), x_sample.reshape(lat.t, D_MODEL)
    past = cache_k.shape[2]
    k_ctx = cache_k[:, 0].reshape(b_lat, past, KD)
    v_ctx = cache_v[:, 0].reshape(b_lat, past, KD)

    zc, q, kv, k_t, v_t = _premix0(stream, x_ctx, x_lat, mods[0], norm_g[0], ab_w_in,
                                   _rope_tables(n_lat, TOKEN_TILE))
    attn_ctx = _ctx_attention(ctx, q, kv, p["sink"])
    attn_lat = _lat_attention(lat, ctx.t, q, kv, k_ctx, v_ctx, p["sink"])
    x1 = _postmix0(stream, zc, attn_ctx, attn_lat, x_ctx, x_lat, mods[0], norm_g[0], p["conv_w"],
                   ab_w_out, mlp_w1, mlp_w2)
    gq, gk, gv, og, gf, gb = _premix1(stream, x1, mods[1], norm_g[1], p["gla_w_in"],
                                      p["gla_w_gate"], p["gla_gate_bias"])
    o_ctx, sf, sb = _gla_scan(ctx, 0, gq, gk, gv, gf, gb, None, None, tri, lvl)
    o_lat, _, _ = _gla_scan(lat, ctx.t, gq, gk, gv, gf, gb, state_fwd[:, 0], state_bwd[:, 0],
                            tri, lvl)
    y_ctx, y_lat = _postmix1(stream, o_ctx, o_lat, og, x1, mods[1], norm_g[1], p["gla_norm_g"],
                             gla_w_out, mlp_w1, mlp_w2)

    def cache_layout(t):
        t = t.reshape(b_ctx, 1, N_KV_HEADS, HEAD_DIM, n_ctx)
        return jnp.transpose(t, (0, 1, 4, 2, 3))

    return (y_ctx.reshape(x_prompt.shape), y_lat.reshape(x_sample.shape),
            cache_layout(k_t), cache_layout(v_t), sf[:, None], sb[:, None])
```

```python
import functools

import jax
import jax.numpy as jnp
from jax import lax
from jax.experimental import pallas as pl
from jax.experimental.pallas import tpu as pltpu

F32 = jnp.float32
BF16 = jnp.bfloat16

D_MODEL = 1024
MOD_CHUNKS = 6
EPS = 1e-6
CONV_DIM = 512
N_Q_HEADS = 8
N_KV_HEADS = 2
GQA_GROUP = 4
HEAD_DIM = 64
WINDOW = 128
ATTN_BLOCK = 128
ATTN_BLOCKS_PER_STEP = 16
CTX_SEQS_PER_STEP = 4
GRID_W = 64
ROPE_BASE = 10000.0
QD = N_Q_HEADS * HEAD_DIM
KD = N_KV_HEADS * HEAD_DIM
AB_IN = 3 * CONV_DIM + QD + 2 * KD
GLA_HEADS = 4
GLA_DK = 128
GLA_DV = 256
GLA_RANK = 16
GLA_GATE_NORM = 16.0
GLA_TILE = 256
GLA_SEQ_VMEM_BUDGET = 32 * 1024 * 1024
LOG2E = 1.4426950408889634
GLA_QK = GLA_HEADS * GLA_DK
GLA_VD = GLA_HEADS * GLA_DV
GLA_MAIN = 2 * GLA_QK + 2 * GLA_VD
D_FF = 4 * D_MODEL
MLP_CHUNK = 512
MLP_CHUNKS = D_FF // MLP_CHUNK
WEIGHT_PIECE = 256
TOKEN_TILE = 512
NEG_INF = -1e30
LANES = 128
VMEM_LIMIT = 60 * 1024 * 1024

NT_DIMS = (((1,), (1,)), ((), ()))
TN_DIMS = (((0,), (0,)), ((), ()))


def _cparams(*sem):
    return pltpu.CompilerParams(dimension_semantics=sem, vmem_limit_bytes=VMEM_LIMIT)


def _bdot(a, b):
    return jnp.dot(a.astype(BF16), b.astype(BF16), preferred_element_type=F32)


def _bdot_nt(a, b):
    return lax.dot_general(a.astype(BF16), b.astype(BF16), NT_DIMS, preferred_element_type=F32)


def _rms(x, g):
    ms = jnp.mean(x * x, axis=-1, keepdims=True)
    return x * lax.rsqrt(ms + EPS) * g


def _mod_chunk(mod_ref, i):
    return mod_ref[:, i * D_MODEL:(i + 1) * D_MODEL]


def _const_spec(shape):
    return pl.BlockSpec(shape, lambda *_: (0,) * len(shape))


def _f32_weight_spec(shape):
    assert shape[0] == 1
    return pl.BlockSpec((None,) + tuple(shape[1:]), lambda *_: (0, 0, 0),
                        pipeline_mode=pl.Buffered(1))


def _cast_once(w_ref, wb_ref):
    @pl.when(pl.program_id(0) == 0)
    def _():
        wb_ref[...] = w_ref[...].astype(BF16)


def _mod_kernel(cond_ref, w_ref, b_ref, o_ref):
    cnd = cond_ref[...]
    s = cnd / (1.0 + jnp.exp(-cnd))
    o_ref[...] = _bdot(s, w_ref[...]) + b_ref[...]


def _modulation(cond8, mod_w, mod_b):
    depth = mod_w.shape[0]
    n = mod_w.shape[2]
    tn = 1536
    return pl.pallas_call(
        _mod_kernel,
        grid=(depth, n // tn),
        in_specs=[
            pl.BlockSpec((8, D_MODEL), lambda l, j: (0, 0)),
            pl.BlockSpec((None, D_MODEL, tn), lambda l, j: (l, 0, j)),
            pl.BlockSpec((None, 1, tn), lambda l, j: (l, 0, j)),
        ],
        out_specs=pl.BlockSpec((None, 8, tn), lambda l, j: (l, 0, j)),
        out_shape=jax.ShapeDtypeStruct((depth, 8, n), F32),
        compiler_params=_cparams("arbitrary", "arbitrary"),
        name="modulation",
    )(cond8, mod_w, mod_b.reshape(depth, 1, n))


class _Group:
    def __init__(self, b, n):
        self.b, self.n, self.t = b, n, b * n


class _Stream:
    def __init__(self, ctx, lat, tm):
        assert ctx.t % tm == 0 and lat.n % tm == 0 and tm % ctx.n == 0
        self.ctx, self.lat, self.tm = ctx, lat, tm
        self.t = ctx.t + lat.t
        self.ctx_tiles = ctx.t // tm
        self.tiles = self.t // tm

    def is_lat(self):
        return pl.program_id(0) >= self.ctx_tiles

    def spec(self, width):
        return pl.BlockSpec((self.tm, width), lambda t: (t, 0))

    def ctx_spec(self, width):
        last = self.ctx_tiles - 1
        return pl.BlockSpec((self.tm, width), lambda t: (jnp.minimum(t, last), 0))

    def lat_spec(self, width):
        first = self.ctx_tiles
        return pl.BlockSpec((self.tm, width), lambda t: (jnp.maximum(t - first, 0), 0))

    def mod_spec(self):
        first, per = self.ctx_tiles, self.lat.n // self.tm
        return pl.BlockSpec((None, 1, MOD_CHUNKS * D_MODEL),
                            lambda t: (jnp.where(t < first, 0, 1 + (t - first) // per), 0, 0))

    def seq_len(self):
        return jnp.where(self.is_lat(), self.lat.n, self.ctx.n)


def _row_spec(tm, width):
    return pl.BlockSpec((tm, width), lambda t: (t, 0))


def _rope(x, cos, sin_lo, sin_hi):
    return (x * cos + pltpu.roll(x, LANES - 16, axis=1) * sin_lo
            + pltpu.roll(x, 16, axis=1) * sin_hi)


def _premix0_kernel(xc_ref, xl_ref, mod_ref, g_ref, w_ref, cos_ref, slo_ref, shi_ref,
                    zc_ref, q_ref, kv_ref, kt_ref, vt_ref, wb_ref, *, stream):
    _cast_once(w_ref, wb_ref)
    is_lat = stream.is_lat()
    x = jnp.where(is_lat, xl_ref[...], xc_ref[...])
    sh1, sc1 = _mod_chunk(mod_ref, 0), _mod_chunk(mod_ref, 1)
    h = _rms(x, g_ref[0:1, :]) * (1.0 + sc1) + sh1
    z = _bdot(h, wb_ref[...])
    c3 = 3 * CONV_DIM
    zc_ref[...] = z[:, :c3]
    scale = HEAD_DIM ** -0.5 * LOG2E
    cos, slo, shi = cos_ref[...], slo_ref[...], shi_ref[...]
    for j in range(QD // LANES):
        qs = z[:, c3 + j * LANES:c3 + (j + 1) * LANES]
        q_ref[:, j * LANES:(j + 1) * LANES] = (_rope(qs, cos, slo, shi) * scale).astype(BF16)
    kv_ref[:, :KD] = _rope(z[:, c3 + QD:c3 + QD + KD], cos, slo, shi)
    kv_ref[:, KD:] = z[:, c3 + QD + KD:]

    @pl.when(jnp.logical_not(is_lat))
    def _():
        n = kt_ref.shape[2]
        for j in range(kt_ref.shape[0]):
            kt_ref[j] = z[j * n:(j + 1) * n, c3 + QD:c3 + QD + KD].T
            vt_ref[j] = z[j * n:(j + 1) * n, c3 + QD + KD:].T


def _premix0(stream, x_ctx, x_lat, mod, norm_g, w_in, rope_tabs):
    tm, ctx, lat = stream.tm, stream.ctx, stream.lat
    first, per, last = stream.ctx_tiles, lat.n // tm, stream.ctx_tiles - 1
    rope_spec = pl.BlockSpec((tm, LANES),
                             lambda t: (jnp.where(t < first, 0, 1 + (t - first) % per), 0))
    cache_spec = pl.BlockSpec((tm // ctx.n, KD, ctx.n), lambda t: (jnp.minimum(t, last), 0, 0))
    return pl.pallas_call(
        functools.partial(_premix0_kernel, stream=stream),
        grid=(stream.tiles,),
        in_specs=[stream.ctx_spec(D_MODEL), stream.lat_spec(D_MODEL), stream.mod_spec(),
                  _const_spec((4, D_MODEL)), _f32_weight_spec(w_in.shape)] + [rope_spec] * 3,
        out_specs=[stream.spec(3 * CONV_DIM), stream.spec(QD), stream.spec(2 * KD),
                   cache_spec, cache_spec],
        out_shape=[jax.ShapeDtypeStruct((stream.t, 3 * CONV_DIM), F32),
                   jax.ShapeDtypeStruct((stream.t, QD), BF16),
                   jax.ShapeDtypeStruct((stream.t, 2 * KD), F32),
                   jax.ShapeDtypeStruct((ctx.b, KD, ctx.n), F32),
                   jax.ShapeDtypeStruct((ctx.b, KD, ctx.n), F32)],
        scratch_shapes=[pltpu.VMEM(w_in.shape[1:], BF16)],
        compiler_params=_cparams("arbitrary"),
        name="premix0",
    )(x_ctx, x_lat, mod, norm_g, w_in, *rope_tabs)


def _rope_tables(n, identity_rows):
    rows = n // GRID_W
    pos_r = jnp.repeat(jnp.arange(rows), GRID_W)
    pos_c = jnp.tile(jnp.arange(GRID_W), rows)
    half = HEAD_DIM // 2
    quarter = half // 2
    inv = ROPE_BASE ** (-(jnp.arange(quarter, dtype=F32) * 2.0 / half))

    def cs(pos):
        ang = pos.astype(F32)[:, None] * inv[None, :]
        return jnp.cos(ang), jnp.sin(ang)

    cr, sr = cs(pos_r)
    cc, sc = cs(pos_c)
    zero = jnp.zeros_like(sr)
    cos = jnp.concatenate([cr, cr, cc, cc], axis=1)
    sin_lo = jnp.concatenate([-sr, zero, -sc, zero], axis=1)
    sin_hi = jnp.concatenate([zero, sr, zero, sc], axis=1)
    rep = LANES // HEAD_DIM
    tables = []
    for t, ident in ((cos, 1.0), (sin_lo, 0.0), (sin_hi, 0.0)):
        head = jnp.full((identity_rows, LANES), ident, F32)
        tables.append(jnp.concatenate([head, jnp.tile(t, (1, rep))], axis=0))
    return tuple(tables)


def _attention_operands(k_all, v_all):
    assert KD == LANES == 2 * HEAD_DIM and GQA_GROUP == 4
    lane = lax.broadcasted_iota(jnp.int32, (1, LANES), 1)
    sub = lax.broadcasted_iota(jnp.int32, (LANES, 1), 0)
    v_t = v_all.T
    k_swapped = pltpu.roll(k_all, HEAD_DIM, axis=1)
    ops = []
    for g in range(N_KV_HEADS):
        k_low, k_high = (k_all, k_swapped) if g == 0 else (k_swapped, k_all)
        kz_even = jnp.where(lane < HEAD_DIM, k_low, 0.0).astype(BF16)
        kz_odd = jnp.where(lane >= HEAD_DIM, k_high, 0.0).astype(BF16)
        own = (sub < HEAD_DIM) if g == 0 else (sub >= HEAD_DIM)
        v_ext_t = jnp.where(own, v_t, 1.0).astype(BF16)
        ops.append((kz_even, kz_odd, v_ext_t))
    return ops


def _concat_operands(parts):
    return [(jnp.concatenate([p[g][0] for p in parts], axis=0),
             jnp.concatenate([p[g][1] for p in parts], axis=0),
             jnp.concatenate([p[g][2] for p in parts], axis=1)) for g in range(N_KV_HEADS)]


def _group_heads(g):
    return [4 * g, 4 * g + 2, 4 * g + 1, 4 * g + 3]


def _scores(sink_ref, q_ref, operands, bias_t, g):
    m = q_ref.shape[0]
    kz_even, kz_odd, _ = operands[g]
    qq = jnp.concatenate([q_ref[:, (2 * g) * LANES:(2 * g + 1) * LANES],
                          q_ref[:, (2 * g + 1) * LANES:(2 * g + 2) * LANES]], axis=0)
    s = jnp.concatenate([_bdot_nt(kz_even, qq), _bdot_nt(kz_odd, qq)], axis=1)
    if bias_t is not None:
        nb = bias_t.shape[0]
        s = jnp.concatenate([s[:nb] + jnp.concatenate([bias_t] * GQA_GROUP, axis=1), s[nb:]],
                            axis=0)
    sink = jnp.concatenate([jnp.full((1, m), sink_ref[h] * LOG2E, F32) for h in _group_heads(g)],
                           axis=1)
    mx = jnp.maximum(jnp.max(s, axis=0, keepdims=True), sink)
    return s, sink, mx


def _weighted_values(s, sink, mx, operands, g, m):
    v_ext_t = operands[g][2]
    p = jnp.exp2(s - mx).astype(BF16)
    oe = jnp.dot(v_ext_t, p, preferred_element_type=F32)
    other = (1 - g) * HEAD_DIM
    den = oe[other:other + 1] + jnp.exp2(sink - mx)
    o_g = oe[g * HEAD_DIM:(g + 1) * HEAD_DIM] / den
    return {h: o_g[:, i * m:(i + 1) * m] for i, h in enumerate(_group_heads(g))}


def _sink_attention(sink_ref, blocks):
    chains = [(blk, g) for blk in range(len(blocks)) for g in range(N_KV_HEADS)]
    outs = [dict() for _ in blocks]

    def scores(chain):
        blk, g = chain
        q_ref, operands, bias_t, _ = blocks[blk]
        return _scores(sink_ref, q_ref, operands, bias_t, g)

    ahead = scores(chains[0])
    for i, (blk, g) in enumerate(chains):
        current = ahead
        if i + 1 < len(chains):
            ahead = scores(chains[i + 1])
        q_ref, operands, _, o_ref = blocks[blk]
        outs[blk].update(_weighted_values(*current, operands, g, q_ref.shape[0]))
        if g == N_KV_HEADS - 1:
            for j in range(N_Q_HEADS // 2):
                pair_t = jnp.concatenate([outs[blk][2 * j], outs[blk][2 * j + 1]], axis=0)
                o_ref[:, j * LANES:(j + 1) * LANES] = pair_t.T.astype(BF16)


def _ctx_attn_kernel(sink_ref, q_ref, kv_ref, o_ref, *, n):
    blocks = []
    for j in range(q_ref.shape[0] // n):
        rows = pl.ds(j * n, n)
        operands = _attention_operands(kv_ref[rows, :KD], kv_ref[rows, KD:])
        blocks.append((q_ref.at[rows], operands, None, o_ref.at[rows]))
    _sink_attention(sink_ref, blocks)


def _ctx_attention(grp, q, kv, sink):
    per = CTX_SEQS_PER_STEP
    assert grp.b % per == 0
    n = per * grp.n
    return pl.pallas_call(
        functools.partial(_ctx_attn_kernel, n=grp.n),
        grid=(grp.b // per,),
        in_specs=[pl.BlockSpec(memory_space=pltpu.SMEM), _row_spec(n, QD), _row_spec(n, 2 * KD)],
        out_specs=_row_spec(n, QD),
        out_shape=jax.ShapeDtypeStruct((grp.t, QD), BF16),
        compiler_params=_cparams("arbitrary"),
        name="ctx_attention",
    )(sink, q, kv)


def _lat_attn_kernel(sink_ref, q_ref, kvp_ref, kvc_ref, kvn_ref, kc_ref, vc_ref, o_ref, *, n):
    band = jnp.concatenate([kvp_ref[...], kvc_ref[...], kvn_ref[...]], axis=0)
    kj = lax.broadcasted_iota(jnp.int32, (3 * ATTN_BLOCK, ATTN_BLOCK), 0)
    qi = lax.broadcasted_iota(jnp.int32, (3 * ATTN_BLOCK, ATTN_BLOCK), 1)
    rel = kj - ATTN_BLOCK - qi
    ctx_ops = _attention_operands(kc_ref[...], vc_ref[...])
    band_ops = [_attention_operands(band[i * ATTN_BLOCK:(i + 1) * ATTN_BLOCK, :KD],
                                    band[i * ATTN_BLOCK:(i + 1) * ATTN_BLOCK, KD:])
                for i in range(ATTN_BLOCKS_PER_STEP + 2)]
    blocks = []
    for j in range(ATTN_BLOCKS_PER_STEP):
        blk = pl.program_id(1) * ATTN_BLOCKS_PER_STEP + j
        kpos = (blk - 1) * ATTN_BLOCK + kj
        valid = (jnp.abs(rel) <= WINDOW) & (kpos >= 0) & (kpos < n)
        bias_t = jnp.where(valid, 0.0, NEG_INF)
        operands = _concat_operands(band_ops[j:j + 3] + [ctx_ops])
        rows = pl.ds(j * ATTN_BLOCK, ATTN_BLOCK)
        blocks.append((q_ref.at[rows], operands, bias_t, o_ref.at[rows]))
    _sink_attention(sink_ref, blocks)


def _lat_attention(grp, row0, q, kv, k_ctx, v_ctx, sink):
    nb = grp.n // ATTN_BLOCK
    per = ATTN_BLOCKS_PER_STEP
    steps = nb // per
    past = k_ctx.shape[1]
    assert row0 % (per * ATTN_BLOCK) == 0
    blk0, step0 = row0 // ATTN_BLOCK, row0 // (per * ATTN_BLOCK)

    def edge(off):
        return pl.BlockSpec((ATTN_BLOCK, 2 * KD),
                            lambda b, i: (blk0 + b * nb + jnp.clip(i * per + off, 0, nb - 1), 0))

    ctx_spec = pl.BlockSpec((None, past, KD), lambda b, i: (b, 0, 0))
    return pl.pallas_call(
        functools.partial(_lat_attn_kernel, n=grp.n),
        grid=(grp.b, steps),
        in_specs=[pl.BlockSpec(memory_space=pltpu.SMEM),
                  pl.BlockSpec((per * ATTN_BLOCK, QD), lambda b, i: (step0 + b * steps + i, 0)),
                  edge(-1),
                  pl.BlockSpec((per * ATTN_BLOCK, 2 * KD), lambda b, i: (step0 + b * steps + i, 0)),
                  edge(per), ctx_spec, ctx_spec],
        out_specs=pl.BlockSpec((per * ATTN_BLOCK, QD), lambda b, i: (b * steps + i, 0)),
        out_shape=jax.ShapeDtypeStruct((grp.t, QD), BF16),
        compiler_params=_cparams("arbitrary", "arbitrary"),
        name="lat_attention",
    )(sink, q, kv, kv, kv, k_ctx, v_ctx)


class _TailWeights:
    def __init__(self, layer, wo_hbm, w1_hbm, w2_hbm, wo_s, w1_s, w2_s, stage1, stage2, sem):
        self.layer = layer
        self.wo_hbm, self.w1_hbm, self.w2_hbm = wo_hbm, w1_hbm, w2_hbm
        self.wo_s, self.w1_s, self.w2_s = wo_s, w1_s, w2_s
        self.stage1, self.stage2, self.sem = stage1, stage2, sem

    def _w1_copy(self, p):
        src = self.w1_hbm.at[self.layer, :, pl.ds(p * WEIGHT_PIECE, WEIGHT_PIECE)]
        return pltpu.make_async_copy(src, self.stage1.at[p % 2], self.sem.at[0, p % 2])

    def _w2_copy(self, p):
        src = self.w2_hbm.at[self.layer, pl.ds(p * WEIGHT_PIECE, WEIGHT_PIECE), :]
        return pltpu.make_async_copy(src, self.stage2.at[p % 2], self.sem.at[1, p % 2])

    def _wo_copy(self, p):
        src = self.wo_hbm.at[0, pl.ds(p * WEIGHT_PIECE, WEIGHT_PIECE), :]
        return pltpu.make_async_copy(src, self.stage2.at[p % 2], self.sem.at[1, p % 2])

    def start(self):
        self._wo_copy(0).start()
        self._wo_copy(1).start()
        self._w1_copy(0).start()
        self._w1_copy(1).start()

    def fetch_out_proj(self):
        pieces = D_MODEL // WEIGHT_PIECE
        for p in range(pieces):
            self._wo_copy(p).wait()
            self.wo_s[p * WEIGHT_PIECE:(p + 1) * WEIGHT_PIECE, :] = self.stage2[p % 2].astype(BF16)
            if p + 2 < pieces:
                self._wo_copy(p + 2).start()
            else:
                self._w2_copy(p + 2 - pieces).start()

    def fetch_mlp_chunk(self, j):
        per = MLP_CHUNK // WEIGHT_PIECE
        total = MLP_CHUNKS * per
        for p in range(j * per, (j + 1) * per):
            part = slice((p % per) * WEIGHT_PIECE, (p % per + 1) * WEIGHT_PIECE)
            self._w1_copy(p).wait()
            self.w1_s[j, :, part] = self.stage1[p % 2].astype(BF16)
            if p + 2 < total:
                self._w1_copy(p + 2).start()
            self._w2_copy(p).wait()
            self.w2_s[j, part, :] = self.stage2[p % 2].astype(BF16)
            if p + 2 < total:
                self._w2_copy(p + 2).start()


def _tail_weight_scratch():
    assert MLP_CHUNK % WEIGHT_PIECE == 0 and D_MODEL // WEIGHT_PIECE >= 2
    return [pltpu.VMEM((D_MODEL, D_MODEL), BF16),
            pltpu.VMEM((MLP_CHUNKS, D_MODEL, MLP_CHUNK), BF16),
            pltpu.VMEM((MLP_CHUNKS, MLP_CHUNK, D_MODEL), BF16),
            pltpu.VMEM((2, D_MODEL, WEIGHT_PIECE), F32),
            pltpu.VMEM((2, WEIGHT_PIECE, D_MODEL), F32),
            pltpu.SemaphoreType.DMA((2, 2))]


def _first_step_streams_weights(body, weights):
    first = pl.program_id(0) == 0

    @pl.when(first)
    def _():
        weights.start()
        body(True)

    @pl.when(jnp.logical_not(first))
    def _():
        body(False)


def _mlp_tail(x, mod_ref, g_ref, weights, streaming):
    sh2, sc2, gt2 = _mod_chunk(mod_ref, 3), _mod_chunk(mod_ref, 4), _mod_chunk(mod_ref, 5)
    hb = (_rms(x, g_ref[2:3, :]) * (1.0 + sc2) + sh2).astype(BF16)
    acc = None
    for j in range(MLP_CHUNKS):
        if streaming:
            weights.fetch_mlp_chunk(j)
        a = jnp.dot(hb, weights.w1_s[j], preferred_element_type=F32)
        a = jnp.maximum(a, 0.0)
        part = jnp.dot((a * a).astype(BF16), weights.w2_s[j], preferred_element_type=F32)
        acc = part if acc is None else acc + part
    return x + gt2 * _rms(acc, g_ref[3:4, :])


def _postmix0_kernel(zc_ref, zp_ref, zn_ref, atc_ref, atl_ref, xc_ref, xl_ref, mod_ref, g_ref,
                     cw_ref, wo_hbm, w1_hbm, w2_hbm, o_ref, *scratch, stream):
    weights = _TailWeights(0, wo_hbm, w1_hbm, w2_hbm, *scratch)
    _first_step_streams_weights(
        functools.partial(_postmix0_body, zc_ref, zp_ref, zn_ref, atc_ref, atl_ref, xc_ref, xl_ref,
                          mod_ref, g_ref, cw_ref, o_ref, weights, stream), weights)


def _postmix0_body(zc_ref, zp_ref, zn_ref, atc_ref, atl_ref, xc_ref, xl_ref, mod_ref, g_ref,
                   cw_ref, o_ref, weights, stream, streaming):
    c, tm = CONV_DIM, stream.tm
    is_lat, n = stream.is_lat(), stream.seq_len()
    zc = zc_ref[...]
    u = zc[:, c:2 * c] * zc[:, 2 * c:]
    u_before = zp_ref[7:8, c:2 * c] * zp_ref[7:8, 2 * c:]
    u_after = zn_ref[0:1, c:2 * c] * zn_ref[0:1, 2 * c:]
    row = lax.broadcasted_iota(jnp.int32, (tm, 1), 0)
    pos = (pl.program_id(0) * tm + row) & (n - 1)
    u_prev = jnp.where(row == 0, u_before, pltpu.roll(u, 1, axis=0))
    u_prev = jnp.where(pos == 0, 0.0, u_prev)
    u_next = jnp.where(row == tm - 1, u_after, pltpu.roll(u, tm - 1, axis=0))
    u_next = jnp.where(pos == n - 1, 0.0, u_next)
    conv = u_prev * cw_ref[0:1, :] + u * cw_ref[1:2, :] + u_next * cw_ref[2:3, :]
    if streaming:
        weights.fetch_out_proj()
    attn = jnp.where(is_lat, atl_ref[...], atc_ref[...])
    mix = (_bdot(zc[:, :c] * conv, weights.wo_s[:c, :])
           + jnp.dot(attn, weights.wo_s[c:, :], preferred_element_type=F32))
    gt1 = _mod_chunk(mod_ref, 2)
    x1 = jnp.where(is_lat, xl_ref[...], xc_ref[...]) + gt1 * _rms(mix, g_ref[1:2, :])
    o_ref[...] = _mlp_tail(x1, mod_ref, g_ref, weights, streaming)


def _postmix0(stream, zc, attn_ctx, attn_lat, x_ctx, x_lat, mod, norm_g, conv_w, w_out, w1, w2):
    tm = stream.tm
    for n in (stream.ctx.n, stream.lat.n):
        assert n & (n - 1) == 0 and stream.ctx.t % n == 0
    r8 = tm // 8
    last8 = stream.t // 8 - 1
    hbm = pl.BlockSpec(memory_space=pl.ANY)
    return pl.pallas_call(
        functools.partial(_postmix0_kernel, stream=stream),
        grid=(stream.tiles,),
        in_specs=[stream.spec(3 * CONV_DIM),
                  pl.BlockSpec((8, 3 * CONV_DIM), lambda t: (jnp.maximum(t * r8 - 1, 0), 0)),
                  pl.BlockSpec((8, 3 * CONV_DIM), lambda t: (jnp.minimum((t + 1) * r8, last8), 0)),
                  stream.ctx_spec(QD), stream.lat_spec(QD),
                  stream.ctx_spec(D_MODEL), stream.lat_spec(D_MODEL), stream.mod_spec(),
                  _const_spec((4, D_MODEL)), _const_spec((3, CONV_DIM)), hbm, hbm, hbm],
        out_specs=stream.spec(D_MODEL),
        out_shape=jax.ShapeDtypeStruct((stream.t, D_MODEL), F32),
        scratch_shapes=_tail_weight_scratch(),
        compiler_params=_cparams("arbitrary"),
        name="postmix0_mlp",
    )(zc, zc, zc, attn_ctx, attn_lat, x_ctx, x_lat, mod, norm_g, conv_w, w_out, w1, w2)


def _premix1_kernel(x_ref, mod_ref, g_ref, w_ref, wg_ref, gb_ref,
                    q_ref, k_ref, v_ref, og_ref, gf_ref, gbk_ref, wb_ref, wr_ref):
    @pl.when(pl.program_id(0) == 0)
    def _():
        for c in range(0, GLA_MAIN, GLA_QK):
            wb_ref[:, c:c + GLA_QK] = w_ref[c:c + GLA_QK, :].T.astype(BF16)
        wr_ref[...] = w_ref[GLA_MAIN:, :].astype(BF16)

    sh1, sc1 = _mod_chunk(mod_ref, 0), _mod_chunk(mod_ref, 1)
    hb = (_rms(x_ref[...], g_ref[0:1, :]) * (1.0 + sc1) + sh1).astype(BF16)
    z = jnp.dot(hb, wb_ref[...], preferred_element_type=F32)
    q_ref[...] = z[:, :GLA_QK] * (GLA_DK ** -0.5)
    k_ref[...] = z[:, GLA_QK:2 * GLA_QK]
    v_ref[...] = z[:, 2 * GLA_QK:2 * GLA_QK + GLA_VD].astype(BF16)
    og_ref[...] = z[:, 2 * GLA_QK + GLA_VD:]
    r = _bdot_nt(hb, wr_ref[...])
    pre = _bdot(r, wg_ref[...]) + gb_ref[...]
    soft = jnp.log2(1.0 + jnp.exp2(jnp.abs(pre) * (-LOG2E)))
    gate = (jnp.minimum(pre, 0.0) * LOG2E - soft) * (1.0 / GLA_GATE_NORM)
    gf_ref[...] = gate[:, :GLA_QK]
    gbk_ref[...] = gate[:, GLA_QK:]


def _premix1(stream, x2d, mod, norm_g, w_in_t, w_gate, gate_bias):
    t, tm = stream.t, stream.tm
    w_in = w_in_t
    assert w_in.shape == (1, GLA_MAIN + 2 * GLA_RANK, D_MODEL)
    return pl.pallas_call(
        _premix1_kernel,
        grid=(t // tm,),
        in_specs=[_row_spec(tm, D_MODEL), stream.mod_spec(), _const_spec((4, D_MODEL)),
                  _f32_weight_spec(w_in.shape), _const_spec((2 * GLA_RANK, 2 * GLA_QK)),
                  _const_spec((1, 2 * GLA_QK))],
        out_specs=[_row_spec(tm, GLA_QK), _row_spec(tm, GLA_QK), _row_spec(tm, GLA_VD),
                   _row_spec(tm, GLA_VD), _row_spec(tm, GLA_QK), _row_spec(tm, GLA_QK)],
        out_shape=[jax.ShapeDtypeStruct((t, GLA_QK), F32), jax.ShapeDtypeStruct((t, GLA_QK), F32),
                   jax.ShapeDtypeStruct((t, GLA_VD), BF16),
                   jax.ShapeDtypeStruct((t, GLA_VD), F32), jax.ShapeDtypeStruct((t, GLA_QK), F32),
                   jax.ShapeDtypeStruct((t, GLA_QK), F32)],
        scratch_shapes=[pltpu.VMEM((D_MODEL, GLA_MAIN), BF16),
                        pltpu.VMEM((2 * GLA_RANK, D_MODEL), BF16)],
        compiler_params=_cparams("arbitrary"),
        name="premix1",
    )(x2d, mod, norm_g, w_in, w_gate, gate_bias)


def _split3(x):
    hi = x.astype(BF16)
    r1 = x - hi.astype(F32)
    mid = r1.astype(BF16)
    lo = (r1 - mid.astype(F32)).astype(BF16)
    return hi, mid, lo


def _level_exponent(b, s, reverse):
    idx = s if reverse else s - 1
    if s >= 8:
        n = GLA_TILE // (2 * s)
        b4 = b.reshape(n, 2, s, GLA_DK)
        first, second = b4[:, 0:1], b4[:, 1:2]
        r = (second[:, :, 0:1] if reverse else first[:, :, s - 1:s])
        parts = [first - r, r - second] if reverse else [r - first, second - r]
        return jnp.concatenate(parts, axis=1).reshape(GLA_TILE, GLA_DK)
    b8 = b.reshape(GLA_TILE // 8, 8, GLA_DK)
    sub = lax.broadcasted_iota(jnp.int32, (1, 8, 1), 1)
    if s == 4:
        r = b8[:, idx:idx + 1, :]
    else:
        assert s == 2
        r = jnp.where(sub < 4, b8[:, idx:idx + 1, :], b8[:, 4 + idx:5 + idx, :])
    in_second = (sub // s) % 2 == 1
    sign = jnp.where(in_second != reverse, 1.0, -1.0)
    return ((b8 - r) * sign).reshape(GLA_TILE, GLA_DK)


class _PairMatrix:
    LEVELS = [(1 << (lv - 1), lv) for lv in range(2, GLA_TILE.bit_length())]

    def __init__(self, q, k, g_f, g_b, b_f, b_b, lvl):
        half = GLA_TILE // 2
        self.lo, self.hi = slice(0, half), slice(half, GLA_TILE)
        self.q_bf, self.k_bf = q.astype(BF16), k.astype(BF16)
        self.b_f, self.b_b, self.lvl = b_f, b_b, lvl
        self.row = lax.broadcasted_iota(jnp.int32, (GLA_TILE, 1), 0)
        odd = self.row % 2 == 1
        own = jnp.sum(q * k, axis=-1, keepdims=True)
        k_adj = jnp.where(odd, pltpu.roll(k, 1, axis=0), pltpu.roll(k, GLA_TILE - 1, axis=0))
        adj = jnp.sum(q * jnp.exp2(jnp.where(odd, g_f, g_b)) * k_adj, axis=-1, keepdims=True)
        self.blocks = [jnp.where(lvl == 0, 2.0 * own[rows], jnp.where(lvl == 1, adj[rows], 0.0))
                       for rows in (self.lo, self.hi)]
        self.cross_f = self.cross_b = None

    def operands(self, s):
        qb, kb, b_f, b_b = self.q_bf, self.k_bf, self.b_f, self.b_b
        if s >= 16:
            n = GLA_TILE // (2 * s)

            def halves(x):
                x4 = x.reshape(n, 2, s, GLA_DK)
                return x4[:, 0:1], x4[:, 1:2]

            def rows(first, second):
                return jnp.concatenate([first, second], axis=1).reshape(GLA_TILE, GLA_DK)

            (bf1, bf2), (bb1, bb2) = halves(b_f), halves(b_b)
            (q1, q2), (k1, k2) = halves(qb), halves(kb)
            r_f, r_b = bf1[:, :, s - 1:s], bb2[:, :, 0:1]
            zero = jnp.zeros_like(q1)
            lhs = jnp.concatenate(
                [rows(zero, q2 * jnp.exp2(bf2 - r_f).astype(BF16)),
                 rows(q1 * jnp.exp2(bb1 - r_b).astype(BF16), zero)], axis=1)
            rhs = jnp.concatenate(
                [rows(k1 * jnp.exp2(r_f - bf1).astype(BF16), zero),
                 rows(zero, k2 * jnp.exp2(r_b - bb2).astype(BF16))], axis=1)
        else:
            second = (self.row // s) % 2 == 1
            f_f = jnp.exp2(_level_exponent(b_f, s, False)).astype(BF16)
            f_b = jnp.exp2(_level_exponent(b_b, s, True)).astype(BF16)
            u = qb * jnp.where(second, f_f, f_b)
            w = kb * jnp.where(second, f_b, f_f)
            zero = jnp.zeros_like(u)
            lhs = jnp.concatenate([jnp.where(second, u, zero), jnp.where(second, zero, u)], axis=1)
            rhs = jnp.concatenate([jnp.where(second, zero, w), jnp.where(second, w, zero)], axis=1)
        return lhs, rhs

    def absorb(self, s, level, pairs):
        lo, hi = self.lo, self.hi
        if 2 * s == GLA_TILE:
            self.cross_f, self.cross_b = pairs[hi, lo], pairs[lo, hi]
        else:
            self.blocks = [jnp.where(self.lvl == level, pairs[lo, lo], self.blocks[0]),
                           jnp.where(self.lvl == level, pairs[hi, hi], self.blocks[1])]

    def matrix(self):
        top = jnp.concatenate([self.blocks[0].astype(BF16), self.cross_b.astype(BF16)], axis=1)
        bottom = jnp.concatenate([self.cross_f.astype(BF16), self.blocks[1].astype(BF16)], axis=1)
        return jnp.concatenate([top, bottom], axis=0)


def _pair_matrix(q, k, g_f, g_b, b_f, b_b, lvl):
    builder = _PairMatrix(q, k, g_f, g_b, b_f, b_b, lvl)
    for s, level in _PairMatrix.LEVELS:
        builder.absorb(s, level, _bdot_nt(*builder.operands(s)))
    return builder.matrix()


def _carry_state(q, k, b, v, s_ref, reverse):
    edge = 0 if reverse else GLA_TILE - 1
    b_last = b[edge:edge + 1, :]
    qe = (q * jnp.exp2(b)).astype(BF16)
    ke = (k * jnp.exp2(b_last - b)).astype(BF16)
    st = s_ref[...]
    s_ref[...] = st * jnp.exp2(b_last) + lax.dot_general(
        v, ke, TN_DIMS, preferred_element_type=F32)
    return _bdot_nt(qe, st)


def _gla_kernel(*refs, zero_init, nt):
    if zero_init:
        (tri_ref, lvl_ref, q_ref, k_ref, gf_ref, gb_ref, v_ref,
         o_ref, sfo_ref, sbo_ref, sf_ref, sb_ref, bb_ref) = refs
        sf_ref[...] = jnp.zeros_like(sf_ref)
        sb_ref[...] = jnp.zeros_like(sb_ref)
    else:
        (tri_ref, lvl_ref, q_ref, k_ref, gf_ref, gb_ref, v_ref, s0f_ref, s0b_ref,
         o_ref, sfo_ref, sbo_ref, sf_ref, sb_ref, bb_ref) = refs
        for hh in range(sf_ref.shape[0]):
            sf_ref[hh] = s0f_ref[hh].T
            sb_ref[hh] = s0b_ref[hh].T
    hp = sf_ref.shape[0]

    def tile_rows(tile):
        r0 = tile * GLA_TILE
        return pl.ds(r0 if isinstance(r0, int) else pl.multiple_of(r0, GLA_TILE), GLA_TILE)

    heads = [(slice(hh * GLA_DK, (hh + 1) * GLA_DK), slice(hh * GLA_DV, (hh + 1) * GLA_DV), hh)
             for hh in range(hp)]

    def forward_sweep(t, carry):
        rows = tile_rows(t)
        g_f, g_b = gf_ref[rows, :], gb_ref[rows, :]
        c = None
        for part in _split3(jnp.concatenate([g_f, g_b], axis=1)):
            term = jnp.dot(tri_ref[...], part, preferred_element_type=F32)
            c = term if c is None else c + term
        width = hp * GLA_DK
        c_b = c[:, width:]
        b_f = c[:, :width]
        b_b = (c_b[GLA_TILE - 1:GLA_TILE, :] - c_b) + g_b
        bb_ref[t] = b_b
        for dk, dv, hh in heads:
            q, k, v = q_ref[rows, dk], k_ref[rows, dk], v_ref[rows, dv]
            att = _pair_matrix(q, k, g_f[:, dk], g_b[:, dk], b_f[:, dk], b_b[:, dk], lvl_ref[...])
            o_ref[rows, dv] = (jnp.dot(att, v, preferred_element_type=F32)
                               + _carry_state(q, k, b_f[:, dk], v, sf_ref.at[hh], False))
        return carry

    def backward_sweep(i, carry):
        t = nt - 1 - i
        rows = tile_rows(t)
        b_b = bb_ref[t]
        for dk, dv, hh in heads:
            o_ref[rows, dv] += _carry_state(q_ref[rows, dk], k_ref[rows, dk], b_b[:, dk],
                                            v_ref[rows, dv], sb_ref.at[hh], True)
        return carry

    if nt == 1:
        forward_sweep(0, 0)
        backward_sweep(0, 0)
    else:
        lax.fori_loop(0, nt, forward_sweep, 0)
        lax.fori_loop(0, nt, backward_sweep, 0)
    for hh in range(hp):
        sfo_ref[hh] = sf_ref[hh].T
        sbo_ref[hh] = sb_ref[hh].T


def _gla_scan(grp, row0, q, k, v, gf, gb, s0f, s0b, tri, lvl):
    n = grp.n
    nt = n // GLA_TILE
    per_head = 2 * n * (4 * GLA_DK * 4 + GLA_DV * 2 + GLA_DV * 4)
    hp = GLA_HEADS
    while hp > 1 and hp * per_head > GLA_SEQ_VMEM_BUDGET:
        hp //= 2
    zero_init = s0f is None
    half = GLA_TILE // 2
    assert row0 % n == 0
    seq0 = row0 // n
    in_dk = pl.BlockSpec((n, hp * GLA_DK), lambda b, h: (seq0 + b, h))
    in_dv = pl.BlockSpec((n, hp * GLA_DV), lambda b, h: (seq0 + b, h))
    seq_dv = pl.BlockSpec((n, hp * GLA_DV), lambda b, h: (b, h))
    state_spec = pl.BlockSpec((None, hp, GLA_DK, GLA_DV), lambda b, h: (b, h, 0, 0))
    in_specs = [_const_spec((GLA_TILE, GLA_TILE)), _const_spec((half, half)),
                in_dk, in_dk, in_dk, in_dk, in_dv]
    args = [tri, lvl, q, k, gf, gb, v]
    if not zero_init:
        in_specs += [state_spec, state_spec]
        args += [s0f, s0b]
    state_shape = jax.ShapeDtypeStruct((grp.b, GLA_HEADS, GLA_DK, GLA_DV), F32)
    return pl.pallas_call(
        functools.partial(_gla_kernel, zero_init=zero_init, nt=nt),
        grid=(grp.b, GLA_HEADS // hp),
        in_specs=in_specs,
        out_specs=[seq_dv, state_spec, state_spec],
        out_shape=[jax.ShapeDtypeStruct((grp.t, GLA_VD), F32), state_shape, state_shape],
        scratch_shapes=[pltpu.VMEM((hp, GLA_DV, GLA_DK), F32), pltpu.VMEM((hp, GLA_DV, GLA_DK), F32),
                        pltpu.VMEM((nt, GLA_TILE, hp * GLA_DK), F32)],
        compiler_params=_cparams("arbitrary", "arbitrary"),
        name="gla_scan",
    )(*args)


def _gla_constants():
    half = GLA_TILE // 2
    i = jnp.arange(GLA_TILE)[:, None]
    j = jnp.arange(GLA_TILE)[None, :]
    tri = (j <= i).astype(BF16)
    ih, jh = i[:half], j[:, :half]
    x = jnp.bitwise_xor(ih, jh)
    lvl = sum((x >= (1 << p)).astype(jnp.int32) for p in range(half.bit_length() - 1))
    return tri, lvl


def _postmix1_kernel(oc_ref, ol_ref, og_ref, x_ref, mod_ref, g_ref, gn_ref, wo_hbm, w1_hbm, w2_hbm,
                     yc_ref, yl_ref, *scratch, stream):
    weights = _TailWeights(1, wo_hbm, w1_hbm, w2_hbm, *scratch)
    _first_step_streams_weights(
        functools.partial(_postmix1_body, oc_ref, ol_ref, og_ref, x_ref, mod_ref, g_ref, gn_ref,
                          yc_ref, yl_ref, weights, stream), weights)


def _postmix1_body(oc_ref, ol_ref, og_ref, x_ref, mod_ref, g_ref, gn_ref, yc_ref, yl_ref,
                   weights, stream, streaming):
    is_lat = stream.is_lat()
    gn = gn_ref[...]
    ys = []
    for h in range(GLA_HEADS):
        cols = slice(h * GLA_DV, (h + 1) * GLA_DV)
        o = _rms(jnp.where(is_lat, ol_ref[:, cols], oc_ref[:, cols]), gn)
        og = og_ref[:, cols]
        ys.append((o * (og / (1.0 + jnp.exp(-og)))).astype(BF16))
    if streaming:
        weights.fetch_out_proj()
    mix = None
    for h, y in enumerate(ys):
        part = jnp.dot(y, weights.wo_s[h * GLA_DV:(h + 1) * GLA_DV, :], preferred_element_type=F32)
        mix = part if mix is None else mix + part
    gt1 = _mod_chunk(mod_ref, 2)
    x1 = x_ref[...] + gt1 * _rms(mix, g_ref[1:2, :])
    y = _mlp_tail(x1, mod_ref, g_ref, weights, streaming)

    @pl.when(jnp.logical_not(is_lat))
    def _():
        yc_ref[...] = y

    @pl.when(is_lat)
    def _():
        yl_ref[...] = y


def _postmix1(stream, o_ctx, o_lat, og, x2d, mod, norm_g, gla_norm_g, w_out, w1, w2):
    hbm = pl.BlockSpec(memory_space=pl.ANY)
    return pl.pallas_call(
        functools.partial(_postmix1_kernel, stream=stream),
        grid=(stream.tiles,),
        in_specs=[stream.ctx_spec(GLA_VD), stream.lat_spec(GLA_VD), stream.spec(GLA_VD),
                  stream.spec(D_MODEL), stream.mod_spec(), _const_spec((4, D_MODEL)),
                  _const_spec((1, GLA_DV)), hbm, hbm, hbm],
        out_specs=[stream.ctx_spec(D_MODEL), stream.lat_spec(D_MODEL)],
        out_shape=[jax.ShapeDtypeStruct((stream.ctx.t, D_MODEL), F32),
                   jax.ShapeDtypeStruct((stream.lat.t, D_MODEL), F32)],
        scratch_shapes=_tail_weight_scratch(),
        compiler_params=_cparams("arbitrary"),
        name="postmix1_mlp",
    )(o_ctx, o_lat, og, x2d, mod, norm_g, gla_norm_g, w_out, w1, w2)


def kernel(x_prompt, x_sample, cache_k, cache_v, state_fwd, state_bwd, c, c_ctx, mod_w, mod_b,
           norm_g, ab_w_in, conv_w, attn_sink, ab_w_out, gla_w_in, gla_gate_w, gla_gate_b,
           gla_norm_g, gla_w_out, mlp_w1, mlp_w2):
    b_ctx, n_ctx, _ = x_prompt.shape
    b_lat, n_lat, _ = x_sample.shape
    assert mod_w.shape[0] == 2 and ab_w_in.shape[0] == 1 and gla_w_in.shape[0] == 1
    assert 1 + b_lat <= 8

    cond8 = jnp.zeros((8, D_MODEL), F32).at[0].set(c_ctx).at[1:1 + b_lat].set(c)
    mod = _modulation(cond8, mod_w, mod_b)
    mods = [mod[l].reshape(8, 1, -1) for l in range(2)]

    w_gate = jnp.zeros((2 * GLA_RANK, 2 * GLA_QK), F32)
    w_gate = w_gate.at[:GLA_RANK, :GLA_QK].set(gla_gate_w[0, 0])
    w_gate = w_gate.at[GLA_RANK:, GLA_QK:].set(gla_gate_w[0, 1])
    tri, lvl = _gla_constants()
    p = {
        "conv_w": conv_w[0],
        "sink": attn_sink[0],
        "gla_w_in": jnp.swapaxes(gla_w_in, 1, 2),
        "gla_w_gate": w_gate.astype(BF16),
        "gla_gate_bias": gla_gate_b[0].reshape(1, 2 * GLA_QK),
        "gla_norm_g": gla_norm_g[0].reshape(1, GLA_DV),
    }

    ctx, lat = _Group(b_ctx, n_ctx), _Group(b_lat, n_lat)
    stream = _Stream(ctx, lat, TOKEN_TILE)
    x_ctx, x_lat = x_prompt.reshape(ctx.t, D_MODEL), x_sample.reshape(lat.t, D_MODEL)
    past = cache_k.shape[2]
    k_ctx = cache_k[:, 0].reshape(b_lat, past, KD)
    v_ctx = cache_v[:, 0].reshape(b_lat, past, KD)

    zc, q, kv, k_t, v_t = _premix0(stream, x_ctx, x_lat, mods[0], norm_g[0], ab_w_in,
                                   _rope_tables(n_lat, TOKEN_TILE))
    attn_ctx = _ctx_attention(ctx, q, kv, p["sink"])
    attn_lat = _lat_attention(lat, ctx.t, q, kv, k_ctx, v_ctx, p["sink"])
    x1 = _postmix0(stream, zc, attn_ctx, attn_lat, x_ctx, x_lat, mods[0], norm_g[0], p["conv_w"],
                   ab_w_out, mlp_w1, mlp_w2)
    gq, gk, gv, og, gf, gb = _premix1(stream, x1, mods[1], norm_g[1], p["gla_w_in"],
                                      p["gla_w_gate"], p["gla_gate_bias"])
    o_ctx, sf, sb = _gla_scan(ctx, 0, gq, gk, gv, gf, gb, None, None, tri, lvl)
    o_lat, _, _ = _gla_scan(lat, ctx.t, gq, gk, gv, gf, gb, state_fwd[:, 0], state_bwd[:, 0],
                            tri, lvl)
    y_ctx, y_lat = _postmix1(stream, o_ctx, o_lat, og, x1, mods[1], norm_g[1], p["gla_norm_g"],
                             gla_w_out, mlp_w1, mlp_w2)

    def cache_layout(t):
        t = t.reshape(b_ctx, 1, N_KV_HEADS, HEAD_DIM, n_ctx)
        return jnp.transpose(t, (0, 1, 4, 2, 3))

    return (y_ctx.reshape(x_prompt.shape), y_lat.reshape(x_sample.shape),
            cache_layout(k_t), cache_layout(v_t), sf[:, None], sb[:, None])
```

```python
import functools

import jax
import jax.numpy as jnp
from jax import lax
from jax.experimental import pallas as pl
from jax.experimental.pallas import tpu as pltpu

F32 = jnp.float32
BF16 = jnp.bfloat16

D_MODEL = 1024
MOD_CHUNKS = 6
EPS = 1e-6
CONV_DIM = 512
N_Q_HEADS = 8
N_KV_HEADS = 2
GQA_GROUP = 4
HEAD_DIM = 64
WINDOW = 128
ATTN_BLOCK = 128
ATTN_BLOCKS_PER_STEP = 16
CTX_SEQS_PER_STEP = 4
GRID_W = 64
ROPE_BASE = 10000.0
QD = N_Q_HEADS * HEAD_DIM
KD = N_KV_HEADS * HEAD_DIM
AB_IN = 3 * CONV_DIM + QD + 2 * KD
GLA_HEADS = 4
GLA_DK = 128
GLA_DV = 256
GLA_RANK = 16
GLA_GATE_NORM = 16.0
GLA_TILE = 256
GLA_SEQ_VMEM_BUDGET = 32 * 1024 * 1024
LOG2E = 1.4426950408889634
GLA_QK = GLA_HEADS * GLA_DK
GLA_VD = GLA_HEADS * GLA_DV
GLA_MAIN = 2 * GLA_QK + 2 * GLA_VD
D_FF = 4 * D_MODEL
MLP_CHUNK = 512
MLP_CHUNKS = D_FF // MLP_CHUNK
WEIGHT_PIECE = 256
TOKEN_TILE = 512
PREMIX_PARTS = 2
NEG_INF = -1e30
LANES = 128
VMEM_LIMIT = 60 * 1024 * 1024

NT_DIMS = (((1,), (1,)), ((), ()))
TN_DIMS = (((0,), (0,)), ((), ()))


def _cparams(*sem):
    return pltpu.CompilerParams(dimension_semantics=sem, vmem_limit_bytes=VMEM_LIMIT)


def _bdot(a, b):
    return jnp.dot(a.astype(BF16), b.astype(BF16), preferred_element_type=F32)


def _bdot_nt(a, b):
    return lax.dot_general(a.astype(BF16), b.astype(BF16), NT_DIMS, preferred_element_type=F32)


def _rms(x, g):
    ms = jnp.mean(x * x, axis=-1, keepdims=True)
    return x * lax.rsqrt(ms + EPS) * g


def _mod_chunk(mod_ref, i):
    return mod_ref[:, i * D_MODEL:(i + 1) * D_MODEL]


def _const_spec(shape):
    return pl.BlockSpec(shape, lambda *_: (0,) * len(shape))


def _f32_weight_spec(shape):
    assert shape[0] == 1
    return pl.BlockSpec((None,) + tuple(shape[1:]), lambda *_: (0, 0, 0),
                        pipeline_mode=pl.Buffered(1))


def _cast_once(w_ref, wb_ref):
    @pl.when(pl.program_id(0) == 0)
    def _():
        wb_ref[...] = w_ref[...].astype(BF16)


def _mod_kernel(cond_ref, w_ref, b_ref, o_ref):
    cnd = cond_ref[...]
    s = cnd / (1.0 + jnp.exp(-cnd))
    o_ref[...] = _bdot(s, w_ref[...]) + b_ref[...]


def _modulation(cond8, mod_w, mod_b):
    depth = mod_w.shape[0]
    n = mod_w.shape[2]
    tn = 1536
    return pl.pallas_call(
        _mod_kernel,
        grid=(depth, n // tn),
        in_specs=[
            pl.BlockSpec((8, D_MODEL), lambda l, j: (0, 0)),
            pl.BlockSpec((None, D_MODEL, tn), lambda l, j: (l, 0, j)),
            pl.BlockSpec((None, 1, tn), lambda l, j: (l, 0, j)),
        ],
        out_specs=pl.BlockSpec((None, 8, tn), lambda l, j: (l, 0, j)),
        out_shape=jax.ShapeDtypeStruct((depth, 8, n), F32),
        compiler_params=_cparams("arbitrary", "arbitrary"),
        name="modulation",
    )(cond8, mod_w, mod_b.reshape(depth, 1, n))


class _Group:
    def __init__(self, b, n):
        self.b, self.n, self.t = b, n, b * n


class _Stream:
    def __init__(self, ctx, lat, tm):
        assert ctx.t % tm == 0 and lat.n % tm == 0 and tm % ctx.n == 0
        self.ctx, self.lat, self.tm = ctx, lat, tm
        self.t = ctx.t + lat.t
        self.ctx_tiles = ctx.t // tm
        self.tiles = self.t // tm

    def is_lat(self):
        return pl.program_id(0) >= self.ctx_tiles

    def spec(self, width):
        return pl.BlockSpec((self.tm, width), lambda t: (t, 0))

    def ctx_spec(self, width):
        last = self.ctx_tiles - 1
        return pl.BlockSpec((self.tm, width), lambda t: (jnp.minimum(t, last), 0))

    def lat_spec(self, width):
        first = self.ctx_tiles
        return pl.BlockSpec((self.tm, width), lambda t: (jnp.maximum(t - first, 0), 0))

    def mod_spec(self):
        first, per = self.ctx_tiles, self.lat.n // self.tm
        return pl.BlockSpec((None, 1, MOD_CHUNKS * D_MODEL),
                            lambda t: (jnp.where(t < first, 0, 1 + (t - first) // per), 0, 0))

    def seq_len(self):
        return jnp.where(self.is_lat(), self.lat.n, self.ctx.n)


def _row_spec(tm, width):
    return pl.BlockSpec((tm, width), lambda t: (t, 0))


def _rope(x, cos, sin_lo, sin_hi):
    return (x * cos + pltpu.roll(x, LANES - 16, axis=1) * sin_lo
            + pltpu.roll(x, 16, axis=1) * sin_hi)


def _premix0_kernel(xc_ref, xl_ref, mod_ref, g_ref, w_ref, cos_ref, slo_ref, shi_ref,
                    zc_ref, q_ref, kv_ref, kt_ref, vt_ref, wb_ref, *, stream):
    _cast_once(w_ref, wb_ref)
    is_lat = stream.is_lat()
    sh1, sc1 = _mod_chunk(mod_ref, 0), _mod_chunk(mod_ref, 1)
    c3 = 3 * CONV_DIM
    scale = HEAD_DIM ** -0.5 * LOG2E
    n = kt_ref.shape[2]
    part = stream.tm // PREMIX_PARTS
    raw_kv = []
    for r0 in range(0, stream.tm, part):
        rows = slice(r0, r0 + part)
        x = jnp.where(is_lat, xl_ref[rows, :], xc_ref[rows, :])
        h = _rms(x, g_ref[0:1, :]) * (1.0 + sc1) + sh1
        z = _bdot(h, wb_ref[...])
        zc_ref[rows, :] = z[:, :c3]
        cos, slo, shi = cos_ref[rows, :], slo_ref[rows, :], shi_ref[rows, :]
        for j in range(QD // LANES):
            qs = z[:, c3 + j * LANES:c3 + (j + 1) * LANES]
            q_ref[rows, j * LANES:(j + 1) * LANES] = (_rope(qs, cos, slo, shi) * scale).astype(BF16)
        kv_ref[rows, :KD] = _rope(z[:, c3 + QD:c3 + QD + KD], cos, slo, shi)
        kv_ref[rows, KD:] = z[:, c3 + QD + KD:]
        raw_kv.append(z[:, c3 + QD:])

    @pl.when(jnp.logical_not(is_lat))
    def _():
        for p, kv in enumerate(raw_kv):
            for j in range(part // n):
                seq = p * (part // n) + j
                kt_ref[seq] = kv[j * n:(j + 1) * n, :KD].T
                vt_ref[seq] = kv[j * n:(j + 1) * n, KD:].T


def _premix0(stream, x_ctx, x_lat, mod, norm_g, w_in, rope_tabs):
    tm, ctx, lat = stream.tm, stream.ctx, stream.lat
    first, per, last = stream.ctx_tiles, lat.n // tm, stream.ctx_tiles - 1
    rope_spec = pl.BlockSpec((tm, LANES),
                             lambda t: (jnp.where(t < first, 0, 1 + (t - first) % per), 0))
    cache_spec = pl.BlockSpec((tm // ctx.n, KD, ctx.n), lambda t: (jnp.minimum(t, last), 0, 0))
    return pl.pallas_call(
        functools.partial(_premix0_kernel, stream=stream),
        grid=(stream.tiles,),
        in_specs=[stream.ctx_spec(D_MODEL), stream.lat_spec(D_MODEL), stream.mod_spec(),
                  _const_spec((4, D_MODEL)), _f32_weight_spec(w_in.shape)] + [rope_spec] * 3,
        out_specs=[stream.spec(3 * CONV_DIM), stream.spec(QD), stream.spec(2 * KD),
                   cache_spec, cache_spec],
        out_shape=[jax.ShapeDtypeStruct((stream.t, 3 * CONV_DIM), F32),
                   jax.ShapeDtypeStruct((stream.t, QD), BF16),
                   jax.ShapeDtypeStruct((stream.t, 2 * KD), F32),
                   jax.ShapeDtypeStruct((ctx.b, KD, ctx.n), F32),
                   jax.ShapeDtypeStruct((ctx.b, KD, ctx.n), F32)],
        scratch_shapes=[pltpu.VMEM(w_in.shape[1:], BF16)],
        compiler_params=_cparams("arbitrary"),
        name="premix0",
    )(x_ctx, x_lat, mod, norm_g, w_in, *rope_tabs)


def _rope_tables(n, identity_rows):
    rows = n // GRID_W
    pos_r = jnp.repeat(jnp.arange(rows), GRID_W)
    pos_c = jnp.tile(jnp.arange(GRID_W), rows)
    half = HEAD_DIM // 2
    quarter = half // 2
    inv = ROPE_BASE ** (-(jnp.arange(quarter, dtype=F32) * 2.0 / half))

    def cs(pos):
        ang = pos.astype(F32)[:, None] * inv[None, :]
        return jnp.cos(ang), jnp.sin(ang)

    cr, sr = cs(pos_r)
    cc, sc = cs(pos_c)
    zero = jnp.zeros_like(sr)
    cos = jnp.concatenate([cr, cr, cc, cc], axis=1)
    sin_lo = jnp.concatenate([-sr, zero, -sc, zero], axis=1)
    sin_hi = jnp.concatenate([zero, sr, zero, sc], axis=1)
    rep = LANES // HEAD_DIM
    tables = []
    for t, ident in ((cos, 1.0), (sin_lo, 0.0), (sin_hi, 0.0)):
        head = jnp.full((identity_rows, LANES), ident, F32)
        tables.append(jnp.concatenate([head, jnp.tile(t, (1, rep))], axis=0))
    return tuple(tables)


def _attention_operands(k_all, v_all):
    assert KD == LANES == 2 * HEAD_DIM and GQA_GROUP == 4
    lane = lax.broadcasted_iota(jnp.int32, (1, LANES), 1)
    sub = lax.broadcasted_iota(jnp.int32, (LANES, 1), 0)
    v_t = v_all.T
    k_swapped = pltpu.roll(k_all, HEAD_DIM, axis=1)
    ops = []
    for g in range(N_KV_HEADS):
        k_low, k_high = (k_all, k_swapped) if g == 0 else (k_swapped, k_all)
        kz_even = jnp.where(lane < HEAD_DIM, k_low, 0.0).astype(BF16)
        kz_odd = jnp.where(lane >= HEAD_DIM, k_high, 0.0).astype(BF16)
        own = (sub < HEAD_DIM) if g == 0 else (sub >= HEAD_DIM)
        v_ext_t = jnp.where(own, v_t, 1.0).astype(BF16)
        ops.append((kz_even, kz_odd, v_ext_t))
    return ops


def _concat_operands(parts):
    return [(jnp.concatenate([p[g][0] for p in parts], axis=0),
             jnp.concatenate([p[g][1] for p in parts], axis=0),
             jnp.concatenate([p[g][2] for p in parts], axis=1)) for g in range(N_KV_HEADS)]


def _group_heads(g):
    return [4 * g, 4 * g + 2, 4 * g + 1, 4 * g + 3]


def _scores(sink_ref, q_ref, operands, bias_t, g):
    m = q_ref.shape[0]
    kz_even, kz_odd, _ = operands[g]
    qq = jnp.concatenate([q_ref[:, (2 * g) * LANES:(2 * g + 1) * LANES],
                          q_ref[:, (2 * g + 1) * LANES:(2 * g + 2) * LANES]], axis=0)
    s = jnp.concatenate([_bdot_nt(kz_even, qq), _bdot_nt(kz_odd, qq)], axis=1)
    if bias_t is not None:
        nb = bias_t.shape[0]
        s = jnp.concatenate([s[:nb] + jnp.concatenate([bias_t] * GQA_GROUP, axis=1), s[nb:]],
                            axis=0)
    sink = jnp.concatenate([jnp.full((1, m), sink_ref[h] * LOG2E, F32) for h in _group_heads(g)],
                           axis=1)
    mx = jnp.maximum(jnp.max(s, axis=0, keepdims=True), sink)
    return s, sink, mx


def _weighted_values(s, sink, mx, operands, g, m):
    v_ext_t = operands[g][2]
    p = jnp.exp2(s - mx).astype(BF16)
    oe = jnp.dot(v_ext_t, p, preferred_element_type=F32)
    other = (1 - g) * HEAD_DIM
    den = oe[other:other + 1] + jnp.exp2(sink - mx)
    o_g = oe[g * HEAD_DIM:(g + 1) * HEAD_DIM] / den
    return {h: o_g[:, i * m:(i + 1) * m] for i, h in enumerate(_group_heads(g))}


def _sink_attention(sink_ref, blocks):
    chains = [(blk, g) for blk in range(len(blocks)) for g in range(N_KV_HEADS)]
    outs = [dict() for _ in blocks]

    def scores(chain):
        blk, g = chain
        q_ref, operands, bias_t, _ = blocks[blk]
        return _scores(sink_ref, q_ref, operands, bias_t, g)

    ahead = scores(chains[0])
    for i, (blk, g) in enumerate(chains):
        current = ahead
        if i + 1 < len(chains):
            ahead = scores(chains[i + 1])
        q_ref, operands, _, o_ref = blocks[blk]
        outs[blk].update(_weighted_values(*current, operands, g, q_ref.shape[0]))
        if g == N_KV_HEADS - 1:
            for j in range(N_Q_HEADS // 2):
                pair_t = jnp.concatenate([outs[blk][2 * j], outs[blk][2 * j + 1]], axis=0)
                o_ref[:, j * LANES:(j + 1) * LANES] = pair_t.T.astype(BF16)


def _ctx_attn_kernel(sink_ref, q_ref, kv_ref, o_ref, *, n):
    blocks = []
    for j in range(q_ref.shape[0] // n):
        rows = pl.ds(j * n, n)
        operands = _attention_operands(kv_ref[rows, :KD], kv_ref[rows, KD:])
        blocks.append((q_ref.at[rows], operands, None, o_ref.at[rows]))
    _sink_attention(sink_ref, blocks)


def _ctx_attention(grp, q, kv, sink):
    per = CTX_SEQS_PER_STEP
    assert grp.b % per == 0
    n = per * grp.n
    return pl.pallas_call(
        functools.partial(_ctx_attn_kernel, n=grp.n),
        grid=(grp.b // per,),
        in_specs=[pl.BlockSpec(memory_space=pltpu.SMEM), _row_spec(n, QD), _row_spec(n, 2 * KD)],
        out_specs=_row_spec(n, QD),
        out_shape=jax.ShapeDtypeStruct((grp.t, QD), BF16),
        compiler_params=_cparams("arbitrary"),
        name="ctx_attention",
    )(sink, q, kv)


def _lat_attn_kernel(sink_ref, q_ref, kvp_ref, kvc_ref, kvn_ref, kc_ref, vc_ref, o_ref, *, n):
    band = jnp.concatenate([kvp_ref[...], kvc_ref[...], kvn_ref[...]], axis=0)
    kj = lax.broadcasted_iota(jnp.int32, (3 * ATTN_BLOCK, ATTN_BLOCK), 0)
    qi = lax.broadcasted_iota(jnp.int32, (3 * ATTN_BLOCK, ATTN_BLOCK), 1)
    rel = kj - ATTN_BLOCK - qi
    ctx_ops = _attention_operands(kc_ref[...], vc_ref[...])
    band_ops = [_attention_operands(band[i * ATTN_BLOCK:(i + 1) * ATTN_BLOCK, :KD],
                                    band[i * ATTN_BLOCK:(i + 1) * ATTN_BLOCK, KD:])
                for i in range(ATTN_BLOCKS_PER_STEP + 2)]
    blocks = []
    for j in range(ATTN_BLOCKS_PER_STEP):
        blk = pl.program_id(1) * ATTN_BLOCKS_PER_STEP + j
        kpos = (blk - 1) * ATTN_BLOCK + kj
        valid = (jnp.abs(rel) <= WINDOW) & (kpos >= 0) & (kpos < n)
        bias_t = jnp.where(valid, 0.0, NEG_INF)
        operands = _concat_operands(band_ops[j:j + 3] + [ctx_ops])
        rows = pl.ds(j * ATTN_BLOCK, ATTN_BLOCK)
        blocks.append((q_ref.at[rows], operands, bias_t, o_ref.at[rows]))
    _sink_attention(sink_ref, blocks)


def _lat_attention(grp, row0, q, kv, k_ctx, v_ctx, sink):
    nb = grp.n // ATTN_BLOCK
    per = ATTN_BLOCKS_PER_STEP
    steps = nb // per
    past = k_ctx.shape[1]
    assert row0 % (per * ATTN_BLOCK) == 0
    blk0, step0 = row0 // ATTN_BLOCK, row0 // (per * ATTN_BLOCK)

    def edge(off):
        return pl.BlockSpec((ATTN_BLOCK, 2 * KD),
                            lambda b, i: (blk0 + b * nb + jnp.clip(i * per + off, 0, nb - 1), 0))

    ctx_spec = pl.BlockSpec((None, past, KD), lambda b, i: (b, 0, 0))
    return pl.pallas_call(
        functools.partial(_lat_attn_kernel, n=grp.n),
        grid=(grp.b, steps),
        in_specs=[pl.BlockSpec(memory_space=pltpu.SMEM),
                  pl.BlockSpec((per * ATTN_BLOCK, QD), lambda b, i: (step0 + b * steps + i, 0)),
                  edge(-1),
                  pl.BlockSpec((per * ATTN_BLOCK, 2 * KD), lambda b, i: (step0 + b * steps + i, 0)),
                  edge(per), ctx_spec, ctx_spec],
        out_specs=pl.BlockSpec((per * ATTN_BLOCK, QD), lambda b, i: (b * steps + i, 0)),
        out_shape=jax.ShapeDtypeStruct((grp.t, QD), BF16),
        compiler_params=_cparams("arbitrary", "arbitrary"),
        name="lat_attention",
    )(sink, q, kv, kv, kv, k_ctx, v_ctx)


class _TailWeights:
    def __init__(self, layer, wo_hbm, w1_hbm, w2_hbm, wo_s, w1_s, w2_s, stage1, stage2, sem):
        self.layer = layer
        self.wo_hbm, self.w1_hbm, self.w2_hbm = wo_hbm, w1_hbm, w2_hbm
        self.wo_s, self.w1_s, self.w2_s = wo_s, w1_s, w2_s
        self.stage1, self.stage2, self.sem = stage1, stage2, sem

    def _w1_copy(self, p):
        src = self.w1_hbm.at[self.layer, :, pl.ds(p * WEIGHT_PIECE, WEIGHT_PIECE)]
        return pltpu.make_async_copy(src, self.stage1.at[p % 2], self.sem.at[0, p % 2])

    def _w2_copy(self, p):
        src = self.w2_hbm.at[self.layer, pl.ds(p * WEIGHT_PIECE, WEIGHT_PIECE), :]
        return pltpu.make_async_copy(src, self.stage2.at[p % 2], self.sem.at[1, p % 2])

    def _wo_copy(self, p):
        src = self.wo_hbm.at[0, pl.ds(p * WEIGHT_PIECE, WEIGHT_PIECE), :]
        return pltpu.make_async_copy(src, self.stage2.at[p % 2], self.sem.at[1, p % 2])

    def start(self):
        self._wo_copy(0).start()
        self._wo_copy(1).start()
        self._w1_copy(0).start()
        self._w1_copy(1).start()

    def fetch_out_proj(self):
        pieces = D_MODEL // WEIGHT_PIECE
        for p in range(pieces):
            self._wo_copy(p).wait()
            self.wo_s[p * WEIGHT_PIECE:(p + 1) * WEIGHT_PIECE, :] = self.stage2[p % 2].astype(BF16)
            if p + 2 < pieces:
                self._wo_copy(p + 2).start()
            else:
                self._w2_copy(p + 2 - pieces).start()

    def fetch_mlp_chunk(self, j):
        per = MLP_CHUNK // WEIGHT_PIECE
        total = MLP_CHUNKS * per
        for p in range(j * per, (j + 1) * per):
            part = slice((p % per) * WEIGHT_PIECE, (p % per + 1) * WEIGHT_PIECE)
            self._w1_copy(p).wait()
            self.w1_s[j, :, part] = self.stage1[p % 2].astype(BF16)
            if p + 2 < total:
                self._w1_copy(p + 2).start()
            self._w2_copy(p).wait()
            self.w2_s[j, part, :] = self.stage2[p % 2].astype(BF16)
            if p + 2 < total:
                self._w2_copy(p + 2).start()


def _tail_weight_scratch():
    assert MLP_CHUNK % WEIGHT_PIECE == 0 and D_MODEL // WEIGHT_PIECE >= 2
    return [pltpu.VMEM((D_MODEL, D_MODEL), BF16),
            pltpu.VMEM((MLP_CHUNKS, D_MODEL, MLP_CHUNK), BF16),
            pltpu.VMEM((MLP_CHUNKS, MLP_CHUNK, D_MODEL), BF16),
            pltpu.VMEM((2, D_MODEL, WEIGHT_PIECE), F32),
            pltpu.VMEM((2, WEIGHT_PIECE, D_MODEL), F32),
            pltpu.SemaphoreType.DMA((2, 2))]


def _first_step_streams_weights(body, weights):
    first = pl.program_id(0) == 0

    @pl.when(first)
    def _():
        weights.start()
        body(True)

    @pl.when(jnp.logical_not(first))
    def _():
        body(False)


def _mlp_tail(x, mod_ref, g_ref, weights, streaming):
    sh2, sc2, gt2 = _mod_chunk(mod_ref, 3), _mod_chunk(mod_ref, 4), _mod_chunk(mod_ref, 5)
    hb = (_rms(x, g_ref[2:3, :]) * (1.0 + sc2) + sh2).astype(BF16)
    acc = None
    for j in range(MLP_CHUNKS):
        if streaming:
            weights.fetch_mlp_chunk(j)
        a = jnp.dot(hb, weights.w1_s[j], preferred_element_type=F32)
        a = jnp.maximum(a, 0.0)
        part = jnp.dot((a * a).astype(BF16), weights.w2_s[j], preferred_element_type=F32)
        acc = part if acc is None else acc + part
    return x + gt2 * _rms(acc, g_ref[3:4, :])


def _postmix0_kernel(zc_ref, zp_ref, zn_ref, atc_ref, atl_ref, xc_ref, xl_ref, mod_ref, g_ref,
                     cw_ref, wo_hbm, w1_hbm, w2_hbm, o_ref, *scratch, stream):
    weights = _TailWeights(0, wo_hbm, w1_hbm, w2_hbm, *scratch)
    _first_step_streams_weights(
        functools.partial(_postmix0_body, zc_ref, zp_ref, zn_ref, atc_ref, atl_ref, xc_ref, xl_ref,
                          mod_ref, g_ref, cw_ref, o_ref, weights, stream), weights)


def _postmix0_body(zc_ref, zp_ref, zn_ref, atc_ref, atl_ref, xc_ref, xl_ref, mod_ref, g_ref,
                   cw_ref, o_ref, weights, stream, streaming):
    c, tm = CONV_DIM, stream.tm
    is_lat, n = stream.is_lat(), stream.seq_len()
    zc = zc_ref[...]
    u = zc[:, c:2 * c] * zc[:, 2 * c:]
    u_before = zp_ref[7:8, c:2 * c] * zp_ref[7:8, 2 * c:]
    u_after = zn_ref[0:1, c:2 * c] * zn_ref[0:1, 2 * c:]
    row = lax.broadcasted_iota(jnp.int32, (tm, 1), 0)
    pos = (pl.program_id(0) * tm + row) & (n - 1)
    u_prev = jnp.where(row == 0, u_before, pltpu.roll(u, 1, axis=0))
    u_prev = jnp.where(pos == 0, 0.0, u_prev)
    u_next = jnp.where(row == tm - 1, u_after, pltpu.roll(u, tm - 1, axis=0))
    u_next = jnp.where(pos == n - 1, 0.0, u_next)
    conv = u_prev * cw_ref[0:1, :] + u * cw_ref[1:2, :] + u_next * cw_ref[2:3, :]
    if streaming:
        weights.fetch_out_proj()
    attn = jnp.where(is_lat, atl_ref[...], atc_ref[...])
    mix = (_bdot(zc[:, :c] * conv, weights.wo_s[:c, :])
           + jnp.dot(attn, weights.wo_s[c:, :], preferred_element_type=F32))
    gt1 = _mod_chunk(mod_ref, 2)
    x1 = jnp.where(is_lat, xl_ref[...], xc_ref[...]) + gt1 * _rms(mix, g_ref[1:2, :])
    o_ref[...] = _mlp_tail(x1, mod_ref, g_ref, weights, streaming)


def _postmix0(stream, zc, attn_ctx, attn_lat, x_ctx, x_lat, mod, norm_g, conv_w, w_out, w1, w2):
    tm = stream.tm
    for n in (stream.ctx.n, stream.lat.n):
        assert n & (n - 1) == 0 and stream.ctx.t % n == 0
    r8 = tm // 8
    last8 = stream.t // 8 - 1
    hbm = pl.BlockSpec(memory_space=pl.ANY)
    return pl.pallas_call(
        functools.partial(_postmix0_kernel, stream=stream),
        grid=(stream.tiles,),
        in_specs=[stream.spec(3 * CONV_DIM),
                  pl.BlockSpec((8, 3 * CONV_DIM), lambda t: (jnp.maximum(t * r8 - 1, 0), 0)),
                  pl.BlockSpec((8, 3 * CONV_DIM), lambda t: (jnp.minimum((t + 1) * r8, last8), 0)),
                  stream.ctx_spec(QD), stream.lat_spec(QD),
                  stream.ctx_spec(D_MODEL), stream.lat_spec(D_MODEL), stream.mod_spec(),
                  _const_spec((4, D_MODEL)), _const_spec((3, CONV_DIM)), hbm, hbm, hbm],
        out_specs=stream.spec(D_MODEL),
        out_shape=jax.ShapeDtypeStruct((stream.t, D_MODEL), F32),
        scratch_shapes=_tail_weight_scratch(),
        compiler_params=_cparams("arbitrary"),
        name="postmix0_mlp",
    )(zc, zc, zc, attn_ctx, attn_lat, x_ctx, x_lat, mod, norm_g, conv_w, w_out, w1, w2)


def _premix1_kernel(x_ref, mod_ref, g_ref, w_ref, wg_ref, gb_ref,
                    q_ref, k_ref, v_ref, og_ref, gf_ref, gbk_ref, wb_ref, wr_ref):
    @pl.when(pl.program_id(0) == 0)
    def _():
        for c in range(0, GLA_MAIN, GLA_QK):
            wb_ref[:, c:c + GLA_QK] = w_ref[c:c + GLA_QK, :].T.astype(BF16)
        wr_ref[...] = w_ref[GLA_MAIN:, :].astype(BF16)

    sh1, sc1 = _mod_chunk(mod_ref, 0), _mod_chunk(mod_ref, 1)
    part = x_ref.shape[0] // PREMIX_PARTS
    for r0 in range(0, x_ref.shape[0], part):
        rows = slice(r0, r0 + part)
        hb = (_rms(x_ref[rows, :], g_ref[0:1, :]) * (1.0 + sc1) + sh1).astype(BF16)
        z = jnp.dot(hb, wb_ref[...], preferred_element_type=F32)
        q_ref[rows, :] = z[:, :GLA_QK] * (GLA_DK ** -0.5)
        k_ref[rows, :] = z[:, GLA_QK:2 * GLA_QK]
        v_ref[rows, :] = z[:, 2 * GLA_QK:2 * GLA_QK + GLA_VD].astype(BF16)
        og_ref[rows, :] = z[:, 2 * GLA_QK + GLA_VD:]
        r = _bdot_nt(hb, wr_ref[...])
        pre = _bdot(r, wg_ref[...]) + gb_ref[...]
        soft = jnp.log2(1.0 + jnp.exp2(jnp.abs(pre) * (-LOG2E)))
        gate = (jnp.minimum(pre, 0.0) * LOG2E - soft) * (1.0 / GLA_GATE_NORM)
        gf_ref[rows, :] = gate[:, :GLA_QK]
        gbk_ref[rows, :] = gate[:, GLA_QK:]


def _premix1(stream, x2d, mod, norm_g, w_in_t, w_gate, gate_bias):
    t, tm = stream.t, stream.tm
    w_in = w_in_t
    assert w_in.shape == (1, GLA_MAIN + 2 * GLA_RANK, D_MODEL)
    return pl.pallas_call(
        _premix1_kernel,
        grid=(t // tm,),
        in_specs=[_row_spec(tm, D_MODEL), stream.mod_spec(), _const_spec((4, D_MODEL)),
                  _f32_weight_spec(w_in.shape), _const_spec((2 * GLA_RANK, 2 * GLA_QK)),
                  _const_spec((1, 2 * GLA_QK))],
        out_specs=[_row_spec(tm, GLA_QK), _row_spec(tm, GLA_QK), _row_spec(tm, GLA_VD),
                   _row_spec(tm, GLA_VD), _row_spec(tm, GLA_QK), _row_spec(tm, GLA_QK)],
        out_shape=[jax.ShapeDtypeStruct((t, GLA_QK), F32), jax.ShapeDtypeStruct((t, GLA_QK), F32),
                   jax.ShapeDtypeStruct((t, GLA_VD), BF16),
                   jax.ShapeDtypeStruct((t, GLA_VD), F32), jax.ShapeDtypeStruct((t, GLA_QK), F32),
                   jax.ShapeDtypeStruct((t, GLA_QK), F32)],
        scratch_shapes=[pltpu.VMEM((D_MODEL, GLA_MAIN), BF16),
                        pltpu.VMEM((2 * GLA_RANK, D_MODEL), BF16)],
        compiler_params=_cparams("arbitrary"),
        name="premix1",
    )(x2d, mod, norm_g, w_in, w_gate, gate_bias)


def _split3(x):
    hi = x.astype(BF16)
    r1 = x - hi.astype(F32)
    mid = r1.astype(BF16)
    lo = (r1 - mid.astype(F32)).astype(BF16)
    return hi, mid, lo


def _level_exponent(b, s, reverse):
    idx = s if reverse else s - 1
    if s >= 8:
        n = GLA_TILE // (2 * s)
        b4 = b.reshape(n, 2, s, GLA_DK)
        first, second = b4[:, 0:1], b4[:, 1:2]
        r = (second[:, :, 0:1] if reverse else first[:, :, s - 1:s])
        parts = [first - r, r - second] if reverse else [r - first, second - r]
        return jnp.concatenate(parts, axis=1).reshape(GLA_TILE, GLA_DK)
    b8 = b.reshape(GLA_TILE // 8, 8, GLA_DK)
    sub = lax.broadcasted_iota(jnp.int32, (1, 8, 1), 1)
    if s == 4:
        r = b8[:, idx:idx + 1, :]
    else:
        assert s == 2
        r = jnp.where(sub < 4, b8[:, idx:idx + 1, :], b8[:, 4 + idx:5 + idx, :])
    in_second = (sub // s) % 2 == 1
    sign = jnp.where(in_second != reverse, 1.0, -1.0)
    return ((b8 - r) * sign).reshape(GLA_TILE, GLA_DK)


class _PairMatrix:
    LEVELS = [(1 << (lv - 1), lv) for lv in range(2, GLA_TILE.bit_length())]

    def __init__(self, q, k, g_f, g_b, b_f, b_b, lvl):
        half = GLA_TILE // 2
        self.lo, self.hi = slice(0, half), slice(half, GLA_TILE)
        self.q_bf, self.k_bf = q.astype(BF16), k.astype(BF16)
        self.b_f, self.b_b, self.lvl = b_f, b_b, lvl
        self.row = lax.broadcasted_iota(jnp.int32, (GLA_TILE, 1), 0)
        odd = self.row % 2 == 1
        own = jnp.sum(q * k, axis=-1, keepdims=True)
        k_adj = jnp.where(odd, pltpu.roll(k, 1, axis=0), pltpu.roll(k, GLA_TILE - 1, axis=0))
        adj = jnp.sum(q * jnp.exp2(jnp.where(odd, g_f, g_b)) * k_adj, axis=-1, keepdims=True)
        self.blocks = [jnp.where(lvl == 0, 2.0 * own[rows], jnp.where(lvl == 1, adj[rows], 0.0))
                       for rows in (self.lo, self.hi)]
        self.cross_f = self.cross_b = None

    def operands(self, s):
        qb, kb, b_f, b_b = self.q_bf, self.k_bf, self.b_f, self.b_b
        if s >= 16:
            n = GLA_TILE // (2 * s)

            def halves(x):
                x4 = x.reshape(n, 2, s, GLA_DK)
                return x4[:, 0:1], x4[:, 1:2]

            def rows(first, second):
                return jnp.concatenate([first, second], axis=1).reshape(GLA_TILE, GLA_DK)

            (bf1, bf2), (bb1, bb2) = halves(b_f), halves(b_b)
            (q1, q2), (k1, k2) = halves(qb), halves(kb)
            r_f, r_b = bf1[:, :, s - 1:s], bb2[:, :, 0:1]
            zero = jnp.zeros_like(q1)
            lhs = jnp.concatenate(
                [rows(zero, q2 * jnp.exp2(bf2 - r_f).astype(BF16)),
                 rows(q1 * jnp.exp2(bb1 - r_b).astype(BF16), zero)], axis=1)
            rhs = jnp.concatenate(
                [rows(k1 * jnp.exp2(r_f - bf1).astype(BF16), zero),
                 rows(zero, k2 * jnp.exp2(r_b - bb2).astype(BF16))], axis=1)
        else:
            second = (self.row // s) % 2 == 1
            f_f = jnp.exp2(_level_exponent(b_f, s, False)).astype(BF16)
            f_b = jnp.exp2(_level_exponent(b_b, s, True)).astype(BF16)
            u = qb * jnp.where(second, f_f, f_b)
            w = kb * jnp.where(second, f_b, f_f)
            zero = jnp.zeros_like(u)
            lhs = jnp.concatenate([jnp.where(second, u, zero), jnp.where(second, zero, u)], axis=1)
            rhs = jnp.concatenate([jnp.where(second, zero, w), jnp.where(second, w, zero)], axis=1)
        return lhs, rhs

    def absorb(self, s, level, pairs):
        lo, hi = self.lo, self.hi
        if 2 * s == GLA_TILE:
            self.cross_f, self.cross_b = pairs[hi, lo], pairs[lo, hi]
        else:
            self.blocks = [jnp.where(self.lvl == level, pairs[lo, lo], self.blocks[0]),
                           jnp.where(self.lvl == level, pairs[hi, hi], self.blocks[1])]

    def matrix(self):
        top = jnp.concatenate([self.blocks[0].astype(BF16), self.cross_b.astype(BF16)], axis=1)
        bottom = jnp.concatenate([self.cross_f.astype(BF16), self.blocks[1].astype(BF16)], axis=1)
        return jnp.concatenate([top, bottom], axis=0)


def _pair_matrix(q, k, g_f, g_b, b_f, b_b, lvl):
    builder = _PairMatrix(q, k, g_f, g_b, b_f, b_b, lvl)
    for s, level in _PairMatrix.LEVELS:
        builder.absorb(s, level, _bdot_nt(*builder.operands(s)))
    return builder.matrix()


def _carry_state(q, k, b, v, s_ref, reverse):
    edge = 0 if reverse else GLA_TILE - 1
    b_last = b[edge:edge + 1, :]
    qe = (q * jnp.exp2(b)).astype(BF16)
    ke = (k * jnp.exp2(b_last - b)).astype(BF16)
    st = s_ref[...]
    s_ref[...] = st * jnp.exp2(b_last) + lax.dot_general(
        v, ke, TN_DIMS, preferred_element_type=F32)
    return _bdot_nt(qe, st)


def _gla_kernel(*refs, zero_init, nt):
    if zero_init:
        (tri_ref, lvl_ref, q_ref, k_ref, gf_ref, gb_ref, v_ref,
         o_ref, sfo_ref, sbo_ref, sf_ref, sb_ref, bb_ref) = refs
        sf_ref[...] = jnp.zeros_like(sf_ref)
        sb_ref[...] = jnp.zeros_like(sb_ref)
    else:
        (tri_ref, lvl_ref, q_ref, k_ref, gf_ref, gb_ref, v_ref, s0f_ref, s0b_ref,
         o_ref, sfo_ref, sbo_ref, sf_ref, sb_ref, bb_ref) = refs
        for hh in range(sf_ref.shape[0]):
            sf_ref[hh] = s0f_ref[hh].T
            sb_ref[hh] = s0b_ref[hh].T
    hp = sf_ref.shape[0]

    def tile_rows(tile):
        r0 = tile * GLA_TILE
        return pl.ds(r0 if isinstance(r0, int) else pl.multiple_of(r0, GLA_TILE), GLA_TILE)

    heads = [(slice(hh * GLA_DK, (hh + 1) * GLA_DK), slice(hh * GLA_DV, (hh + 1) * GLA_DV), hh)
             for hh in range(hp)]

    def forward_sweep(t, carry):
        rows = tile_rows(t)
        g_f, g_b = gf_ref[rows, :], gb_ref[rows, :]
        c = None
        for part in _split3(jnp.concatenate([g_f, g_b], axis=1)):
            term = jnp.dot(tri_ref[...], part, preferred_element_type=F32)
            c = term if c is None else c + term
        width = hp * GLA_DK
        c_b = c[:, width:]
        b_f = c[:, :width]
        b_b = (c_b[GLA_TILE - 1:GLA_TILE, :] - c_b) + g_b
        bb_ref[t] = b_b
        for dk, dv, hh in heads:
            q, k, v = q_ref[rows, dk], k_ref[rows, dk], v_ref[rows, dv]
            att = _pair_matrix(q, k, g_f[:, dk], g_b[:, dk], b_f[:, dk], b_b[:, dk], lvl_ref[...])
            o_ref[rows, dv] = (jnp.dot(att, v, preferred_element_type=F32)
                               + _carry_state(q, k, b_f[:, dk], v, sf_ref.at[hh], False))
        return carry

    def backward_sweep(i, carry):
        t = nt - 1 - i
        rows = tile_rows(t)
        b_b = bb_ref[t]
        for dk, dv, hh in heads:
            o_ref[rows, dv] += _carry_state(q_ref[rows, dk], k_ref[rows, dk], b_b[:, dk],
                                            v_ref[rows, dv], sb_ref.at[hh], True)
        return carry

    if nt == 1:
        forward_sweep(0, 0)
        backward_sweep(0, 0)
    else:
        lax.fori_loop(0, nt, forward_sweep, 0)
        lax.fori_loop(0, nt, backward_sweep, 0)
    for hh in range(hp):
        sfo_ref[hh] = sf_ref[hh].T
        sbo_ref[hh] = sb_ref[hh].T


def _gla_scan(grp, row0, q, k, v, gf, gb, s0f, s0b, tri, lvl):
    n = grp.n
    nt = n // GLA_TILE
    per_head = 2 * n * (4 * GLA_DK * 4 + GLA_DV * 2 + GLA_DV * 4)
    hp = GLA_HEADS
    while hp > 1 and hp * per_head > GLA_SEQ_VMEM_BUDGET:
        hp //= 2
    zero_init = s0f is None
    half = GLA_TILE // 2
    assert row0 % n == 0
    seq0 = row0 // n
    in_dk = pl.BlockSpec((n, hp * GLA_DK), lambda b, h: (seq0 + b, h))
    in_dv = pl.BlockSpec((n, hp * GLA_DV), lambda b, h: (seq0 + b, h))
    seq_dv = pl.BlockSpec((n, hp * GLA_DV), lambda b, h: (b, h))
    state_spec = pl.BlockSpec((None, hp, GLA_DK, GLA_DV), lambda b, h: (b, h, 0, 0))
    in_specs = [_const_spec((GLA_TILE, GLA_TILE)), _const_spec((half, half)),
                in_dk, in_dk, in_dk, in_dk, in_dv]
    args = [tri, lvl, q, k, gf, gb, v]
    if not zero_init:
        in_specs += [state_spec, state_spec]
        args += [s0f, s0b]
    state_shape = jax.ShapeDtypeStruct((grp.b, GLA_HEADS, GLA_DK, GLA_DV), F32)
    return pl.pallas_call(
        functools.partial(_gla_kernel, zero_init=zero_init, nt=nt),
        grid=(grp.b, GLA_HEADS // hp),
        in_specs=in_specs,
        out_specs=[seq_dv, state_spec, state_spec],
        out_shape=[jax.ShapeDtypeStruct((grp.t, GLA_VD), F32), state_shape, state_shape],
        scratch_shapes=[pltpu.VMEM((hp, GLA_DV, GLA_DK), F32), pltpu.VMEM((hp, GLA_DV, GLA_DK), F32),
                        pltpu.VMEM((nt, GLA_TILE, hp * GLA_DK), F32)],
        compiler_params=_cparams("arbitrary", "arbitrary"),
        name="gla_scan",
    )(*args)


def _gla_constants():
    half = GLA_TILE // 2
    i = jnp.arange(GLA_TILE)[:, None]
    j = jnp.arange(GLA_TILE)[None, :]
    tri = (j <= i).astype(BF16)
    ih, jh = i[:half], j[:, :half]
    x = jnp.bitwise_xor(ih, jh)
    lvl = sum((x >= (1 << p)).astype(jnp.int32) for p in range(half.bit_length() - 1))
    return tri, lvl


def _postmix1_kernel(oc_ref, ol_ref, og_ref, x_ref, mod_ref, g_ref, gn_ref, wo_hbm, w1_hbm, w2_hbm,
                     yc_ref, yl_ref, *scratch, stream):
    weights = _TailWeights(1, wo_hbm, w1_hbm, w2_hbm, *scratch)
    _first_step_streams_weights(
        functools.partial(_postmix1_body, oc_ref, ol_ref, og_ref, x_ref, mod_ref, g_ref, gn_ref,
                          yc_ref, yl_ref, weights, stream), weights)


def _postmix1_body(oc_ref, ol_ref, og_ref, x_ref, mod_ref, g_ref, gn_ref, yc_ref, yl_ref,
                   weights, stream, streaming):
    is_lat = stream.is_lat()
    gn = gn_ref[...]
    ys = []
    for h in range(GLA_HEADS):
        cols = slice(h * GLA_DV, (h + 1) * GLA_DV)
        o = _rms(jnp.where(is_lat, ol_ref[:, cols], oc_ref[:, cols]), gn)
        og = og_ref[:, cols]
        ys.append((o * (og / (1.0 + jnp.exp(-og)))).astype(BF16))
    if streaming:
        weights.fetch_out_proj()
    mix = None
    for h, y in enumerate(ys):
        part = jnp.dot(y, weights.wo_s[h * GLA_DV:(h + 1) * GLA_DV, :], preferred_element_type=F32)
        mix = part if mix is None else mix + part
    gt1 = _mod_chunk(mod_ref, 2)
    x1 = x_ref[...] + gt1 * _rms(mix, g_ref[1:2, :])
    y = _mlp_tail(x1, mod_ref, g_ref, weights, streaming)

    @pl.when(jnp.logical_not(is_lat))
    def _():
        yc_ref[...] = y

    @pl.when(is_lat)
    def _():
        yl_ref[...] = y


def _postmix1(stream, o_ctx, o_lat, og, x2d, mod, norm_g, gla_norm_g, w_out, w1, w2):
    hbm = pl.BlockSpec(memory_space=pl.ANY)
    return pl.pallas_call(
        functools.partial(_postmix1_kernel, stream=stream),
        grid=(stream.tiles,),
        in_specs=[stream.ctx_spec(GLA_VD), stream.lat_spec(GLA_VD), stream.spec(GLA_VD),
                  stream.spec(D_MODEL), stream.mod_spec(), _const_spec((4, D_MODEL)),
                  _const_spec((1, GLA_DV)), hbm, hbm, hbm],
        out_specs=[stream.ctx_spec(D_MODEL), stream.lat_spec(D_MODEL)],
        out_shape=[jax.ShapeDtypeStruct((stream.ctx.t, D_MODEL), F32),
                   jax.ShapeDtypeStruct((stream.lat.t, D_MODEL), F32)],
        scratch_shapes=_tail_weight_scratch(),
        compiler_params=_cparams("arbitrary"),
        name="postmix1_mlp",
    )(o_ctx, o_lat, og, x2d, mod, norm_g, gla_norm_g, w_out, w1, w2)


def kernel(x_prompt, x_sample, cache_k, cache_v, state_fwd, state_bwd, c, c_ctx, mod_w, mod_b,
           norm_g, ab_w_in, conv_w, attn_sink, ab_w_out, gla_w_in, gla_gate_w, gla_gate_b,
           gla_norm_g, gla_w_out, mlp_w1, mlp_w2):
    b_ctx, n_ctx, _ = x_prompt.shape
    b_lat, n_lat, _ = x_sample.shape
    assert mod_w.shape[0] == 2 and ab_w_in.shape[0] == 1 and gla_w_in.shape[0] == 1
    assert 1 + b_lat <= 8

    cond8 = jnp.zeros((8, D_MODEL), F32).at[0].set(c_ctx).at[1:1 + b_lat].set(c)
    mod = _modulation(cond8, mod_w, mod_b)
    mods = [mod[l].reshape(8, 1, -1) for l in range(2)]

    w_gate = jnp.zeros((2 * GLA_RANK, 2 * GLA_QK), F32)
    w_gate = w_gate.at[:GLA_RANK, :GLA_QK].set(gla_gate_w[0, 0])
    w_gate = w_gate.at[GLA_RANK:, GLA_QK:].set(gla_gate_w[0, 1])
    tri, lvl = _gla_constants()
    p = {
        "conv_w": conv_w[0],
        "sink": attn_sink[0],
        "gla_w_in": jnp.swapaxes(gla_w_in, 1, 2),
        "gla_w_gate": w_gate.astype(BF16),
        "gla_gate_bias": gla_gate_b[0].reshape(1, 2 * GLA_QK),
        "gla_norm_g": gla_norm_g[0].reshape(1, GLA_DV),
    }

    ctx, lat = _Group(b_ctx, n_ctx), _Group(b_lat, n_lat)
    stream = _Stream(ctx, lat, TOKEN_TILE)
    x_ctx, x_lat = x_prompt.reshape(ctx.t, D_MODEL), x_sample.reshape(lat.t, D_MODEL)
    past = cache_k.shape[2]
    k_ctx = cache_k[:, 0].reshape(b_lat, past, KD)
    v_ctx = cache_v[:, 0].reshape(b_lat, past, KD)

    wide = _Stream(ctx, lat, PREMIX_PARTS * TOKEN_TILE)
    zc, q, kv, k_t, v_t = _premix0(wide, x_ctx, x_lat, mods[0], norm_g[0], ab_w_in,
                                   _rope_tables(n_lat, wide.tm))
    attn_ctx = _ctx_attention(ctx, q, kv, p["sink"])
    attn_lat = _lat_attention(lat, ctx.t, q, kv, k_ctx, v_ctx, p["sink"])
    x1 = _postmix0(stream, zc, attn_ctx, attn_lat, x_ctx, x_lat, mods[0], norm_g[0], p["conv_w"],
                   ab_w_out, mlp_w1, mlp_w2)
    gq, gk, gv, og, gf, gb = _premix1(wide, x1, mods[1], norm_g[1], p["gla_w_in"],
                                      p["gla_w_gate"], p["gla_gate_bias"])
    o_ctx, sf, sb = _gla_scan(ctx, 0, gq, gk, gv, gf, gb, None, None, tri, lvl)
    o_lat, _, _ = _gla_scan(lat, ctx.t, gq, gk, gv, gf, gb, state_fwd[:, 0], state_bwd[:, 0],
                            tri, lvl)
    y_ctx, y_lat = _postmix1(stream, o_ctx, o_lat, og, x1, mods[1], norm_g[1], p["gla_norm_g"],
                             gla_w_out, mlp_w1, mlp_w2)

    def cache_layout(t):
        t = t.reshape(b_ctx, 1, N_KV_HEADS, HEAD_DIM, n_ctx)
        return jnp.transpose(t, (0, 1, 4, 2, 3))

    return (y_ctx.reshape(x_prompt.shape), y_lat.reshape(x_sample.shape),
            cache_layout(k_t), cache_layout(v_t), sf[:, None], sb[:, None])
```

```python
import functools

import jax
import jax.numpy as jnp
from jax import lax
from jax.experimental import pallas as pl
from jax.experimental.pallas import tpu as pltpu

F32 = jnp.float32
BF16 = jnp.bfloat16

D_MODEL = 1024
MOD_CHUNKS = 6
EPS = 1e-6
CONV_DIM = 512
N_Q_HEADS = 8
N_KV_HEADS = 2
GQA_GROUP = 4
HEAD_DIM = 64
WINDOW = 128
ATTN_BLOCK = 128
ATTN_BLOCKS_PER_STEP = 16
CTX_SEQS_PER_STEP = 4
GRID_W = 64
ROPE_BASE = 10000.0
QD = N_Q_HEADS * HEAD_DIM
KD = N_KV_HEADS * HEAD_DIM
AB_IN = 3 * CONV_DIM + QD + 2 * KD
GLA_HEADS = 4
GLA_DK = 128
GLA_DV = 256
GLA_RANK = 16
GLA_GATE_NORM = 16.0
GLA_TILE = 256
GLA_SEQ_VMEM_BUDGET = 32 * 1024 * 1024
LOG2E = 1.4426950408889634
GLA_QK = GLA_HEADS * GLA_DK
GLA_VD = GLA_HEADS * GLA_DV
GLA_MAIN = 2 * GLA_QK + 2 * GLA_VD
D_FF = 4 * D_MODEL
MLP_CHUNK = 512
MLP_CHUNKS = D_FF // MLP_CHUNK
WEIGHT_PIECE = 256
TOKEN_TILE = 512
PREMIX_PARTS = 2
NEG_INF = -1e30
LANES = 128
VMEM_LIMIT = 60 * 1024 * 1024

NT_DIMS = (((1,), (1,)), ((), ()))
TN_DIMS = (((0,), (0,)), ((), ()))


def _cparams(*sem):
    return pltpu.CompilerParams(dimension_semantics=sem, vmem_limit_bytes=VMEM_LIMIT)


def _bdot(a, b):
    return jnp.dot(a.astype(BF16), b.astype(BF16), preferred_element_type=F32)


def _bdot_nt(a, b):
    return lax.dot_general(a.astype(BF16), b.astype(BF16), NT_DIMS, preferred_element_type=F32)


def _rms(x, g):
    ms = jnp.mean(x * x, axis=-1, keepdims=True)
    return x * lax.rsqrt(ms + EPS) * g


def _mod_chunk(mod_ref, i):
    return mod_ref[:, i * D_MODEL:(i + 1) * D_MODEL]


def _const_spec(shape):
    return pl.BlockSpec(shape, lambda *_: (0,) * len(shape))


def _f32_weight_spec(shape):
    assert shape[0] == 1
    return pl.BlockSpec((None,) + tuple(shape[1:]), lambda *_: (0, 0, 0),
                        pipeline_mode=pl.Buffered(1))


def _cast_once(w_ref, wb_ref):
    @pl.when(pl.program_id(0) == 0)
    def _():
        wb_ref[...] = w_ref[...].astype(BF16)


def _mod_kernel(cond_ref, w_ref, b_ref, o_ref):
    cnd = cond_ref[...]
    s = cnd / (1.0 + jnp.exp(-cnd))
    o_ref[...] = _bdot(s, w_ref[...]) + b_ref[...]


def _modulation(cond8, mod_w, mod_b):
    depth = mod_w.shape[0]
    n = mod_w.shape[2]
    tn = 1536
    return pl.pallas_call(
        _mod_kernel,
        grid=(depth, n // tn),
        in_specs=[
            pl.BlockSpec((8, D_MODEL), lambda l, j: (0, 0)),
            pl.BlockSpec((None, D_MODEL, tn), lambda l, j: (l, 0, j)),
            pl.BlockSpec((None, 1, tn), lambda l, j: (l, 0, j)),
        ],
        out_specs=pl.BlockSpec((None, 8, tn), lambda l, j: (l, 0, j)),
        out_shape=jax.ShapeDtypeStruct((depth, 8, n), F32),
        compiler_params=_cparams("arbitrary", "arbitrary"),
        name="modulation",
    )(cond8, mod_w, mod_b.reshape(depth, 1, n))


class _Group:
    def __init__(self, b, n):
        self.b, self.n, self.t = b, n, b * n


class _Stream:
    def __init__(self, ctx, lat, tm):
        assert ctx.t % tm == 0 and lat.n % tm == 0 and tm % ctx.n == 0
        self.ctx, self.lat, self.tm = ctx, lat, tm
        self.t = ctx.t + lat.t
        self.ctx_tiles = ctx.t // tm
        self.tiles = self.t // tm

    def is_lat(self):
        return pl.program_id(0) >= self.ctx_tiles

    def spec(self, width):
        return pl.BlockSpec((self.tm, width), lambda t: (t, 0))

    def ctx_spec(self, width):
        last = self.ctx_tiles - 1
        return pl.BlockSpec((self.tm, width), lambda t: (jnp.minimum(t, last), 0))

    def lat_spec(self, width):
        first = self.ctx_tiles
        return pl.BlockSpec((self.tm, width), lambda t: (jnp.maximum(t - first, 0), 0))

    def mod_spec(self):
        first, per = self.ctx_tiles, self.lat.n // self.tm
        return pl.BlockSpec((None, 1, MOD_CHUNKS * D_MODEL),
                            lambda t: (jnp.where(t < first, 0, 1 + (t - first) // per), 0, 0))

    def seq_len(self):
        return jnp.where(self.is_lat(), self.lat.n, self.ctx.n)


def _row_spec(tm, width):
    return pl.BlockSpec((tm, width), lambda t: (t, 0))


def _rope(x, cos, sin_lo, sin_hi):
    return (x * cos + pltpu.roll(x, LANES - 16, axis=1) * sin_lo
            + pltpu.roll(x, 16, axis=1) * sin_hi)


def _premix0_kernel(xc_ref, xl_ref, mod_ref, g_ref, w_ref, cos_ref, slo_ref, shi_ref,
                    zc_ref, q_ref, kv_ref, kt_ref, vt_ref, wb_ref, *, stream):
    _cast_once(w_ref, wb_ref)
    is_lat = stream.is_lat()
    sh1, sc1 = _mod_chunk(mod_ref, 0), _mod_chunk(mod_ref, 1)
    c3 = 3 * CONV_DIM
    scale = HEAD_DIM ** -0.5 * LOG2E
    n = kt_ref.shape[2]
    part = stream.tm // PREMIX_PARTS
    raw_kv = []
    for r0 in range(0, stream.tm, part):
        rows = slice(r0, r0 + part)
        x = jnp.where(is_lat, xl_ref[rows, :], xc_ref[rows, :])
        h = _rms(x, g_ref[0:1, :]) * (1.0 + sc1) + sh1
        z = _bdot(h, wb_ref[...])
        zc_ref[rows, :] = z[:, :c3]
        cos, slo, shi = cos_ref[rows, :], slo_ref[rows, :], shi_ref[rows, :]
        for j in range(QD // LANES):
            qs = z[:, c3 + j * LANES:c3 + (j + 1) * LANES]
            q_ref[rows, j * LANES:(j + 1) * LANES] = (_rope(qs, cos, slo, shi) * scale).astype(BF16)
        kv_ref[rows, :KD] = _rope(z[:, c3 + QD:c3 + QD + KD], cos, slo, shi)
        kv_ref[rows, KD:] = z[:, c3 + QD + KD:]
        raw_kv.append(z[:, c3 + QD:])

    @pl.when(jnp.logical_not(is_lat))
    def _():
        for p, kv in enumerate(raw_kv):
            for j in range(part // n):
                seq = p * (part // n) + j
                kt_ref[seq] = kv[j * n:(j + 1) * n, :KD].T
                vt_ref[seq] = kv[j * n:(j + 1) * n, KD:].T


def _premix0(stream, x_ctx, x_lat, mod, norm_g, w_in, rope_tabs):
    tm, ctx, lat = stream.tm, stream.ctx, stream.lat
    first, per, last = stream.ctx_tiles, lat.n // tm, stream.ctx_tiles - 1
    rope_spec = pl.BlockSpec((tm, LANES),
                             lambda t: (jnp.where(t < first, 0, 1 + (t - first) % per), 0))
    cache_spec = pl.BlockSpec((tm // ctx.n, KD, ctx.n), lambda t: (jnp.minimum(t, last), 0, 0))
    return pl.pallas_call(
        functools.partial(_premix0_kernel, stream=stream),
        grid=(stream.tiles,),
        in_specs=[stream.ctx_spec(D_MODEL), stream.lat_spec(D_MODEL), stream.mod_spec(),
                  _const_spec((4, D_MODEL)), _f32_weight_spec(w_in.shape)] + [rope_spec] * 3,
        out_specs=[stream.spec(3 * CONV_DIM), stream.spec(QD), stream.spec(2 * KD),
                   cache_spec, cache_spec],
        out_shape=[jax.ShapeDtypeStruct((stream.t, 3 * CONV_DIM), F32),
                   jax.ShapeDtypeStruct((stream.t, QD), BF16),
                   jax.ShapeDtypeStruct((stream.t, 2 * KD), F32),
                   jax.ShapeDtypeStruct((ctx.b, KD, ctx.n), F32),
                   jax.ShapeDtypeStruct((ctx.b, KD, ctx.n), F32)],
        scratch_shapes=[pltpu.VMEM(w_in.shape[1:], BF16)],
        compiler_params=_cparams("arbitrary"),
        name="premix0",
    )(x_ctx, x_lat, mod, norm_g, w_in, *rope_tabs)


def _rope_tables(n, identity_rows):
    rows = n // GRID_W
    pos_r = jnp.repeat(jnp.arange(rows), GRID_W)
    pos_c = jnp.tile(jnp.arange(GRID_W), rows)
    half = HEAD_DIM // 2
    quarter = half // 2
    inv = ROPE_BASE ** (-(jnp.arange(quarter, dtype=F32) * 2.0 / half))

    def cs(pos):
        ang = pos.astype(F32)[:, None] * inv[None, :]
        return jnp.cos(ang), jnp.sin(ang)

    cr, sr = cs(pos_r)
    cc, sc = cs(pos_c)
    zero = jnp.zeros_like(sr)
    cos = jnp.concatenate([cr, cr, cc, cc], axis=1)
    sin_lo = jnp.concatenate([-sr, zero, -sc, zero], axis=1)
    sin_hi = jnp.concatenate([zero, sr, zero, sc], axis=1)
    rep = LANES // HEAD_DIM
    tables = []
    for t, ident in ((cos, 1.0), (sin_lo, 0.0), (sin_hi, 0.0)):
        head = jnp.full((identity_rows, LANES), ident, F32)
        tables.append(jnp.concatenate([head, jnp.tile(t, (1, rep))], axis=0))
    return tuple(tables)


def _attention_operands(k_all, v_all):
    assert KD == LANES == 2 * HEAD_DIM and GQA_GROUP == 4
    lane = lax.broadcasted_iota(jnp.int32, (1, LANES), 1)
    sub = lax.broadcasted_iota(jnp.int32, (LANES, 1), 0)
    v_t = v_all.T
    k_swapped = pltpu.roll(k_all, HEAD_DIM, axis=1)
    ops = []
    for g in range(N_KV_HEADS):
        k_low, k_high = (k_all, k_swapped) if g == 0 else (k_swapped, k_all)
        kz_even = jnp.where(lane < HEAD_DIM, k_low, 0.0).astype(BF16)
        kz_odd = jnp.where(lane >= HEAD_DIM, k_high, 0.0).astype(BF16)
        own = (sub < HEAD_DIM) if g == 0 else (sub >= HEAD_DIM)
        v_ext_t = jnp.where(own, v_t, 1.0).astype(BF16)
        ops.append((kz_even, kz_odd, v_ext_t))
    return ops


def _concat_operands(parts):
    return [(jnp.concatenate([p[g][0] for p in parts], axis=0),
             jnp.concatenate([p[g][1] for p in parts], axis=0),
             jnp.concatenate([p[g][2] for p in parts], axis=1)) for g in range(N_KV_HEADS)]


def _group_heads(g):
    return [4 * g, 4 * g + 2, 4 * g + 1, 4 * g + 3]


def _scores(sink_ref, q_ref, operands, bias_t, g):
    m = q_ref.shape[0]
    kz_even, kz_odd, _ = operands[g]
    qq = jnp.concatenate([q_ref[:, (2 * g) * LANES:(2 * g + 1) * LANES],
                          q_ref[:, (2 * g + 1) * LANES:(2 * g + 2) * LANES]], axis=0)
    s = jnp.concatenate([_bdot_nt(kz_even, qq), _bdot_nt(kz_odd, qq)], axis=1)
    if bias_t is not None:
        nb = bias_t.shape[0]
        s = jnp.concatenate([s[:nb] + jnp.concatenate([bias_t] * GQA_GROUP, axis=1), s[nb:]],
                            axis=0)
    sink = jnp.concatenate([jnp.full((1, m), sink_ref[h] * LOG2E, F32) for h in _group_heads(g)],
                           axis=1)
    mx = jnp.maximum(jnp.max(s, axis=0, keepdims=True), sink)
    return s, sink, mx


def _weighted_values(s, sink, mx, operands, g, m):
    v_ext_t = operands[g][2]
    p = jnp.exp2(s - mx).astype(BF16)
    oe = jnp.dot(v_ext_t, p, preferred_element_type=F32)
    other = (1 - g) * HEAD_DIM
    den = oe[other:other + 1] + jnp.exp2(sink - mx)
    o_g = oe[g * HEAD_DIM:(g + 1) * HEAD_DIM] / den
    return {h: o_g[:, i * m:(i + 1) * m] for i, h in enumerate(_group_heads(g))}


def _sink_attention(sink_ref, blocks):
    chains = [(blk, g) for blk in range(len(blocks)) for g in range(N_KV_HEADS)]
    outs = [dict() for _ in blocks]

    def scores(chain):
        blk, g = chain
        q_ref, operands, bias_t, _ = blocks[blk]
        return _scores(sink_ref, q_ref, operands, bias_t, g)

    ahead = scores(chains[0])
    for i, (blk, g) in enumerate(chains):
        current = ahead
        if i + 1 < len(chains):
            ahead = scores(chains[i + 1])
        q_ref, operands, _, o_ref = blocks[blk]
        outs[blk].update(_weighted_values(*current, operands, g, q_ref.shape[0]))
        if g == N_KV_HEADS - 1:
            for j in range(N_Q_HEADS // 2):
                pair_t = jnp.concatenate([outs[blk][2 * j], outs[blk][2 * j + 1]], axis=0)
                o_ref[:, j * LANES:(j + 1) * LANES] = pair_t.T.astype(BF16)


def _ctx_attn_kernel(sink_ref, q_ref, kv_ref, o_ref, *, n):
    blocks = []
    for j in range(q_ref.shape[0] // n):
        rows = pl.ds(j * n, n)
        operands = _attention_operands(kv_ref[rows, :KD], kv_ref[rows, KD:])
        blocks.append((q_ref.at[rows], operands, None, o_ref.at[rows]))
    _sink_attention(sink_ref, blocks)


def _ctx_attention(grp, q, kv, sink):
    per = CTX_SEQS_PER_STEP
    assert grp.b % per == 0
    n = per * grp.n
    return pl.pallas_call(
        functools.partial(_ctx_attn_kernel, n=grp.n),
        grid=(grp.b // per,),
        in_specs=[pl.BlockSpec(memory_space=pltpu.SMEM), _row_spec(n, QD), _row_spec(n, 2 * KD)],
        out_specs=_row_spec(n, QD),
        out_shape=jax.ShapeDtypeStruct((grp.t, QD), BF16),
        compiler_params=_cparams("arbitrary"),
        name="ctx_attention",
    )(sink, q, kv)


def _lat_attn_kernel(sink_ref, q_ref, kvp_ref, kvc_ref, kvn_ref, kc_ref, vc_ref, o_ref, *, n):
    band = jnp.concatenate([kvp_ref[...], kvc_ref[...], kvn_ref[...]], axis=0)
    kj = lax.broadcasted_iota(jnp.int32, (3 * ATTN_BLOCK, ATTN_BLOCK), 0)
    qi = lax.broadcasted_iota(jnp.int32, (3 * ATTN_BLOCK, ATTN_BLOCK), 1)
    rel = kj - ATTN_BLOCK - qi
    ctx_ops = _attention_operands(kc_ref[...], vc_ref[...])
    band_ops = [_attention_operands(band[i * ATTN_BLOCK:(i + 1) * ATTN_BLOCK, :KD],
                                    band[i * ATTN_BLOCK:(i + 1) * ATTN_BLOCK, KD:])
                for i in range(ATTN_BLOCKS_PER_STEP + 2)]
    blocks = []
    for j in range(ATTN_BLOCKS_PER_STEP):
        blk = pl.program_id(1) * ATTN_BLOCKS_PER_STEP + j
        kpos = (blk - 1) * ATTN_BLOCK + kj
        valid = (jnp.abs(rel) <= WINDOW) & (kpos >= 0) & (kpos < n)
        bias_t = jnp.where(valid, 0.0, NEG_INF)
        operands = _concat_operands(band_ops[j:j + 3] + [ctx_ops])
        rows = pl.ds(j * ATTN_BLOCK, ATTN_BLOCK)
        blocks.append((q_ref.at[rows], operands, bias_t, o_ref.at[rows]))
    _sink_attention(sink_ref, blocks)


def _lat_attention(grp, row0, q, kv, k_ctx, v_ctx, sink):
    nb = grp.n // ATTN_BLOCK
    per = ATTN_BLOCKS_PER_STEP
    steps = nb // per
    past = k_ctx.shape[1]
    assert row0 % (per * ATTN_BLOCK) == 0
    blk0, step0 = row0 // ATTN_BLOCK, row0 // (per * ATTN_BLOCK)

    def edge(off):
        return pl.BlockSpec((ATTN_BLOCK, 2 * KD),
                            lambda b, i: (blk0 + b * nb + jnp.clip(i * per + off, 0, nb - 1), 0))

    ctx_spec = pl.BlockSpec((None, past, KD), lambda b, i: (b, 0, 0))
    return pl.pallas_call(
        functools.partial(_lat_attn_kernel, n=grp.n),
        grid=(grp.b, steps),
        in_specs=[pl.BlockSpec(memory_space=pltpu.SMEM),
                  pl.BlockSpec((per * ATTN_BLOCK, QD), lambda b, i: (step0 + b * steps + i, 0)),
                  edge(-1),
                  pl.BlockSpec((per * ATTN_BLOCK, 2 * KD), lambda b, i: (step0 + b * steps + i, 0)),
                  edge(per), ctx_spec, ctx_spec],
        out_specs=pl.BlockSpec((per * ATTN_BLOCK, QD), lambda b, i: (b * steps + i, 0)),
        out_shape=jax.ShapeDtypeStruct((grp.t, QD), BF16),
        compiler_params=_cparams("arbitrary", "arbitrary"),
        name="lat_attention",
    )(sink, q, kv, kv, kv, k_ctx, v_ctx)


class _TailWeights:
    def __init__(self, layer, wo_hbm, w1_hbm, w2_hbm, wo_s, w1_s, w2_s, stage1, stage2, sem):
        self.layer = layer
        self.wo_hbm, self.w1_hbm, self.w2_hbm = wo_hbm, w1_hbm, w2_hbm
        self.wo_s, self.w1_s, self.w2_s = wo_s, w1_s, w2_s
        self.stage1, self.stage2, self.sem = stage1, stage2, sem

    def _w1_copy(self, p):
        src = self.w1_hbm.at[self.layer, :, pl.ds(p * WEIGHT_PIECE, WEIGHT_PIECE)]
        return pltpu.make_async_copy(src, self.stage1.at[p % 2], self.sem.at[0, p % 2])

    def _w2_copy(self, p):
        src = self.w2_hbm.at[self.layer, pl.ds(p * WEIGHT_PIECE, WEIGHT_PIECE), :]
        return pltpu.make_async_copy(src, self.stage2.at[p % 2], self.sem.at[1, p % 2])

    def _wo_copy(self, p):
        src = self.wo_hbm.at[0, pl.ds(p * WEIGHT_PIECE, WEIGHT_PIECE), :]
        return pltpu.make_async_copy(src, self.stage2.at[p % 2], self.sem.at[1, p % 2])

    def start(self):
        self._wo_copy(0).start()
        self._wo_copy(1).start()
        self._w1_copy(0).start()
        self._w1_copy(1).start()

    def fetch_out_proj(self):
        pieces = D_MODEL // WEIGHT_PIECE
        for p in range(pieces):
            self._wo_copy(p).wait()
            self.wo_s[p * WEIGHT_PIECE:(p + 1) * WEIGHT_PIECE, :] = self.stage2[p % 2].astype(BF16)
            if p + 2 < pieces:
                self._wo_copy(p + 2).start()
            else:
                self._w2_copy(p + 2 - pieces).start()

    def fetch_mlp_chunk(self, j):
        per = MLP_CHUNK // WEIGHT_PIECE
        total = MLP_CHUNKS * per
        for p in range(j * per, (j + 1) * per):
            part = slice((p % per) * WEIGHT_PIECE, (p % per + 1) * WEIGHT_PIECE)
            self._w1_copy(p).wait()
            self.w1_s[j, :, part] = self.stage1[p % 2].astype(BF16)
            if p + 2 < total:
                self._w1_copy(p + 2).start()
            self._w2_copy(p).wait()
            self.w2_s[j, part, :] = self.stage2[p % 2].astype(BF16)
            if p + 2 < total:
                self._w2_copy(p + 2).start()


def _tail_weight_scratch():
    assert MLP_CHUNK % WEIGHT_PIECE == 0 and D_MODEL // WEIGHT_PIECE >= 2
    return [pltpu.VMEM((D_MODEL, D_MODEL), BF16),
            pltpu.VMEM((MLP_CHUNKS, D_MODEL, MLP_CHUNK), BF16),
            pltpu.VMEM((MLP_CHUNKS, MLP_CHUNK, D_MODEL), BF16),
            pltpu.VMEM((2, D_MODEL, WEIGHT_PIECE), F32),
            pltpu.VMEM((2, WEIGHT_PIECE, D_MODEL), F32),
            pltpu.SemaphoreType.DMA((2, 2))]


def _first_step_streams_weights(body, weights):
    first = pl.program_id(0) == 0

    @pl.when(first)
    def _():
        weights.start()
        body(True)

    @pl.when(jnp.logical_not(first))
    def _():
        body(False)


def _mlp_tail(x, mod_ref, g_ref, weights, streaming):
    sh2, sc2, gt2 = _mod_chunk(mod_ref, 3), _mod_chunk(mod_ref, 4), _mod_chunk(mod_ref, 5)
    hb = (_rms(x, g_ref[2:3, :]) * (1.0 + sc2) + sh2).astype(BF16)
    acc = None
    for j in range(MLP_CHUNKS):
        if streaming:
            weights.fetch_mlp_chunk(j)
        a = jnp.dot(hb, weights.w1_s[j], preferred_element_type=F32)
        a = jnp.maximum(a, 0.0)
        part = jnp.dot((a * a).astype(BF16), weights.w2_s[j], preferred_element_type=F32)
        acc = part if acc is None else acc + part
    return x + gt2 * _rms(acc, g_ref[3:4, :])


def _postmix0_kernel(zc_ref, zp_ref, zn_ref, atc_ref, atl_ref, xc_ref, xl_ref, mod_ref, g_ref,
                     cw_ref, wo_hbm, w1_hbm, w2_hbm, o_ref, *scratch, stream):
    weights = _TailWeights(0, wo_hbm, w1_hbm, w2_hbm, *scratch)
    _first_step_streams_weights(
        functools.partial(_postmix0_body, zc_ref, zp_ref, zn_ref, atc_ref, atl_ref, xc_ref, xl_ref,
                          mod_ref, g_ref, cw_ref, o_ref, weights, stream), weights)


def _postmix0_body(zc_ref, zp_ref, zn_ref, atc_ref, atl_ref, xc_ref, xl_ref, mod_ref, g_ref,
                   cw_ref, o_ref, weights, stream, streaming):
    c, tm = CONV_DIM, stream.tm
    is_lat, n = stream.is_lat(), stream.seq_len()
    zc = zc_ref[...]
    u = zc[:, c:2 * c] * zc[:, 2 * c:]
    u_before = zp_ref[7:8, c:2 * c] * zp_ref[7:8, 2 * c:]
    u_after = zn_ref[0:1, c:2 * c] * zn_ref[0:1, 2 * c:]
    row = lax.broadcasted_iota(jnp.int32, (tm, 1), 0)
    pos = (pl.program_id(0) * tm + row) & (n - 1)
    u_prev = jnp.where(row == 0, u_before, pltpu.roll(u, 1, axis=0))
    u_prev = jnp.where(pos == 0, 0.0, u_prev)
    u_next = jnp.where(row == tm - 1, u_after, pltpu.roll(u, tm - 1, axis=0))
    u_next = jnp.where(pos == n - 1, 0.0, u_next)
    conv = u_prev * cw_ref[0:1, :] + u * cw_ref[1:2, :] + u_next * cw_ref[2:3, :]
    if streaming:
        weights.fetch_out_proj()
    gated = (zc[:, :c] * conv).astype(BF16)
    gt1 = _mod_chunk(mod_ref, 2)
    x1_parts = []
    for r0 in range(0, tm, tm // 2):
        rows = slice(r0, r0 + tm // 2)
        attn = jnp.where(is_lat, atl_ref[rows, :], atc_ref[rows, :])
        mix = (jnp.dot(gated[rows], weights.wo_s[:c, :], preferred_element_type=F32)
               + jnp.dot(attn, weights.wo_s[c:, :], preferred_element_type=F32))
        x1_parts.append(jnp.where(is_lat, xl_ref[rows, :], xc_ref[rows, :])
                        + gt1 * _rms(mix, g_ref[1:2, :]))
    x1 = jnp.concatenate(x1_parts, axis=0)
    o_ref[...] = _mlp_tail(x1, mod_ref, g_ref, weights, streaming)


def _postmix0(stream, zc, attn_ctx, attn_lat, x_ctx, x_lat, mod, norm_g, conv_w, w_out, w1, w2):
    tm = stream.tm
    for n in (stream.ctx.n, stream.lat.n):
        assert n & (n - 1) == 0 and stream.ctx.t % n == 0
    r8 = tm // 8
    last8 = stream.t // 8 - 1
    hbm = pl.BlockSpec(memory_space=pl.ANY)
    return pl.pallas_call(
        functools.partial(_postmix0_kernel, stream=stream),
        grid=(stream.tiles,),
        in_specs=[stream.spec(3 * CONV_DIM),
                  pl.BlockSpec((8, 3 * CONV_DIM), lambda t: (jnp.maximum(t * r8 - 1, 0), 0)),
                  pl.BlockSpec((8, 3 * CONV_DIM), lambda t: (jnp.minimum((t + 1) * r8, last8), 0)),
                  stream.ctx_spec(QD), stream.lat_spec(QD),
                  stream.ctx_spec(D_MODEL), stream.lat_spec(D_MODEL), stream.mod_spec(),
                  _const_spec((4, D_MODEL)), _const_spec((3, CONV_DIM)), hbm, hbm, hbm],
        out_specs=stream.spec(D_MODEL),
        out_shape=jax.ShapeDtypeStruct((stream.t, D_MODEL), F32),
        scratch_shapes=_tail_weight_scratch(),
        compiler_params=_cparams("arbitrary"),
        name="postmix0_mlp",
    )(zc, zc, zc, attn_ctx, attn_lat, x_ctx, x_lat, mod, norm_g, conv_w, w_out, w1, w2)


def _premix1_kernel(x_ref, mod_ref, g_ref, w_ref, wg_ref, gb_ref,
                    q_ref, k_ref, v_ref, og_ref, gf_ref, gbk_ref, wb_ref, wr_ref):
    @pl.when(pl.program_id(0) == 0)
    def _():
        for c in range(0, GLA_MAIN, GLA_QK):
            wb_ref[:, c:c + GLA_QK] = w_ref[c:c + GLA_QK, :].T.astype(BF16)
        wr_ref[...] = w_ref[GLA_MAIN:, :].astype(BF16)

    sh1, sc1 = _mod_chunk(mod_ref, 0), _mod_chunk(mod_ref, 1)
    part = x_ref.shape[0] // PREMIX_PARTS
    for r0 in range(0, x_ref.shape[0], part):
        rows = slice(r0, r0 + part)
        hb = (_rms(x_ref[rows, :], g_ref[0:1, :]) * (1.0 + sc1) + sh1).astype(BF16)
        z = jnp.dot(hb, wb_ref[...], preferred_element_type=F32)
        q_ref[rows, :] = z[:, :GLA_QK] * (GLA_DK ** -0.5)
        k_ref[rows, :] = z[:, GLA_QK:2 * GLA_QK]
        v_ref[rows, :] = z[:, 2 * GLA_QK:2 * GLA_QK + GLA_VD].astype(BF16)
        og_ref[rows, :] = z[:, 2 * GLA_QK + GLA_VD:]
        r = _bdot_nt(hb, wr_ref[...])
        pre = _bdot(r, wg_ref[...]) + gb_ref[...]
        soft = jnp.log2(1.0 + jnp.exp2(jnp.abs(pre) * (-LOG2E)))
        gate = (jnp.minimum(pre, 0.0) * LOG2E - soft) * (1.0 / GLA_GATE_NORM)
        gf_ref[rows, :] = gate[:, :GLA_QK]
        gbk_ref[rows, :] = gate[:, GLA_QK:]


def _premix1(stream, x2d, mod, norm_g, w_in_t, w_gate, gate_bias):
    t, tm = stream.t, stream.tm
    w_in = w_in_t
    assert w_in.shape == (1, GLA_MAIN + 2 * GLA_RANK, D_MODEL)
    return pl.pallas_call(
        _premix1_kernel,
        grid=(t // tm,),
        in_specs=[_row_spec(tm, D_MODEL), stream.mod_spec(), _const_spec((4, D_MODEL)),
                  _f32_weight_spec(w_in.shape), _const_spec((2 * GLA_RANK, 2 * GLA_QK)),
                  _const_spec((1, 2 * GLA_QK))],
        out_specs=[_row_spec(tm, GLA_QK), _row_spec(tm, GLA_QK), _row_spec(tm, GLA_VD),
                   _row_spec(tm, GLA_VD), _row_spec(tm, GLA_QK), _row_spec(tm, GLA_QK)],
        out_shape=[jax.ShapeDtypeStruct((t, GLA_QK), F32), jax.ShapeDtypeStruct((t, GLA_QK), F32),
                   jax.ShapeDtypeStruct((t, GLA_VD), BF16),
                   jax.ShapeDtypeStruct((t, GLA_VD), F32), jax.ShapeDtypeStruct((t, GLA_QK), F32),
                   jax.ShapeDtypeStruct((t, GLA_QK), F32)],
        scratch_shapes=[pltpu.VMEM((D_MODEL, GLA_MAIN), BF16),
                        pltpu.VMEM((2 * GLA_RANK, D_MODEL), BF16)],
        compiler_params=_cparams("arbitrary"),
        name="premix1",
    )(x2d, mod, norm_g, w_in, w_gate, gate_bias)


def _split3(x):
    hi = x.astype(BF16)
    r1 = x - hi.astype(F32)
    mid = r1.astype(BF16)
    lo = (r1 - mid.astype(F32)).astype(BF16)
    return hi, mid, lo


def _level_exponent(b, s, reverse):
    idx = s if reverse else s - 1
    if s >= 8:
        n = GLA_TILE // (2 * s)
        b4 = b.reshape(n, 2, s, GLA_DK)
        first, second = b4[:, 0:1], b4[:, 1:2]
        r = (second[:, :, 0:1] if reverse else first[:, :, s - 1:s])
        parts = [first - r, r - second] if reverse else [r - first, second - r]
        return jnp.concatenate(parts, axis=1).reshape(GLA_TILE, GLA_DK)
    b8 = b.reshape(GLA_TILE // 8, 8, GLA_DK)
    sub = lax.broadcasted_iota(jnp.int32, (1, 8, 1), 1)
    if s == 4:
        r = b8[:, idx:idx + 1, :]
    else:
        assert s == 2
        r = jnp.where(sub < 4, b8[:, idx:idx + 1, :], b8[:, 4 + idx:5 + idx, :])
    in_second = (sub // s) % 2 == 1
    sign = jnp.where(in_second != reverse, 1.0, -1.0)
    return ((b8 - r) * sign).reshape(GLA_TILE, GLA_DK)


class _PairMatrix:
    LEVELS = [(1 << (lv - 1), lv) for lv in range(2, GLA_TILE.bit_length())]

    def __init__(self, q, k, g_f, g_b, b_f, b_b, lvl):
        half = GLA_TILE // 2
        self.lo, self.hi = slice(0, half), slice(half, GLA_TILE)
        self.q_bf, self.k_bf = q.astype(BF16), k.astype(BF16)
        self.b_f, self.b_b, self.lvl = b_f, b_b, lvl
        self.row = lax.broadcasted_iota(jnp.int32, (GLA_TILE, 1), 0)
        odd = self.row % 2 == 1
        own = jnp.sum(q * k, axis=-1, keepdims=True)
        k_adj = jnp.where(odd, pltpu.roll(k, 1, axis=0), pltpu.roll(k, GLA_TILE - 1, axis=0))
        adj = jnp.sum(q * jnp.exp2(jnp.where(odd, g_f, g_b)) * k_adj, axis=-1, keepdims=True)
        self.blocks = [jnp.where(lvl == 0, 2.0 * own[rows], jnp.where(lvl == 1, adj[rows], 0.0))
                       for rows in (self.lo, self.hi)]
        self.cross_f = self.cross_b = None

    def operands(self, s):
        qb, kb, b_f, b_b = self.q_bf, self.k_bf, self.b_f, self.b_b
        if s >= 16:
            n = GLA_TILE // (2 * s)

            def halves(x):
                x4 = x.reshape(n, 2, s, GLA_DK)
                return x4[:, 0:1], x4[:, 1:2]

            def rows(first, second):
                return jnp.concatenate([first, second], axis=1).reshape(GLA_TILE, GLA_DK)

            (bf1, bf2), (bb1, bb2) = halves(b_f), halves(b_b)
            (q1, q2), (k1, k2) = halves(qb), halves(kb)
            r_f, r_b = bf1[:, :, s - 1:s], bb2[:, :, 0:1]
            zero = jnp.zeros_like(q1)
            lhs = jnp.concatenate(
                [rows(zero, q2 * jnp.exp2(bf2 - r_f).astype(BF16)),
                 rows(q1 * jnp.exp2(bb1 - r_b).astype(BF16), zero)], axis=1)
            rhs = jnp.concatenate(
                [rows(k1 * jnp.exp2(r_f - bf1).astype(BF16), zero),
                 rows(zero, k2 * jnp.exp2(r_b - bb2).astype(BF16))], axis=1)
        else:
            second = (self.row // s) % 2 == 1
            f_f = jnp.exp2(_level_exponent(b_f, s, False)).astype(BF16)
            f_b = jnp.exp2(_level_exponent(b_b, s, True)).astype(BF16)
            u = qb * jnp.where(second, f_f, f_b)
            w = kb * jnp.where(second, f_b, f_f)
            zero = jnp.zeros_like(u)
            lhs = jnp.concatenate([jnp.where(second, u, zero), jnp.where(second, zero, u)], axis=1)
            rhs = jnp.concatenate([jnp.where(second, zero, w), jnp.where(second, w, zero)], axis=1)
        return lhs, rhs

    def absorb(self, s, level, pairs):
        lo, hi = self.lo, self.hi
        if 2 * s == GLA_TILE:
            self.cross_f, self.cross_b = pairs[hi, lo], pairs[lo, hi]
        else:
            self.blocks = [jnp.where(self.lvl == level, pairs[lo, lo], self.blocks[0]),
                           jnp.where(self.lvl == level, pairs[hi, hi], self.blocks[1])]

    def matrix(self):
        top = jnp.concatenate([self.blocks[0].astype(BF16), self.cross_b.astype(BF16)], axis=1)
        bottom = jnp.concatenate([self.cross_f.astype(BF16), self.blocks[1].astype(BF16)], axis=1)
        return jnp.concatenate([top, bottom], axis=0)


def _pair_matrix(q, k, g_f, g_b, b_f, b_b, lvl):
    builder = _PairMatrix(q, k, g_f, g_b, b_f, b_b, lvl)
    for s, level in _PairMatrix.LEVELS:
        builder.absorb(s, level, _bdot_nt(*builder.operands(s)))
    return builder.matrix()


def _carry_state(q, k, b, v, s_ref, reverse):
    edge = 0 if reverse else GLA_TILE - 1
    b_last = b[edge:edge + 1, :]
    qe = (q * jnp.exp2(b)).astype(BF16)
    ke = (k * jnp.exp2(b_last - b)).astype(BF16)
    st = s_ref[...]
    s_ref[...] = st * jnp.exp2(b_last) + lax.dot_general(
        v, ke, TN_DIMS, preferred_element_type=F32)
    return _bdot_nt(qe, st)


def _gla_kernel(*refs, zero_init, nt):
    if zero_init:
        (tri_ref, lvl_ref, q_ref, k_ref, gf_ref, gb_ref, v_ref,
         o_ref, sfo_ref, sbo_ref, sf_ref, sb_ref, bb_ref) = refs
        sf_ref[...] = jnp.zeros_like(sf_ref)
        sb_ref[...] = jnp.zeros_like(sb_ref)
    else:
        (tri_ref, lvl_ref, q_ref, k_ref, gf_ref, gb_ref, v_ref, s0f_ref, s0b_ref,
         o_ref, sfo_ref, sbo_ref, sf_ref, sb_ref, bb_ref) = refs
        for hh in range(sf_ref.shape[0]):
            sf_ref[hh] = s0f_ref[hh].T
            sb_ref[hh] = s0b_ref[hh].T
    hp = sf_ref.shape[0]

    def tile_rows(tile):
        r0 = tile * GLA_TILE
        return pl.ds(r0 if isinstance(r0, int) else pl.multiple_of(r0, GLA_TILE), GLA_TILE)

    heads = [(slice(hh * GLA_DK, (hh + 1) * GLA_DK), slice(hh * GLA_DV, (hh + 1) * GLA_DV), hh)
             for hh in range(hp)]

    def forward_sweep(t, carry):
        rows = tile_rows(t)
        g_f, g_b = gf_ref[rows, :], gb_ref[rows, :]
        c = None
        for part in _split3(jnp.concatenate([g_f, g_b], axis=1)):
            term = jnp.dot(tri_ref[...], part, preferred_element_type=F32)
            c = term if c is None else c + term
        width = hp * GLA_DK
        c_b = c[:, width:]
        b_f = c[:, :width]
        b_b = (c_b[GLA_TILE - 1:GLA_TILE, :] - c_b) + g_b
        bb_ref[t] = b_b
        for dk, dv, hh in heads:
            q, k, v = q_ref[rows, dk], k_ref[rows, dk], v_ref[rows, dv]
            att = _pair_matrix(q, k, g_f[:, dk], g_b[:, dk], b_f[:, dk], b_b[:, dk], lvl_ref[...])
            o_ref[rows, dv] = (jnp.dot(att, v, preferred_element_type=F32)
                               + _carry_state(q, k, b_f[:, dk], v, sf_ref.at[hh], False))
        return carry

    def backward_sweep(i, carry):
        t = nt - 1 - i
        rows = tile_rows(t)
        b_b = bb_ref[t]
        for dk, dv, hh in heads:
            o_ref[rows, dv] += _carry_state(q_ref[rows, dk], k_ref[rows, dk], b_b[:, dk],
                                            v_ref[rows, dv], sb_ref.at[hh], True)
        return carry

    if nt == 1:
        forward_sweep(0, 0)
        backward_sweep(0, 0)
    else:
        lax.fori_loop(0, nt, forward_sweep, 0)
        lax.fori_loop(0, nt, backward_sweep, 0)
    for hh in range(hp):
        sfo_ref[hh] = sf_ref[hh].T
        sbo_ref[hh] = sb_ref[hh].T


def _gla_scan(grp, row0, q, k, v, gf, gb, s0f, s0b, tri, lvl):
    n = grp.n
    nt = n // GLA_TILE
    per_head = 2 * n * (4 * GLA_DK * 4 + GLA_DV * 2 + GLA_DV * 4)
    hp = GLA_HEADS
    while hp > 1 and hp * per_head > GLA_SEQ_VMEM_BUDGET:
        hp //= 2
    zero_init = s0f is None
    half = GLA_TILE // 2
    assert row0 % n == 0
    seq0 = row0 // n
    in_dk = pl.BlockSpec((n, hp * GLA_DK), lambda b, h: (seq0 + b, h))
    in_dv = pl.BlockSpec((n, hp * GLA_DV), lambda b, h: (seq0 + b, h))
    seq_dv = pl.BlockSpec((n, hp * GLA_DV), lambda b, h: (b, h))
    state_spec = pl.BlockSpec((None, hp, GLA_DK, GLA_DV), lambda b, h: (b, h, 0, 0))
    in_specs = [_const_spec((GLA_TILE, GLA_TILE)), _const_spec((half, half)),
                in_dk, in_dk, in_dk, in_dk, in_dv]
    args = [tri, lvl, q, k, gf, gb, v]
    if not zero_init:
        in_specs += [state_spec, state_spec]
        args += [s0f, s0b]
    state_shape = jax.ShapeDtypeStruct((grp.b, GLA_HEADS, GLA_DK, GLA_DV), F32)
    return pl.pallas_call(
        functools.partial(_gla_kernel, zero_init=zero_init, nt=nt),
        grid=(grp.b, GLA_HEADS // hp),
        in_specs=in_specs,
        out_specs=[seq_dv, state_spec, state_spec],
        out_shape=[jax.ShapeDtypeStruct((grp.t, GLA_VD), F32), state_shape, state_shape],
        scratch_shapes=[pltpu.VMEM((hp, GLA_DV, GLA_DK), F32), pltpu.VMEM((hp, GLA_DV, GLA_DK), F32),
                        pltpu.VMEM((nt, GLA_TILE, hp * GLA_DK), F32)],
        compiler_params=_cparams("arbitrary", "arbitrary"),
        name="gla_scan",
    )(*args)


def _gla_constants():
    half = GLA_TILE // 2
    i = jnp.arange(GLA_TILE)[:, None]
    j = jnp.arange(GLA_TILE)[None, :]
    tri = (j <= i).astype(BF16)
    ih, jh = i[:half], j[:, :half]
    x = jnp.bitwise_xor(ih, jh)
    lvl = sum((x >= (1 << p)).astype(jnp.int32) for p in range(half.bit_length() - 1))
    return tri, lvl


def _postmix1_kernel(oc_ref, ol_ref, og_ref, x_ref, mod_ref, g_ref, gn_ref, wo_hbm, w1_hbm, w2_hbm,
                     yc_ref, yl_ref, *scratch, stream):
    weights = _TailWeights(1, wo_hbm, w1_hbm, w2_hbm, *scratch)
    _first_step_streams_weights(
        functools.partial(_postmix1_body, oc_ref, ol_ref, og_ref, x_ref, mod_ref, g_ref, gn_ref,
                          yc_ref, yl_ref, weights, stream), weights)


def _postmix1_body(oc_ref, ol_ref, og_ref, x_ref, mod_ref, g_ref, gn_ref, yc_ref, yl_ref,
                   weights, stream, streaming):
    is_lat = stream.is_lat()
    tm = stream.tm
    gn = gn_ref[...]
    gt1 = _mod_chunk(mod_ref, 2)
    if streaming:
        weights.fetch_out_proj()
    x1_parts = []
    for r0 in range(0, tm, tm // 2):
        rows = slice(r0, r0 + tm // 2)
        mix = None
        for h in range(GLA_HEADS):
            cols = slice(h * GLA_DV, (h + 1) * GLA_DV)
            o = _rms(jnp.where(is_lat, ol_ref[rows, cols], oc_ref[rows, cols]), gn)
            og = og_ref[rows, cols]
            y = (o * (og / (1.0 + jnp.exp(-og)))).astype(BF16)
            part = jnp.dot(y, weights.wo_s[cols, :], preferred_element_type=F32)
            mix = part if mix is None else mix + part
        x1_parts.append(x_ref[rows, :] + gt1 * _rms(mix, g_ref[1:2, :]))
    x1 = jnp.concatenate(x1_parts, axis=0)
    y = _mlp_tail(x1, mod_ref, g_ref, weights, streaming)

    @pl.when(jnp.logical_not(is_lat))
    def _():
        yc_ref[...] = y

    @pl.when(is_lat)
    def _():
        yl_ref[...] = y


def _postmix1(stream, o_ctx, o_lat, og, x2d, mod, norm_g, gla_norm_g, w_out, w1, w2):
    hbm = pl.BlockSpec(memory_space=pl.ANY)
    return pl.pallas_call(
        functools.partial(_postmix1_kernel, stream=stream),
        grid=(stream.tiles,),
        in_specs=[stream.ctx_spec(GLA_VD), stream.lat_spec(GLA_VD), stream.spec(GLA_VD),
                  stream.spec(D_MODEL), stream.mod_spec(), _const_spec((4, D_MODEL)),
                  _const_spec((1, GLA_DV)), hbm, hbm, hbm],
        out_specs=[stream.ctx_spec(D_MODEL), stream.lat_spec(D_MODEL)],
        out_shape=[jax.ShapeDtypeStruct((stream.ctx.t, D_MODEL), F32),
                   jax.ShapeDtypeStruct((stream.lat.t, D_MODEL), F32)],
        scratch_shapes=_tail_weight_scratch(),
        compiler_params=_cparams("arbitrary"),
        name="postmix1_mlp",
    )(o_ctx, o_lat, og, x2d, mod, norm_g, gla_norm_g, w_out, w1, w2)


def kernel(x_prompt, x_sample, cache_k, cache_v, state_fwd, state_bwd, c, c_ctx, mod_w, mod_b,
           norm_g, ab_w_in, conv_w, attn_sink, ab_w_out, gla_w_in, gla_gate_w, gla_gate_b,
           gla_norm_g, gla_w_out, mlp_w1, mlp_w2):
    b_ctx, n_ctx, _ = x_prompt.shape
    b_lat, n_lat, _ = x_sample.shape
    assert mod_w.shape[0] == 2 and ab_w_in.shape[0] == 1 and gla_w_in.shape[0] == 1
    assert 1 + b_lat <= 8

    cond8 = jnp.zeros((8, D_MODEL), F32).at[0].set(c_ctx).at[1:1 + b_lat].set(c)
    mod = _modulation(cond8, mod_w, mod_b)
    mods = [mod[l].reshape(8, 1, -1) for l in range(2)]

    w_gate = jnp.zeros((2 * GLA_RANK, 2 * GLA_QK), F32)
    w_gate = w_gate.at[:GLA_RANK, :GLA_QK].set(gla_gate_w[0, 0])
    w_gate = w_gate.at[GLA_RANK:, GLA_QK:].set(gla_gate_w[0, 1])
    tri, lvl = _gla_constants()
    p = {
        "conv_w": conv_w[0],
        "sink": attn_sink[0],
        "gla_w_in": jnp.swapaxes(gla_w_in, 1, 2),
        "gla_w_gate": w_gate.astype(BF16),
        "gla_gate_bias": gla_gate_b[0].reshape(1, 2 * GLA_QK),
        "gla_norm_g": gla_norm_g[0].reshape(1, GLA_DV),
    }

    ctx, lat = _Group(b_ctx, n_ctx), _Group(b_lat, n_lat)
    stream = _Stream(ctx, lat, TOKEN_TILE)
    x_ctx, x_lat = x_prompt.reshape(ctx.t, D_MODEL), x_sample.reshape(lat.t, D_MODEL)
    past = cache_k.shape[2]
    k_ctx = cache_k[:, 0].reshape(b_lat, past, KD)
    v_ctx = cache_v[:, 0].reshape(b_lat, past, KD)

    wide = _Stream(ctx, lat, PREMIX_PARTS * TOKEN_TILE)
    zc, q, kv, k_t, v_t = _premix0(wide, x_ctx, x_lat, mods[0], norm_g[0], ab_w_in,
                                   _rope_tables(n_lat, wide.tm))
    attn_ctx = _ctx_attention(ctx, q, kv, p["sink"])
    attn_lat = _lat_attention(lat, ctx.t, q, kv, k_ctx, v_ctx, p["sink"])
    x1 = _postmix0(stream, zc, attn_ctx, attn_lat, x_ctx, x_lat, mods[0], norm_g[0], p["conv_w"],
                   ab_w_out, mlp_w1, mlp_w2)
    gq, gk, gv, og, gf, gb = _premix1(wide, x1, mods[1], norm_g[1], p["gla_w_in"],
                                      p["gla_w_gate"], p["gla_gate_bias"])
    o_ctx, sf, sb = _gla_scan(ctx, 0, gq, gk, gv, gf, gb, None, None, tri, lvl)
    o_lat, _, _ = _gla_scan(lat, ctx.t, gq, gk, gv, gf, gb, state_fwd[:, 0], state_bwd[:, 0],
                            tri, lvl)
    y_ctx, y_lat = _postmix1(stream, o_ctx, o_lat, og, x1, mods[1], norm_g[1], p["gla_norm_g"],
                             gla_w_out, mlp_w1, mlp_w2)

    def cache_layout(t):
        t = t.reshape(b_ctx, 1, N_KV_HEADS, HEAD_DIM, n_ctx)
        return jnp.transpose(t, (0, 1, 4, 2, 3))

    return (y_ctx.reshape(x_prompt.shape), y_lat.reshape(x_sample.shape),
            cache_layout(k_t), cache_layout(v_t), sf[:, None], sb[:, None])
```

```python
import functools

import jax
import jax.numpy as jnp
from jax import lax
from jax.experimental import pallas as pl
from jax.experimental.pallas import tpu as pltpu

F32 = jnp.float32
BF16 = jnp.bfloat16

D_MODEL = 1024
MOD_CHUNKS = 6
EPS = 1e-6
CONV_DIM = 512
N_Q_HEADS = 8
N_KV_HEADS = 2
GQA_GROUP = 4
HEAD_DIM = 64
WINDOW = 128
ATTN_BLOCK = 128
ATTN_BLOCKS_PER_STEP = 16
CTX_SEQS_PER_STEP = 4
GRID_W = 64
ROPE_BASE = 10000.0
QD = N_Q_HEADS * HEAD_DIM
KD = N_KV_HEADS * HEAD_DIM
AB_IN = 3 * CONV_DIM + QD + 2 * KD
GLA_HEADS = 4
GLA_DK = 128
GLA_DV = 256
GLA_RANK = 16
GLA_GATE_NORM = 16.0
GLA_TILE = 256
GLA_SEQ_VMEM_BUDGET = 52 * 1024 * 1024
LOG2E = 1.4426950408889634
GLA_QK = GLA_HEADS * GLA_DK
GLA_VD = GLA_HEADS * GLA_DV
GLA_MAIN = 2 * GLA_QK + 2 * GLA_VD
D_FF = 4 * D_MODEL
MLP_CHUNK = 512
MLP_CHUNKS = D_FF // MLP_CHUNK
WEIGHT_PIECE = 256
TOKEN_TILE = 512
PREMIX_PARTS = 2
NEG_INF = -1e30
LANES = 128
VMEM_LIMIT = 60 * 1024 * 1024

NT_DIMS = (((1,), (1,)), ((), ()))
TN_DIMS = (((0,), (0,)), ((), ()))


def _cparams(*sem):
    return pltpu.CompilerParams(dimension_semantics=sem, vmem_limit_bytes=VMEM_LIMIT)


def _bdot(a, b):
    return jnp.dot(a.astype(BF16), b.astype(BF16), preferred_element_type=F32)


def _bdot_nt(a, b):
    return lax.dot_general(a.astype(BF16), b.astype(BF16), NT_DIMS, preferred_element_type=F32)


def _rms(x, g):
    ms = jnp.mean(x * x, axis=-1, keepdims=True)
    return x * lax.rsqrt(ms + EPS) * g


def _mod_chunk(mod_ref, i):
    return mod_ref[:, i * D_MODEL:(i + 1) * D_MODEL]


def _const_spec(shape):
    return pl.BlockSpec(shape, lambda *_: (0,) * len(shape))


def _f32_weight_spec(shape):
    assert shape[0] == 1
    return pl.BlockSpec((None,) + tuple(shape[1:]), lambda *_: (0, 0, 0),
                        pipeline_mode=pl.Buffered(1))


def _cast_once(w_ref, wb_ref):
    @pl.when(pl.program_id(0) == 0)
    def _():
        wb_ref[...] = w_ref[...].astype(BF16)


def _mod_kernel(cond_ref, w_ref, b_ref, o_ref):
    cnd = cond_ref[...]
    s = cnd / (1.0 + jnp.exp(-cnd))
    o_ref[...] = _bdot(s, w_ref[...]) + b_ref[...]


def _modulation(cond8, mod_w, mod_b):
    depth = mod_w.shape[0]
    n = mod_w.shape[2]
    tn = 1536
    return pl.pallas_call(
        _mod_kernel,
        grid=(depth, n // tn),
        in_specs=[
            pl.BlockSpec((8, D_MODEL), lambda l, j: (0, 0)),
            pl.BlockSpec((None, D_MODEL, tn), lambda l, j: (l, 0, j)),
            pl.BlockSpec((None, 1, tn), lambda l, j: (l, 0, j)),
        ],
        out_specs=pl.BlockSpec((None, 8, tn), lambda l, j: (l, 0, j)),
        out_shape=jax.ShapeDtypeStruct((depth, 8, n), F32),
        compiler_params=_cparams("arbitrary", "arbitrary"),
        name="modulation",
    )(cond8, mod_w, mod_b.reshape(depth, 1, n))


class _Group:
    def __init__(self, b, n):
        self.b, self.n, self.t = b, n, b * n


class _Stream:
    def __init__(self, ctx, lat, tm):
        assert ctx.t % tm == 0 and lat.n % tm == 0 and tm % ctx.n == 0
        self.ctx, self.lat, self.tm = ctx, lat, tm
        self.t = ctx.t + lat.t
        self.ctx_tiles = ctx.t // tm
        self.tiles = self.t // tm

    def is_lat(self):
        return pl.program_id(0) >= self.ctx_tiles

    def spec(self, width):
        return pl.BlockSpec((self.tm, width), lambda t: (t, 0))

    def ctx_spec(self, width):
        last = self.ctx_tiles - 1
        return pl.BlockSpec((self.tm, width), lambda t: (jnp.minimum(t, last), 0))

    def lat_spec(self, width):
        first = self.ctx_tiles
        return pl.BlockSpec((self.tm, width), lambda t: (jnp.maximum(t - first, 0), 0))

    def mod_spec(self):
        first, per = self.ctx_tiles, self.lat.n // self.tm
        return pl.BlockSpec((None, 1, MOD_CHUNKS * D_MODEL),
                            lambda t: (jnp.where(t < first, 0, 1 + (t - first) // per), 0, 0))

    def seq_len(self):
        return jnp.where(self.is_lat(), self.lat.n, self.ctx.n)


def _row_spec(tm, width):
    return pl.BlockSpec((tm, width), lambda t: (t, 0))


def _rope(x, cos, sin_lo, sin_hi):
    return (x * cos + pltpu.roll(x, LANES - 16, axis=1) * sin_lo
            + pltpu.roll(x, 16, axis=1) * sin_hi)


def _premix0_kernel(xc_ref, xl_ref, mod_ref, g_ref, w_ref, cos_ref, slo_ref, shi_ref,
                    zc_ref, q_ref, kv_ref, kt_ref, vt_ref, wb_ref, *, stream):
    _cast_once(w_ref, wb_ref)
    is_lat = stream.is_lat()
    sh1, sc1 = _mod_chunk(mod_ref, 0), _mod_chunk(mod_ref, 1)
    c3 = 3 * CONV_DIM
    scale = HEAD_DIM ** -0.5 * LOG2E
    n = kt_ref.shape[2]
    part = stream.tm // PREMIX_PARTS
    raw_kv = []
    for r0 in range(0, stream.tm, part):
        rows = slice(r0, r0 + part)
        x = jnp.where(is_lat, xl_ref[rows, :], xc_ref[rows, :])
        h = _rms(x, g_ref[0:1, :]) * (1.0 + sc1) + sh1
        z = _bdot(h, wb_ref[...])
        zc_ref[rows, :] = z[:, :c3]
        cos, slo, shi = cos_ref[rows, :], slo_ref[rows, :], shi_ref[rows, :]
        for j in range(QD // LANES):
            qs = z[:, c3 + j * LANES:c3 + (j + 1) * LANES]
            q_ref[rows, j * LANES:(j + 1) * LANES] = (_rope(qs, cos, slo, shi) * scale).astype(BF16)
        kv_ref[rows, :KD] = _rope(z[:, c3 + QD:c3 + QD + KD], cos, slo, shi)
        kv_ref[rows, KD:] = z[:, c3 + QD + KD:]
        raw_kv.append(z[:, c3 + QD:])

    @pl.when(jnp.logical_not(is_lat))
    def _():
        for p, kv in enumerate(raw_kv):
            for j in range(part // n):
                seq = p * (part // n) + j
                kt_ref[seq] = kv[j * n:(j + 1) * n, :KD].T
                vt_ref[seq] = kv[j * n:(j + 1) * n, KD:].T


def _premix0(stream, x_ctx, x_lat, mod, norm_g, w_in, rope_tabs):
    tm, ctx, lat = stream.tm, stream.ctx, stream.lat
    first, per, last = stream.ctx_tiles, lat.n // tm, stream.ctx_tiles - 1
    rope_spec = pl.BlockSpec((tm, LANES),
                             lambda t: (jnp.where(t < first, 0, 1 + (t - first) % per), 0))
    cache_spec = pl.BlockSpec((tm // ctx.n, KD, ctx.n), lambda t: (jnp.minimum(t, last), 0, 0))
    return pl.pallas_call(
        functools.partial(_premix0_kernel, stream=stream),
        grid=(stream.tiles,),
        in_specs=[stream.ctx_spec(D_MODEL), stream.lat_spec(D_MODEL), stream.mod_spec(),
                  _const_spec((4, D_MODEL)), _f32_weight_spec(w_in.shape)] + [rope_spec] * 3,
        out_specs=[stream.spec(3 * CONV_DIM), stream.spec(QD), stream.spec(2 * KD),
                   cache_spec, cache_spec],
        out_shape=[jax.ShapeDtypeStruct((stream.t, 3 * CONV_DIM), F32),
                   jax.ShapeDtypeStruct((stream.t, QD), BF16),
                   jax.ShapeDtypeStruct((stream.t, 2 * KD), F32),
                   jax.ShapeDtypeStruct((ctx.b, KD, ctx.n), F32),
                   jax.ShapeDtypeStruct((ctx.b, KD, ctx.n), F32)],
        scratch_shapes=[pltpu.VMEM(w_in.shape[1:], BF16)],
        compiler_params=_cparams("arbitrary"),
        name="premix0",
    )(x_ctx, x_lat, mod, norm_g, w_in, *rope_tabs)


def _rope_tables(n, identity_rows):
    rows = n // GRID_W
    pos_r = jnp.repeat(jnp.arange(rows), GRID_W)
    pos_c = jnp.tile(jnp.arange(GRID_W), rows)
    half = HEAD_DIM // 2
    quarter = half // 2
    inv = ROPE_BASE ** (-(jnp.arange(quarter, dtype=F32) * 2.0 / half))

    def cs(pos):
        ang = pos.astype(F32)[:, None] * inv[None, :]
        return jnp.cos(ang), jnp.sin(ang)

    cr, sr = cs(pos_r)
    cc, sc = cs(pos_c)
    zero = jnp.zeros_like(sr)
    cos = jnp.concatenate([cr, cr, cc, cc], axis=1)
    sin_lo = jnp.concatenate([-sr, zero, -sc, zero], axis=1)
    sin_hi = jnp.concatenate([zero, sr, zero, sc], axis=1)
    rep = LANES // HEAD_DIM
    tables = []
    for t, ident in ((cos, 1.0), (sin_lo, 0.0), (sin_hi, 0.0)):
        head = jnp.full((identity_rows, LANES), ident, F32)
        tables.append(jnp.concatenate([head, jnp.tile(t, (1, rep))], axis=0))
    return tuple(tables)


def _attention_operands(k_all, v_all):
    assert KD == LANES == 2 * HEAD_DIM and GQA_GROUP == 4
    lane = lax.broadcasted_iota(jnp.int32, (1, LANES), 1)
    sub = lax.broadcasted_iota(jnp.int32, (LANES, 1), 0)
    v_t = v_all.T
    k_swapped = pltpu.roll(k_all, HEAD_DIM, axis=1)
    ops = []
    for g in range(N_KV_HEADS):
        k_low, k_high = (k_all, k_swapped) if g == 0 else (k_swapped, k_all)
        kz_even = jnp.where(lane < HEAD_DIM, k_low, 0.0).astype(BF16)
        kz_odd = jnp.where(lane >= HEAD_DIM, k_high, 0.0).astype(BF16)
        own = (sub < HEAD_DIM) if g == 0 else (sub >= HEAD_DIM)
        v_ext_t = jnp.where(own, v_t, 1.0).astype(BF16)
        ops.append((kz_even, kz_odd, v_ext_t))
    return ops


def _concat_operands(parts):
    return [(jnp.concatenate([p[g][0] for p in parts], axis=0),
             jnp.concatenate([p[g][1] for p in parts], axis=0),
             jnp.concatenate([p[g][2] for p in parts], axis=1)) for g in range(N_KV_HEADS)]


def _group_heads(g):
    return [4 * g, 4 * g + 2, 4 * g + 1, 4 * g + 3]


def _scores(sink_ref, q_ref, operands, bias_t, g):
    m = q_ref.shape[0]
    kz_even, kz_odd, _ = operands[g]
    qq = jnp.concatenate([q_ref[:, (2 * g) * LANES:(2 * g + 1) * LANES],
                          q_ref[:, (2 * g + 1) * LANES:(2 * g + 2) * LANES]], axis=0)
    s = jnp.concatenate([_bdot_nt(kz_even, qq), _bdot_nt(kz_odd, qq)], axis=1)
    if bias_t is not None:
        nb = bias_t.shape[0]
        s = jnp.concatenate([s[:nb] + jnp.concatenate([bias_t] * GQA_GROUP, axis=1), s[nb:]],
                            axis=0)
    sink = jnp.concatenate([jnp.full((1, m), sink_ref[h] * LOG2E, F32) for h in _group_heads(g)],
                           axis=1)
    mx = jnp.maximum(jnp.max(s, axis=0, keepdims=True), sink)
    return s, sink, mx


def _weighted_values(s, sink, mx, operands, g, m):
    v_ext_t = operands[g][2]
    p = jnp.exp2(s - mx).astype(BF16)
    oe = jnp.dot(v_ext_t, p, preferred_element_type=F32)
    other = (1 - g) * HEAD_DIM
    den = oe[other:other + 1] + jnp.exp2(sink - mx)
    o_g = oe[g * HEAD_DIM:(g + 1) * HEAD_DIM] / den
    return {h: o_g[:, i * m:(i + 1) * m] for i, h in enumerate(_group_heads(g))}


def _sink_attention(sink_ref, blocks):
    chains = [(blk, g) for blk in range(len(blocks)) for g in range(N_KV_HEADS)]
    outs = [dict() for _ in blocks]

    def scores(chain):
        blk, g = chain
        q_ref, operands, bias_t, _ = blocks[blk]
        return _scores(sink_ref, q_ref, operands, bias_t, g)

    ahead = scores(chains[0])
    for i, (blk, g) in enumerate(chains):
        current = ahead
        if i + 1 < len(chains):
            ahead = scores(chains[i + 1])
        q_ref, operands, _, o_ref = blocks[blk]
        outs[blk].update(_weighted_values(*current, operands, g, q_ref.shape[0]))
        if g == N_KV_HEADS - 1:
            for j in range(N_Q_HEADS // 2):
                pair_t = jnp.concatenate([outs[blk][2 * j], outs[blk][2 * j + 1]], axis=0)
                o_ref[:, j * LANES:(j + 1) * LANES] = pair_t.T.astype(BF16)


def _ctx_attn_kernel(sink_ref, q_ref, kv_ref, o_ref, *, n):
    blocks = []
    for j in range(q_ref.shape[0] // n):
        rows = pl.ds(j * n, n)
        operands = _attention_operands(kv_ref[rows, :KD], kv_ref[rows, KD:])
        blocks.append((q_ref.at[rows], operands, None, o_ref.at[rows]))
    _sink_attention(sink_ref, blocks)


def _ctx_attention(grp, q, kv, sink):
    per = CTX_SEQS_PER_STEP
    assert grp.b % per == 0
    n = per * grp.n
    return pl.pallas_call(
        functools.partial(_ctx_attn_kernel, n=grp.n),
        grid=(grp.b // per,),
        in_specs=[pl.BlockSpec(memory_space=pltpu.SMEM), _row_spec(n, QD), _row_spec(n, 2 * KD)],
        out_specs=_row_spec(n, QD),
        out_shape=jax.ShapeDtypeStruct((grp.t, QD), BF16),
        compiler_params=_cparams("arbitrary"),
        name="ctx_attention",
    )(sink, q, kv)


def _lat_attn_kernel(sink_ref, q_ref, kvp_ref, kvc_ref, kvn_ref, kc_ref, vc_ref, o_ref, *, n):
    band = jnp.concatenate([kvp_ref[...], kvc_ref[...], kvn_ref[...]], axis=0)
    kj = lax.broadcasted_iota(jnp.int32, (3 * ATTN_BLOCK, ATTN_BLOCK), 0)
    qi = lax.broadcasted_iota(jnp.int32, (3 * ATTN_BLOCK, ATTN_BLOCK), 1)
    rel = kj - ATTN_BLOCK - qi
    ctx_ops = _attention_operands(kc_ref[...], vc_ref[...])
    band_ops = [_attention_operands(band[i * ATTN_BLOCK:(i + 1) * ATTN_BLOCK, :KD],
                                    band[i * ATTN_BLOCK:(i + 1) * ATTN_BLOCK, KD:])
                for i in range(ATTN_BLOCKS_PER_STEP + 2)]
    blocks = []
    for j in range(ATTN_BLOCKS_PER_STEP):
        blk = pl.program_id(1) * ATTN_BLOCKS_PER_STEP + j
        kpos = (blk - 1) * ATTN_BLOCK + kj
        valid = (jnp.abs(rel) <= WINDOW) & (kpos >= 0) & (kpos < n)
        bias_t = jnp.where(valid, 0.0, NEG_INF)
        operands = _concat_operands(band_ops[j:j + 3] + [ctx_ops])
        rows = pl.ds(j * ATTN_BLOCK, ATTN_BLOCK)
        blocks.append((q_ref.at[rows], operands, bias_t, o_ref.at[rows]))
    _sink_attention(sink_ref, blocks)


def _lat_attention(grp, row0, q, kv, k_ctx, v_ctx, sink):
    nb = grp.n // ATTN_BLOCK
    per = ATTN_BLOCKS_PER_STEP
    steps = nb // per
    past = k_ctx.shape[1]
    assert row0 % (per * ATTN_BLOCK) == 0
    blk0, step0 = row0 // ATTN_BLOCK, row0 // (per * ATTN_BLOCK)

    def edge(off):
        return pl.BlockSpec((ATTN_BLOCK, 2 * KD),
                            lambda b, i: (blk0 + b * nb + jnp.clip(i * per + off, 0, nb - 1), 0))

    ctx_spec = pl.BlockSpec((None, past, KD), lambda b, i: (b, 0, 0))
    return pl.pallas_call(
        functools.partial(_lat_attn_kernel, n=grp.n),
        grid=(grp.b, steps),
        in_specs=[pl.BlockSpec(memory_space=pltpu.SMEM),
                  pl.BlockSpec((per * ATTN_BLOCK, QD), lambda b, i: (step0 + b * steps + i, 0)),
                  edge(-1),
                  pl.BlockSpec((per * ATTN_BLOCK, 2 * KD), lambda b, i: (step0 + b * steps + i, 0)),
                  edge(per), ctx_spec, ctx_spec],
        out_specs=pl.BlockSpec((per * ATTN_BLOCK, QD), lambda b, i: (b * steps + i, 0)),
        out_shape=jax.ShapeDtypeStruct((grp.t, QD), BF16),
        compiler_params=_cparams("arbitrary", "arbitrary"),
        name="lat_attention",
    )(sink, q, kv, kv, kv, k_ctx, v_ctx)


class _TailWeights:
    def __init__(self, layer, wo_hbm, w1_hbm, w2_hbm, wo_s, w1_s, w2_s, stage1, stage2, sem):
        self.layer = layer
        self.wo_hbm, self.w1_hbm, self.w2_hbm = wo_hbm, w1_hbm, w2_hbm
        self.wo_s, self.w1_s, self.w2_s = wo_s, w1_s, w2_s
        self.stage1, self.stage2, self.sem = stage1, stage2, sem

    def _w1_copy(self, p):
        src = self.w1_hbm.at[self.layer, :, pl.ds(p * WEIGHT_PIECE, WEIGHT_PIECE)]
        return pltpu.make_async_copy(src, self.stage1.at[p % 2], self.sem.at[0, p % 2])

    def _w2_copy(self, p):
        src = self.w2_hbm.at[self.layer, pl.ds(p * WEIGHT_PIECE, WEIGHT_PIECE), :]
        return pltpu.make_async_copy(src, self.stage2.at[p % 2], self.sem.at[1, p % 2])

    def _wo_copy(self, p):
        src = self.wo_hbm.at[0, pl.ds(p * WEIGHT_PIECE, WEIGHT_PIECE), :]
        return pltpu.make_async_copy(src, self.stage2.at[p % 2], self.sem.at[1, p % 2])

    def start(self):
        self._wo_copy(0).start()
        self._wo_copy(1).start()
        self._w1_copy(0).start()
        self._w1_copy(1).start()

    def fetch_out_proj(self):
        pieces = D_MODEL // WEIGHT_PIECE
        for p in range(pieces):
            self._wo_copy(p).wait()
            self.wo_s[p * WEIGHT_PIECE:(p + 1) * WEIGHT_PIECE, :] = self.stage2[p % 2].astype(BF16)
            if p + 2 < pieces:
                self._wo_copy(p + 2).start()
            else:
                self._w2_copy(p + 2 - pieces).start()

    def fetch_mlp_chunk(self, j):
        per = MLP_CHUNK // WEIGHT_PIECE
        total = MLP_CHUNKS * per
        for p in range(j * per, (j + 1) * per):
            part = slice((p % per) * WEIGHT_PIECE, (p % per + 1) * WEIGHT_PIECE)
            self._w1_copy(p).wait()
            self.w1_s[j, :, part] = self.stage1[p % 2].astype(BF16)
            if p + 2 < total:
                self._w1_copy(p + 2).start()
            self._w2_copy(p).wait()
            self.w2_s[j, part, :] = self.stage2[p % 2].astype(BF16)
            if p + 2 < total:
                self._w2_copy(p + 2).start()


def _tail_weight_scratch():
    assert MLP_CHUNK % WEIGHT_PIECE == 0 and D_MODEL // WEIGHT_PIECE >= 2
    return [pltpu.VMEM((D_MODEL, D_MODEL), BF16),
            pltpu.VMEM((MLP_CHUNKS, D_MODEL, MLP_CHUNK), BF16),
            pltpu.VMEM((MLP_CHUNKS, MLP_CHUNK, D_MODEL), BF16),
            pltpu.VMEM((2, D_MODEL, WEIGHT_PIECE), F32),
            pltpu.VMEM((2, WEIGHT_PIECE, D_MODEL), F32),
            pltpu.SemaphoreType.DMA((2, 2))]


def _first_step_streams_weights(body, weights):
    first = pl.program_id(0) == 0

    @pl.when(first)
    def _():
        weights.start()
        body(True)

    @pl.when(jnp.logical_not(first))
    def _():
        body(False)


def _mlp_tail(x, mod_ref, g_ref, weights, streaming):
    sh2, sc2, gt2 = _mod_chunk(mod_ref, 3), _mod_chunk(mod_ref, 4), _mod_chunk(mod_ref, 5)
    hb = (_rms(x, g_ref[2:3, :]) * (1.0 + sc2) + sh2).astype(BF16)
    acc = None
    for j in range(MLP_CHUNKS):
        if streaming:
            weights.fetch_mlp_chunk(j)
        a = jnp.dot(hb, weights.w1_s[j], preferred_element_type=F32)
        a = jnp.maximum(a, 0.0)
        part = jnp.dot((a * a).astype(BF16), weights.w2_s[j], preferred_element_type=F32)
        acc = part if acc is None else acc + part
    return x + gt2 * _rms(acc, g_ref[3:4, :])


def _postmix0_kernel(zc_ref, zp_ref, zn_ref, atc_ref, atl_ref, xc_ref, xl_ref, mod_ref, g_ref,
                     cw_ref, wo_hbm, w1_hbm, w2_hbm, o_ref, *scratch, stream):
    weights = _TailWeights(0, wo_hbm, w1_hbm, w2_hbm, *scratch)
    _first_step_streams_weights(
        functools.partial(_postmix0_body, zc_ref, zp_ref, zn_ref, atc_ref, atl_ref, xc_ref, xl_ref,
                          mod_ref, g_ref, cw_ref, o_ref, weights, stream), weights)


def _postmix0_body(zc_ref, zp_ref, zn_ref, atc_ref, atl_ref, xc_ref, xl_ref, mod_ref, g_ref,
                   cw_ref, o_ref, weights, stream, streaming):
    c, tm = CONV_DIM, stream.tm
    is_lat, n = stream.is_lat(), stream.seq_len()
    zc = zc_ref[...]
    u = zc[:, c:2 * c] * zc[:, 2 * c:]
    u_before = zp_ref[7:8, c:2 * c] * zp_ref[7:8, 2 * c:]
    u_after = zn_ref[0:1, c:2 * c] * zn_ref[0:1, 2 * c:]
    row = lax.broadcasted_iota(jnp.int32, (tm, 1), 0)
    pos = (pl.program_id(0) * tm + row) & (n - 1)
    u_prev = jnp.where(row == 0, u_before, pltpu.roll(u, 1, axis=0))
    u_prev = jnp.where(pos == 0, 0.0, u_prev)
    u_next = jnp.where(row == tm - 1, u_after, pltpu.roll(u, tm - 1, axis=0))
    u_next = jnp.where(pos == n - 1, 0.0, u_next)
    conv = u_prev * cw_ref[0:1, :] + u * cw_ref[1:2, :] + u_next * cw_ref[2:3, :]
    if streaming:
        weights.fetch_out_proj()
    attn = jnp.where(is_lat, atl_ref[...], atc_ref[...])
    mix = (_bdot(zc[:, :c] * conv, weights.wo_s[:c, :])
           + jnp.dot(attn, weights.wo_s[c:, :], preferred_element_type=F32))
    gt1 = _mod_chunk(mod_ref, 2)
    x1 = jnp.where(is_lat, xl_ref[...], xc_ref[...]) + gt1 * _rms(mix, g_ref[1:2, :])
    o_ref[...] = _mlp_tail(x1, mod_ref, g_ref, weights, streaming)


def _postmix0(stream, zc, attn_ctx, attn_lat, x_ctx, x_lat, mod, norm_g, conv_w, w_out, w1, w2):
    tm = stream.tm
    for n in (stream.ctx.n, stream.lat.n):
        assert n & (n - 1) == 0 and stream.ctx.t % n == 0
    r8 = tm // 8
    last8 = stream.t // 8 - 1
    hbm = pl.BlockSpec(memory_space=pl.ANY)
    return pl.pallas_call(
        functools.partial(_postmix0_kernel, stream=stream),
        grid=(stream.tiles,),
        in_specs=[stream.spec(3 * CONV_DIM),
                  pl.BlockSpec((8, 3 * CONV_DIM), lambda t: (jnp.maximum(t * r8 - 1, 0), 0)),
                  pl.BlockSpec((8, 3 * CONV_DIM), lambda t: (jnp.minimum((t + 1) * r8, last8), 0)),
                  stream.ctx_spec(QD), stream.lat_spec(QD),
                  stream.ctx_spec(D_MODEL), stream.lat_spec(D_MODEL), stream.mod_spec(),
                  _const_spec((4, D_MODEL)), _const_spec((3, CONV_DIM)), hbm, hbm, hbm],
        out_specs=stream.spec(D_MODEL),
        out_shape=jax.ShapeDtypeStruct((stream.t, D_MODEL), F32),
        scratch_shapes=_tail_weight_scratch(),
        compiler_params=_cparams("arbitrary"),
        name="postmix0_mlp",
    )(zc, zc, zc, attn_ctx, attn_lat, x_ctx, x_lat, mod, norm_g, conv_w, w_out, w1, w2)


def _premix1_kernel(x_ref, mod_ref, g_ref, w_ref, wg_ref, gb_ref,
                    q_ref, k_ref, v_ref, og_ref, gf_ref, gbk_ref, wb_ref, wr_ref):
    @pl.when(pl.program_id(0) == 0)
    def _():
        for c in range(0, GLA_MAIN, GLA_QK):
            wb_ref[:, c:c + GLA_QK] = w_ref[c:c + GLA_QK, :].T.astype(BF16)
        wr_ref[...] = w_ref[GLA_MAIN:, :].astype(BF16)

    sh1, sc1 = _mod_chunk(mod_ref, 0), _mod_chunk(mod_ref, 1)
    part = x_ref.shape[0] // PREMIX_PARTS
    for r0 in range(0, x_ref.shape[0], part):
        rows = slice(r0, r0 + part)
        hb = (_rms(x_ref[rows, :], g_ref[0:1, :]) * (1.0 + sc1) + sh1).astype(BF16)
        z = jnp.dot(hb, wb_ref[...], preferred_element_type=F32)
        q_ref[rows, :] = z[:, :GLA_QK] * (GLA_DK ** -0.5)
        k_ref[rows, :] = z[:, GLA_QK:2 * GLA_QK]
        v_ref[rows, :] = z[:, 2 * GLA_QK:2 * GLA_QK + GLA_VD].astype(BF16)
        og_ref[rows, :] = z[:, 2 * GLA_QK + GLA_VD:]
        r = _bdot_nt(hb, wr_ref[...])
        pre = _bdot(r, wg_ref[...]) + gb_ref[...]
        soft = jnp.log2(1.0 + jnp.exp2(jnp.abs(pre) * (-LOG2E)))
        gate = (jnp.minimum(pre, 0.0) * LOG2E - soft) * (1.0 / GLA_GATE_NORM)
        gf_ref[rows, :] = gate[:, :GLA_QK]
        gbk_ref[rows, :] = gate[:, GLA_QK:]


def _premix1(stream, x2d, mod, norm_g, w_in_t, w_gate, gate_bias):
    t, tm = stream.t, stream.tm
    w_in = w_in_t
    assert w_in.shape == (1, GLA_MAIN + 2 * GLA_RANK, D_MODEL)
    return pl.pallas_call(
        _premix1_kernel,
        grid=(t // tm,),
        in_specs=[_row_spec(tm, D_MODEL), stream.mod_spec(), _const_spec((4, D_MODEL)),
                  _f32_weight_spec(w_in.shape), _const_spec((2 * GLA_RANK, 2 * GLA_QK)),
                  _const_spec((1, 2 * GLA_QK))],
        out_specs=[_row_spec(tm, GLA_QK), _row_spec(tm, GLA_QK), _row_spec(tm, GLA_VD),
                   _row_spec(tm, GLA_VD), _row_spec(tm, GLA_QK), _row_spec(tm, GLA_QK)],
        out_shape=[jax.ShapeDtypeStruct((t, GLA_QK), F32), jax.ShapeDtypeStruct((t, GLA_QK), F32),
                   jax.ShapeDtypeStruct((t, GLA_VD), BF16),
                   jax.ShapeDtypeStruct((t, GLA_VD), F32), jax.ShapeDtypeStruct((t, GLA_QK), F32),
                   jax.ShapeDtypeStruct((t, GLA_QK), F32)],
        scratch_shapes=[pltpu.VMEM((D_MODEL, GLA_MAIN), BF16),
                        pltpu.VMEM((2 * GLA_RANK, D_MODEL), BF16)],
        compiler_params=_cparams("arbitrary"),
        name="premix1",
    )(x2d, mod, norm_g, w_in, w_gate, gate_bias)


def _split3(x):
    hi = x.astype(BF16)
    r1 = x - hi.astype(F32)
    mid = r1.astype(BF16)
    lo = (r1 - mid.astype(F32)).astype(BF16)
    return hi, mid, lo


def _level_exponent(b, s, reverse):
    idx = s if reverse else s - 1
    if s >= 8:
        n = GLA_TILE // (2 * s)
        b4 = b.reshape(n, 2, s, GLA_DK)
        first, second = b4[:, 0:1], b4[:, 1:2]
        r = (second[:, :, 0:1] if reverse else first[:, :, s - 1:s])
        parts = [first - r, r - second] if reverse else [r - first, second - r]
        return jnp.concatenate(parts, axis=1).reshape(GLA_TILE, GLA_DK)
    b8 = b.reshape(GLA_TILE // 8, 8, GLA_DK)
    sub = lax.broadcasted_iota(jnp.int32, (1, 8, 1), 1)
    if s == 4:
        r = b8[:, idx:idx + 1, :]
    else:
        assert s == 2
        r = jnp.where(sub < 4, b8[:, idx:idx + 1, :], b8[:, 4 + idx:5 + idx, :])
    in_second = (sub // s) % 2 == 1
    sign = jnp.where(in_second != reverse, 1.0, -1.0)
    return ((b8 - r) * sign).reshape(GLA_TILE, GLA_DK)


class _PairMatrix:
    LEVELS = [(1 << (lv - 1), lv) for lv in range(2, GLA_TILE.bit_length())]

    def __init__(self, q, k, g_f, g_b, b_f, b_b, lvl):
        half = GLA_TILE // 2
        self.lo, self.hi = slice(0, half), slice(half, GLA_TILE)
        self.q_bf, self.k_bf = q.astype(BF16), k.astype(BF16)
        self.b_f, self.b_b, self.lvl = b_f, b_b, lvl
        self.row = lax.broadcasted_iota(jnp.int32, (GLA_TILE, 1), 0)
        odd = self.row % 2 == 1
        own = jnp.sum(q * k, axis=-1, keepdims=True)
        k_adj = jnp.where(odd, pltpu.roll(k, 1, axis=0), pltpu.roll(k, GLA_TILE - 1, axis=0))
        adj = jnp.sum(q * jnp.exp2(jnp.where(odd, g_f, g_b)) * k_adj, axis=-1, keepdims=True)
        self.blocks = [jnp.where(lvl == 0, 2.0 * own[rows], jnp.where(lvl == 1, adj[rows], 0.0))
                       for rows in (self.lo, self.hi)]
        self.cross_f = self.cross_b = None

    def operands(self, s):
        qb, kb, b_f, b_b = self.q_bf, self.k_bf, self.b_f, self.b_b
        if s >= 16:
            n = GLA_TILE // (2 * s)

            def halves(x):
                x4 = x.reshape(n, 2, s, GLA_DK)
                return x4[:, 0:1], x4[:, 1:2]

            def rows(first, second):
                return jnp.concatenate([first, second], axis=1).reshape(GLA_TILE, GLA_DK)

            (bf1, bf2), (bb1, bb2) = halves(b_f), halves(b_b)
            (q1, q2), (k1, k2) = halves(qb), halves(kb)
            r_f, r_b = bf1[:, :, s - 1:s], bb2[:, :, 0:1]
            zero = jnp.zeros_like(q1)
            lhs = jnp.concatenate(
                [rows(zero, q2 * jnp.exp2(bf2 - r_f).astype(BF16)),
                 rows(q1 * jnp.exp2(bb1 - r_b).astype(BF16), zero)], axis=1)
            rhs = jnp.concatenate(
                [rows(k1 * jnp.exp2(r_f - bf1).astype(BF16), zero),
                 rows(zero, k2 * jnp.exp2(r_b - bb2).astype(BF16))], axis=1)
        else:
            second = (self.row // s) % 2 == 1
            f_f = jnp.exp2(_level_exponent(b_f, s, False)).astype(BF16)
            f_b = jnp.exp2(_level_exponent(b_b, s, True)).astype(BF16)
            u = qb * jnp.where(second, f_f, f_b)
            w = kb * jnp.where(second, f_b, f_f)
            zero = jnp.zeros_like(u)
            lhs = jnp.concatenate([jnp.where(second, u, zero), jnp.where(second, zero, u)], axis=1)
            rhs = jnp.concatenate([jnp.where(second, zero, w), jnp.where(second, w, zero)], axis=1)
        return lhs, rhs

    def absorb(self, s, level, pairs):
        lo, hi = self.lo, self.hi
        if 2 * s == GLA_TILE:
            self.cross_f, self.cross_b = pairs[hi, lo], pairs[lo, hi]
        else:
            self.blocks = [jnp.where(self.lvl == level, pairs[lo, lo], self.blocks[0]),
                           jnp.where(self.lvl == level, pairs[hi, hi], self.blocks[1])]

    def matrix(self):
        top = jnp.concatenate([self.blocks[0].astype(BF16), self.cross_b.astype(BF16)], axis=1)
        bottom = jnp.concatenate([self.cross_f.astype(BF16), self.blocks[1].astype(BF16)], axis=1)
        return jnp.concatenate([top, bottom], axis=0)


def _pair_matrix(q, k, g_f, g_b, b_f, b_b, lvl):
    builder = _PairMatrix(q, k, g_f, g_b, b_f, b_b, lvl)
    for s, level in _PairMatrix.LEVELS:
        builder.absorb(s, level, _bdot_nt(*builder.operands(s)))
    return builder.matrix()


def _carry_state(q, k, b, v, s_ref, reverse):
    edge = 0 if reverse else GLA_TILE - 1
    b_last = b[edge:edge + 1, :]
    qe = (q * jnp.exp2(b)).astype(BF16)
    ke = (k * jnp.exp2(b_last - b)).astype(BF16)
    st = s_ref[...]
    s_ref[...] = st * jnp.exp2(b_last) + lax.dot_general(
        v, ke, TN_DIMS, preferred_element_type=F32)
    return _bdot_nt(qe, st)


def _gla_kernel(*refs, zero_init, nt):
    if zero_init:
        (tri_ref, lvl_ref, q_ref, k_ref, gf_ref, gb_ref, v_ref,
         o_ref, sfo_ref, sbo_ref, sf_ref, sb_ref, bb_ref) = refs
        sf_ref[...] = jnp.zeros_like(sf_ref)
        sb_ref[...] = jnp.zeros_like(sb_ref)
    else:
        (tri_ref, lvl_ref, q_ref, k_ref, gf_ref, gb_ref, v_ref, s0f_ref, s0b_ref,
         o_ref, sfo_ref, sbo_ref, sf_ref, sb_ref, bb_ref) = refs
        for hh in range(sf_ref.shape[0]):
            sf_ref[hh] = s0f_ref[hh].T
            sb_ref[hh] = s0b_ref[hh].T
    hp = sf_ref.shape[0]

    def tile_rows(tile):
        r0 = tile * GLA_TILE
        return pl.ds(r0 if isinstance(r0, int) else pl.multiple_of(r0, GLA_TILE), GLA_TILE)

    heads = [(slice(hh * GLA_DK, (hh + 1) * GLA_DK), slice(hh * GLA_DV, (hh + 1) * GLA_DV), hh)
             for hh in range(hp)]

    def forward_sweep(t, carry):
        rows = tile_rows(t)
        g_f, g_b = gf_ref[rows, :], gb_ref[rows, :]
        c = None
        for part in _split3(jnp.concatenate([g_f, g_b], axis=1)):
            term = jnp.dot(tri_ref[...], part, preferred_element_type=F32)
            c = term if c is None else c + term
        width = hp * GLA_DK
        c_b = c[:, width:]
        b_f = c[:, :width]
        b_b = (c_b[GLA_TILE - 1:GLA_TILE, :] - c_b) + g_b
        bb_ref[t] = b_b
        for dk, dv, hh in heads:
            q, k, v = q_ref[rows, dk], k_ref[rows, dk], v_ref[rows, dv]
            att = _pair_matrix(q, k, g_f[:, dk], g_b[:, dk], b_f[:, dk], b_b[:, dk], lvl_ref[...])
            o_ref[rows, dv] = (jnp.dot(att, v, preferred_element_type=F32)
                               + _carry_state(q, k, b_f[:, dk], v, sf_ref.at[hh], False))
        return carry

    def backward_sweep(i, carry):
        t = nt - 1 - i
        rows = tile_rows(t)
        b_b = bb_ref[t]
        for dk, dv, hh in heads:
            o_ref[rows, dv] += _carry_state(q_ref[rows, dk], k_ref[rows, dk], b_b[:, dk],
                                            v_ref[rows, dv], sb_ref.at[hh], True)
        return carry

    if nt == 1:
        forward_sweep(0, 0)
        backward_sweep(0, 0)
    else:
        lax.fori_loop(0, nt, forward_sweep, 0)
        lax.fori_loop(0, nt, backward_sweep, 0)
    for hh in range(hp):
        sfo_ref[hh] = sf_ref[hh].T
        sbo_ref[hh] = sb_ref[hh].T


def _gla_scan(grp, row0, q, k, v, gf, gb, s0f, s0b, tri, lvl):
    n = grp.n
    nt = n // GLA_TILE
    in_head = 2 * n * (4 * GLA_DK * 4 + GLA_DV * 2)
    out_head = n * GLA_DV * 4
    hp, out_mode = GLA_HEADS, None
    while hp * (in_head + 2 * out_head) > GLA_SEQ_VMEM_BUDGET:
        if hp * (in_head + out_head) <= GLA_SEQ_VMEM_BUDGET:
            out_mode = pl.Buffered(1)
            break
        assert hp > 1
        hp //= 2
    zero_init = s0f is None
    half = GLA_TILE // 2
    assert row0 % n == 0
    seq0 = row0 // n
    in_dk = pl.BlockSpec((n, hp * GLA_DK), lambda b, h: (seq0 + b, h))
    in_dv = pl.BlockSpec((n, hp * GLA_DV), lambda b, h: (seq0 + b, h))
    seq_dv = pl.BlockSpec((n, hp * GLA_DV), lambda b, h: (b, h), pipeline_mode=out_mode)
    state_spec = pl.BlockSpec((None, hp, GLA_DK, GLA_DV), lambda b, h: (b, h, 0, 0),
                              pipeline_mode=out_mode)
    in_specs = [_const_spec((GLA_TILE, GLA_TILE)), _const_spec((half, half)),
                in_dk, in_dk, in_dk, in_dk, in_dv]
    args = [tri, lvl, q, k, gf, gb, v]
    if not zero_init:
        in_specs += [state_spec, state_spec]
        args += [s0f, s0b]
    state_shape = jax.ShapeDtypeStruct((grp.b, GLA_HEADS, GLA_DK, GLA_DV), F32)
    return pl.pallas_call(
        functools.partial(_gla_kernel, zero_init=zero_init, nt=nt),
        grid=(grp.b, GLA_HEADS // hp),
        in_specs=in_specs,
        out_specs=[seq_dv, state_spec, state_spec],
        out_shape=[jax.ShapeDtypeStruct((grp.t, GLA_VD), F32), state_shape, state_shape],
        scratch_shapes=[pltpu.VMEM((hp, GLA_DV, GLA_DK), F32), pltpu.VMEM((hp, GLA_DV, GLA_DK), F32),
                        pltpu.VMEM((nt, GLA_TILE, hp * GLA_DK), F32)],
        compiler_params=_cparams("arbitrary", "arbitrary"),
        name="gla_scan",
    )(*args)


def _gla_constants():
    half = GLA_TILE // 2
    i = jnp.arange(GLA_TILE)[:, None]
    j = jnp.arange(GLA_TILE)[None, :]
    tri = (j <= i).astype(BF16)
    ih, jh = i[:half], j[:, :half]
    x = jnp.bitwise_xor(ih, jh)
    lvl = sum((x >= (1 << p)).astype(jnp.int32) for p in range(half.bit_length() - 1))
    return tri, lvl


def _postmix1_kernel(oc_ref, ol_ref, og_ref, x_ref, mod_ref, g_ref, gn_ref, wo_hbm, w1_hbm, w2_hbm,
                     yc_ref, yl_ref, *scratch, stream):
    weights = _TailWeights(1, wo_hbm, w1_hbm, w2_hbm, *scratch)
    _first_step_streams_weights(
        functools.partial(_postmix1_body, oc_ref, ol_ref, og_ref, x_ref, mod_ref, g_ref, gn_ref,
                          yc_ref, yl_ref, weights, stream), weights)


def _postmix1_body(oc_ref, ol_ref, og_ref, x_ref, mod_ref, g_ref, gn_ref, yc_ref, yl_ref,
                   weights, stream, streaming):
    is_lat = stream.is_lat()
    gn = gn_ref[...]
    ys = []
    for h in range(GLA_HEADS):
        cols = slice(h * GLA_DV, (h + 1) * GLA_DV)
        o = _rms(jnp.where(is_lat, ol_ref[:, cols], oc_ref[:, cols]), gn)
        og = og_ref[:, cols]
        ys.append((o * (og / (1.0 + jnp.exp(-og)))).astype(BF16))
    if streaming:
        weights.fetch_out_proj()
    mix = None
    for h, y in enumerate(ys):
        part = jnp.dot(y, weights.wo_s[h * GLA_DV:(h + 1) * GLA_DV, :], preferred_element_type=F32)
        mix = part if mix is None else mix + part
    gt1 = _mod_chunk(mod_ref, 2)
    x1 = x_ref[...] + gt1 * _rms(mix, g_ref[1:2, :])
    y = _mlp_tail(x1, mod_ref, g_ref, weights, streaming)

    @pl.when(jnp.logical_not(is_lat))
    def _():
        yc_ref[...] = y

    @pl.when(is_lat)
    def _():
        yl_ref[...] = y


def _postmix1(stream, o_ctx, o_lat, og, x2d, mod, norm_g, gla_norm_g, w_out, w1, w2):
    hbm = pl.BlockSpec(memory_space=pl.ANY)
    return pl.pallas_call(
        functools.partial(_postmix1_kernel, stream=stream),
        grid=(stream.tiles,),
        in_specs=[stream.ctx_spec(GLA_VD), stream.lat_spec(GLA_VD), stream.spec(GLA_VD),
                  stream.spec(D_MODEL), stream.mod_spec(), _const_spec((4, D_MODEL)),
                  _const_spec((1, GLA_DV)), hbm, hbm, hbm],
        out_specs=[stream.ctx_spec(D_MODEL), stream.lat_spec(D_MODEL)],
        out_shape=[jax.ShapeDtypeStruct((stream.ctx.t, D_MODEL), F32),
                   jax.ShapeDtypeStruct((stream.lat.t, D_MODEL), F32)],
        scratch_shapes=_tail_weight_scratch(),
        compiler_params=_cparams("arbitrary"),
        name="postmix1_mlp",
    )(o_ctx, o_lat, og, x2d, mod, norm_g, gla_norm_g, w_out, w1, w2)


def kernel(x_prompt, x_sample, cache_k, cache_v, state_fwd, state_bwd, c, c_ctx, mod_w, mod_b,
           norm_g, ab_w_in, conv_w, attn_sink, ab_w_out, gla_w_in, gla_gate_w, gla_gate_b,
           gla_norm_g, gla_w_out, mlp_w1, mlp_w2):
    b_ctx, n_ctx, _ = x_prompt.shape
    b_lat, n_lat, _ = x_sample.shape
    assert mod_w.shape[0] == 2 and ab_w_in.shape[0] == 1 and gla_w_in.shape[0] == 1
    assert 1 + b_lat <= 8

    cond8 = jnp.zeros((8, D_MODEL), F32).at[0].set(c_ctx).at[1:1 + b_lat].set(c)
    mod = _modulation(cond8, mod_w, mod_b)
    mods = [mod[l].reshape(8, 1, -1) for l in range(2)]

    w_gate = jnp.zeros((2 * GLA_RANK, 2 * GLA_QK), F32)
    w_gate = w_gate.at[:GLA_RANK, :GLA_QK].set(gla_gate_w[0, 0])
    w_gate = w_gate.at[GLA_RANK:, GLA_QK:].set(gla_gate_w[0, 1])
    tri, lvl = _gla_constants()
    p = {
        "conv_w": conv_w[0],
        "sink": attn_sink[0],
        "gla_w_in": jnp.swapaxes(gla_w_in, 1, 2),
        "gla_w_gate": w_gate.astype(BF16),
        "gla_gate_bias": gla_gate_b[0].reshape(1, 2 * GLA_QK),
        "gla_norm_g": gla_norm_g[0].reshape(1, GLA_DV),
    }

    ctx, lat = _Group(b_ctx, n_ctx), _Group(b_lat, n_lat)
    stream = _Stream(ctx, lat, TOKEN_TILE)
    x_ctx, x_lat = x_prompt.reshape(ctx.t, D_MODEL), x_sample.reshape(lat.t, D_MODEL)
    past = cache_k.shape[2]
    k_ctx = cache_k[:, 0].reshape(b_lat, past, KD)
    v_ctx = cache_v[:, 0].reshape(b_lat, past, KD)

    wide = _Stream(ctx, lat, PREMIX_PARTS * TOKEN_TILE)
    zc, q, kv, k_t, v_t = _premix0(wide, x_ctx, x_lat, mods[0], norm_g[0], ab_w_in,
                                   _rope_tables(n_lat, wide.tm))
    attn_ctx = _ctx_attention(ctx, q, kv, p["sink"])
    attn_lat = _lat_attention(lat, ctx.t, q, kv, k_ctx, v_ctx, p["sink"])
    x1 = _postmix0(stream, zc, attn_ctx, attn_lat, x_ctx, x_lat, mods[0], norm_g[0], p["conv_w"],
                   ab_w_out, mlp_w1, mlp_w2)
    gq, gk, gv, og, gf, gb = _premix1(wide, x1, mods[1], norm_g[1], p["gla_w_in"],
                                      p["gla_w_gate"], p["gla_gate_bias"])
    o_ctx, sf, sb = _gla_scan(ctx, 0, gq, gk, gv, gf, gb, None, None, tri, lvl)
    o_lat, _, _ = _gla_scan(lat, ctx.t, gq, gk, gv, gf, gb, state_fwd[:, 0], state_bwd[:, 0],
                            tri, lvl)
    y_ctx, y_lat = _postmix1(stream, o_ctx, o_lat, og, x1, mods[1], norm_g[1], p["gla_norm_g"],
                             gla_w_out, mlp_w1, mlp_w2)

    def cache_layout(t):
        t = t.reshape(b_ctx, 1, N_KV_HEADS, HEAD_DIM, n_ctx)
        return jnp.transpose(t, (0, 1, 4, 2, 3))

    return (y_ctx.reshape(x_prompt.shape), y_lat.reshape(x_sample.shape),
            cache_layout(k_t), cache_layout(v_t), sf[:, None], sb[:, None])
```

```python
import functools

import jax
import jax.numpy as jnp
from jax import lax
from jax.experimental import pallas as pl
from jax.experimental.pallas import tpu as pltpu

F32 = jnp.float32
BF16 = jnp.bfloat16

D_MODEL = 1024
MOD_CHUNKS = 6
EPS = 1e-6
CONV_DIM = 512
N_Q_HEADS = 8
N_KV_HEADS = 2
GQA_GROUP = 4
HEAD_DIM = 64
WINDOW = 128
ATTN_BLOCK = 128
ATTN_BLOCKS_PER_STEP = 16
CTX_SEQS_PER_STEP = 8
GRID_W = 64
ROPE_BASE = 10000.0
QD = N_Q_HEADS * HEAD_DIM
KD = N_KV_HEADS * HEAD_DIM
AB_IN = 3 * CONV_DIM + QD + 2 * KD
GLA_HEADS = 4
GLA_DK = 128
GLA_DV = 256
GLA_RANK = 16
GLA_GATE_NORM = 16.0
GLA_TILE = 256
GLA_SEQ_VMEM_BUDGET = 32 * 1024 * 1024
LOG2E = 1.4426950408889634
GLA_QK = GLA_HEADS * GLA_DK
GLA_VD = GLA_HEADS * GLA_DV
GLA_MAIN = 2 * GLA_QK + 2 * GLA_VD
D_FF = 4 * D_MODEL
MLP_CHUNK = 512
MLP_CHUNKS = D_FF // MLP_CHUNK
WEIGHT_PIECE = 256
TOKEN_TILE = 512
PREMIX_PARTS = 2
NEG_INF = -1e30
LANES = 128
VMEM_LIMIT = 60 * 1024 * 1024

NT_DIMS = (((1,), (1,)), ((), ()))
TN_DIMS = (((0,), (0,)), ((), ()))


def _cparams(*sem):
    return pltpu.CompilerParams(dimension_semantics=sem, vmem_limit_bytes=VMEM_LIMIT)


def _bdot(a, b):
    return jnp.dot(a.astype(BF16), b.astype(BF16), preferred_element_type=F32)


def _bdot_nt(a, b):
    return lax.dot_general(a.astype(BF16), b.astype(BF16), NT_DIMS, preferred_element_type=F32)


def _rms(x, g):
    ms = jnp.mean(x * x, axis=-1, keepdims=True)
    return x * lax.rsqrt(ms + EPS) * g


def _mod_chunk(mod_ref, i):
    return mod_ref[:, i * D_MODEL:(i + 1) * D_MODEL]


def _const_spec(shape):
    return pl.BlockSpec(shape, lambda *_: (0,) * len(shape))


def _f32_weight_spec(shape):
    assert shape[0] == 1
    return pl.BlockSpec((None,) + tuple(shape[1:]), lambda *_: (0, 0, 0),
                        pipeline_mode=pl.Buffered(1))


def _cast_once(w_ref, wb_ref):
    @pl.when(pl.program_id(0) == 0)
    def _():
        wb_ref[...] = w_ref[...].astype(BF16)


def _mod_kernel(cond_ref, w_ref, b_ref, o_ref):
    cnd = cond_ref[...]
    s = cnd / (1.0 + jnp.exp(-cnd))
    o_ref[...] = _bdot(s, w_ref[...]) + b_ref[...]


def _modulation(cond8, mod_w, mod_b):
    depth = mod_w.shape[0]
    n = mod_w.shape[2]
    tn = 3072
    return pl.pallas_call(
        _mod_kernel,
        grid=(depth, n // tn),
        in_specs=[
            pl.BlockSpec((8, D_MODEL), lambda l, j: (0, 0)),
            pl.BlockSpec((None, D_MODEL, tn), lambda l, j: (l, 0, j)),
            pl.BlockSpec((None, 1, tn), lambda l, j: (l, 0, j)),
        ],
        out_specs=pl.BlockSpec((None, 8, tn), lambda l, j: (l, 0, j)),
        out_shape=jax.ShapeDtypeStruct((depth, 8, n), F32),
        compiler_params=_cparams("arbitrary", "arbitrary"),
        name="modulation",
    )(cond8, mod_w, mod_b.reshape(depth, 1, n))


class _Group:
    def __init__(self, b, n):
        self.b, self.n, self.t = b, n, b * n


class _Stream:
    def __init__(self, ctx, lat, tm):
        assert ctx.t % tm == 0 and lat.n % tm == 0 and tm % ctx.n == 0
        self.ctx, self.lat, self.tm = ctx, lat, tm
        self.t = ctx.t + lat.t
        self.ctx_tiles = ctx.t // tm
        self.tiles = self.t // tm

    def is_lat(self):
        return pl.program_id(0) >= self.ctx_tiles

    def spec(self, width):
        return pl.BlockSpec((self.tm, width), lambda t: (t, 0))

    def ctx_spec(self, width):
        last = self.ctx_tiles - 1
        return pl.BlockSpec((self.tm, width), lambda t: (jnp.minimum(t, last), 0))

    def lat_spec(self, width):
        first = self.ctx_tiles
        return pl.BlockSpec((self.tm, width), lambda t: (jnp.maximum(t - first, 0), 0))

    def mod_spec(self):
        first, per = self.ctx_tiles, self.lat.n // self.tm
        return pl.BlockSpec((None, 1, MOD_CHUNKS * D_MODEL),
                            lambda t: (jnp.where(t < first, 0, 1 + (t - first) // per), 0, 0))

    def seq_len(self):
        return jnp.where(self.is_lat(), self.lat.n, self.ctx.n)


def _row_spec(tm, width):
    return pl.BlockSpec((tm, width), lambda t: (t, 0))


def _rope(x, cos, sin_lo, sin_hi):
    return (x * cos + pltpu.roll(x, LANES - 16, axis=1) * sin_lo
            + pltpu.roll(x, 16, axis=1) * sin_hi)


def _premix0_kernel(xc_ref, xl_ref, mod_ref, g_ref, w_ref, cos_ref, slo_ref, shi_ref,
                    zc_ref, q_ref, kv_ref, kt_ref, vt_ref, wb_ref, *, stream):
    _cast_once(w_ref, wb_ref)
    is_lat = stream.is_lat()
    sh1, sc1 = _mod_chunk(mod_ref, 0), _mod_chunk(mod_ref, 1)
    c3 = 3 * CONV_DIM
    scale = HEAD_DIM ** -0.5 * LOG2E
    n = kt_ref.shape[2]
    part = stream.tm // PREMIX_PARTS
    raw_kv = []
    for r0 in range(0, stream.tm, part):
        rows = slice(r0, r0 + part)
        x = jnp.where(is_lat, xl_ref[rows, :], xc_ref[rows, :])
        h = _rms(x, g_ref[0:1, :]) * (1.0 + sc1) + sh1
        z = _bdot(h, wb_ref[...])
        zc_ref[rows, :] = z[:, :c3]
        cos, slo, shi = cos_ref[rows, :], slo_ref[rows, :], shi_ref[rows, :]
        for j in range(QD // LANES):
            qs = z[:, c3 + j * LANES:c3 + (j + 1) * LANES]
            q_ref[rows, j * LANES:(j + 1) * LANES] = (_rope(qs, cos, slo, shi) * scale).astype(BF16)
        kv_ref[rows, :KD] = _rope(z[:, c3 + QD:c3 + QD + KD], cos, slo, shi)
        kv_ref[rows, KD:] = z[:, c3 + QD + KD:]
        raw_kv.append(z[:, c3 + QD:])

    @pl.when(jnp.logical_not(is_lat))
    def _():
        for p, kv in enumerate(raw_kv):
            for j in range(part // n):
                seq = p * (part // n) + j
                kt_ref[seq] = kv[j * n:(j + 1) * n, :KD].T
                vt_ref[seq] = kv[j * n:(j + 1) * n, KD:].T


def _premix0(stream, x_ctx, x_lat, mod, norm_g, w_in, rope_tabs):
    tm, ctx, lat = stream.tm, stream.ctx, stream.lat
    first, per, last = stream.ctx_tiles, lat.n // tm, stream.ctx_tiles - 1
    rope_spec = pl.BlockSpec((tm, LANES),
                             lambda t: (jnp.where(t < first, 0, 1 + (t - first) % per), 0))
    cache_spec = pl.BlockSpec((tm // ctx.n, KD, ctx.n), lambda t: (jnp.minimum(t, last), 0, 0))
    return pl.pallas_call(
        functools.partial(_premix0_kernel, stream=stream),
        grid=(stream.tiles,),
        in_specs=[stream.ctx_spec(D_MODEL), stream.lat_spec(D_MODEL), stream.mod_spec(),
                  _const_spec((4, D_MODEL)), _f32_weight_spec(w_in.shape)] + [rope_spec] * 3,
        out_specs=[stream.spec(3 * CONV_DIM), stream.spec(QD), stream.spec(2 * KD),
                   cache_spec, cache_spec],
        out_shape=[jax.ShapeDtypeStruct((stream.t, 3 * CONV_DIM), F32),
                   jax.ShapeDtypeStruct((stream.t, QD), BF16),
                   jax.ShapeDtypeStruct((stream.t, 2 * KD), F32),
                   jax.ShapeDtypeStruct((ctx.b, KD, ctx.n), F32),
                   jax.ShapeDtypeStruct((ctx.b, KD, ctx.n), F32)],
        scratch_shapes=[pltpu.VMEM(w_in.shape[1:], BF16)],
        compiler_params=_cparams("arbitrary"),
        name="premix0",
    )(x_ctx, x_lat, mod, norm_g, w_in, *rope_tabs)


def _rope_tables(n, identity_rows):
    rows = n // GRID_W
    pos_r = jnp.repeat(jnp.arange(rows), GRID_W)
    pos_c = jnp.tile(jnp.arange(GRID_W), rows)
    half = HEAD_DIM // 2
    quarter = half // 2
    inv = ROPE_BASE ** (-(jnp.arange(quarter, dtype=F32) * 2.0 / half))

    def cs(pos):
        ang = pos.astype(F32)[:, None] * inv[None, :]
        return jnp.cos(ang), jnp.sin(ang)

    cr, sr = cs(pos_r)
    cc, sc = cs(pos_c)
    zero = jnp.zeros_like(sr)
    cos = jnp.concatenate([cr, cr, cc, cc], axis=1)
    sin_lo = jnp.concatenate([-sr, zero, -sc, zero], axis=1)
    sin_hi = jnp.concatenate([zero, sr, zero, sc], axis=1)
    rep = LANES // HEAD_DIM
    tables = []
    for t, ident in ((cos, 1.0), (sin_lo, 0.0), (sin_hi, 0.0)):
        head = jnp.full((identity_rows, LANES), ident, F32)
        tables.append(jnp.concatenate([head, jnp.tile(t, (1, rep))], axis=0))
    return tuple(tables)


def _attention_operands(k_all, v_all):
    assert KD == LANES == 2 * HEAD_DIM and GQA_GROUP == 4
    lane = lax.broadcasted_iota(jnp.int32, (1, LANES), 1)
    sub = lax.broadcasted_iota(jnp.int32, (LANES, 1), 0)
    v_t = v_all.T
    k_swapped = pltpu.roll(k_all, HEAD_DIM, axis=1)
    ops = []
    for g in range(N_KV_HEADS):
        k_low, k_high = (k_all, k_swapped) if g == 0 else (k_swapped, k_all)
        kz_even = jnp.where(lane < HEAD_DIM, k_low, 0.0).astype(BF16)
        kz_odd = jnp.where(lane >= HEAD_DIM, k_high, 0.0).astype(BF16)
        own = (sub < HEAD_DIM) if g == 0 else (sub >= HEAD_DIM)
        v_ext_t = jnp.where(own, v_t, 1.0).astype(BF16)
        ops.append((kz_even, kz_odd, v_ext_t))
    return ops


def _concat_operands(parts):
    return [(jnp.concatenate([p[g][0] for p in parts], axis=0),
             jnp.concatenate([p[g][1] for p in parts], axis=0),
             jnp.concatenate([p[g][2] for p in parts], axis=1)) for g in range(N_KV_HEADS)]


def _group_heads(g):
    return [4 * g, 4 * g + 2, 4 * g + 1, 4 * g + 3]


def _scores(sink_ref, q_ref, operands, bias_t, g):
    m = q_ref.shape[0]
    kz_even, kz_odd, _ = operands[g]
    qq = jnp.concatenate([q_ref[:, (2 * g) * LANES:(2 * g + 1) * LANES],
                          q_ref[:, (2 * g + 1) * LANES:(2 * g + 2) * LANES]], axis=0)
    s = jnp.concatenate([_bdot_nt(kz_even, qq), _bdot_nt(kz_odd, qq)], axis=1)
    if bias_t is not None:
        nb = bias_t.shape[0]
        s = jnp.concatenate([s[:nb] + jnp.concatenate([bias_t] * GQA_GROUP, axis=1), s[nb:]],
                            axis=0)
    sink = jnp.concatenate([jnp.full((1, m), sink_ref[h] * LOG2E, F32) for h in _group_heads(g)],
                           axis=1)
    mx = jnp.maximum(jnp.max(s, axis=0, keepdims=True), sink)
    return s, sink, mx


def _weighted_values(s, sink, mx, operands, g, m):
    v_ext_t = operands[g][2]
    p = jnp.exp2(s - mx).astype(BF16)
    oe = jnp.dot(v_ext_t, p, preferred_element_type=F32)
    other = (1 - g) * HEAD_DIM
    den = oe[other:other + 1] + jnp.exp2(sink - mx)
    o_g = oe[g * HEAD_DIM:(g + 1) * HEAD_DIM] / den
    return {h: o_g[:, i * m:(i + 1) * m] for i, h in enumerate(_group_heads(g))}


def _sink_attention(sink_ref, blocks):
    chains = [(blk, g) for blk in range(len(blocks)) for g in range(N_KV_HEADS)]
    outs = [dict() for _ in blocks]

    def scores(chain):
        blk, g = chain
        q_ref, operands, bias_t, _ = blocks[blk]
        return _scores(sink_ref, q_ref, operands, bias_t, g)

    ahead = scores(chains[0])
    for i, (blk, g) in enumerate(chains):
        current = ahead
        if i + 1 < len(chains):
            ahead = scores(chains[i + 1])
        q_ref, operands, _, o_ref = blocks[blk]
        outs[blk].update(_weighted_values(*current, operands, g, q_ref.shape[0]))
        if g == N_KV_HEADS - 1:
            for j in range(N_Q_HEADS // 2):
                pair_t = jnp.concatenate([outs[blk][2 * j], outs[blk][2 * j + 1]], axis=0)
                o_ref[:, j * LANES:(j + 1) * LANES] = pair_t.T.astype(BF16)


def _ctx_attn_kernel(sink_ref, q_ref, kv_ref, o_ref, *, n):
    blocks = []
    for j in range(q_ref.shape[0] // n):
        rows = pl.ds(j * n, n)
        operands = _attention_operands(kv_ref[rows, :KD], kv_ref[rows, KD:])
        blocks.append((q_ref.at[rows], operands, None, o_ref.at[rows]))
    _sink_attention(sink_ref, blocks)


def _ctx_attention(grp, q, kv, sink):
    per = CTX_SEQS_PER_STEP
    assert grp.b % per == 0
    n = per * grp.n
    return pl.pallas_call(
        functools.partial(_ctx_attn_kernel, n=grp.n),
        grid=(grp.b // per,),
        in_specs=[pl.BlockSpec(memory_space=pltpu.SMEM), _row_spec(n, QD), _row_spec(n, 2 * KD)],
        out_specs=_row_spec(n, QD),
        out_shape=jax.ShapeDtypeStruct((grp.t, QD), BF16),
        compiler_params=_cparams("arbitrary"),
        name="ctx_attention",
    )(sink, q, kv)


def _lat_attn_kernel(sink_ref, q_ref, kvp_ref, kvc_ref, kvn_ref, kc_ref, vc_ref, o_ref, *, n):
    band = jnp.concatenate([kvp_ref[...], kvc_ref[...], kvn_ref[...]], axis=0)
    kj = lax.broadcasted_iota(jnp.int32, (3 * ATTN_BLOCK, ATTN_BLOCK), 0)
    qi = lax.broadcasted_iota(jnp.int32, (3 * ATTN_BLOCK, ATTN_BLOCK), 1)
    rel = kj - ATTN_BLOCK - qi
    ctx_ops = _attention_operands(kc_ref[...], vc_ref[...])
    band_ops = [_attention_operands(band[i * ATTN_BLOCK:(i + 1) * ATTN_BLOCK, :KD],
                                    band[i * ATTN_BLOCK:(i + 1) * ATTN_BLOCK, KD:])
                for i in range(ATTN_BLOCKS_PER_STEP + 2)]
    blocks = []
    for j in range(ATTN_BLOCKS_PER_STEP):
        blk = pl.program_id(1) * ATTN_BLOCKS_PER_STEP + j
        kpos = (blk - 1) * ATTN_BLOCK + kj
        valid = (jnp.abs(rel) <= WINDOW) & (kpos >= 0) & (kpos < n)
        bias_t = jnp.where(valid, 0.0, NEG_INF)
        operands = _concat_operands(band_ops[j:j + 3] + [ctx_ops])
        rows = pl.ds(j * ATTN_BLOCK, ATTN_BLOCK)
        blocks.append((q_ref.at[rows], operands, bias_t, o_ref.at[rows]))
    _sink_attention(sink_ref, blocks)


def _lat_attention(grp, row0, q, kv, k_ctx, v_ctx, sink):
    nb = grp.n // ATTN_BLOCK
    per = ATTN_BLOCKS_PER_STEP
    steps = nb // per
    past = k_ctx.shape[1]
    assert row0 % (per * ATTN_BLOCK) == 0
    blk0, step0 = row0 // ATTN_BLOCK, row0 // (per * ATTN_BLOCK)

    def edge(off):
        return pl.BlockSpec((ATTN_BLOCK, 2 * KD),
                            lambda b, i: (blk0 + b * nb + jnp.clip(i * per + off, 0, nb - 1), 0))

    ctx_spec = pl.BlockSpec((None, past, KD), lambda b, i: (b, 0, 0))
    return pl.pallas_call(
        functools.partial(_lat_attn_kernel, n=grp.n),
        grid=(grp.b, steps),
        in_specs=[pl.BlockSpec(memory_space=pltpu.SMEM),
                  pl.BlockSpec((per * ATTN_BLOCK, QD), lambda b, i: (step0 + b * steps + i, 0)),
                  edge(-1),
                  pl.BlockSpec((per * ATTN_BLOCK, 2 * KD), lambda b, i: (step0 + b * steps + i, 0)),
                  edge(per), ctx_spec, ctx_spec],
        out_specs=pl.BlockSpec((per * ATTN_BLOCK, QD), lambda b, i: (b * steps + i, 0)),
        out_shape=jax.ShapeDtypeStruct((grp.t, QD), BF16),
        compiler_params=_cparams("arbitrary", "arbitrary"),
        name="lat_attention",
    )(sink, q, kv, kv, kv, k_ctx, v_ctx)


class _TailWeights:
    def __init__(self, layer, wo_hbm, w1_hbm, w2_hbm, wo_s, w1_s, w2_s, stage1, stage2, sem):
        self.layer = layer
        self.wo_hbm, self.w1_hbm, self.w2_hbm = wo_hbm, w1_hbm, w2_hbm
        self.wo_s, self.w1_s, self.w2_s = wo_s, w1_s, w2_s
        self.stage1, self.stage2, self.sem = stage1, stage2, sem

    def _w1_copy(self, p):
        src = self.w1_hbm.at[self.layer, :, pl.ds(p * WEIGHT_PIECE, WEIGHT_PIECE)]
        return pltpu.make_async_copy(src, self.stage1.at[p % 2], self.sem.at[0, p % 2])

    def _w2_copy(self, p):
        src = self.w2_hbm.at[self.layer, pl.ds(p * WEIGHT_PIECE, WEIGHT_PIECE), :]
        return pltpu.make_async_copy(src, self.stage2.at[p % 2], self.sem.at[1, p % 2])

    def _wo_copy(self, p):
        src = self.wo_hbm.at[0, pl.ds(p * WEIGHT_PIECE, WEIGHT_PIECE), :]
        return pltpu.make_async_copy(src, self.stage2.at[p % 2], self.sem.at[1, p % 2])

    def start(self):
        self._wo_copy(0).start()
        self._wo_copy(1).start()
        self._w1_copy(0).start()
        self._w1_copy(1).start()

    def fetch_out_proj(self):
        pieces = D_MODEL // WEIGHT_PIECE
        for p in range(pieces):
            self._wo_copy(p).wait()
            self.wo_s[p * WEIGHT_PIECE:(p + 1) * WEIGHT_PIECE, :] = self.stage2[p % 2].astype(BF16)
            if p + 2 < pieces:
                self._wo_copy(p + 2).start()
            else:
                self._w2_copy(p + 2 - pieces).start()

    def fetch_mlp_chunk(self, j):
        per = MLP_CHUNK // WEIGHT_PIECE
        total = MLP_CHUNKS * per
        for p in range(j * per, (j + 1) * per):
            part = slice((p % per) * WEIGHT_PIECE, (p % per + 1) * WEIGHT_PIECE)
            self._w1_copy(p).wait()
            self.w1_s[j, :, part] = self.stage1[p % 2].astype(BF16)
            if p + 2 < total:
                self._w1_copy(p + 2).start()
            self._w2_copy(p).wait()
            self.w2_s[j, part, :] = self.stage2[p % 2].astype(BF16)
            if p + 2 < total:
                self._w2_copy(p + 2).start()


def _tail_weight_scratch():
    assert MLP_CHUNK % WEIGHT_PIECE == 0 and D_MODEL // WEIGHT_PIECE >= 2
    return [pltpu.VMEM((D_MODEL, D_MODEL), BF16),
            pltpu.VMEM((MLP_CHUNKS, D_MODEL, MLP_CHUNK), BF16),
            pltpu.VMEM((MLP_CHUNKS, MLP_CHUNK, D_MODEL), BF16),
            pltpu.VMEM((2, D_MODEL, WEIGHT_PIECE), F32),
            pltpu.VMEM((2, WEIGHT_PIECE, D_MODEL), F32),
            pltpu.SemaphoreType.DMA((2, 2))]


def _first_step_streams_weights(body, weights):
    first = pl.program_id(0) == 0

    @pl.when(first)
    def _():
        weights.start()
        body(True)

    @pl.when(jnp.logical_not(first))
    def _():
        body(False)


def _mlp_tail(x, mod_ref, g_ref, weights, streaming):
    sh2, sc2, gt2 = _mod_chunk(mod_ref, 3), _mod_chunk(mod_ref, 4), _mod_chunk(mod_ref, 5)
    hb = (_rms(x, g_ref[2:3, :]) * (1.0 + sc2) + sh2).astype(BF16)
    acc = None
    for j in range(MLP_CHUNKS):
        if streaming:
            weights.fetch_mlp_chunk(j)
        a = jnp.dot(hb, weights.w1_s[j], preferred_element_type=F32)
        a = jnp.maximum(a, 0.0)
        part = jnp.dot((a * a).astype(BF16), weights.w2_s[j], preferred_element_type=F32)
        acc = part if acc is None else acc + part
    return x + gt2 * _rms(acc, g_ref[3:4, :])


def _postmix0_kernel(zc_ref, zp_ref, zn_ref, atc_ref, atl_ref, xc_ref, xl_ref, mod_ref, g_ref,
                     cw_ref, wo_hbm, w1_hbm, w2_hbm, o_ref, *scratch, stream):
    weights = _TailWeights(0, wo_hbm, w1_hbm, w2_hbm, *scratch)
    _first_step_streams_weights(
        functools.partial(_postmix0_body, zc_ref, zp_ref, zn_ref, atc_ref, atl_ref, xc_ref, xl_ref,
                          mod_ref, g_ref, cw_ref, o_ref, weights, stream), weights)


def _postmix0_body(zc_ref, zp_ref, zn_ref, atc_ref, atl_ref, xc_ref, xl_ref, mod_ref, g_ref,
                   cw_ref, o_ref, weights, stream, streaming):
    c, tm = CONV_DIM, stream.tm
    is_lat, n = stream.is_lat(), stream.seq_len()
    zc = zc_ref[...]
    u = zc[:, c:2 * c] * zc[:, 2 * c:]
    u_before = zp_ref[7:8, c:2 * c] * zp_ref[7:8, 2 * c:]
    u_after = zn_ref[0:1, c:2 * c] * zn_ref[0:1, 2 * c:]
    row = lax.broadcasted_iota(jnp.int32, (tm, 1), 0)
    pos = (pl.program_id(0) * tm + row) & (n - 1)
    u_prev = jnp.where(row == 0, u_before, pltpu.roll(u, 1, axis=0))
    u_prev = jnp.where(pos == 0, 0.0, u_prev)
    u_next = jnp.where(row == tm - 1, u_after, pltpu.roll(u, tm - 1, axis=0))
    u_next = jnp.where(pos == n - 1, 0.0, u_next)
    conv = u_prev * cw_ref[0:1, :] + u * cw_ref[1:2, :] + u_next * cw_ref[2:3, :]
    if streaming:
        weights.fetch_out_proj()
    attn = jnp.where(is_lat, atl_ref[...], atc_ref[...])
    mix = (_bdot(zc[:, :c] * conv, weights.wo_s[:c, :])
           + jnp.dot(attn, weights.wo_s[c:, :], preferred_element_type=F32))
    gt1 = _mod_chunk(mod_ref, 2)
    x1 = jnp.where(is_lat, xl_ref[...], xc_ref[...]) + gt1 * _rms(mix, g_ref[1:2, :])
    o_ref[...] = _mlp_tail(x1, mod_ref, g_ref, weights, streaming)


def _postmix0(stream, zc, attn_ctx, attn_lat, x_ctx, x_lat, mod, norm_g, conv_w, w_out, w1, w2):
    tm = stream.tm
    for n in (stream.ctx.n, stream.lat.n):
        assert n & (n - 1) == 0 and stream.ctx.t % n == 0
    r8 = tm // 8
    last8 = stream.t // 8 - 1
    hbm = pl.BlockSpec(memory_space=pl.ANY)
    return pl.pallas_call(
        functools.partial(_postmix0_kernel, stream=stream),
        grid=(stream.tiles,),
        in_specs=[stream.spec(3 * CONV_DIM),
                  pl.BlockSpec((8, 3 * CONV_DIM), lambda t: (jnp.maximum(t * r8 - 1, 0), 0)),
                  pl.BlockSpec((8, 3 * CONV_DIM), lambda t: (jnp.minimum((t + 1) * r8, last8), 0)),
                  stream.ctx_spec(QD), stream.lat_spec(QD),
                  stream.ctx_spec(D_MODEL), stream.lat_spec(D_MODEL), stream.mod_spec(),
                  _const_spec((4, D_MODEL)), _const_spec((3, CONV_DIM)), hbm, hbm, hbm],
        out_specs=stream.spec(D_MODEL),
        out_shape=jax.ShapeDtypeStruct((stream.t, D_MODEL), F32),
        scratch_shapes=_tail_weight_scratch(),
        compiler_params=_cparams("arbitrary"),
        name="postmix0_mlp",
    )(zc, zc, zc, attn_ctx, attn_lat, x_ctx, x_lat, mod, norm_g, conv_w, w_out, w1, w2)


def _premix1_kernel(x_ref, mod_ref, g_ref, w_ref, wg_ref, gb_ref,
                    q_ref, k_ref, v_ref, og_ref, gf_ref, gbk_ref, wb_ref, wr_ref):
    @pl.when(pl.program_id(0) == 0)
    def _():
        for c in range(0, GLA_MAIN, GLA_QK):
            wb_ref[:, c:c + GLA_QK] = w_ref[c:c + GLA_QK, :].T.astype(BF16)
        wr_ref[...] = w_ref[GLA_MAIN:, :].astype(BF16)

    sh1, sc1 = _mod_chunk(mod_ref, 0), _mod_chunk(mod_ref, 1)
    part = x_ref.shape[0] // PREMIX_PARTS
    for r0 in range(0, x_ref.shape[0], part):
        rows = slice(r0, r0 + part)
        hb = (_rms(x_ref[rows, :], g_ref[0:1, :]) * (1.0 + sc1) + sh1).astype(BF16)
        z = jnp.dot(hb, wb_ref[...], preferred_element_type=F32)
        q_ref[rows, :] = z[:, :GLA_QK] * (GLA_DK ** -0.5)
        k_ref[rows, :] = z[:, GLA_QK:2 * GLA_QK]
        v_ref[rows, :] = z[:, 2 * GLA_QK:2 * GLA_QK + GLA_VD].astype(BF16)
        og_ref[rows, :] = z[:, 2 * GLA_QK + GLA_VD:]
        r = _bdot_nt(hb, wr_ref[...])
        pre = _bdot(r, wg_ref[...]) + gb_ref[...]
        soft = jnp.log2(1.0 + jnp.exp2(jnp.abs(pre) * (-LOG2E)))
        gate = (jnp.minimum(pre, 0.0) * LOG2E - soft) * (1.0 / GLA_GATE_NORM)
        gf_ref[rows, :] = gate[:, :GLA_QK]
        gbk_ref[rows, :] = gate[:, GLA_QK:]


def _premix1(stream, x2d, mod, norm_g, w_in_t, w_gate, gate_bias):
    t, tm = stream.t, stream.tm
    w_in = w_in_t
    assert w_in.shape == (1, GLA_MAIN + 2 * GLA_RANK, D_MODEL)
    return pl.pallas_call(
        _premix1_kernel,
        grid=(t // tm,),
        in_specs=[_row_spec(tm, D_MODEL), stream.mod_spec(), _const_spec((4, D_MODEL)),
                  _f32_weight_spec(w_in.shape), _const_spec((2 * GLA_RANK, 2 * GLA_QK)),
                  _const_spec((1, 2 * GLA_QK))],
        out_specs=[_row_spec(tm, GLA_QK), _row_spec(tm, GLA_QK), _row_spec(tm, GLA_VD),
                   _row_spec(tm, GLA_VD), _row_spec(tm, GLA_QK), _row_spec(tm, GLA_QK)],
        out_shape=[jax.ShapeDtypeStruct((t, GLA_QK), F32), jax.ShapeDtypeStruct((t, GLA_QK), F32),
                   jax.ShapeDtypeStruct((t, GLA_VD), BF16),
                   jax.ShapeDtypeStruct((t, GLA_VD), F32), jax.ShapeDtypeStruct((t, GLA_QK), F32),
                   jax.ShapeDtypeStruct((t, GLA_QK), F32)],
        scratch_shapes=[pltpu.VMEM((D_MODEL, GLA_MAIN), BF16),
                        pltpu.VMEM((2 * GLA_RANK, D_MODEL), BF16)],
        compiler_params=_cparams("arbitrary"),
        name="premix1",
    )(x2d, mod, norm_g, w_in, w_gate, gate_bias)


def _split3(x):
    hi = x.astype(BF16)
    r1 = x - hi.astype(F32)
    mid = r1.astype(BF16)
    lo = (r1 - mid.astype(F32)).astype(BF16)
    return hi, mid, lo


def _level_exponent(b, s, reverse):
    idx = s if reverse else s - 1
    if s >= 8:
        n = GLA_TILE // (2 * s)
        b4 = b.reshape(n, 2, s, GLA_DK)
        first, second = b4[:, 0:1], b4[:, 1:2]
        r = (second[:, :, 0:1] if reverse else first[:, :, s - 1:s])
        parts = [first - r, r - second] if reverse else [r - first, second - r]
        return jnp.concatenate(parts, axis=1).reshape(GLA_TILE, GLA_DK)
    b8 = b.reshape(GLA_TILE // 8, 8, GLA_DK)
    sub = lax.broadcasted_iota(jnp.int32, (1, 8, 1), 1)
    if s == 4:
        r = b8[:, idx:idx + 1, :]
    else:
        assert s == 2
        r = jnp.where(sub < 4, b8[:, idx:idx + 1, :], b8[:, 4 + idx:5 + idx, :])
    in_second = (sub // s) % 2 == 1
    sign = jnp.where(in_second != reverse, 1.0, -1.0)
    return ((b8 - r) * sign).reshape(GLA_TILE, GLA_DK)


class _PairMatrix:
    LEVELS = [(1 << (lv - 1), lv) for lv in range(2, GLA_TILE.bit_length())]

    def __init__(self, q, k, g_f, g_b, b_f, b_b, lvl):
        half = GLA_TILE // 2
        self.lo, self.hi = slice(0, half), slice(half, GLA_TILE)
        self.q_bf, self.k_bf = q.astype(BF16), k.astype(BF16)
        self.b_f, self.b_b, self.lvl = b_f, b_b, lvl
        self.row = lax.broadcasted_iota(jnp.int32, (GLA_TILE, 1), 0)
        odd = self.row % 2 == 1
        own = jnp.sum(q * k, axis=-1, keepdims=True)
        k_adj = jnp.where(odd, pltpu.roll(k, 1, axis=0), pltpu.roll(k, GLA_TILE - 1, axis=0))
        adj = jnp.sum(q * jnp.exp2(jnp.where(odd, g_f, g_b)) * k_adj, axis=-1, keepdims=True)
        self.blocks = [jnp.where(lvl == 0, 2.0 * own[rows], jnp.where(lvl == 1, adj[rows], 0.0))
                       for rows in (self.lo, self.hi)]
        self.cross_f = self.cross_b = None

    def operands(self, s):
        qb, kb, b_f, b_b = self.q_bf, self.k_bf, self.b_f, self.b_b
        if s >= 16:
            n = GLA_TILE // (2 * s)

            def halves(x):
                x4 = x.reshape(n, 2, s, GLA_DK)
                return x4[:, 0:1], x4[:, 1:2]

            def rows(first, second):
                return jnp.concatenate([first, second], axis=1).reshape(GLA_TILE, GLA_DK)

            (bf1, bf2), (bb1, bb2) = halves(b_f), halves(b_b)
            (q1, q2), (k1, k2) = halves(qb), halves(kb)
            r_f, r_b = bf1[:, :, s - 1:s], bb2[:, :, 0:1]
            zero = jnp.zeros_like(q1)
            lhs = jnp.concatenate(
                [rows(zero, q2 * jnp.exp2(bf2 - r_f).astype(BF16)),
                 rows(q1 * jnp.exp2(bb1 - r_b).astype(BF16), zero)], axis=1)
            rhs = jnp.concatenate(
                [rows(k1 * jnp.exp2(r_f - bf1).astype(BF16), zero),
                 rows(zero, k2 * jnp.exp2(r_b - bb2).astype(BF16))], axis=1)
        else:
            second = (self.row // s) % 2 == 1
            f_f = jnp.exp2(_level_exponent(b_f, s, False)).astype(BF16)
            f_b = jnp.exp2(_level_exponent(b_b, s, True)).astype(BF16)
            u = qb * jnp.where(second, f_f, f_b)
            w = kb * jnp.where(second, f_b, f_f)
            zero = jnp.zeros_like(u)
            lhs = jnp.concatenate([jnp.where(second, u, zero), jnp.where(second, zero, u)], axis=1)
            rhs = jnp.concatenate([jnp.where(second, zero, w), jnp.where(second, w, zero)], axis=1)
        return lhs, rhs

    def absorb(self, s, level, pairs):
        lo, hi = self.lo, self.hi
        if 2 * s == GLA_TILE:
            self.cross_f, self.cross_b = pairs[hi, lo], pairs[lo, hi]
        else:
            self.blocks = [jnp.where(self.lvl == level, pairs[lo, lo], self.blocks[0]),
                           jnp.where(self.lvl == level, pairs[hi, hi], self.blocks[1])]

    def matrix(self):
        top = jnp.concatenate([self.blocks[0].astype(BF16), self.cross_b.astype(BF16)], axis=1)
        bottom = jnp.concatenate([self.cross_f.astype(BF16), self.blocks[1].astype(BF16)], axis=1)
        return jnp.concatenate([top, bottom], axis=0)


def _pair_matrix(q, k, g_f, g_b, b_f, b_b, lvl):
    builder = _PairMatrix(q, k, g_f, g_b, b_f, b_b, lvl)
    for s, level in _PairMatrix.LEVELS:
        builder.absorb(s, level, _bdot_nt(*builder.operands(s)))
    return builder.matrix()


def _carry_state(q, k, b, v, s_ref, reverse):
    edge = 0 if reverse else GLA_TILE - 1
    b_last = b[edge:edge + 1, :]
    qe = (q * jnp.exp2(b)).astype(BF16)
    ke = (k * jnp.exp2(b_last - b)).astype(BF16)
    st = s_ref[...]
    s_ref[...] = st * jnp.exp2(b_last) + lax.dot_general(
        v, ke, TN_DIMS, preferred_element_type=F32)
    return _bdot_nt(qe, st)


def _gla_kernel(*refs, zero_init, nt):
    if zero_init:
        (tri_ref, lvl_ref, q_ref, k_ref, gf_ref, gb_ref, v_ref,
         o_ref, sfo_ref, sbo_ref, sf_ref, sb_ref, bb_ref) = refs
        sf_ref[...] = jnp.zeros_like(sf_ref)
        sb_ref[...] = jnp.zeros_like(sb_ref)
    else:
        (tri_ref, lvl_ref, q_ref, k_ref, gf_ref, gb_ref, v_ref, s0f_ref, s0b_ref,
         o_ref, sfo_ref, sbo_ref, sf_ref, sb_ref, bb_ref) = refs
        for hh in range(sf_ref.shape[0]):
            sf_ref[hh] = s0f_ref[hh].T
            sb_ref[hh] = s0b_ref[hh].T
    hp = sf_ref.shape[0]

    def tile_rows(tile):
        r0 = tile * GLA_TILE
        return pl.ds(r0 if isinstance(r0, int) else pl.multiple_of(r0, GLA_TILE), GLA_TILE)

    heads = [(slice(hh * GLA_DK, (hh + 1) * GLA_DK), slice(hh * GLA_DV, (hh + 1) * GLA_DV), hh)
             for hh in range(hp)]

    def forward_sweep(t, carry):
        rows = tile_rows(t)
        g_f, g_b = gf_ref[rows, :], gb_ref[rows, :]
        c = None
        for part in _split3(jnp.concatenate([g_f, g_b], axis=1)):
            term = jnp.dot(tri_ref[...], part, preferred_element_type=F32)
            c = term if c is None else c + term
        width = hp * GLA_DK
        c_b = c[:, width:]
        b_f = c[:, :width]
        b_b = (c_b[GLA_TILE - 1:GLA_TILE, :] - c_b) + g_b
        bb_ref[t] = b_b
        for dk, dv, hh in heads:
            q, k, v = q_ref[rows, dk], k_ref[rows, dk], v_ref[rows, dv]
            att = _pair_matrix(q, k, g_f[:, dk], g_b[:, dk], b_f[:, dk], b_b[:, dk], lvl_ref[...])
            o_ref[rows, dv] = (jnp.dot(att, v, preferred_element_type=F32)
                               + _carry_state(q, k, b_f[:, dk], v, sf_ref.at[hh], False))
        return carry

    def backward_sweep(i, carry):
        t = nt - 1 - i
        rows = tile_rows(t)
        b_b = bb_ref[t]
        for dk, dv, hh in heads:
            o_ref[rows, dv] += _carry_state(q_ref[rows, dk], k_ref[rows, dk], b_b[:, dk],
                                            v_ref[rows, dv], sb_ref.at[hh], True)
        return carry

    if nt == 1:
        forward_sweep(0, 0)
        backward_sweep(0, 0)
    else:
        lax.fori_loop(0, nt, forward_sweep, 0)
        lax.fori_loop(0, nt, backward_sweep, 0)
    for hh in range(hp):
        sfo_ref[hh] = sf_ref[hh].T
        sbo_ref[hh] = sb_ref[hh].T


def _gla_scan(grp, row0, q, k, v, gf, gb, s0f, s0b, tri, lvl):
    n = grp.n
    nt = n // GLA_TILE
    per_head = 2 * n * (4 * GLA_DK * 4 + GLA_DV * 2 + GLA_DV * 4)
    hp = GLA_HEADS
    while hp > 1 and hp * per_head > GLA_SEQ_VMEM_BUDGET:
        hp //= 2
    zero_init = s0f is None
    half = GLA_TILE // 2
    assert row0 % n == 0
    seq0 = row0 // n
    in_dk = pl.BlockSpec((n, hp * GLA_DK), lambda b, h: (seq0 + b, h))
    in_dv = pl.BlockSpec((n, hp * GLA_DV), lambda b, h: (seq0 + b, h))
    seq_dv = pl.BlockSpec((n, hp * GLA_DV), lambda b, h: (b, h))
    state_spec = pl.BlockSpec((None, hp, GLA_DK, GLA_DV), lambda b, h: (b, h, 0, 0))
    in_specs = [_const_spec((GLA_TILE, GLA_TILE)), _const_spec((half, half)),
                in_dk, in_dk, in_dk, in_dk, in_dv]
    args = [tri, lvl, q, k, gf, gb, v]
    if not zero_init:
        in_specs += [state_spec, state_spec]
        args += [s0f, s0b]
    state_shape = jax.ShapeDtypeStruct((grp.b, GLA_HEADS, GLA_DK, GLA_DV), F32)
    return pl.pallas_call(
        functools.partial(_gla_kernel, zero_init=zero_init, nt=nt),
        grid=(grp.b, GLA_HEADS // hp),
        in_specs=in_specs,
        out_specs=[seq_dv, state_spec, state_spec],
        out_shape=[jax.ShapeDtypeStruct((grp.t, GLA_VD), F32), state_shape, state_shape],
        scratch_shapes=[pltpu.VMEM((hp, GLA_DV, GLA_DK), F32), pltpu.VMEM((hp, GLA_DV, GLA_DK), F32),
                        pltpu.VMEM((nt, GLA_TILE, hp * GLA_DK), F32)],
        compiler_params=_cparams("arbitrary", "arbitrary"),
        name="gla_scan",
    )(*args)


def _gla_constants():
    half = GLA_TILE // 2
    i = jnp.arange(GLA_TILE)[:, None]
    j = jnp.arange(GLA_TILE)[None, :]
    tri = (j <= i).astype(BF16)
    ih, jh = i[:half], j[:, :half]
    x = jnp.bitwise_xor(ih, jh)
    lvl = sum((x >= (1 << p)).astype(jnp.int32) for p in range(half.bit_length() - 1))
    return tri, lvl


def _postmix1_kernel(oc_ref, ol_ref, og_ref, x_ref, mod_ref, g_ref, gn_ref, wo_hbm, w1_hbm, w2_hbm,
                     yc_ref, yl_ref, *scratch, stream):
    weights = _TailWeights(1, wo_hbm, w1_hbm, w2_hbm, *scratch)
    _first_step_streams_weights(
        functools.partial(_postmix1_body, oc_ref, ol_ref, og_ref, x_ref, mod_ref, g_ref, gn_ref,
                          yc_ref, yl_ref, weights, stream), weights)


def _postmix1_body(oc_ref, ol_ref, og_ref, x_ref, mod_ref, g_ref, gn_ref, yc_ref, yl_ref,
                   weights, stream, streaming):
    is_lat = stream.is_lat()
    gn = gn_ref[...]
    ys = []
    for h in range(GLA_HEADS):
        cols = slice(h * GLA_DV, (h + 1) * GLA_DV)
        o = _rms(jnp.where(is_lat, ol_ref[:, cols], oc_ref[:, cols]), gn)
        og = og_ref[:, cols]
        ys.append((o * (og / (1.0 + jnp.exp(-og)))).astype(BF16))
    if streaming:
        weights.fetch_out_proj()
    mix = None
    for h, y in enumerate(ys):
        part = jnp.dot(y, weights.wo_s[h * GLA_DV:(h + 1) * GLA_DV, :], preferred_element_type=F32)
        mix = part if mix is None else mix + part
    gt1 = _mod_chunk(mod_ref, 2)
    x1 = x_ref[...] + gt1 * _rms(mix, g_ref[1:2, :])
    y = _mlp_tail(x1, mod_ref, g_ref, weights, streaming)

    @pl.when(jnp.logical_not(is_lat))
    def _():
        yc_ref[...] = y

    @pl.when(is_lat)
    def _():
        yl_ref[...] = y


def _postmix1(stream, o_ctx, o_lat, og, x2d, mod, norm_g, gla_norm_g, w_out, w1, w2):
    hbm = pl.BlockSpec(memory_space=pl.ANY)
    return pl.pallas_call(
        functools.partial(_postmix1_kernel, stream=stream),
        grid=(stream.tiles,),
        in_specs=[stream.ctx_spec(GLA_VD), stream.lat_spec(GLA_VD), stream.spec(GLA_VD),
                  stream.spec(D_MODEL), stream.mod_spec(), _const_spec((4, D_MODEL)),
                  _const_spec((1, GLA_DV)), hbm, hbm, hbm],
        out_specs=[stream.ctx_spec(D_MODEL), stream.lat_spec(D_MODEL)],
        out_shape=[jax.ShapeDtypeStruct((stream.ctx.t, D_MODEL), F32),
                   jax.ShapeDtypeStruct((stream.lat.t, D_MODEL), F32)],
        scratch_shapes=_tail_weight_scratch(),
        compiler_params=_cparams("arbitrary"),
        name="postmix1_mlp",
    )(o_ctx, o_lat, og, x2d, mod, norm_g, gla_norm_g, w_out, w1, w2)


def kernel(x_prompt, x_sample, cache_k, cache_v, state_fwd, state_bwd, c, c_ctx, mod_w, mod_b,
           norm_g, ab_w_in, conv_w, attn_sink, ab_w_out, gla_w_in, gla_gate_w, gla_gate_b,
           gla_norm_g, gla_w_out, mlp_w1, mlp_w2):
    b_ctx, n_ctx, _ = x_prompt.shape
    b_lat, n_lat, _ = x_sample.shape
    assert mod_w.shape[0] == 2 and ab_w_in.shape[0] == 1 and gla_w_in.shape[0] == 1
    assert 1 + b_lat <= 8

    cond8 = jnp.zeros((8, D_MODEL), F32).at[0].set(c_ctx).at[1:1 + b_lat].set(c)
    mod = _modulation(cond8, mod_w, mod_b)
    mods = [mod[l].reshape(8, 1, -1) for l in range(2)]

    w_gate = jnp.zeros((2 * GLA_RANK, 2 * GLA_QK), F32)
    w_gate = w_gate.at[:GLA_RANK, :GLA_QK].set(gla_gate_w[0, 0])
    w_gate = w_gate.at[GLA_RANK:, GLA_QK:].set(gla_gate_w[0, 1])
    tri, lvl = _gla_constants()
    p = {
        "conv_w": conv_w[0],
        "sink": attn_sink[0],
        "gla_w_in": jnp.swapaxes(gla_w_in, 1, 2),
        "gla_w_gate": w_gate.astype(BF16),
        "gla_gate_bias": gla_gate_b[0].reshape(1, 2 * GLA_QK),
        "gla_norm_g": gla_norm_g[0].reshape(1, GLA_DV),
    }

    ctx, lat = _Group(b_ctx, n_ctx), _Group(b_lat, n_lat)
    stream = _Stream(ctx, lat, TOKEN_TILE)
    x_ctx, x_lat = x_prompt.reshape(ctx.t, D_MODEL), x_sample.reshape(lat.t, D_MODEL)
    past = cache_k.shape[2]
    k_ctx = cache_k[:, 0].reshape(b_lat, past, KD)
    v_ctx = cache_v[:, 0].reshape(b_lat, past, KD)

    wide = _Stream(ctx, lat, PREMIX_PARTS * TOKEN_TILE)
    zc, q, kv, k_t, v_t = _premix0(wide, x_ctx, x_lat, mods[0], norm_g[0], ab_w_in,
                                   _rope_tables(n_lat, wide.tm))
    attn_ctx = _ctx_attention(ctx, q, kv, p["sink"])
    attn_lat = _lat_attention(lat, ctx.t, q, kv, k_ctx, v_ctx, p["sink"])
    x1 = _postmix0(stream, zc, attn_ctx, attn_lat, x_ctx, x_lat, mods[0], norm_g[0], p["conv_w"],
                   ab_w_out, mlp_w1, mlp_w2)
    gq, gk, gv, og, gf, gb = _premix1(wide, x1, mods[1], norm_g[1], p["gla_w_in"],
                                      p["gla_w_gate"], p["gla_gate_bias"])
    o_ctx, sf, sb = _gla_scan(ctx, 0, gq, gk, gv, gf, gb, None, None, tri, lvl)
    o_lat, _, _ = _gla_scan(lat, ctx.t, gq, gk, gv, gf, gb, state_fwd[:, 0], state_bwd[:, 0],
                            tri, lvl)
    y_ctx, y_lat = _postmix1(stream, o_ctx, o_lat, og, x1, mods[1], norm_g[1], p["gla_norm_g"],
                             gla_w_out, mlp_w1, mlp_w2)

    def cache_layout(t):
        t = t.reshape(b_ctx, 1, N_KV_HEADS, HEAD_DIM, n_ctx)
        return jnp.transpose(t, (0, 1, 4, 2, 3))

    return (y_ctx.reshape(x_prompt.shape), y_lat.reshape(x_sample.shape),
            cache_layout(k_t), cache_layout(v_t), sf[:, None], sb[:, None])
```

```python
import functools

import jax
import jax.numpy as jnp
from jax import lax
from jax.experimental import pallas as pl
from jax.experimental.pallas import tpu as pltpu

F32 = jnp.float32
BF16 = jnp.bfloat16

D_MODEL = 1024
MOD_CHUNKS = 6
EPS = 1e-6
CONV_DIM = 512
N_Q_HEADS = 8
N_KV_HEADS = 2
GQA_GROUP = 4
HEAD_DIM = 64
WINDOW = 128
ATTN_BLOCK = 128
ATTN_BLOCKS_PER_STEP = 16
CTX_SEQS_PER_STEP = 4
GRID_W = 64
ROPE_BASE = 10000.0
QD = N_Q_HEADS * HEAD_DIM
KD = N_KV_HEADS * HEAD_DIM
AB_IN = 3 * CONV_DIM + QD + 2 * KD
GLA_HEADS = 4
GLA_DK = 128
GLA_DV = 256
GLA_RANK = 16
GLA_GATE_NORM = 16.0
GLA_TILE = 256
GLA_SEQ_VMEM_BUDGET = 32 * 1024 * 1024
LOG2E = 1.4426950408889634
GLA_QK = GLA_HEADS * GLA_DK
GLA_VD = GLA_HEADS * GLA_DV
GLA_MAIN = 2 * GLA_QK + 2 * GLA_VD
D_FF = 4 * D_MODEL
MLP_CHUNK = 512
MLP_CHUNKS = D_FF // MLP_CHUNK
WEIGHT_PIECE = 256
TOKEN_TILE = 512
PREMIX_PARTS = 2
NEG_INF = -1e30
LANES = 128
VMEM_LIMIT = 60 * 1024 * 1024

NT_DIMS = (((1,), (1,)), ((), ()))
TN_DIMS = (((0,), (0,)), ((), ()))


def _cparams(*sem):
    return pltpu.CompilerParams(dimension_semantics=sem, vmem_limit_bytes=VMEM_LIMIT)


def _bdot(a, b):
    return jnp.dot(a.astype(BF16), b.astype(BF16), preferred_element_type=F32)


def _bdot_nt(a, b):
    return lax.dot_general(a.astype(BF16), b.astype(BF16), NT_DIMS, preferred_element_type=F32)


def _rms(x, g):
    ms = jnp.mean(x * x, axis=-1, keepdims=True)
    return x * lax.rsqrt(ms + EPS) * g


def _mod_chunk(mod_ref, i):
    return mod_ref[:, i * D_MODEL:(i + 1) * D_MODEL]


def _const_spec(shape):
    return pl.BlockSpec(shape, lambda *_: (0,) * len(shape))


def _f32_weight_spec(shape):
    assert shape[0] == 1
    return pl.BlockSpec((None,) + tuple(shape[1:]), lambda *_: (0, 0, 0),
                        pipeline_mode=pl.Buffered(1))


def _cast_once(w_ref, wb_ref):
    @pl.when(pl.program_id(0) == 0)
    def _():
        wb_ref[...] = w_ref[...].astype(BF16)


def _mod_kernel(cond_ref, w_ref, b_ref, o_ref):
    cnd = cond_ref[...]
    s = cnd / (1.0 + jnp.exp(-cnd))
    o_ref[...] = _bdot(s, w_ref[...]) + b_ref[...]


def _modulation(cond8, mod_w, mod_b):
    depth = mod_w.shape[0]
    n = mod_w.shape[2]
    tn = 1536
    return pl.pallas_call(
        _mod_kernel,
        grid=(depth, n // tn),
        in_specs=[
            pl.BlockSpec((8, D_MODEL), lambda l, j: (0, 0)),
            pl.BlockSpec((None, D_MODEL, tn), lambda l, j: (l, 0, j)),
            pl.BlockSpec((None, 1, tn), lambda l, j: (l, 0, j)),
        ],
        out_specs=pl.BlockSpec((None, 8, tn), lambda l, j: (l, 0, j)),
        out_shape=jax.ShapeDtypeStruct((depth, 8, n), F32),
        compiler_params=_cparams("arbitrary", "arbitrary"),
        name="modulation",
    )(cond8, mod_w, mod_b.reshape(depth, 1, n))


class _Group:
    def __init__(self, b, n):
        self.b, self.n, self.t = b, n, b * n


class _Stream:
    def __init__(self, ctx, lat, tm):
        assert ctx.t % tm == 0 and lat.n % tm == 0 and tm % ctx.n == 0
        self.ctx, self.lat, self.tm = ctx, lat, tm
        self.t = ctx.t + lat.t
        self.ctx_tiles = ctx.t // tm
        self.tiles = self.t // tm

    def is_lat(self):
        return pl.program_id(0) >= self.ctx_tiles

    def spec(self, width):
        return pl.BlockSpec((self.tm, width), lambda t: (t, 0))

    def ctx_spec(self, width):
        last = self.ctx_tiles - 1
        return pl.BlockSpec((self.tm, width), lambda t: (jnp.minimum(t, last), 0))

    def lat_spec(self, width):
        first = self.ctx_tiles
        return pl.BlockSpec((self.tm, width), lambda t: (jnp.maximum(t - first, 0), 0))

    def mod_spec(self):
        first, per = self.ctx_tiles, self.lat.n // self.tm
        return pl.BlockSpec((None, 1, MOD_CHUNKS * D_MODEL),
                            lambda t: (jnp.where(t < first, 0, 1 + (t - first) // per), 0, 0))

    def seq_len(self):
        return jnp.where(self.is_lat(), self.lat.n, self.ctx.n)


def _row_spec(tm, width):
    return pl.BlockSpec((tm, width), lambda t: (t, 0))


def _rope(x, cos, sin_lo, sin_hi):
    return (x * cos + pltpu.roll(x, LANES - 16, axis=1) * sin_lo
            + pltpu.roll(x, 16, axis=1) * sin_hi)


def _premix0_kernel(xc_ref, xl_ref, mod_ref, g_ref, w_ref, cos_ref, slo_ref, shi_ref,
                    zc_ref, q_ref, kv_ref, kt_ref, vt_ref, wb_ref, *, stream):
    _cast_once(w_ref, wb_ref)
    is_lat = stream.is_lat()
    sh1, sc1 = _mod_chunk(mod_ref, 0), _mod_chunk(mod_ref, 1)
    c3 = 3 * CONV_DIM
    scale = HEAD_DIM ** -0.5 * LOG2E
    n = kt_ref.shape[2]
    part = stream.tm // PREMIX_PARTS
    raw_kv = []
    for r0 in range(0, stream.tm, part):
        rows = slice(r0, r0 + part)
        x = jnp.where(is_lat, xl_ref[rows, :], xc_ref[rows, :])
        h = _rms(x, g_ref[0:1, :]) * (1.0 + sc1) + sh1
        z = _bdot(h, wb_ref[...])
        zc_ref[rows, :] = z[:, :c3]
        cos, slo, shi = cos_ref[rows, :], slo_ref[rows, :], shi_ref[rows, :]
        for j in range(QD // LANES):
            qs = z[:, c3 + j * LANES:c3 + (j + 1) * LANES]
            q_ref[rows, j * LANES:(j + 1) * LANES] = (_rope(qs, cos, slo, shi) * scale).astype(BF16)
        kv_ref[rows, :KD] = _rope(z[:, c3 + QD:c3 + QD + KD], cos, slo, shi)
        kv_ref[rows, KD:] = z[:, c3 + QD + KD:]
        raw_kv.append(z[:, c3 + QD:])

    @pl.when(jnp.logical_not(is_lat))
    def _():
        for p, kv in enumerate(raw_kv):
            for j in range(part // n):
                seq = p * (part // n) + j
                kt_ref[seq] = kv[j * n:(j + 1) * n, :KD].T
                vt_ref[seq] = kv[j * n:(j + 1) * n, KD:].T


def _premix0(stream, x_ctx, x_lat, mod, norm_g, w_in, rope_tabs):
    tm, ctx, lat = stream.tm, stream.ctx, stream.lat
    first, per, last = stream.ctx_tiles, lat.n // tm, stream.ctx_tiles - 1
    rope_spec = pl.BlockSpec((tm, LANES),
                             lambda t: (jnp.where(t < first, 0, 1 + (t - first) % per), 0))
    cache_spec = pl.BlockSpec((tm // ctx.n, KD, ctx.n), lambda t: (jnp.minimum(t, last), 0, 0))
    return pl.pallas_call(
        functools.partial(_premix0_kernel, stream=stream),
        grid=(stream.tiles,),
        in_specs=[stream.ctx_spec(D_MODEL), stream.lat_spec(D_MODEL), stream.mod_spec(),
                  _const_spec((4, D_MODEL)), _f32_weight_spec(w_in.shape)] + [rope_spec] * 3,
        out_specs=[stream.spec(3 * CONV_DIM), stream.spec(QD), stream.spec(2 * KD),
                   cache_spec, cache_spec],
        out_shape=[jax.ShapeDtypeStruct((stream.t, 3 * CONV_DIM), F32),
                   jax.ShapeDtypeStruct((stream.t, QD), BF16),
                   jax.ShapeDtypeStruct((stream.t, 2 * KD), F32),
                   jax.ShapeDtypeStruct((ctx.b, KD, ctx.n), F32),
                   jax.ShapeDtypeStruct((ctx.b, KD, ctx.n), F32)],
        scratch_shapes=[pltpu.VMEM(w_in.shape[1:], BF16)],
        compiler_params=_cparams("arbitrary"),
        name="premix0",
    )(x_ctx, x_lat, mod, norm_g, w_in, *rope_tabs)


def _rope_tables(n, identity_rows):
    rows = n // GRID_W
    pos_r = jnp.repeat(jnp.arange(rows), GRID_W)
    pos_c = jnp.tile(jnp.arange(GRID_W), rows)
    half = HEAD_DIM // 2
    quarter = half // 2
    inv = ROPE_BASE ** (-(jnp.arange(quarter, dtype=F32) * 2.0 / half))

    def cs(pos):
        ang = pos.astype(F32)[:, None] * inv[None, :]
        return jnp.cos(ang), jnp.sin(ang)

    cr, sr = cs(pos_r)
    cc, sc = cs(pos_c)
    zero = jnp.zeros_like(sr)
    cos = jnp.concatenate([cr, cr, cc, cc], axis=1)
    sin_lo = jnp.concatenate([-sr, zero, -sc, zero], axis=1)
    sin_hi = jnp.concatenate([zero, sr, zero, sc], axis=1)
    rep = LANES // HEAD_DIM
    tables = []
    for t, ident in ((cos, 1.0), (sin_lo, 0.0), (sin_hi, 0.0)):
        head = jnp.full((identity_rows, LANES), ident, F32)
        tables.append(jnp.concatenate([head, jnp.tile(t, (1, rep))], axis=0))
    return tuple(tables)


def _attention_operands(k_all, v_all):
    assert KD == LANES == 2 * HEAD_DIM and GQA_GROUP == 4
    lane = lax.broadcasted_iota(jnp.int32, (1, LANES), 1)
    sub = lax.broadcasted_iota(jnp.int32, (LANES, 1), 0)
    v_t = v_all.T
    k_swapped = pltpu.roll(k_all, HEAD_DIM, axis=1)
    ops = []
    for g in range(N_KV_HEADS):
        k_low, k_high = (k_all, k_swapped) if g == 0 else (k_swapped, k_all)
        kz_even = jnp.where(lane < HEAD_DIM, k_low, 0.0).astype(BF16)
        kz_odd = jnp.where(lane >= HEAD_DIM, k_high, 0.0).astype(BF16)
        own = (sub < HEAD_DIM) if g == 0 else (sub >= HEAD_DIM)
        v_ext_t = jnp.where(own, v_t, 1.0).astype(BF16)
        ops.append((kz_even, kz_odd, v_ext_t))
    return ops


def _concat_operands(parts):
    return [(jnp.concatenate([p[g][0] for p in parts], axis=0),
             jnp.concatenate([p[g][1] for p in parts], axis=0),
             jnp.concatenate([p[g][2] for p in parts], axis=1)) for g in range(N_KV_HEADS)]


def _group_heads(g):
    return [4 * g, 4 * g + 2, 4 * g + 1, 4 * g + 3]


def _scores(sink_ref, q_ref, operands, bias_t, g):
    m = q_ref.shape[0]
    kz_even, kz_odd, _ = operands[g]
    qq = jnp.concatenate([q_ref[:, (2 * g) * LANES:(2 * g + 1) * LANES],
                          q_ref[:, (2 * g + 1) * LANES:(2 * g + 2) * LANES]], axis=0)
    s = jnp.concatenate([_bdot_nt(kz_even, qq), _bdot_nt(kz_odd, qq)], axis=1)
    if bias_t is not None:
        nb = bias_t.shape[0]
        s = jnp.concatenate([s[:nb] + jnp.concatenate([bias_t] * GQA_GROUP, axis=1), s[nb:]],
                            axis=0)
    sink = jnp.concatenate([jnp.full((1, m), sink_ref[h] * LOG2E, F32) for h in _group_heads(g)],
                           axis=1)
    mx = jnp.maximum(jnp.max(s, axis=0, keepdims=True), sink)
    return s, sink, mx


def _weighted_values(s, sink, mx, operands, g, m):
    v_ext_t = operands[g][2]
    p = jnp.exp2(s - mx).astype(BF16)
    oe = jnp.dot(v_ext_t, p, preferred_element_type=F32)
    other = (1 - g) * HEAD_DIM
    den = oe[other:other + 1] + jnp.exp2(sink - mx)
    o_g = oe[g * HEAD_DIM:(g + 1) * HEAD_DIM] / den
    return {h: o_g[:, i * m:(i + 1) * m] for i, h in enumerate(_group_heads(g))}


def _sink_attention(sink_ref, blocks):
    chains = [(blk, g) for blk in range(len(blocks)) for g in range(N_KV_HEADS)]
    outs = [dict() for _ in blocks]

    def scores(chain):
        blk, g = chain
        q_ref, operands, bias_t, _ = blocks[blk]
        return _scores(sink_ref, q_ref, operands, bias_t, g)

    ahead = scores(chains[0])
    for i, (blk, g) in enumerate(chains):
        current = ahead
        if i + 1 < len(chains):
            ahead = scores(chains[i + 1])
        q_ref, operands, _, o_ref = blocks[blk]
        outs[blk].update(_weighted_values(*current, operands, g, q_ref.shape[0]))
        if g == N_KV_HEADS - 1:
            for j in range(N_Q_HEADS // 2):
                pair_t = jnp.concatenate([outs[blk][2 * j], outs[blk][2 * j + 1]], axis=0)
                o_ref[:, j * LANES:(j + 1) * LANES] = pair_t.T.astype(BF16)


def _ctx_attn_kernel(sink_ref, q_ref, kv_ref, o_ref, *, n):
    blocks = []
    for j in range(q_ref.shape[0] // n):
        rows = pl.ds(j * n, n)
        operands = _attention_operands(kv_ref[rows, :KD], kv_ref[rows, KD:])
        blocks.append((q_ref.at[rows], operands, None, o_ref.at[rows]))
    _sink_attention(sink_ref, blocks)


def _ctx_attention(grp, q, kv, sink):
    per = CTX_SEQS_PER_STEP
    assert grp.b % per == 0
    n = per * grp.n
    return pl.pallas_call(
        functools.partial(_ctx_attn_kernel, n=grp.n),
        grid=(grp.b // per,),
        in_specs=[pl.BlockSpec(memory_space=pltpu.SMEM), _row_spec(n, QD), _row_spec(n, 2 * KD)],
        out_specs=_row_spec(n, QD),
        out_shape=jax.ShapeDtypeStruct((grp.t, QD), BF16),
        compiler_params=_cparams("arbitrary"),
        name="ctx_attention",
    )(sink, q, kv)


def _lat_attn_kernel(sink_ref, q_ref, kvp_ref, kvc_ref, kvn_ref, kc_ref, vc_ref, o_ref, *, n):
    band = jnp.concatenate([kvp_ref[...], kvc_ref[...], kvn_ref[...]], axis=0)
    kj = lax.broadcasted_iota(jnp.int32, (3 * ATTN_BLOCK, ATTN_BLOCK), 0)
    qi = lax.broadcasted_iota(jnp.int32, (3 * ATTN_BLOCK, ATTN_BLOCK), 1)
    rel = kj - ATTN_BLOCK - qi
    ctx_ops = _attention_operands(kc_ref[...], vc_ref[...])
    band_ops = [_attention_operands(band[i * ATTN_BLOCK:(i + 1) * ATTN_BLOCK, :KD],
                                    band[i * ATTN_BLOCK:(i + 1) * ATTN_BLOCK, KD:])
                for i in range(ATTN_BLOCKS_PER_STEP + 2)]
    blocks = []
    for j in range(ATTN_BLOCKS_PER_STEP):
        blk = pl.program_id(1) * ATTN_BLOCKS_PER_STEP + j
        kpos = (blk - 1) * ATTN_BLOCK + kj
        valid = (jnp.abs(rel) <= WINDOW) & (kpos >= 0) & (kpos < n)
        bias_t = jnp.where(valid, 0.0, NEG_INF)
        operands = _concat_operands(band_ops[j:j + 3] + [ctx_ops])
        rows = pl.ds(j * ATTN_BLOCK, ATTN_BLOCK)
        blocks.append((q_ref.at[rows], operands, bias_t, o_ref.at[rows]))
    _sink_attention(sink_ref, blocks)


def _lat_attention(grp, row0, q, kv, k_ctx, v_ctx, sink):
    nb = grp.n // ATTN_BLOCK
    per = ATTN_BLOCKS_PER_STEP
    steps = nb // per
    past = k_ctx.shape[1]
    assert row0 % (per * ATTN_BLOCK) == 0
    blk0, step0 = row0 // ATTN_BLOCK, row0 // (per * ATTN_BLOCK)

    def edge(off):
        return pl.BlockSpec((ATTN_BLOCK, 2 * KD),
                            lambda b, i: (blk0 + b * nb + jnp.clip(i * per + off, 0, nb - 1), 0))

    ctx_spec = pl.BlockSpec((None, past, KD), lambda b, i: (b, 0, 0))
    return pl.pallas_call(
        functools.partial(_lat_attn_kernel, n=grp.n),
        grid=(grp.b, steps),
        in_specs=[pl.BlockSpec(memory_space=pltpu.SMEM),
                  pl.BlockSpec((per * ATTN_BLOCK, QD), lambda b, i: (step0 + b * steps + i, 0)),
                  edge(-1),
                  pl.BlockSpec((per * ATTN_BLOCK, 2 * KD), lambda b, i: (step0 + b * steps + i, 0)),
                  edge(per), ctx_spec, ctx_spec],
        out_specs=pl.BlockSpec((per * ATTN_BLOCK, QD), lambda b, i: (b * steps + i, 0)),
        out_shape=jax.ShapeDtypeStruct((grp.t, QD), BF16),
        compiler_params=_cparams("arbitrary", "arbitrary"),
        name="lat_attention",
    )(sink, q, kv, kv, kv, k_ctx, v_ctx)


class _TailWeights:
    def __init__(self, layer, wo_hbm, w1_hbm, w2_hbm, wo_s, w1_s, w2_s, stage1, stage2, sem):
        self.layer = layer
        self.wo_hbm, self.w1_hbm, self.w2_hbm = wo_hbm, w1_hbm, w2_hbm
        self.wo_s, self.w1_s, self.w2_s = wo_s, w1_s, w2_s
        self.stage1, self.stage2, self.sem = stage1, stage2, sem

    def _w1_copy(self, p):
        src = self.w1_hbm.at[self.layer, :, pl.ds(p * WEIGHT_PIECE, WEIGHT_PIECE)]
        return pltpu.make_async_copy(src, self.stage1.at[p % 2], self.sem.at[0, p % 2])

    def _w2_copy(self, p):
        src = self.w2_hbm.at[self.layer, pl.ds(p * WEIGHT_PIECE, WEIGHT_PIECE), :]
        return pltpu.make_async_copy(src, self.stage2.at[p % 2], self.sem.at[1, p % 2])

    def _wo_copy(self, p):
        src = self.wo_hbm.at[0, pl.ds(p * WEIGHT_PIECE, WEIGHT_PIECE), :]
        return pltpu.make_async_copy(src, self.stage2.at[p % 2], self.sem.at[1, p % 2])

    def start(self):
        self._wo_copy(0).start()
        self._wo_copy(1).start()
        self._w1_copy(0).start(priority=1)
        self._w1_copy(1).start(priority=1)

    def fetch_out_proj(self):
        pieces = D_MODEL // WEIGHT_PIECE
        for p in range(pieces):
            self._wo_copy(p).wait()
            self.wo_s[p * WEIGHT_PIECE:(p + 1) * WEIGHT_PIECE, :] = self.stage2[p % 2].astype(BF16)
            if p + 2 < pieces:
                self._wo_copy(p + 2).start()
            else:
                self._w2_copy(p + 2 - pieces).start()

    def fetch_mlp_chunk(self, j):
        per = MLP_CHUNK // WEIGHT_PIECE
        total = MLP_CHUNKS * per
        for p in range(j * per, (j + 1) * per):
            part = slice((p % per) * WEIGHT_PIECE, (p % per + 1) * WEIGHT_PIECE)
            self._w1_copy(p).wait()
            self.w1_s[j, :, part] = self.stage1[p % 2].astype(BF16)
            if p + 2 < total:
                self._w1_copy(p + 2).start(priority=1)
            self._w2_copy(p).wait()
            self.w2_s[j, part, :] = self.stage2[p % 2].astype(BF16)
            if p + 2 < total:
                self._w2_copy(p + 2).start()


def _tail_weight_scratch():
    assert MLP_CHUNK % WEIGHT_PIECE == 0 and D_MODEL // WEIGHT_PIECE >= 2
    return [pltpu.VMEM((D_MODEL, D_MODEL), BF16),
            pltpu.VMEM((MLP_CHUNKS, D_MODEL, MLP_CHUNK), BF16),
            pltpu.VMEM((MLP_CHUNKS, MLP_CHUNK, D_MODEL), BF16),
            pltpu.VMEM((2, D_MODEL, WEIGHT_PIECE), F32),
            pltpu.VMEM((2, WEIGHT_PIECE, D_MODEL), F32),
            pltpu.SemaphoreType.DMA((2, 2))]


def _first_step_streams_weights(body, weights):
    first = pl.program_id(0) == 0

    @pl.when(first)
    def _():
        weights.start()
        body(True)

    @pl.when(jnp.logical_not(first))
    def _():
        body(False)


def _mlp_tail(x, mod_ref, g_ref, weights, streaming):
    sh2, sc2, gt2 = _mod_chunk(mod_ref, 3), _mod_chunk(mod_ref, 4), _mod_chunk(mod_ref, 5)
    hb = (_rms(x, g_ref[2:3, :]) * (1.0 + sc2) + sh2).astype(BF16)
    acc = None
    for j in range(MLP_CHUNKS):
        if streaming:
            weights.fetch_mlp_chunk(j)
        a = jnp.dot(hb, weights.w1_s[j], preferred_element_type=F32)
        a = jnp.maximum(a, 0.0)
        part = jnp.dot((a * a).astype(BF16), weights.w2_s[j], preferred_element_type=F32)
        acc = part if acc is None else acc + part
    return x + gt2 * _rms(acc, g_ref[3:4, :])


def _postmix0_kernel(zc_ref, zp_ref, zn_ref, atc_ref, atl_ref, xc_ref, xl_ref, mod_ref, g_ref,
                     cw_ref, wo_hbm, w1_hbm, w2_hbm, o_ref, *scratch, stream):
    weights = _TailWeights(0, wo_hbm, w1_hbm, w2_hbm, *scratch)
    _first_step_streams_weights(
        functools.partial(_postmix0_body, zc_ref, zp_ref, zn_ref, atc_ref, atl_ref, xc_ref, xl_ref,
                          mod_ref, g_ref, cw_ref, o_ref, weights, stream), weights)


def _postmix0_body(zc_ref, zp_ref, zn_ref, atc_ref, atl_ref, xc_ref, xl_ref, mod_ref, g_ref,
                   cw_ref, o_ref, weights, stream, streaming):
    c, tm = CONV_DIM, stream.tm
    is_lat, n = stream.is_lat(), stream.seq_len()
    zc = zc_ref[...]
    u = zc[:, c:2 * c] * zc[:, 2 * c:]
    u_before = zp_ref[7:8, c:2 * c] * zp_ref[7:8, 2 * c:]
    u_after = zn_ref[0:1, c:2 * c] * zn_ref[0:1, 2 * c:]
    row = lax.broadcasted_iota(jnp.int32, (tm, 1), 0)
    pos = (pl.program_id(0) * tm + row) & (n - 1)
    u_prev = jnp.where(row == 0, u_before, pltpu.roll(u, 1, axis=0))
    u_prev = jnp.where(pos == 0, 0.0, u_prev)
    u_next = jnp.where(row == tm - 1, u_after, pltpu.roll(u, tm - 1, axis=0))
    u_next = jnp.where(pos == n - 1, 0.0, u_next)
    conv = u_prev * cw_ref[0:1, :] + u * cw_ref[1:2, :] + u_next * cw_ref[2:3, :]
    if streaming:
        weights.fetch_out_proj()
    attn = jnp.where(is_lat, atl_ref[...], atc_ref[...])
    mix = (_bdot(zc[:, :c] * conv, weights.wo_s[:c, :])
           + jnp.dot(attn, weights.wo_s[c:, :], preferred_element_type=F32))
    gt1 = _mod_chunk(mod_ref, 2)
    x1 = jnp.where(is_lat, xl_ref[...], xc_ref[...]) + gt1 * _rms(mix, g_ref[1:2, :])
    o_ref[...] = _mlp_tail(x1, mod_ref, g_ref, weights, streaming)


def _postmix0(stream, zc, attn_ctx, attn_lat, x_ctx, x_lat, mod, norm_g, conv_w, w_out, w1, w2):
    tm = stream.tm
    for n in (stream.ctx.n, stream.lat.n):
        assert n & (n - 1) == 0 and stream.ctx.t % n == 0
    r8 = tm // 8
    last8 = stream.t // 8 - 1
    hbm = pl.BlockSpec(memory_space=pl.ANY)
    return pl.pallas_call(
        functools.partial(_postmix0_kernel, stream=stream),
        grid=(stream.tiles,),
        in_specs=[stream.spec(3 * CONV_DIM),
                  pl.BlockSpec((8, 3 * CONV_DIM), lambda t: (jnp.maximum(t * r8 - 1, 0), 0)),
                  pl.BlockSpec((8, 3 * CONV_DIM), lambda t: (jnp.minimum((t + 1) * r8, last8), 0)),
                  stream.ctx_spec(QD), stream.lat_spec(QD),
                  stream.ctx_spec(D_MODEL), stream.lat_spec(D_MODEL), stream.mod_spec(),
                  _const_spec((4, D_MODEL)), _const_spec((3, CONV_DIM)), hbm, hbm, hbm],
        out_specs=stream.spec(D_MODEL),
        out_shape=jax.ShapeDtypeStruct((stream.t, D_MODEL), F32),
        scratch_shapes=_tail_weight_scratch(),
        compiler_params=_cparams("arbitrary"),
        name="postmix0_mlp",
    )(zc, zc, zc, attn_ctx, attn_lat, x_ctx, x_lat, mod, norm_g, conv_w, w_out, w1, w2)


def _premix1_kernel(x_ref, mod_ref, g_ref, w_ref, wg_ref, gb_ref,
                    q_ref, k_ref, v_ref, og_ref, gf_ref, gbk_ref, wb_ref, wr_ref):
    @pl.when(pl.program_id(0) == 0)
    def _():
        for c in range(0, GLA_MAIN, GLA_QK):
            wb_ref[:, c:c + GLA_QK] = w_ref[c:c + GLA_QK, :].T.astype(BF16)
        wr_ref[...] = w_ref[GLA_MAIN:, :].astype(BF16)

    sh1, sc1 = _mod_chunk(mod_ref, 0), _mod_chunk(mod_ref, 1)
    part = x_ref.shape[0] // PREMIX_PARTS
    for r0 in range(0, x_ref.shape[0], part):
        rows = slice(r0, r0 + part)
        hb = (_rms(x_ref[rows, :], g_ref[0:1, :]) * (1.0 + sc1) + sh1).astype(BF16)
        z = jnp.dot(hb, wb_ref[...], preferred_element_type=F32)
        q_ref[rows, :] = z[:, :GLA_QK] * (GLA_DK ** -0.5)
        k_ref[rows, :] = z[:, GLA_QK:2 * GLA_QK]
        v_ref[rows, :] = z[:, 2 * GLA_QK:2 * GLA_QK + GLA_VD].astype(BF16)
        og_ref[rows, :] = z[:, 2 * GLA_QK + GLA_VD:]
        r = _bdot_nt(hb, wr_ref[...])
        pre = _bdot(r, wg_ref[...]) + gb_ref[...]
        soft = jnp.log2(1.0 + jnp.exp2(jnp.abs(pre) * (-LOG2E)))
        gate = (jnp.minimum(pre, 0.0) * LOG2E - soft) * (1.0 / GLA_GATE_NORM)
        gf_ref[rows, :] = gate[:, :GLA_QK]
        gbk_ref[rows, :] = gate[:, GLA_QK:]


def _premix1(stream, x2d, mod, norm_g, w_in_t, w_gate, gate_bias):
    t, tm = stream.t, stream.tm
    w_in = w_in_t
    assert w_in.shape == (1, GLA_MAIN + 2 * GLA_RANK, D_MODEL)
    return pl.pallas_call(
        _premix1_kernel,
        grid=(t // tm,),
        in_specs=[_row_spec(tm, D_MODEL), stream.mod_spec(), _const_spec((4, D_MODEL)),
                  _f32_weight_spec(w_in.shape), _const_spec((2 * GLA_RANK, 2 * GLA_QK)),
                  _const_spec((1, 2 * GLA_QK))],
        out_specs=[_row_spec(tm, GLA_QK), _row_spec(tm, GLA_QK), _row_spec(tm, GLA_VD),
                   _row_spec(tm, GLA_VD), _row_spec(tm, GLA_QK), _row_spec(tm, GLA_QK)],
        out_shape=[jax.ShapeDtypeStruct((t, GLA_QK), F32), jax.ShapeDtypeStruct((t, GLA_QK), F32),
                   jax.ShapeDtypeStruct((t, GLA_VD), BF16),
                   jax.ShapeDtypeStruct((t, GLA_VD), F32), jax.ShapeDtypeStruct((t, GLA_QK), F32),
                   jax.ShapeDtypeStruct((t, GLA_QK), F32)],
        scratch_shapes=[pltpu.VMEM((D_MODEL, GLA_MAIN), BF16),
                        pltpu.VMEM((2 * GLA_RANK, D_MODEL), BF16)],
        compiler_params=_cparams("arbitrary"),
        name="premix1",
    )(x2d, mod, norm_g, w_in, w_gate, gate_bias)


def _split3(x):
    hi = x.astype(BF16)
    r1 = x - hi.astype(F32)
    mid = r1.astype(BF16)
    lo = (r1 - mid.astype(F32)).astype(BF16)
    return hi, mid, lo


def _level_exponent(b, s, reverse):
    idx = s if reverse else s - 1
    if s >= 8:
        n = GLA_TILE // (2 * s)
        b4 = b.reshape(n, 2, s, GLA_DK)
        first, second = b4[:, 0:1], b4[:, 1:2]
        r = (second[:, :, 0:1] if reverse else first[:, :, s - 1:s])
        parts = [first - r, r - second] if reverse else [r - first, second - r]
        return jnp.concatenate(parts, axis=1).reshape(GLA_TILE, GLA_DK)
    b8 = b.reshape(GLA_TILE // 8, 8, GLA_DK)
    sub = lax.broadcasted_iota(jnp.int32, (1, 8, 1), 1)
    if s == 4:
        r = b8[:, idx:idx + 1, :]
    else:
        assert s == 2
        r = jnp.where(sub < 4, b8[:, idx:idx + 1, :], b8[:, 4 + idx:5 + idx, :])
    in_second = (sub // s) % 2 == 1
    sign = jnp.where(in_second != reverse, 1.0, -1.0)
    return ((b8 - r) * sign).reshape(GLA_TILE, GLA_DK)


class _PairMatrix:
    LEVELS = [(1 << (lv - 1), lv) for lv in range(2, GLA_TILE.bit_length())]

    def __init__(self, q, k, g_f, g_b, b_f, b_b, lvl):
        half = GLA_TILE // 2
        self.lo, self.hi = slice(0, half), slice(half, GLA_TILE)
        self.q_bf, self.k_bf = q.astype(BF16), k.astype(BF16)
        self.b_f, self.b_b, self.lvl = b_f, b_b, lvl
        self.row = lax.broadcasted_iota(jnp.int32, (GLA_TILE, 1), 0)
        odd = self.row % 2 == 1
        own = jnp.sum(q * k, axis=-1, keepdims=True)
        k_adj = jnp.where(odd, pltpu.roll(k, 1, axis=0), pltpu.roll(k, GLA_TILE - 1, axis=0))
        adj = jnp.sum(q * jnp.exp2(jnp.where(odd, g_f, g_b)) * k_adj, axis=-1, keepdims=True)
        self.blocks = [jnp.where(lvl == 0, 2.0 * own[rows], jnp.where(lvl == 1, adj[rows], 0.0))
                       for rows in (self.lo, self.hi)]
        self.cross_f = self.cross_b = None

    def operands(self, s):
        qb, kb, b_f, b_b = self.q_bf, self.k_bf, self.b_f, self.b_b
        if s >= 16:
            n = GLA_TILE // (2 * s)

            def halves(x):
                x4 = x.reshape(n, 2, s, GLA_DK)
                return x4[:, 0:1], x4[:, 1:2]

            def rows(first, second):
                return jnp.concatenate([first, second], axis=1).reshape(GLA_TILE, GLA_DK)

            (bf1, bf2), (bb1, bb2) = halves(b_f), halves(b_b)
            (q1, q2), (k1, k2) = halves(qb), halves(kb)
            r_f, r_b = bf1[:, :, s - 1:s], bb2[:, :, 0:1]
            zero = jnp.zeros_like(q1)
            lhs = jnp.concatenate(
                [rows(zero, q2 * jnp.exp2(bf2 - r_f).astype(BF16)),
                 rows(q1 * jnp.exp2(bb1 - r_b).astype(BF16), zero)], axis=1)
            rhs = jnp.concatenate(
                [rows(k1 * jnp.exp2(r_f - bf1).astype(BF16), zero),
                 rows(zero, k2 * jnp.exp2(r_b - bb2).astype(BF16))], axis=1)
        else:
            second = (self.row // s) % 2 == 1
            f_f = jnp.exp2(_level_exponent(b_f, s, False)).astype(BF16)
            f_b = jnp.exp2(_level_exponent(b_b, s, True)).astype(BF16)
            u = qb * jnp.where(second, f_f, f_b)
            w = kb * jnp.where(second, f_b, f_f)
            zero = jnp.zeros_like(u)
            lhs = jnp.concatenate([jnp.where(second, u, zero), jnp.where(second, zero, u)], axis=1)
            rhs = jnp.concatenate([jnp.where(second, zero, w), jnp.where(second, w, zero)], axis=1)
        return lhs, rhs

    def absorb(self, s, level, pairs):
        lo, hi = self.lo, self.hi
        if 2 * s == GLA_TILE:
            self.cross_f, self.cross_b = pairs[hi, lo], pairs[lo, hi]
        else:
            self.blocks = [jnp.where(self.lvl == level, pairs[lo, lo], self.blocks[0]),
                           jnp.where(self.lvl == level, pairs[hi, hi], self.blocks[1])]

    def matrix(self):
        top = jnp.concatenate([self.blocks[0].astype(BF16), self.cross_b.astype(BF16)], axis=1)
        bottom = jnp.concatenate([self.cross_f.astype(BF16), self.blocks[1].astype(BF16)], axis=1)
        return jnp.concatenate([top, bottom], axis=0)


def _pair_matrix(q, k, g_f, g_b, b_f, b_b, lvl):
    builder = _PairMatrix(q, k, g_f, g_b, b_f, b_b, lvl)
    for s, level in _PairMatrix.LEVELS:
        builder.absorb(s, level, _bdot_nt(*builder.operands(s)))
    return builder.matrix()


def _carry_state(q, k, b, v, s_ref, reverse):
    edge = 0 if reverse else GLA_TILE - 1
    b_last = b[edge:edge + 1, :]
    qe = (q * jnp.exp2(b)).astype(BF16)
    ke = (k * jnp.exp2(b_last - b)).astype(BF16)
    st = s_ref[...]
    s_ref[...] = st * jnp.exp2(b_last) + lax.dot_general(
        v, ke, TN_DIMS, preferred_element_type=F32)
    return _bdot_nt(qe, st)


def _gla_kernel(*refs, zero_init, nt):
    if zero_init:
        (tri_ref, lvl_ref, q_ref, k_ref, gf_ref, gb_ref, v_ref,
         o_ref, sfo_ref, sbo_ref, sf_ref, sb_ref, bb_ref) = refs
        sf_ref[...] = jnp.zeros_like(sf_ref)
        sb_ref[...] = jnp.zeros_like(sb_ref)
    else:
        (tri_ref, lvl_ref, q_ref, k_ref, gf_ref, gb_ref, v_ref, s0f_ref, s0b_ref,
         o_ref, sfo_ref, sbo_ref, sf_ref, sb_ref, bb_ref) = refs
        for hh in range(sf_ref.shape[0]):
            sf_ref[hh] = s0f_ref[hh].T
            sb_ref[hh] = s0b_ref[hh].T
    hp = sf_ref.shape[0]

    def tile_rows(tile):
        r0 = tile * GLA_TILE
        return pl.ds(r0 if isinstance(r0, int) else pl.multiple_of(r0, GLA_TILE), GLA_TILE)

    heads = [(slice(hh * GLA_DK, (hh + 1) * GLA_DK), slice(hh * GLA_DV, (hh + 1) * GLA_DV), hh)
             for hh in range(hp)]

    def forward_sweep(t, carry):
        rows = tile_rows(t)
        g_f, g_b = gf_ref[rows, :], gb_ref[rows, :]
        c = None
        for part in _split3(jnp.concatenate([g_f, g_b], axis=1)):
            term = jnp.dot(tri_ref[...], part, preferred_element_type=F32)
            c = term if c is None else c + term
        width = hp * GLA_DK
        c_b = c[:, width:]
        b_f = c[:, :width]
        b_b = (c_b[GLA_TILE - 1:GLA_TILE, :] - c_b) + g_b
        bb_ref[t] = b_b
        for dk, dv, hh in heads:
            q, k, v = q_ref[rows, dk], k_ref[rows, dk], v_ref[rows, dv]
            att = _pair_matrix(q, k, g_f[:, dk], g_b[:, dk], b_f[:, dk], b_b[:, dk], lvl_ref[...])
            o_ref[rows, dv] = (jnp.dot(att, v, preferred_element_type=F32)
                               + _carry_state(q, k, b_f[:, dk], v, sf_ref.at[hh], False))
        return carry

    def backward_sweep(i, carry):
        t = nt - 1 - i
        rows = tile_rows(t)
        b_b = bb_ref[t]
        for dk, dv, hh in heads:
            o_ref[rows, dv] += _carry_state(q_ref[rows, dk], k_ref[rows, dk], b_b[:, dk],
                                            v_ref[rows, dv], sb_ref.at[hh], True)
        return carry

    if nt == 1:
        forward_sweep(0, 0)
        backward_sweep(0, 0)
    else:
        lax.fori_loop(0, nt, forward_sweep, 0)
        lax.fori_loop(0, nt, backward_sweep, 0)
    for hh in range(hp):
        sfo_ref[hh] = sf_ref[hh].T
        sbo_ref[hh] = sb_ref[hh].T


def _gla_scan(grp, row0, q, k, v, gf, gb, s0f, s0b, tri, lvl):
    n = grp.n
    nt = n // GLA_TILE
    per_head = 2 * n * (4 * GLA_DK * 4 + GLA_DV * 2 + GLA_DV * 4)
    hp = GLA_HEADS
    while hp > 1 and hp * per_head > GLA_SEQ_VMEM_BUDGET:
        hp //= 2
    zero_init = s0f is None
    half = GLA_TILE // 2
    assert row0 % n == 0
    seq0 = row0 // n
    in_dk = pl.BlockSpec((n, hp * GLA_DK), lambda b, h: (seq0 + b, h))
    in_dv = pl.BlockSpec((n, hp * GLA_DV), lambda b, h: (seq0 + b, h))
    seq_dv = pl.BlockSpec((n, hp * GLA_DV), lambda b, h: (b, h))
    state_spec = pl.BlockSpec((None, hp, GLA_DK, GLA_DV), lambda b, h: (b, h, 0, 0))
    in_specs = [_const_spec((GLA_TILE, GLA_TILE)), _const_spec((half, half)),
                in_dk, in_dk, in_dk, in_dk, in_dv]
    args = [tri, lvl, q, k, gf, gb, v]
    if not zero_init:
        in_specs += [state_spec, state_spec]
        args += [s0f, s0b]
    state_shape = jax.ShapeDtypeStruct((grp.b, GLA_HEADS, GLA_DK, GLA_DV), F32)
    return pl.pallas_call(
        functools.partial(_gla_kernel, zero_init=zero_init, nt=nt),
        grid=(grp.b, GLA_HEADS // hp),
        in_specs=in_specs,
        out_specs=[seq_dv, state_spec, state_spec],
        out_shape=[jax.ShapeDtypeStruct((grp.t, GLA_VD), F32), state_shape, state_shape],
        scratch_shapes=[pltpu.VMEM((hp, GLA_DV, GLA_DK), F32), pltpu.VMEM((hp, GLA_DV, GLA_DK), F32),
                        pltpu.VMEM((nt, GLA_TILE, hp * GLA_DK), F32)],
        compiler_params=_cparams("arbitrary", "arbitrary"),
        name="gla_scan",
    )(*args)


def _gla_constants():
    half = GLA_TILE // 2
    i = jnp.arange(GLA_TILE)[:, None]
    j = jnp.arange(GLA_TILE)[None, :]
    tri = (j <= i).astype(BF16)
    ih, jh = i[:half], j[:, :half]
    x = jnp.bitwise_xor(ih, jh)
    lvl = sum((x >= (1 << p)).astype(jnp.int32) for p in range(half.bit_length() - 1))
    return tri, lvl


def _postmix1_kernel(oc_ref, ol_ref, og_ref, x_ref, mod_ref, g_ref, gn_ref, wo_hbm, w1_hbm, w2_hbm,
                     yc_ref, yl_ref, *scratch, stream):
    weights = _TailWeights(1, wo_hbm, w1_hbm, w2_hbm, *scratch)
    _first_step_streams_weights(
        functools.partial(_postmix1_body, oc_ref, ol_ref, og_ref, x_ref, mod_ref, g_ref, gn_ref,
                          yc_ref, yl_ref, weights, stream), weights)


def _postmix1_body(oc_ref, ol_ref, og_ref, x_ref, mod_ref, g_ref, gn_ref, yc_ref, yl_ref,
                   weights, stream, streaming):
    is_lat = stream.is_lat()
    gn = gn_ref[...]
    ys = []
    for h in range(GLA_HEADS):
        cols = slice(h * GLA_DV, (h + 1) * GLA_DV)
        o = _rms(jnp.where(is_lat, ol_ref[:, cols], oc_ref[:, cols]), gn)
        og = og_ref[:, cols]
        ys.append((o * (og / (1.0 + jnp.exp(-og)))).astype(BF16))
    if streaming:
        weights.fetch_out_proj()
    mix = None
    for h, y in enumerate(ys):
        part = jnp.dot(y, weights.wo_s[h * GLA_DV:(h + 1) * GLA_DV, :], preferred_element_type=F32)
        mix = part if mix is None else mix + part
    gt1 = _mod_chunk(mod_ref, 2)
    x1 = x_ref[...] + gt1 * _rms(mix, g_ref[1:2, :])
    y = _mlp_tail(x1, mod_ref, g_ref, weights, streaming)

    @pl.when(jnp.logical_not(is_lat))
    def _():
        yc_ref[...] = y

    @pl.when(is_lat)
    def _():
        yl_ref[...] = y


def _postmix1(stream, o_ctx, o_lat, og, x2d, mod, norm_g, gla_norm_g, w_out, w1, w2):
    hbm = pl.BlockSpec(memory_space=pl.ANY)
    return pl.pallas_call(
        functools.partial(_postmix1_kernel, stream=stream),
        grid=(stream.tiles,),
        in_specs=[stream.ctx_spec(GLA_VD), stream.lat_spec(GLA_VD), stream.spec(GLA_VD),
                  stream.spec(D_MODEL), stream.mod_spec(), _const_spec((4, D_MODEL)),
                  _const_spec((1, GLA_DV)), hbm, hbm, hbm],
        out_specs=[stream.ctx_spec(D_MODEL), stream.lat_spec(D_MODEL)],
        out_shape=[jax.ShapeDtypeStruct((stream.ctx.t, D_MODEL), F32),
                   jax.ShapeDtypeStruct((stream.lat.t, D_MODEL), F32)],
        scratch_shapes=_tail_weight_scratch(),
        compiler_params=_cparams("arbitrary"),
        name="postmix1_mlp",
    )(o_ctx, o_lat, og, x2d, mod, norm_g, gla_norm_g, w_out, w1, w2)


def kernel(x_prompt, x_sample, cache_k, cache_v, state_fwd, state_bwd, c, c_ctx, mod_w, mod_b,
           norm_g, ab_w_in, conv_w, attn_sink, ab_w_out, gla_w_in, gla_gate_w, gla_gate_b,
           gla_norm_g, gla_w_out, mlp_w1, mlp_w2):
    b_ctx, n_ctx, _ = x_prompt.shape
    b_lat, n_lat, _ = x_sample.shape
    assert mod_w.shape[0] == 2 and ab_w_in.shape[0] == 1 and gla_w_in.shape[0] == 1
    assert 1 + b_lat <= 8

    cond8 = jnp.zeros((8, D_MODEL), F32).at[0].set(c_ctx).at[1:1 + b_lat].set(c)
    mod = _modulation(cond8, mod_w, mod_b)
    mods = [mod[l].reshape(8, 1, -1) for l in range(2)]

    w_gate = jnp.zeros((2 * GLA_RANK, 2 * GLA_QK), F32)
    w_gate = w_gate.at[:GLA_RANK, :GLA_QK].set(gla_gate_w[0, 0])
    w_gate = w_gate.at[GLA_RANK:, GLA_QK:].set(gla_gate_w[0, 1])
    tri, lvl = _gla_constants()
    p = {
        "conv_w": conv_w[0],
        "sink": attn_sink[0],
        "gla_w_in": jnp.swapaxes(gla_w_in, 1, 2),
        "gla_w_gate": w_gate.astype(BF16),
        "gla_gate_bias": gla_gate_b[0].reshape(1, 2 * GLA_QK),
        "gla_norm_g": gla_norm_g[0].reshape(1, GLA_DV),
    }

    ctx, lat = _Group(b_ctx, n_ctx), _Group(b_lat, n_lat)
    stream = _Stream(ctx, lat, TOKEN_TILE)
    x_ctx, x_lat = x_prompt.reshape(ctx.t, D_MODEL), x_sample.reshape(lat.t, D_MODEL)
    past = cache_k.shape[2]
    k_ctx = cache_k[:, 0].reshape(b_lat, past, KD)
    v_ctx = cache_v[:, 0].reshape(b_lat, past, KD)

    wide = _Stream(ctx, lat, PREMIX_PARTS * TOKEN_TILE)
    zc, q, kv, k_t, v_t = _premix0(wide, x_ctx, x_lat, mods[0], norm_g[0], ab_w_in,
                                   _rope_tables(n_lat, wide.tm))
    attn_ctx = _ctx_attention(ctx, q, kv, p["sink"])
    attn_lat = _lat_attention(lat, ctx.t, q, kv, k_ctx, v_ctx, p["sink"])
    x1 = _postmix0(stream, zc, attn_ctx, attn_lat, x_ctx, x_lat, mods[0], norm_g[0], p["conv_w"],
                   ab_w_out, mlp_w1, mlp_w2)
    gq, gk, gv, og, gf, gb = _premix1(wide, x1, mods[1], norm_g[1], p["gla_w_in"],
                                      p["gla_w_gate"], p["gla_gate_bias"])
    o_ctx, sf, sb = _gla_scan(ctx, 0, gq, gk, gv, gf, gb, None, None, tri, lvl)
    o_lat, _, _ = _gla_scan(lat, ctx.t, gq, gk, gv, gf, gb, state_fwd[:, 0], state_bwd[:, 0],
                            tri, lvl)
    y_ctx, y_lat = _postmix1(stream, o_ctx, o_lat, og, x1, mods[1], norm_g[1], p["gla_norm_g"],
                             gla_w_out, mlp_w1, mlp_w2)

    def cache_layout(t):
        t = t.reshape(b_ctx, 1, N_KV_HEADS, HEAD_DIM, n_ctx)
        return jnp.transpose(t, (0, 1, 4, 2, 3))

    return (y_ctx.reshape(x_prompt.shape), y_lat.reshape(x_sample.shape),
            cache_layout(k_t), cache_layout(v_t), sf[:, None], sb[:, None])
```
